```python
import math
import jax, jax.numpy as jnp
from jax import lax
import numpy as np

D_MODEL = 2048
BATCH = 4
SEQ = 2048
DEPTH = 1
DEC_BATCH = 128
DEC_SEQ = 4
PAST_LEN = 16384
PAGE_SIZE = 128

D_CONV = D_MODEL // 2
N_CONV_GROUPS = 16
D_MLSTM = D_MODEL - D_CONV
N_HEADS = 4
HEAD_DIM = D_MLSTM // N_HEADS
CONV_K = 3
D_FF = 5632
CHUNK = 64
ALPHA = (2 * DEPTH) ** 0.25
BETA = (8 * DEPTH) ** -0.25
LN_EPS = 1e-5
IN_COLS = 3 * D_CONV + 4 * D_MLSTM + 2 * N_HEADS

kernel_name = "hymba_shortconv_mlstm_convffn_deepnorm_adaln_step"


def _layernorm(x):
    xf = x.astype(jnp.float32)
    mu = jnp.mean(xf, axis=-1, keepdims=True)
    var = jnp.mean(jnp.square(xf - mu), axis=-1, keepdims=True)
    return (xf - mu) * lax.rsqrt(var + LN_EPS)


def _causal_dwconv(x, buf, w):
    T = x.shape[1]
    xp = jnp.concatenate([buf.astype(x.dtype), x], axis=1)
    y = w[0] * xp[:, 0:T]
    for j in range(1, CONV_K):
        y = y + w[j] * xp[:, j:j + T]
    return y, xp[:, -(CONV_K - 1):]


def _mlstm(q, k, v, log_i, log_f, C0, n0, m0):
    Bsz, T = q.shape[0], q.shape[1]
    L = math.gcd(T, CHUNK)
    nc = T // L

    def to_chunks(a):
        a = a.reshape((Bsz, nc, L) + a.shape[2:])
        a = jnp.moveaxis(a, 1, 0)
        return jnp.swapaxes(a, 2, 3)

    qc, kc, vc = to_chunks(q), to_chunks(k), to_chunks(v)
    ic, fc = to_chunks(log_i), to_chunks(log_f)
    causal = jnp.tril(jnp.ones((L, L), dtype=bool))

    def step(carry, xs):
        C, n, m = carry
        qq, kk, vv, ii, ff = xs
        b = jnp.cumsum(ff, axis=-1)
        a = b + m[..., None]
        dlog = b[..., :, None] - b[..., None, :] + ii[..., None, :]
        dlog = jnp.where(causal, dlog, -jnp.inf)
        mt = jnp.maximum(a, jnp.max(dlog, axis=-1))
        dw = jnp.exp(dlog - mt[..., None])
        inter = jnp.exp(a - mt)
        s = jnp.einsum('bhtd,bhsd->bhts', qq, kk) * dw
        num = jnp.einsum('bhts,bhse->bhte', s, vv) + inter[..., None] * jnp.einsum('bhtd,bhde->bhte', qq, C)
        den = jnp.sum(s, axis=-1) + inter * jnp.einsum('bhtd,bhd->bht', qq, n)
        h = num / jnp.maximum(jnp.abs(den), jnp.exp(-mt))[..., None]
        m_new = mt[..., -1]
        wc = jnp.exp(b[..., -1:] - b + ii - m_new[..., None])
        dc = jnp.exp(a[..., -1] - m_new)
        C_new = dc[..., None, None] * C + jnp.einsum('bhs,bhsd,bhse->bhde', wc, kk, vv)
        n_new = dc[..., None] * n + jnp.einsum('bhs,bhsd->bhd', wc, kk)
        return (C_new, n_new, m_new), h

    (C, n, m), hs = lax.scan(step, (C0, n0, m0), (qc, kc, vc, ic, fc))
    h = jnp.swapaxes(hs, 2, 3)
    h = jnp.moveaxis(h, 0, 1).reshape(Bsz, T, N_HEADS, HEAD_DIM)
    return h, C, n, m


def _layer(x, c, conv_buf, C0, n0, m0, ffn_buf, w_ada, b_ada, w_in, b_gate, w_conv, w_mh_norm,
           w_out, ln1_g, ln1_b, w_up, w_ffn_conv, w_down, ln2_g, ln2_b):
    Bsz, T, _ = x.shape
    mod = jax.nn.silu(c.astype(jnp.float32)) @ w_ada + b_ada
    sh1, sc1, g1, sh2, sc2, g2 = [t[:, None, :] for t in jnp.split(mod, 6, axis=-1)]

    u = _layernorm(x) * (1.0 + sc1) + sh1
    p = u @ w_in
    cuts = [D_CONV, 2 * D_CONV, 3 * D_CONV,
            3 * D_CONV + D_MLSTM, 3 * D_CONV + 2 * D_MLSTM,
            3 * D_CONV + 3 * D_MLSTM, 3 * D_CONV + 4 * D_MLSTM]
    bg, cg, hc, q, k, v, o, gates = jnp.split(p, cuts, axis=-1)

    yc, conv_new = _causal_dwconv(cg * hc, conv_buf, w_conv)
    y_conv = bg * yc

    gates = gates + b_gate
    log_i = gates[..., :N_HEADS]
    log_f = jax.nn.log_sigmoid(gates[..., N_HEADS:])
    qh = q.reshape(Bsz, T, N_HEADS, HEAD_DIM).astype(jnp.float32)
    kh = k.reshape(Bsz, T, N_HEADS, HEAD_DIM).astype(jnp.float32) * (HEAD_DIM ** -0.5)
    vh = v.reshape(Bsz, T, N_HEADS, HEAD_DIM).astype(jnp.float32)
    hm, C1, n1, m1 = _mlstm(qh, kh, vh, log_i.astype(jnp.float32), log_f.astype(jnp.float32),
                            C0.astype(jnp.float32), n0.astype(jnp.float32), m0.astype(jnp.float32))
    hm = _layernorm(hm).reshape(Bsz, T, D_MLSTM) * w_mh_norm
    y_mlstm = jax.nn.sigmoid(o) * hm

    mix = jnp.concatenate([y_conv, y_mlstm], axis=-1) @ w_out
    x = _layernorm(ALPHA * x + (1.0 + g1) * mix) * ln1_g + ln1_b

    u2 = _layernorm(x) * (1.0 + sc2) + sh2
    up = u2 @ w_up
    a, g = jnp.split(up, 2, axis=-1)
    ac, ffn_new = _causal_dwconv(a, ffn_buf, w_ffn_conv)
    y = (jax.nn.silu(ac) * g) @ w_down
    x = _layernorm(ALPHA * x + (1.0 + g2) * y) * ln2_g + ln2_b
    return x, conv_new, C1, n1, m1, ffn_new


def setup_inputs(seed: int = 0) -> dict:
    key = jax.random.key(seed)
    ks = jax.random.split(key, 24)
    nrm = lambda k, s: jax.random.normal(k, s, dtype=jnp.float32)
    f_bias = jnp.linspace(3.0, 6.0, N_HEADS, dtype=jnp.float32)
    i_bias = jnp.full((N_HEADS,), -2.0, dtype=jnp.float32)
    b_gate = jnp.concatenate([i_bias, f_bias])[None, :] + 0.1 * nrm(ks[10], (DEPTH, 2 * N_HEADS))
    return {
        "x_prompt": nrm(ks[0], (BATCH, SEQ, D_MODEL)),
        "x_sample": nrm(ks[1], (DEC_BATCH, DEC_SEQ, D_MODEL)),
        "c_prompt": nrm(ks[2], (BATCH, D_MODEL)),
        "c_sample": nrm(ks[3], (DEC_BATCH, D_MODEL)),
        "state_conv": nrm(ks[4], (DEPTH, DEC_BATCH, CONV_K - 1, D_CONV)),
        "state_mlstm_C": 0.1 * nrm(ks[5], (DEPTH, DEC_BATCH, N_HEADS, HEAD_DIM, HEAD_DIM)),
        "state_mlstm_n": nrm(ks[6], (DEPTH, DEC_BATCH, N_HEADS, HEAD_DIM)),
        "state_mlstm_m": 0.5 * nrm(ks[7], (DEPTH, DEC_BATCH, N_HEADS)),
        "state_ffn_conv": nrm(ks[8], (DEPTH, DEC_BATCH, CONV_K - 1, D_FF)),
        "w_ada": 0.1 * D_MODEL ** -0.5 * nrm(ks[9], (DEPTH, D_MODEL, 6 * D_MODEL)),
        "b_ada": 0.02 * nrm(ks[11], (DEPTH, 6 * D_MODEL)),
        "w_in": D_MODEL ** -0.5 * nrm(ks[12], (DEPTH, D_MODEL, IN_COLS)),
        "b_gate": b_gate,
        "w_conv": CONV_K ** -0.5 * nrm(ks[13], (DEPTH, CONV_K, D_CONV)),
        "w_mh_norm": 1.0 + 0.02 * nrm(ks[14], (DEPTH, D_MLSTM)),
        "w_out": BETA * D_MODEL ** -0.5 * nrm(ks[15], (DEPTH, D_MODEL, D_MODEL)),
        "ln1_g": 1.0 + 0.02 * nrm(ks[16], (DEPTH, D_MODEL)),
        "ln1_b": 0.02 * nrm(ks[17], (DEPTH, D_MODEL)),
        "w_up": D_MODEL ** -0.5 * nrm(ks[18], (DEPTH, D_MODEL, 2 * D_FF)),
        "w_ffn_conv": CONV_K ** -0.5 * nrm(ks[19], (DEPTH, CONV_K, D_FF)),
        "w_down": BETA * D_FF ** -0.5 * nrm(ks[20], (DEPTH, D_FF, D_MODEL)),
        "ln2_g": 1.0 + 0.02 * nrm(ks[21], (DEPTH, D_MODEL)),
        "ln2_b": 0.02 * nrm(ks[22], (DEPTH, D_MODEL)),
    }


def reference(x_prompt, x_sample, c_prompt, c_sample, state_conv, state_mlstm_C, state_mlstm_n,
              state_mlstm_m, state_ffn_conv, w_ada, b_ada, w_in, b_gate, w_conv, w_mh_norm, w_out,
              ln1_g, ln1_b, w_up, w_ffn_conv, w_down, ln2_g, ln2_b):
    Bp = x_prompt.shape[0]
    f32 = jnp.float32
    xp = x_prompt
    xs = x_sample
    pc, pC, pn, pm, pf = [], [], [], [], []
    sc, sC, sn, sm, sf = [], [], [], [], []
    for l in range(DEPTH):
        wl = (w_ada[l], b_ada[l], w_in[l], b_gate[l], w_conv[l], w_mh_norm[l], w_out[l],
              ln1_g[l], ln1_b[l], w_up[l], w_ffn_conv[l], w_down[l], ln2_g[l], ln2_b[l])
        xp, c1, C1, n1, m1, f1 = _layer(
            xp, c_prompt,
            jnp.zeros((Bp, CONV_K - 1, D_CONV), f32),
            jnp.zeros((Bp, N_HEADS, HEAD_DIM, HEAD_DIM), f32),
            jnp.zeros((Bp, N_HEADS, HEAD_DIM), f32),
            jnp.zeros((Bp, N_HEADS), f32),
            jnp.zeros((Bp, CONV_K - 1, D_FF), f32), *wl)
        pc.append(c1); pC.append(C1); pn.append(n1); pm.append(m1); pf.append(f1)
        xs, c2, C2, n2, m2, f2 = _layer(
            xs, c_sample, state_conv[l], state_mlstm_C[l], state_mlstm_n[l], state_mlstm_m[l],
            state_ffn_conv[l], *wl)
        sc.append(c2); sC.append(C2); sn.append(n2); sm.append(m2); sf.append(f2)
    y_prompt = xp.astype(x_prompt.dtype)
    y_sample = xs.astype(x_sample.dtype)
    return (y_prompt, y_sample,
            jnp.stack(pc), jnp.stack(pC), jnp.stack(pn), jnp.stack(pm), jnp.stack(pf),
            jnp.stack(sc), jnp.stack(sC), jnp.stack(sn), jnp.stack(sm), jnp.stack(sf))
```

```python
import functools

import jax
import jax.numpy as jnp
from jax import lax
from jax.experimental import pallas as pl
from jax.experimental.pallas import tpu as pltpu

F32 = jnp.float32
BF16 = jnp.bfloat16

N_HEADS = 4
HEAD_DIM = 256
CONV_K = 3
LN_EPS = 1e-5
NEG = -1e30
GATE_LANES = 128
STATE_ROWS = 8
VMEM_LIMIT = 56 * 1024 * 1024


def _cparams(sem):
    return pltpu.CompilerParams(dimension_semantics=sem, vmem_limit_bytes=VMEM_LIMIT)


def _ln(x):
    mu = jnp.mean(x, axis=-1, keepdims=True)
    xc = x - mu
    var = jnp.mean(xc * xc, axis=-1, keepdims=True)
    return xc * lax.rsqrt(var + LN_EPS)


def _log_sigmoid(x):
    return jnp.minimum(x, 0.0) - jnp.log1p(jnp.exp(-jnp.abs(x)))


def _sigmoid(x):
    return 1.0 / (1.0 + jnp.exp(-x))


def _dot(a, b):
    return jnp.dot(a, b, preferred_element_type=F32)


def _causal_conv3(z, w, t, e1, e2):
    z1 = jnp.where(t >= 1, pltpu.roll(z, 1, 0), e1)
    z2 = jnp.where(t >= 2, pltpu.roll(z, 2, 0), e2)
    return w[0:1] * z2 + w[1:2] * z1 + w[2:3] * z


def _ada_kernel(c_ref, w_ref, b_ref, o_ref):
    c = c_ref[...]
    s = (c * _sigmoid(c)).astype(BF16)
    o_ref[...] = _dot(s, w_ref[...].astype(BF16)) + b_ref[...]


def _ada(c, w, b, tn=1024):
    r, d = c.shape
    n = w.shape[1]
    return pl.pallas_call(
        _ada_kernel,
        grid=(n // tn,),
        in_specs=[
            pl.BlockSpec((r, d), lambda j: (0, 0)),
            pl.BlockSpec((d, tn), lambda j: (0, j)),
            pl.BlockSpec((1, tn), lambda j: (0, j)),
        ],
        out_specs=pl.BlockSpec((r, tn), lambda j: (0, j)),
        out_shape=jax.ShapeDtypeStruct((r, n), F32),
        compiler_params=_cparams(("arbitrary",)),
        name="ada",
    )(c, w, b.reshape(1, n))


def _inproj_kernel(*refs, tm, tpb, period, sample):
    if sample:
        (x_ref, sh_ref, sc_ref, wb_ref, wc_ref, wh_ref, wq_ref, wk_ref, wv_ref, wo_ref, wg_ref,
         wconv_ref, e1_ref, e2_ref,
         yc_ref, q_ref, k_ref, v_ref, o_ref, g_ref, zt_ref, u_scr) = refs
    else:
        (x_ref, sh_ref, sc_ref, wb_ref, wc_ref, wh_ref, wq_ref, wk_ref, wv_ref, wo_ref, wg_ref,
         wconv_ref,
         yc_ref, q_ref, k_ref, v_ref, o_ref, g_ref, zt_ref, u_scr, carry_scr) = refs
    m = pl.program_id(0)
    j = pl.program_id(1)

    @pl.when(j == 0)
    def _():
        u = _ln(x_ref[...]) * (1.0 + sc_ref[0]) + sh_ref[0]
        ub = u.astype(BF16)
        u_scr[...] = ub
        g_ref[...] = _dot(ub, wg_ref[...])

    u = u_scr[...]
    z = _dot(u, wc_ref[...]) * _dot(u, wh_ref[...])
    t = lax.broadcasted_iota(jnp.int32, z.shape, 0)
    if sample:
        t = lax.rem(t, period)
        e1 = e1_ref[...]
        e2 = e2_ref[...]
    else:
        @pl.when(lax.rem(m, tpb) == 0)
        def _():
            carry_scr[j] = jnp.zeros(carry_scr.shape[1:], F32)

        prev = carry_scr[j]
        p0 = prev[STATE_ROWS - 2:STATE_ROWS - 1]
        p1 = prev[STATE_ROWS - 1:STATE_ROWS]
        e1 = p1
        e2 = jnp.where(t == 0, p0, p1)
    yc = _causal_conv3(z, wconv_ref[...], t, e1, e2)
    yc_ref[...] = (_dot(u, wb_ref[...]) * yc).astype(BF16)
    if sample:
        zt_ref[0] = z
    else:
        zt_ref[0] = z[tm - STATE_ROWS:]
        carry_scr[j] = z[tm - STATE_ROWS:]
    q_ref[...] = _dot(u, wq_ref[...]).astype(BF16)
    k_ref[...] = (_dot(u, wk_ref[...]) * (HEAD_DIM ** -0.5)).astype(BF16)
    v_ref[...] = _dot(u, wv_ref[...]).astype(BF16)
    o_ref[...] = _dot(u, wo_ref[...]).astype(BF16)


def _inproj(x, mod, w_in, w_gate, w_conv, *, tm, tpb, sample, period=0, e1=None, e2=None):
    rows, d = x.shape
    dc = w_conv.shape[1]
    tn = HEAD_DIM
    nj = dc // tn
    nm = rows // tm
    nb, r, _ = mod.shape
    tail = tm if sample else STATE_ROWS

    def wspec(off):
        return pl.BlockSpec((d, tn), lambda m, j, off=off: (0, off * nj + j))

    in_specs = [
        pl.BlockSpec((tm, d), lambda m, j: (m, 0)),
        pl.BlockSpec((1, r, d), lambda m, j: (m // tpb, 0, 0)),
        pl.BlockSpec((1, r, d), lambda m, j: (m // tpb, 0, 1)),
        wspec(0), wspec(1), wspec(2), wspec(3), wspec(4), wspec(5), wspec(6),
        pl.BlockSpec((d, GATE_LANES), lambda m, j: (0, 0)),
        pl.BlockSpec((CONV_K, tn), lambda m, j: (0, j)),
    ]
    args = [x, mod, mod, w_in, w_in, w_in, w_in, w_in, w_in, w_in, w_gate, w_conv]
    scratch = [pltpu.VMEM((tm, d), BF16)]
    if sample:
        in_specs += [pl.BlockSpec((tm, tn), lambda m, j: (m, j))] * 2
        args += [e1, e2]
    else:
        scratch.append(pltpu.VMEM((nj, STATE_ROWS, tn), F32))
    act = pl.BlockSpec((tm, tn), lambda m, j: (m, j))
    out_specs = [act, act, act, act, act,
                 pl.BlockSpec((tm, GATE_LANES), lambda m, j: (m, 0)),
                 pl.BlockSpec((1, tail, tn), lambda m, j: (m, 0, j))]
    out_shape = [jax.ShapeDtypeStruct((rows, dc), BF16)] * 5 + [
        jax.ShapeDtypeStruct((rows, GATE_LANES), F32),
        jax.ShapeDtypeStruct((nm, tail, dc), F32)]
    return pl.pallas_call(
        functools.partial(_inproj_kernel, tm=tm, tpb=tpb, period=period, sample=sample),
        grid=(nm, nj),
        in_specs=in_specs,
        out_specs=out_specs,
        out_shape=out_shape,
        scratch_shapes=scratch,
        compiler_params=_cparams(("arbitrary", "arbitrary")),
        name="inproj_sample" if sample else "inproj_prompt",
    )(*args)


def _split3(x):
    hi = x.astype(BF16)
    r1 = x - hi.astype(F32)
    mid = r1.astype(BF16)
    lo = (r1 - mid.astype(F32)).astype(BF16)
    return hi, mid, lo


def _head_out(hh, o, wmh):
    return (_sigmoid(o.astype(F32)) * (_ln(hh) * wmh)).astype(BF16)


def _mlstm_chunk_kernel(q_ref, k_ref, v_ref, o_ref, g_ref, bg_ref, wmh_ref,
                        y_ref, c_ref, n_ref, m_ref, *, L):
    c = pl.program_id(1)

    @pl.when(c == 0)
    def _():
        c_ref[...] = jnp.zeros_like(c_ref)
        n_ref[...] = jnp.zeros_like(n_ref)
        m_ref[...] = jnp.zeros_like(m_ref)

    row = lax.broadcasted_iota(jnp.int32, (L, L), 0)
    col = lax.broadcasted_iota(jnp.int32, (L, L), 1)
    causal = col <= row
    tril = jnp.where(causal, 1.0, 0.0).astype(BF16)

    g = g_ref[0] + bg_ref[...]
    lf = _log_sigmoid(g)
    hi, mid, lo = _split3(lf)
    bcum = _dot(tril, hi) + _dot(tril, mid) + _dot(tril, lo)
    bcum_t = bcum.T
    g_t = g.T

    for h in range(N_HEADS):
        hs = slice(h * HEAD_DIM, (h + 1) * HEAD_DIM)
        bc = bcum[:, N_HEADS + h:N_HEADS + h + 1]
        br = bcum_t[N_HEADS + h:N_HEADS + h + 1, :]
        li_r = g_t[h:h + 1, :]
        li_c = g[:, h:h + 1]
        m_prev = m_ref[0, h:h + 1, 0:1]
        a = bc + m_prev
        dlog = jnp.where(causal, bc - br + li_r, NEG)
        mt = jnp.maximum(a, jnp.max(dlog, axis=1, keepdims=True))
        dw = jnp.exp(dlog - mt)
        inter = jnp.exp(a - mt)
        qh = q_ref[0, :, hs]
        kh = k_ref[0, :, hs]
        vh = v_ref[0, :, hs]
        s = lax.dot_general(qh, kh, (((1,), (1,)), ((), ())), preferred_element_type=F32) * dw
        c_old = c_ref[0, h]
        n_old = n_ref[0, h:h + 1, :]
        num = _dot(s.astype(BF16), vh) + inter * _dot(qh, c_old.astype(BF16))
        qn = jnp.sum(qh.astype(F32) * n_old, axis=1, keepdims=True)
        den = jnp.sum(s, axis=1, keepdims=True) + inter * qn
        hh = num / jnp.maximum(jnp.abs(den), jnp.exp(-mt))
        y_ref[0, :, hs] = _head_out(hh, o_ref[0, :, hs], wmh_ref[:, hs])

        m_new = mt[L - 1:L, :]
        b_last = bc[L - 1:L, :]
        wc = jnp.exp(b_last - bc + li_c - m_new)
        dc = jnp.exp(b_last + m_prev - m_new)
        vw = (vh.astype(F32) * wc).astype(BF16)
        c_ref[0, h] = dc * c_old + lax.dot_general(
            kh, vw, (((0,), (0,)), ((), ())), preferred_element_type=F32)
        n_ref[0, h:h + 1, :] = dc * n_old + jnp.sum(kh.astype(F32) * wc, axis=0, keepdims=True)
        m_ref[0, h:h + 1, :] = jnp.broadcast_to(m_new, (1, GATE_LANES))


def _mlstm_prompt(q, k, v, o, gates, bg, wmh, *, L):
    b, t, dm = q.shape
    nc = t // L
    act = pl.BlockSpec((1, L, dm), lambda i, c: (i, c, 0))
    return pl.pallas_call(
        functools.partial(_mlstm_chunk_kernel, L=L),
        grid=(b, nc),
        in_specs=[act, act, act, act,
                  pl.BlockSpec((1, L, GATE_LANES), lambda i, c: (i, c, 0)),
                  pl.BlockSpec((1, GATE_LANES), lambda i, c: (0, 0)),
                  pl.BlockSpec((1, dm), lambda i, c: (0, 0))],
        out_specs=[act,
                   pl.BlockSpec((1, N_HEADS, HEAD_DIM, HEAD_DIM), lambda i, c: (i, 0, 0, 0)),
                   pl.BlockSpec((1, N_HEADS, HEAD_DIM), lambda i, c: (i, 0, 0)),
                   pl.BlockSpec((1, N_HEADS, GATE_LANES), lambda i, c: (i, 0, 0))],
        out_shape=[jax.ShapeDtypeStruct((b, t, dm), BF16),
                   jax.ShapeDtypeStruct((b, N_HEADS, HEAD_DIM, HEAD_DIM), F32),
                   jax.ShapeDtypeStruct((b, N_HEADS, HEAD_DIM), F32),
                   jax.ShapeDtypeStruct((b, N_HEADS, GATE_LANES), F32)],
        compiler_params=_cparams(("arbitrary", "arbitrary")),
        name="mlstm_prompt",
    )(q, k, v, o, gates, bg, wmh)


SLAB = 16


def _mlstm_step_kernel(q_ref, k_ref, v_ref, o_ref, g_ref, bg_ref, mrow_ref, nrow_ref,
                       c0_ref, n0_ref, wmh_ref,
                       y_ref, c_ref, n_ref, m_ref, *, bb, T):
    R = bb * T
    per_slab = SLAB // T
    t = lax.rem(lax.broadcasted_iota(jnp.int32, (R, GATE_LANES), 0), T)

    def down(x, d):
        return pltpu.roll(x, d, 0)

    def up(x, d):
        return pltpu.roll(x, x.shape[0] - d, 0)

    def seg_last(x):
        out = x
        for d in range(1, T):
            out = jnp.where(t == T - 1 - d, up(x, d), out)
        return out

    g = g_ref[...] + bg_ref[...]
    li = pltpu.roll(g, N_HEADS, 1)
    lf = _log_sigmoid(g)
    b = lf
    for d in range(1, T):
        b = b + jnp.where(t >= d, down(lf, d), 0.0)
    m_prev = mrow_ref[...]
    a = b + m_prev
    dl = [li] + [jnp.where(t >= d, b - down(b, d) + down(li, d), NEG) for d in range(1, T)]
    mt = a
    for d in range(T):
        mt = jnp.maximum(mt, dl[d])
    dw = [jnp.exp(dl[d] - mt) for d in range(T)]
    inter = jnp.exp(a - mt)
    emt = jnp.exp(-mt)
    m_new = seg_last(mt)
    b_last = seg_last(b)
    wc = jnp.exp(b_last - b + li - m_new)
    dc = jnp.exp(b_last + m_prev - m_new)

    row_s = lax.broadcasted_iota(jnp.int32, (SLAB, HEAD_DIM), 0)
    row_r = lax.broadcasted_iota(jnp.int32, (R, HEAD_DIM), 0)

    for h in range(N_HEADS):
        hs = slice(h * HEAD_DIM, (h + 1) * HEAD_DIM)
        ln = N_HEADS + h

        def col(x):
            return x[:, ln:ln + 1]

        qb = q_ref[:, hs]
        kb = k_ref[:, hs]
        vb = v_ref[:, hs]
        qf = qb.astype(F32)
        kf = kb.astype(F32)
        vf = vb.astype(F32)
        num = jnp.zeros((R, HEAD_DIM), F32)
        den = jnp.zeros((R, 1), F32)
        for d in range(T):
            kd = kf if d == 0 else down(kf, d)
            vd = vf if d == 0 else down(vf, d)
            sw = jnp.sum(qf * kd, axis=1, keepdims=True) * col(dw[d])
            num = num + sw * vd
            den = den + sw

        qc_slabs = []
        for si in range(R // SLAB):
            q16 = qb[si * SLAB:(si + 1) * SLAB]
            acc = jnp.zeros((SLAB, HEAD_DIM), F32)
            for bl in range(per_slab):
                bi = si * per_slab + bl
                r = _dot(q16, c0_ref[bi, h].astype(BF16))
                acc = jnp.where(row_s // T == bl, r, acc)
            qc_slabs.append(acc)
        qc = jnp.concatenate(qc_slabs, axis=0)
        qn = jnp.sum(qf * nrow_ref[:, hs], axis=1, keepdims=True)
        num = num + col(inter) * qc
        den = den + col(inter) * qn
        hh = num / jnp.maximum(jnp.abs(den), col(emt))
        y_ref[:, hs] = _head_out(hh, o_ref[:, hs], wmh_ref[:, hs])

        wk = kf * col(wc)
        vw = vf * col(wc)
        for si in range(R // SLAB):
            k16 = kb[si * SLAB:(si + 1) * SLAB]
            vw16 = vw[si * SLAB:(si + 1) * SLAB]
            for bl in range(per_slab):
                bi = si * per_slab + bl
                last = bi * T + T - 1
                vwb = jnp.where(row_s // T == bl, vw16, 0.0).astype(BF16)
                dcb = dc[last:last + 1, ln:ln + 1]
                c_ref[bi, h] = dcb * c0_ref[bi, h] + lax.dot_general(
                    k16, vwb, (((0,), (0,)), ((), ())), preferred_element_type=F32)
                n_ref[bi, h:h + 1, :] = dcb * n0_ref[bi, h:h + 1, :] + jnp.sum(
                    jnp.where(row_r // T == bi, wk, 0.0), axis=0, keepdims=True)
                m_ref[bi, h:h + 1, :] = jnp.broadcast_to(
                    m_new[last:last + 1, ln:ln + 1], (1, GATE_LANES))


def _mlstm_sample(q, k, v, o, gates, bg, mrow, nrow, c0, n0, wmh, *, bb, T):
    rows, dm = q.shape
    nb = c0.shape[0]
    R = bb * T
    act = pl.BlockSpec((R, dm), lambda i: (i, 0))
    gat = pl.BlockSpec((R, GATE_LANES), lambda i: (i, 0))
    cspec = pl.BlockSpec((bb, N_HEADS, HEAD_DIM, HEAD_DIM), lambda i: (i, 0, 0, 0))
    nspec = pl.BlockSpec((bb, N_HEADS, HEAD_DIM), lambda i: (i, 0, 0))
    return pl.pallas_call(
        functools.partial(_mlstm_step_kernel, bb=bb, T=T),
        grid=(nb // bb,),
        in_specs=[act, act, act, act, gat,
                  pl.BlockSpec((1, GATE_LANES), lambda i: (0, 0)),
                  gat,
                  pl.BlockSpec((R, dm), lambda i: (i, 0)),
                  cspec, nspec,
                  pl.BlockSpec((1, dm), lambda i: (0, 0))],
        out_specs=[act, cspec, nspec,
                   pl.BlockSpec((bb, N_HEADS, GATE_LANES), lambda i: (i, 0, 0))],
        out_shape=[jax.ShapeDtypeStruct((rows, dm), BF16),
                   jax.ShapeDtypeStruct(c0.shape, F32),
                   jax.ShapeDtypeStruct(n0.shape, F32),
                   jax.ShapeDtypeStruct((nb, N_HEADS, GATE_LANES), F32)],
        compiler_params=_cparams(("arbitrary",)),
        name="mlstm_sample",
    )(q, k, v, o, gates, bg, mrow, nrow, c0, n0, wmh)


def _outproj_kernel(yc_ref, ym_ref, w_ref, x_ref, g1_ref, lg_ref, lb_ref, o_ref, *, alpha):
    dc = yc_ref.shape[1]
    mix = _dot(yc_ref[...], w_ref[0:dc, :]) + _dot(ym_ref[...], w_ref[dc:, :])
    o_ref[...] = _ln(alpha * x_ref[...] + (1.0 + g1_ref[0]) * mix) * lg_ref[...] + lb_ref[...]


def _outproj(yc, ym, w_out, x, mod, ln_g, ln_b, *, tm, tpb, alpha):
    rows, d = x.shape
    dc = yc.shape[1]
    dm = ym.shape[1]
    _, r, _ = mod.shape
    vec = pl.BlockSpec((1, d), lambda m: (0, 0))
    return pl.pallas_call(
        functools.partial(_outproj_kernel, alpha=alpha),
        grid=(rows // tm,),
        in_specs=[pl.BlockSpec((tm, dc), lambda m: (m, 0)),
                  pl.BlockSpec((tm, dm), lambda m: (m, 0)),
                  pl.BlockSpec((dc + dm, d), lambda m: (0, 0)),
                  pl.BlockSpec((tm, d), lambda m: (m, 0)),
                  pl.BlockSpec((1, r, d), lambda m: (m // tpb, 0, 2)),
                  vec, vec],
        out_specs=pl.BlockSpec((tm, d), lambda m: (m, 0)),
        out_shape=jax.ShapeDtypeStruct((rows, d), F32),
        compiler_params=_cparams(("arbitrary",)),
        name="outproj",
    )(yc, ym, w_out, x, mod, ln_g, ln_b)


def _ffn_kernel(*refs, tm, tpb, period, sample, alpha):
    if sample:
        (x_ref, sh_ref, sc_ref, g2_ref, wa_ref, wg_ref, wconv_ref, wd_ref, lg_ref, lb_ref,
         e1_ref, e2_ref, y_ref, at_ref, u_scr, acc_scr) = refs
    else:
        (x_ref, sh_ref, sc_ref, g2_ref, wa_ref, wg_ref, wconv_ref, wd_ref, lg_ref, lb_ref,
         y_ref, at_ref, u_scr, acc_scr, carry_scr) = refs
    m = pl.program_id(0)
    f = pl.program_id(1)
    nf = pl.num_programs(1)

    @pl.when(f == 0)
    def _():
        u = _ln(x_ref[...]) * (1.0 + sc_ref[0]) + sh_ref[0]
        u_scr[...] = u.astype(BF16)
        acc_scr[...] = jnp.zeros_like(acc_scr)

    u = u_scr[...]
    a = _dot(u, wa_ref[...])
    t = lax.broadcasted_iota(jnp.int32, a.shape, 0)
    if sample:
        t = lax.rem(t, period)
        e1 = e1_ref[...]
        e2 = e2_ref[...]
        at_ref[0] = a
    else:
        @pl.when(lax.rem(m, tpb) == 0)
        def _():
            carry_scr[f] = jnp.zeros(carry_scr.shape[1:], F32)

        prev = carry_scr[f]
        p0 = prev[STATE_ROWS - 2:STATE_ROWS - 1]
        p1 = prev[STATE_ROWS - 1:STATE_ROWS]
        e1 = p1
        e2 = jnp.where(t == 0, p0, p1)
        at_ref[0] = a[tm - STATE_ROWS:]
        carry_scr[f] = a[tm - STATE_ROWS:]
    ac = _causal_conv3(a, wconv_ref[...], t, e1, e2)
    hcur = (ac * _sigmoid(ac) * _dot(u, wg_ref[...])).astype(BF16)
    acc_scr[...] += _dot(hcur, wd_ref[...])

    @pl.when(f == nf - 1)
    def _():
        y_ref[...] = (_ln(alpha * x_ref[...] + (1.0 + g2_ref[0]) * acc_scr[...])
                      * lg_ref[...] + lb_ref[...])


def _ffn(x, mod, w_up, w_conv, w_down, ln_g, ln_b, *, tm, tpb, tf, sample, alpha, period=0,
         e1=None, e2=None):
    rows, d = x.shape
    ff = w_down.shape[0]
    nf = ff // tf
    nm = rows // tm
    nb, r, _ = mod.shape
    tail = tm if sample else STATE_ROWS
    vec = pl.BlockSpec((1, d), lambda m, f: (0, 0))
    in_specs = [
        pl.BlockSpec((tm, d), lambda m, f: (m, 0)),
        pl.BlockSpec((1, r, d), lambda m, f: (m // tpb, 0, 3)),
        pl.BlockSpec((1, r, d), lambda m, f: (m // tpb, 0, 4)),
        pl.BlockSpec((1, r, d), lambda m, f: (m // tpb, 0, 5)),
        pl.BlockSpec((d, tf), lambda m, f: (0, f)),
        pl.BlockSpec((d, tf), lambda m, f: (0, nf + f)),
        pl.BlockSpec((CONV_K, tf), lambda m, f: (0, f)),
        pl.BlockSpec((tf, d), lambda m, f: (f, 0)),
        vec, vec,
    ]
    args = [x, mod, mod, mod, w_up, w_up, w_conv, w_down, ln_g, ln_b]
    scratch = [pltpu.VMEM((tm, d), BF16), pltpu.VMEM((tm, d), F32)]
    if sample:
        in_specs += [pl.BlockSpec((tm, tf), lambda m, f: (m, f))] * 2
        args += [e1, e2]
    else:
        scratch.append(pltpu.VMEM((nf, STATE_ROWS, tf), F32))
    return pl.pallas_call(
        functools.partial(_ffn_kernel, tm=tm, tpb=tpb, period=period, sample=sample, alpha=alpha),
        grid=(nm, nf),
        in_specs=in_specs,
        out_specs=[pl.BlockSpec((tm, d), lambda m, f: (m, 0)),
                   pl.BlockSpec((1, tail, tf), lambda m, f: (m, 0, f))],
        out_shape=[jax.ShapeDtypeStruct((rows, d), F32),
                   jax.ShapeDtypeStruct((nm, tail, ff), F32)],
        scratch_shapes=scratch,
        compiler_params=_cparams(("arbitrary", "arbitrary")),
        name="ffn_sample" if sample else "ffn_prompt",
    )(*args)


def _conv_prev(state, T):
    b, _, c = state.shape
    e1 = jnp.pad(state[:, 1:2], ((0, 0), (0, T - 1), (0, 0))).reshape(b * T, c)
    e2 = jnp.pad(state, ((0, 0), (0, T - 2), (0, 0))).reshape(b * T, c)
    return e1, e2


def _tail2(full, B, T):
    return full.reshape(B, T, -1)[:, T - 2:]


def _layer_prompt(x, mod, wts, *, alpha):
    B, T, D = x.shape
    (w_in, w_gate, bg, w_conv, wmh, w_out, ln1_g, ln1_b, w_up, w_fconv, w_down, ln2_g, ln2_b) = wts
    tm = 1024
    tpb = T // tm
    x2 = x.reshape(B * T, D)
    yc, q, k, v, o, gates, ztail = _inproj(x2, mod, w_in, w_gate, w_conv, tm=tm, tpb=tpb, sample=False)
    dm = q.shape[1]
    sh = (B, T, dm)
    ym, C, n, m = _mlstm_prompt(q.reshape(sh), k.reshape(sh), v.reshape(sh), o.reshape(sh),
                                gates.reshape(B, T, GATE_LANES), bg, wmh, L=128)
    tm2 = 512
    x1 = _outproj(yc, ym.reshape(B * T, dm), w_out, x2, mod, ln1_g, ln1_b,
                  tm=tm2, tpb=T // tm2, alpha=alpha)
    y, atail = _ffn(x1, mod, w_up, w_fconv, w_down, ln2_g, ln2_b,
                    tm=tm2, tpb=T // tm2, tf=512, sample=False, alpha=alpha)
    return (y.reshape(B, T, D), ztail[tpb - 1::tpb, STATE_ROWS - 2:], C, n, m[..., 0],
            atail[T // tm2 - 1::T // tm2, STATE_ROWS - 2:])


def _layer_sample(x, mod, conv_buf, C0, n0, m0, ffn_buf, wts, *, alpha):
    B, T, D = x.shape
    (w_in, w_gate, bg, w_conv, wmh, w_out, ln1_g, ln1_b, w_up, w_fconv, w_down, ln2_g, ln2_b) = wts
    rows = B * T
    x2 = x.reshape(rows, D)
    e1, e2 = _conv_prev(conv_buf, T)
    yc, q, k, v, o, gates, zfull = _inproj(x2, mod, w_in, w_gate, w_conv, tm=rows, tpb=1,
                                           sample=True, period=T, e1=e1, e2=e2)
    mrow = jnp.pad(jnp.repeat(m0, T, axis=0), ((0, 0), (N_HEADS, GATE_LANES - 2 * N_HEADS)))
    nrow = jnp.repeat(n0.reshape(B, N_HEADS * HEAD_DIM), T, axis=0)
    ym, C, n, m = _mlstm_sample(q, k, v, o, gates, bg, mrow, nrow, C0, n0, wmh, bb=8, T=T)
    x1 = _outproj(yc, ym, w_out, x2, mod, ln1_g, ln1_b, tm=rows, tpb=1, alpha=alpha)
    f1, f2 = _conv_prev(ffn_buf, T)
    y, afull = _ffn(x1, mod, w_up, w_fconv, w_down, ln2_g, ln2_b, tm=rows, tpb=1, tf=512,
                    sample=True, alpha=alpha, period=T, e1=f1, e2=f2)
    return (y.reshape(B, T, D), _tail2(zfull, B, T), C, n, m[..., 0], _tail2(afull, B, T))


def kernel(x_prompt, x_sample, c_prompt, c_sample, state_conv, state_mlstm_C, state_mlstm_n,
           state_mlstm_m, state_ffn_conv, w_ada, b_ada, w_in, b_gate, w_conv, w_mh_norm, w_out,
           ln1_g, ln1_b, w_up, w_ffn_conv, w_down, ln2_g, ln2_b):
    depth = w_in.shape[0]
    alpha = (2 * depth) ** 0.25
    Bp = x_prompt.shape[0]
    Bs, Ts, D = x_sample.shape
    dc = w_conv.shape[-1]
    dm = w_mh_norm.shape[-1]
    n_main = 3 * dc + 4 * dm
    assert dc == dm == N_HEADS * HEAD_DIM and Ts >= CONV_K - 1 and SLAB % Ts == 0

    xp, xs = x_prompt, x_sample
    outs_p = [[] for _ in range(5)]
    outs_s = [[] for _ in range(5)]
    for l in range(depth):
        mod = _ada(jnp.concatenate([c_prompt, c_sample], axis=0), w_ada[l], b_ada[l])
        mod_p = mod[:Bp].reshape(Bp, 1, 6 * D)
        mod_s = jnp.repeat(mod[Bp:], Ts, axis=0).reshape(1, Bs * Ts, 6 * D)
        w_in_l = w_in[l]
        wts = (
            w_in_l[:, :n_main].astype(BF16),
            jnp.pad(w_in_l[:, n_main:], ((0, 0), (0, GATE_LANES - 2 * N_HEADS))).astype(BF16),
            jnp.pad(b_gate[l], (0, GATE_LANES - 2 * N_HEADS)).reshape(1, GATE_LANES),
            w_conv[l],
            w_mh_norm[l].reshape(1, dm),
            w_out[l].astype(BF16),
            ln1_g[l].reshape(1, D), ln1_b[l].reshape(1, D),
            w_up[l].astype(BF16),
            w_ffn_conv[l],
            w_down[l].astype(BF16),
            ln2_g[l].reshape(1, D), ln2_b[l].reshape(1, D),
        )
        xp, *st_p = _layer_prompt(xp, mod_p, wts, alpha=alpha)
        xs, *st_s = _layer_sample(xs, mod_s, state_conv[l], state_mlstm_C[l], state_mlstm_n[l],
                                  state_mlstm_m[l], state_ffn_conv[l], wts, alpha=alpha)
        for acc, val in zip(outs_p, st_p):
            acc.append(val)
        for acc, val in zip(outs_s, st_s):
            acc.append(val)
    return (xp.astype(x_prompt.dtype), xs.astype(x_sample.dtype),
            *[jnp.stack(a) for a in outs_p], *[jnp.stack(a) for a in outs_s])
```

```python
import functools

import jax
import jax.numpy as jnp
from jax import lax
from jax.experimental import pallas as pl
from jax.experimental.pallas import tpu as pltpu

F32 = jnp.float32
BF16 = jnp.bfloat16

N_HEADS = 4
HEAD_DIM = 256
CONV_K = 3
LN_EPS = 1e-5
NEG = -1e30
GATE_LANES = 128
STATE_ROWS = 8
VMEM_LIMIT = 56 * 1024 * 1024


def _cparams(sem):
    return pltpu.CompilerParams(dimension_semantics=sem, vmem_limit_bytes=VMEM_LIMIT)


def _ln(x):
    mu = jnp.mean(x, axis=-1, keepdims=True)
    xc = x - mu
    var = jnp.mean(xc * xc, axis=-1, keepdims=True)
    return xc * lax.rsqrt(var + LN_EPS)


def _log_sigmoid(x):
    return jnp.minimum(x, 0.0) - jnp.log1p(jnp.exp(-jnp.abs(x)))


def _sigmoid(x):
    return 1.0 / (1.0 + jnp.exp(-x))


def _dot(a, b):
    return jnp.dot(a, b, preferred_element_type=F32)


def _causal_conv3(z, w, t, e1, e2):
    z1 = jnp.where(t >= 1, pltpu.roll(z, 1, 0), e1)
    z2 = jnp.where(t >= 2, pltpu.roll(z, 2, 0), e2)
    return w[0:1] * z2 + w[1:2] * z1 + w[2:3] * z


def _ada_kernel(c_ref, w_ref, b_ref, o_ref):
    c = c_ref[...]
    s = (c * _sigmoid(c)).astype(BF16)
    o_ref[...] = _dot(s, w_ref[...].astype(BF16)) + b_ref[...]


def _ada(c, w, b, tn=1024):
    r, d = c.shape
    n = w.shape[1]
    return pl.pallas_call(
        _ada_kernel,
        grid=(n // tn,),
        in_specs=[
            pl.BlockSpec((r, d), lambda j: (0, 0)),
            pl.BlockSpec((d, tn), lambda j: (0, j)),
            pl.BlockSpec((1, tn), lambda j: (0, j)),
        ],
        out_specs=pl.BlockSpec((r, tn), lambda j: (0, j)),
        out_shape=jax.ShapeDtypeStruct((r, n), F32),
        compiler_params=_cparams(("arbitrary",)),
        name="ada",
    )(c, w, b.reshape(1, n))


def _inproj_kernel(*refs, tm, tpb, period, sample):
    if sample:
        (x_ref, sh_ref, sc_ref, wb_ref, wc_ref, wh_ref, wq_ref, wk_ref, wv_ref, wo_ref, wg_ref,
         wconv_ref, e1_ref, e2_ref,
         yc_ref, q_ref, k_ref, v_ref, o_ref, g_ref, zt_ref, u_scr) = refs
    else:
        (x_ref, sh_ref, sc_ref, wb_ref, wc_ref, wh_ref, wq_ref, wk_ref, wv_ref, wo_ref, wg_ref,
         wconv_ref,
         yc_ref, q_ref, k_ref, v_ref, o_ref, g_ref, zt_ref, u_scr, carry_scr) = refs
    m = pl.program_id(0)
    j = pl.program_id(1)

    @pl.when(j == 0)
    def _():
        u = _ln(x_ref[...]) * (1.0 + sc_ref[0]) + sh_ref[0]
        ub = u.astype(BF16)
        u_scr[...] = ub
        g_ref[...] = _dot(ub, wg_ref[...])

    u = u_scr[...]
    z = _dot(u, wc_ref[...]) * _dot(u, wh_ref[...])
    t = lax.broadcasted_iota(jnp.int32, z.shape, 0)
    if sample:
        t = lax.rem(t, period)
        e1 = e1_ref[...]
        e2 = e2_ref[...]
    else:
        @pl.when(lax.rem(m, tpb) == 0)
        def _():
            carry_scr[j] = jnp.zeros(carry_scr.shape[1:], F32)

        prev = carry_scr[j]
        p0 = prev[STATE_ROWS - 2:STATE_ROWS - 1]
        p1 = prev[STATE_ROWS - 1:STATE_ROWS]
        e1 = p1
        e2 = jnp.where(t == 0, p0, p1)
    yc = _causal_conv3(z, wconv_ref[...], t, e1, e2)
    yc_ref[...] = (_dot(u, wb_ref[...]) * yc).astype(BF16)
    if sample:
        zt_ref[0] = z
    else:
        zt_ref[0] = z[tm - STATE_ROWS:]
        carry_scr[j] = z[tm - STATE_ROWS:]
    q_ref[...] = _dot(u, wq_ref[...]).astype(BF16)
    k_ref[...] = (_dot(u, wk_ref[...]) * (HEAD_DIM ** -0.5)).astype(BF16)
    v_ref[...] = _dot(u, wv_ref[...]).astype(BF16)
    o_ref[...] = _dot(u, wo_ref[...]).astype(BF16)


def _inproj(x, mod, w_in, w_gate, w_conv, *, tm, tpb, sample, period=0, e1=None, e2=None):
    rows, d = x.shape
    dc = w_conv.shape[1]
    tn = HEAD_DIM
    nj = dc // tn
    nm = rows // tm
    r = tm if sample else 1
    tail = tm if sample else STATE_ROWS

    def wspec(off):
        return pl.BlockSpec((d, tn), lambda m, j, off=off: (0, off * nj + j))

    in_specs = [
        pl.BlockSpec((tm, d), lambda m, j: (m, 0)),
        pl.BlockSpec((1, r, d), lambda m, j: (m // tpb, 0, 0)),
        pl.BlockSpec((1, r, d), lambda m, j: (m // tpb, 0, 1)),
        wspec(0), wspec(1), wspec(2), wspec(3), wspec(4), wspec(5), wspec(6),
        pl.BlockSpec((d, GATE_LANES), lambda m, j: (0, 0)),
        pl.BlockSpec((CONV_K, tn), lambda m, j: (0, j)),
    ]
    args = [x, mod, mod, w_in, w_in, w_in, w_in, w_in, w_in, w_in, w_gate, w_conv]
    scratch = [pltpu.VMEM((tm, d), BF16)]
    if sample:
        in_specs += [pl.BlockSpec((tm, tn), lambda m, j: (m, j))] * 2
        args += [e1, e2]
    else:
        scratch.append(pltpu.VMEM((nj, STATE_ROWS, tn), F32))
    act = pl.BlockSpec((tm, tn), lambda m, j: (m, j))
    out_specs = [act, act, act, act, act,
                 pl.BlockSpec((tm, GATE_LANES), lambda m, j: (m, 0)),
                 pl.BlockSpec((1, tail, tn), lambda m, j: (m, 0, j))]
    out_shape = [jax.ShapeDtypeStruct((rows, dc), BF16)] * 5 + [
        jax.ShapeDtypeStruct((rows, GATE_LANES), F32),
        jax.ShapeDtypeStruct((nm, tail, dc), F32)]
    return pl.pallas_call(
        functools.partial(_inproj_kernel, tm=tm, tpb=tpb, period=period, sample=sample),
        grid=(nm, nj),
        in_specs=in_specs,
        out_specs=out_specs,
        out_shape=out_shape,
        scratch_shapes=scratch,
        compiler_params=_cparams(("arbitrary", "arbitrary")),
        name="inproj_sample" if sample else "inproj_prompt",
    )(*args)


def _split3(x):
    hi = x.astype(BF16)
    r1 = x - hi.astype(F32)
    mid = r1.astype(BF16)
    lo = (r1 - mid.astype(F32)).astype(BF16)
    return hi, mid, lo


def _head_out(hh, o, wmh):
    return (_sigmoid(o.astype(F32)) * (_ln(hh) * wmh)).astype(BF16)


def _mlstm_chunk_kernel(q_ref, k_ref, v_ref, o_ref, g_ref, bg_ref, wmh_ref,
                        y_ref, c_ref, n_ref, m_ref, *, L):
    c = pl.program_id(1)

    @pl.when(c == 0)
    def _():
        c_ref[...] = jnp.zeros_like(c_ref)
        n_ref[...] = jnp.zeros_like(n_ref)
        m_ref[...] = jnp.zeros_like(m_ref)

    row = lax.broadcasted_iota(jnp.int32, (L, L), 0)
    col = lax.broadcasted_iota(jnp.int32, (L, L), 1)
    causal = col <= row
    tril = jnp.where(causal, 1.0, 0.0).astype(BF16)

    g = g_ref[...] + bg_ref[...]
    lf = _log_sigmoid(g)
    hi, mid, lo = _split3(lf)
    bcum = _dot(tril, hi) + _dot(tril, mid) + _dot(tril, lo)
    bcum_t = bcum.T
    g_t = g.T

    for h in range(N_HEADS):
        hs = slice(h * HEAD_DIM, (h + 1) * HEAD_DIM)
        bc = bcum[:, N_HEADS + h:N_HEADS + h + 1]
        br = bcum_t[N_HEADS + h:N_HEADS + h + 1, :]
        li_r = g_t[h:h + 1, :]
        li_c = g[:, h:h + 1]
        m_prev = m_ref[0, h:h + 1, 0:1]
        a = bc + m_prev
        dlog = jnp.where(causal, bc - br + li_r, NEG)
        mt = jnp.maximum(a, jnp.max(dlog, axis=1, keepdims=True))
        dw = jnp.exp(dlog - mt)
        inter = jnp.exp(a - mt)
        qh = q_ref[:, hs]
        kh = k_ref[:, hs]
        vh = v_ref[:, hs]
        s = lax.dot_general(qh, kh, (((1,), (1,)), ((), ())), preferred_element_type=F32) * dw
        c_old = c_ref[0, h]
        n_old = n_ref[0, h:h + 1, :]
        num = _dot(s.astype(BF16), vh) + inter * _dot(qh, c_old.astype(BF16))
        qn = jnp.sum(qh.astype(F32) * n_old, axis=1, keepdims=True)
        den = jnp.sum(s, axis=1, keepdims=True) + inter * qn
        hh = num / jnp.maximum(jnp.abs(den), jnp.exp(-mt))
        y_ref[:, hs] = _head_out(hh, o_ref[:, hs], wmh_ref[:, hs])

        m_new = mt[L - 1:L, :]
        b_last = bc[L - 1:L, :]
        wc = jnp.exp(b_last - bc + li_c - m_new)
        dc = jnp.exp(b_last + m_prev - m_new)
        vw = (vh.astype(F32) * wc).astype(BF16)
        c_ref[0, h] = dc * c_old + lax.dot_general(
            kh, vw, (((0,), (0,)), ((), ())), preferred_element_type=F32)
        n_ref[0, h:h + 1, :] = dc * n_old + jnp.sum(kh.astype(F32) * wc, axis=0, keepdims=True)
        m_ref[0, h:h + 1, :] = jnp.broadcast_to(m_new, (1, GATE_LANES))


def _mlstm_prompt(q, k, v, o, gates, bg, wmh, *, b, L):
    rows, dm = q.shape
    nc = rows // b // L
    act = pl.BlockSpec((L, dm), lambda i, c: (i * nc + c, 0))
    return pl.pallas_call(
        functools.partial(_mlstm_chunk_kernel, L=L),
        grid=(b, nc),
        in_specs=[act, act, act, act,
                  pl.BlockSpec((L, GATE_LANES), lambda i, c: (i * nc + c, 0)),
                  pl.BlockSpec((1, GATE_LANES), lambda i, c: (0, 0)),
                  pl.BlockSpec((1, dm), lambda i, c: (0, 0))],
        out_specs=[act,
                   pl.BlockSpec((1, N_HEADS, HEAD_DIM, HEAD_DIM), lambda i, c: (i, 0, 0, 0)),
                   pl.BlockSpec((1, N_HEADS, HEAD_DIM), lambda i, c: (i, 0, 0)),
                   pl.BlockSpec((1, N_HEADS, GATE_LANES), lambda i, c: (i, 0, 0))],
        out_shape=[jax.ShapeDtypeStruct((rows, dm), BF16),
                   jax.ShapeDtypeStruct((b, N_HEADS, HEAD_DIM, HEAD_DIM), F32),
                   jax.ShapeDtypeStruct((b, N_HEADS, HEAD_DIM), F32),
                   jax.ShapeDtypeStruct((b, N_HEADS, GATE_LANES), F32)],
        compiler_params=_cparams(("arbitrary", "arbitrary")),
        name="mlstm_prompt",
    )(q, k, v, o, gates, bg, wmh)


SLAB = 16


def _mlstm_step_kernel(q_ref, k_ref, v_ref, o_ref, g_ref, bg_ref, mrow_ref, nrow_ref,
                       c0_ref, n0_ref, wmh_ref,
                       y_ref, c_ref, n_ref, m_ref, *, bb, T):
    R = bb * T
    per_slab = SLAB // T
    t = lax.rem(lax.broadcasted_iota(jnp.int32, (R, GATE_LANES), 0), T)

    def down(x, d):
        return pltpu.roll(x, d, 0)

    def up(x, d):
        return pltpu.roll(x, x.shape[0] - d, 0)

    def seg_last(x):
        out = x
        for d in range(1, T):
            out = jnp.where(t == T - 1 - d, up(x, d), out)
        return out

    g = g_ref[...] + bg_ref[...]
    li = pltpu.roll(g, N_HEADS, 1)
    lf = _log_sigmoid(g)
    b = lf
    for d in range(1, T):
        b = b + jnp.where(t >= d, down(lf, d), 0.0)
    m_prev = mrow_ref[...]
    a = b + m_prev
    dl = [li] + [jnp.where(t >= d, b - down(b, d) + down(li, d), NEG) for d in range(1, T)]
    mt = a
    for d in range(T):
        mt = jnp.maximum(mt, dl[d])
    dw = [jnp.exp(dl[d] - mt) for d in range(T)]
    inter = jnp.exp(a - mt)
    emt = jnp.exp(-mt)
    m_new = seg_last(mt)
    b_last = seg_last(b)
    wc = jnp.exp(b_last - b + li - m_new)
    dc = jnp.exp(b_last + m_prev - m_new)

    row_s = lax.broadcasted_iota(jnp.int32, (SLAB, HEAD_DIM), 0)
    row_r = lax.broadcasted_iota(jnp.int32, (R, HEAD_DIM), 0)

    for h in range(N_HEADS):
        hs = slice(h * HEAD_DIM, (h + 1) * HEAD_DIM)
        ln = N_HEADS + h

        def col(x):
            return x[:, ln:ln + 1]

        qb = q_ref[:, hs]
        kb = k_ref[:, hs]
        vb = v_ref[:, hs]
        qf = qb.astype(F32)
        kf = kb.astype(F32)
        vf = vb.astype(F32)
        num = jnp.zeros((R, HEAD_DIM), F32)
        den = jnp.zeros((R, 1), F32)
        for d in range(T):
            kd = kf if d == 0 else down(kf, d)
            vd = vf if d == 0 else down(vf, d)
            sw = jnp.sum(qf * kd, axis=1, keepdims=True) * col(dw[d])
            num = num + sw * vd
            den = den + sw

        qc_slabs = []
        for si in range(R // SLAB):
            q16 = qb[si * SLAB:(si + 1) * SLAB]
            acc = jnp.zeros((SLAB, HEAD_DIM), F32)
            for bl in range(per_slab):
                bi = si * per_slab + bl
                r = _dot(q16, c0_ref[bi, h].astype(BF16))
                acc = jnp.where(row_s // T == bl, r, acc)
            qc_slabs.append(acc)
        qc = jnp.concatenate(qc_slabs, axis=0)
        qn = jnp.sum(qf * nrow_ref[:, hs], axis=1, keepdims=True)
        num = num + col(inter) * qc
        den = den + col(inter) * qn
        hh = num / jnp.maximum(jnp.abs(den), col(emt))
        y_ref[:, hs] = _head_out(hh, o_ref[:, hs], wmh_ref[:, hs])

        wk = kf * col(wc)
        vw = vf * col(wc)
        for si in range(R // SLAB):
            k16 = kb[si * SLAB:(si + 1) * SLAB]
            vw16 = vw[si * SLAB:(si + 1) * SLAB]
            for bl in range(per_slab):
                bi = si * per_slab + bl
                last = bi * T + T - 1
                vwb = jnp.where(row_s // T == bl, vw16, 0.0).astype(BF16)
                dcb = dc[last:last + 1, ln:ln + 1]
                c_ref[bi, h] = dcb * c0_ref[bi, h] + lax.dot_general(
                    k16, vwb, (((0,), (0,)), ((), ())), preferred_element_type=F32)
                n_ref[bi, h:h + 1, :] = dcb * n0_ref[bi, h:h + 1, :] + jnp.sum(
                    jnp.where(row_r // T == bi, wk, 0.0), axis=0, keepdims=True)
                m_ref[bi, h:h + 1, :] = jnp.broadcast_to(
                    m_new[last:last + 1, ln:ln + 1], (1, GATE_LANES))


def _mlstm_sample(q, k, v, o, gates, bg, mrow, nrow, c0, n0, wmh, *, bb, T):
    rows, dm = q.shape
    nb = c0.shape[0]
    R = bb * T
    act = pl.BlockSpec((R, dm), lambda i: (i, 0))
    gat = pl.BlockSpec((R, GATE_LANES), lambda i: (i, 0))
    cspec = pl.BlockSpec((bb, N_HEADS, HEAD_DIM, HEAD_DIM), lambda i: (i, 0, 0, 0))
    nspec = pl.BlockSpec((bb, N_HEADS, HEAD_DIM), lambda i: (i, 0, 0))
    return pl.pallas_call(
        functools.partial(_mlstm_step_kernel, bb=bb, T=T),
        grid=(nb // bb,),
        in_specs=[act, act, act, act, gat,
                  pl.BlockSpec((1, GATE_LANES), lambda i: (0, 0)),
                  gat,
                  pl.BlockSpec((R, dm), lambda i: (i, 0)),
                  cspec, nspec,
                  pl.BlockSpec((1, dm), lambda i: (0, 0))],
        out_specs=[act, cspec, nspec,
                   pl.BlockSpec((bb, N_HEADS, GATE_LANES), lambda i: (i, 0, 0))],
        out_shape=[jax.ShapeDtypeStruct((rows, dm), BF16),
                   jax.ShapeDtypeStruct(c0.shape, F32),
                   jax.ShapeDtypeStruct(n0.shape, F32),
                   jax.ShapeDtypeStruct((nb, N_HEADS, GATE_LANES), F32)],
        compiler_params=_cparams(("arbitrary",)),
        name="mlstm_sample",
    )(q, k, v, o, gates, bg, mrow, nrow, c0, n0, wmh)


def _outproj_kernel(yc_ref, ym_ref, w_ref, x_ref, g1_ref, lg_ref, lb_ref, o_ref, *, alpha):
    dc = yc_ref.shape[1]
    mix = _dot(yc_ref[...], w_ref[0:dc, :]) + _dot(ym_ref[...], w_ref[dc:, :])
    o_ref[...] = _ln(alpha * x_ref[...] + (1.0 + g1_ref[0]) * mix) * lg_ref[...] + lb_ref[...]


def _outproj(yc, ym, w_out, x, mod, ln_g, ln_b, *, tm, tpb, alpha):
    rows, d = x.shape
    dc = yc.shape[1]
    dm = ym.shape[1]
    r = 1 if mod.shape[1] == 1 else tm
    vec = pl.BlockSpec((1, d), lambda m: (0, 0))
    return pl.pallas_call(
        functools.partial(_outproj_kernel, alpha=alpha),
        grid=(rows // tm,),
        in_specs=[pl.BlockSpec((tm, dc), lambda m: (m, 0)),
                  pl.BlockSpec((tm, dm), lambda m: (m, 0)),
                  pl.BlockSpec((dc + dm, d), lambda m: (0, 0)),
                  pl.BlockSpec((tm, d), lambda m: (m, 0)),
                  pl.BlockSpec((1, r, d), lambda m: (m // tpb, 0, 2)),
                  vec, vec],
        out_specs=pl.BlockSpec((tm, d), lambda m: (m, 0)),
        out_shape=jax.ShapeDtypeStruct((rows, d), F32),
        compiler_params=_cparams(("arbitrary",)),
        name="outproj",
    )(yc, ym, w_out, x, mod, ln_g, ln_b)


def _ffn_kernel(*refs, tm, tpb, period, sample, alpha):
    if sample:
        (x_ref, sh_ref, sc_ref, g2_ref, wa_ref, wg_ref, wconv_ref, wd_ref, lg_ref, lb_ref,
         e1_ref, e2_ref, y_ref, at_ref, u_scr) = refs
    else:
        (x_ref, sh_ref, sc_ref, g2_ref, wa_ref, wg_ref, wconv_ref, wd_ref, lg_ref, lb_ref,
         y_ref, at_ref, u_scr, carry_scr) = refs
    m = pl.program_id(0)
    f = pl.program_id(1)
    nf = pl.num_programs(1)

    @pl.when(f == 0)
    def _():
        u = _ln(x_ref[...]) * (1.0 + sc_ref[0]) + sh_ref[0]
        u_scr[...] = u.astype(BF16)
        y_ref[...] = jnp.zeros_like(y_ref)

    u = u_scr[...]
    a = _dot(u, wa_ref[...])
    t = lax.broadcasted_iota(jnp.int32, a.shape, 0)
    if sample:
        t = lax.rem(t, period)
        e1 = e1_ref[...]
        e2 = e2_ref[...]
        at_ref[0] = a
    else:
        @pl.when(lax.rem(m, tpb) == 0)
        def _():
            carry_scr[f] = jnp.zeros(carry_scr.shape[1:], F32)

        prev = carry_scr[f]
        p0 = prev[STATE_ROWS - 2:STATE_ROWS - 1]
        p1 = prev[STATE_ROWS - 1:STATE_ROWS]
        e1 = p1
        e2 = jnp.where(t == 0, p0, p1)
        at_ref[0] = a[tm - STATE_ROWS:]
        carry_scr[f] = a[tm - STATE_ROWS:]
    ac = _causal_conv3(a, wconv_ref[...], t, e1, e2)
    hcur = (ac * _sigmoid(ac) * _dot(u, wg_ref[...])).astype(BF16)
    y_ref[...] += _dot(hcur, wd_ref[...])

    @pl.when(f == nf - 1)
    def _():
        y_ref[...] = (_ln(alpha * x_ref[...] + (1.0 + g2_ref[0]) * y_ref[...])
                      * lg_ref[...] + lb_ref[...])


def _ffn(x, mod, w_up, w_conv, w_down, ln_g, ln_b, *, tm, tpb, tf, sample, alpha, period=0,
         e1=None, e2=None):
    rows, d = x.shape
    ff = w_down.shape[0]
    nf = ff // tf
    nm = rows // tm
    r = tm if sample else 1
    tail = tm if sample else STATE_ROWS
    vec = pl.BlockSpec((1, d), lambda m, f: (0, 0))
    in_specs = [
        pl.BlockSpec((tm, d), lambda m, f: (m, 0), pipeline_mode=pl.Buffered(1)),
        pl.BlockSpec((1, r, d), lambda m, f: (m // tpb, 0, 3)),
        pl.BlockSpec((1, r, d), lambda m, f: (m // tpb, 0, 4)),
        pl.BlockSpec((1, r, d), lambda m, f: (m // tpb, 0, 5)),
        pl.BlockSpec((d, tf), lambda m, f: (0, f)),
        pl.BlockSpec((d, tf), lambda m, f: (0, nf + f)),
        pl.BlockSpec((CONV_K, tf), lambda m, f: (0, f)),
        pl.BlockSpec((tf, d), lambda m, f: (f, 0)),
        vec, vec,
    ]
    args = [x, mod, mod, mod, w_up, w_up, w_conv, w_down, ln_g, ln_b]
    scratch = [pltpu.VMEM((tm, d), BF16)]
    if sample:
        in_specs += [pl.BlockSpec((tm, tf), lambda m, f: (m, f))] * 2
        args += [e1, e2]
    else:
        scratch.append(pltpu.VMEM((nf, STATE_ROWS, tf), F32))
    return pl.pallas_call(
        functools.partial(_ffn_kernel, tm=tm, tpb=tpb, period=period, sample=sample, alpha=alpha),
        grid=(nm, nf),
        in_specs=in_specs,
        out_specs=[pl.BlockSpec((tm, d), lambda m, f: (m, 0)),
                   pl.BlockSpec((1, tail, tf), lambda m, f: (m, 0, f))],
        out_shape=[jax.ShapeDtypeStruct((rows, d), F32),
                   jax.ShapeDtypeStruct((nm, tail, ff), F32)],
        scratch_shapes=scratch,
        compiler_params=_cparams(("arbitrary", "arbitrary")),
        name="ffn_sample" if sample else "ffn_prompt",
    )(*args)


def _conv_prev(state, T):
    b, _, c = state.shape
    e1 = jnp.pad(state[:, 1:2], ((0, 0), (0, T - 1), (0, 0))).reshape(b * T, c)
    e2 = jnp.pad(state, ((0, 0), (0, T - 2), (0, 0))).reshape(b * T, c)
    return e1, e2


def _tail2(full, B, T):
    return full.reshape(B, T, -1)[:, T - 2:]


def _layer_prompt(x, mod, wts, *, alpha):
    B, T, D = x.shape
    (w_in, w_gate, bg, w_conv, wmh, w_out, ln1_g, ln1_b, w_up, w_fconv, w_down, ln2_g, ln2_b) = wts
    tm = 1024
    tpb = T // tm
    x2 = x.reshape(B * T, D)
    yc, q, k, v, o, gates, ztail = _inproj(x2, mod, w_in, w_gate, w_conv, tm=tm, tpb=tpb, sample=False)
    ym, C, n, m = _mlstm_prompt(q, k, v, o, gates, bg, wmh, b=B, L=128)
    tm2 = 512
    x1 = _outproj(yc, ym, w_out, x2, mod, ln1_g, ln1_b, tm=tm2, tpb=T // tm2, alpha=alpha)
    y, atail = _ffn(x1, mod, w_up, w_fconv, w_down, ln2_g, ln2_b,
                    tm=tm, tpb=tpb, tf=512, sample=False, alpha=alpha)
    return (y.reshape(B, T, D), ztail[tpb - 1::tpb, STATE_ROWS - 2:], C, n, m[..., 0],
            atail[tpb - 1::tpb, STATE_ROWS - 2:])


def _layer_sample(x, mod, conv_buf, C0, n0, m0, ffn_buf, wts, *, alpha):
    B, T, D = x.shape
    (w_in, w_gate, bg, w_conv, wmh, w_out, ln1_g, ln1_b, w_up, w_fconv, w_down, ln2_g, ln2_b) = wts
    rows = B * T
    x2 = x.reshape(rows, D)
    e1, e2 = _conv_prev(conv_buf, T)
    yc, q, k, v, o, gates, zfull = _inproj(x2, mod, w_in, w_gate, w_conv, tm=rows, tpb=1,
                                           sample=True, period=T, e1=e1, e2=e2)
    mrow = jnp.pad(jnp.repeat(m0, T, axis=0), ((0, 0), (N_HEADS, GATE_LANES - 2 * N_HEADS)))
    nrow = jnp.repeat(n0.reshape(B, N_HEADS * HEAD_DIM), T, axis=0)
    ym, C, n, m = _mlstm_sample(q, k, v, o, gates, bg, mrow, nrow, C0, n0, wmh, bb=8, T=T)
    x1 = _outproj(yc, ym, w_out, x2, mod, ln1_g, ln1_b, tm=rows, tpb=1, alpha=alpha)
    f1, f2 = _conv_prev(ffn_buf, T)
    y, afull = _ffn(x1, mod, w_up, w_fconv, w_down, ln2_g, ln2_b, tm=rows, tpb=1, tf=512,
                    sample=True, alpha=alpha, period=T, e1=f1, e2=f2)
    return (y.reshape(B, T, D), _tail2(zfull, B, T), C, n, m[..., 0], _tail2(afull, B, T))


def kernel(x_prompt, x_sample, c_prompt, c_sample, state_conv, state_mlstm_C, state_mlstm_n,
           state_mlstm_m, state_ffn_conv, w_ada, b_ada, w_in, b_gate, w_conv, w_mh_norm, w_out,
           ln1_g, ln1_b, w_up, w_ffn_conv, w_down, ln2_g, ln2_b):
    depth = w_in.shape[0]
    alpha = (2 * depth) ** 0.25
    Bp = x_prompt.shape[0]
    Bs, Ts, D = x_sample.shape
    dc = w_conv.shape[-1]
    dm = w_mh_norm.shape[-1]
    n_main = 3 * dc + 4 * dm
    assert dc == dm == N_HEADS * HEAD_DIM and Ts >= CONV_K - 1 and SLAB % Ts == 0

    xp, xs = x_prompt, x_sample
    outs_p = [[] for _ in range(5)]
    outs_s = [[] for _ in range(5)]
    for l in range(depth):
        c_all = jnp.concatenate([jnp.repeat(c_sample, Ts, axis=0), c_prompt], axis=0)
        mod = _ada(c_all, w_ada[l], b_ada[l])
        mod_s = mod.reshape(1, Bs * Ts + Bp, 6 * D)
        mod_p = mod[Bs * Ts:].reshape(Bp, 1, 6 * D)
        w_in_l = w_in[l]
        wts = (
            w_in_l.astype(BF16),
            jnp.pad(w_in_l[:, n_main:], ((0, 0), (0, GATE_LANES - 2 * N_HEADS))).astype(BF16),
            jnp.pad(b_gate[l], (0, GATE_LANES - 2 * N_HEADS)).reshape(1, GATE_LANES),
            w_conv[l],
            w_mh_norm[l].reshape(1, dm),
            w_out[l].astype(BF16),
            ln1_g[l].reshape(1, D), ln1_b[l].reshape(1, D),
            w_up[l].astype(BF16),
            w_ffn_conv[l],
            w_down[l].astype(BF16),
            ln2_g[l].reshape(1, D), ln2_b[l].reshape(1, D),
        )
        xp, *st_p = _layer_prompt(xp, mod_p, wts, alpha=alpha)
        xs, *st_s = _layer_sample(xs, mod_s, state_conv[l], state_mlstm_C[l], state_mlstm_n[l],
                                  state_mlstm_m[l], state_ffn_conv[l], wts, alpha=alpha)
        for acc, val in zip(outs_p, st_p):
            acc.append(val)
        for acc, val in zip(outs_s, st_s):
            acc.append(val)
    return (xp.astype(x_prompt.dtype), xs.astype(x_sample.dtype),
            *[jnp.stack(a) for a in outs_p], *[jnp.stack(a) for a in outs_s])
```

```python
import functools

import jax
import jax.numpy as jnp
from jax import lax
from jax.experimental import pallas as pl
from jax.experimental.pallas import tpu as pltpu

F32 = jnp.float32
BF16 = jnp.bfloat16

N_HEADS = 4
HEAD_DIM = 256
CONV_K = 3
LN_EPS = 1e-5
NEG = -1e30
LANES = 128
GATE_LANES = LANES
STATE_ROWS = 8
VMEM_LIMIT = 56 * 1024 * 1024


def _cparams(sem):
    return pltpu.CompilerParams(dimension_semantics=sem, vmem_limit_bytes=VMEM_LIMIT)


def _ln(x):
    mu = jnp.mean(x, axis=-1, keepdims=True)
    xc = x - mu
    var = jnp.mean(xc * xc, axis=-1, keepdims=True)
    return xc * lax.rsqrt(var + LN_EPS)


def _log_sigmoid(x):
    return jnp.minimum(x, 0.0) - jnp.log1p(jnp.exp(-jnp.abs(x)))


def _sigmoid(x):
    return 1.0 / (1.0 + jnp.exp(-x))


def _dot(a, b):
    return jnp.dot(a, b, preferred_element_type=F32)


def _conv3_rows(z, w, carry_ref, idx, first):
    @pl.when(first)
    def _():
        carry_ref[idx] = jnp.zeros(carry_ref.shape[1:], F32)

    prev = carry_ref[idx]
    p0 = prev[STATE_ROWS - 2:STATE_ROWS - 1]
    p1 = prev[STATE_ROWS - 1:STATE_ROWS]
    t = lax.broadcasted_iota(jnp.int32, z.shape, 0)
    z1 = jnp.where(t >= 1, pltpu.roll(z, 1, 0), p1)
    z2 = jnp.where(t >= 2, pltpu.roll(z, 2, 0), jnp.where(t == 0, p0, p1))
    tail = z[z.shape[0] - STATE_ROWS:]
    carry_ref[idx] = tail
    return w[0:1] * z2 + w[1:2] * z1 + w[2:3] * z, tail


def _conv3_sequences(z, w, s0_ref, s1_ref, z_scr, y_scr, t0_ref, t1_ref, T):
    nseq = z.shape[0] // T
    y = w[0:1] * pltpu.roll(z, 2, 0) + w[1:2] * pltpu.roll(z, 1, 0) + w[2:3] * z

    def rows(t):
        return pl.ds(t, nseq, stride=T)

    cols = []
    for c in range(z.shape[1] // LANES):
        cs = slice(c * LANES, (c + 1) * LANES)
        w0, w1, w2 = w[0:1, cs], w[1:2, cs], w[2:3, cs]
        z_scr[c] = z[:, cs]
        y_scr[c] = y[:, cs]
        s0 = s0_ref[:, cs]
        s1 = s1_ref[:, cs]
        z0 = z_scr[c, rows(0), :]
        z1 = z_scr[c, rows(1), :]
        y_scr[c, rows(0), :] = w0 * s0 + w1 * s1 + w2 * z0
        y_scr[c, rows(1), :] = w0 * s1 + w1 * z0 + w2 * z1
        t0_ref[:, cs] = z_scr[c, rows(T - 2), :]
        t1_ref[:, cs] = z_scr[c, rows(T - 1), :]
        cols.append(y_scr[c])
    return jnp.concatenate(cols, axis=1)


def _ada_kernel(c_ref, w_ref, b_ref, o_ref):
    c = c_ref[...]
    s = (c * _sigmoid(c)).astype(BF16)
    o_ref[...] = _dot(s, w_ref[...].astype(BF16)) + b_ref[...]


def _ada(c, w, b, tn=1024):
    r, d = c.shape
    n = w.shape[1]
    return pl.pallas_call(
        _ada_kernel,
        grid=(n // tn,),
        in_specs=[
            pl.BlockSpec((r, d), lambda j: (0, 0)),
            pl.BlockSpec((d, tn), lambda j: (0, j)),
            pl.BlockSpec((1, tn), lambda j: (0, j)),
        ],
        out_specs=pl.BlockSpec((r, tn), lambda j: (0, j)),
        out_shape=jax.ShapeDtypeStruct((r, n), F32),
        compiler_params=_cparams(("arbitrary",)),
        name="ada",
    )(c, w, b.reshape(1, n))


def _inproj_kernel(*refs, tpb, period, sample):
    if sample:
        (x_ref, sh_ref, sc_ref, wb_ref, wc_ref, wh_ref, wq_ref, wk_ref, wv_ref, wo_ref, wg_ref,
         wconv_ref, s0_ref, s1_ref,
         yc_ref, q_ref, k_ref, v_ref, o_ref, g_ref, t0_ref, t1_ref, u_scr, z_scr, y_scr) = refs
    else:
        (x_ref, sh_ref, sc_ref, wb_ref, wc_ref, wh_ref, wq_ref, wk_ref, wv_ref, wo_ref, wg_ref,
         wconv_ref,
         yc_ref, q_ref, k_ref, v_ref, o_ref, g_ref, zt_ref, u_scr, carry_scr) = refs
    m = pl.program_id(0)
    j = pl.program_id(1)

    @pl.when(j == 0)
    def _():
        u = _ln(x_ref[...]) * (1.0 + sc_ref[0]) + sh_ref[0]
        ub = u.astype(BF16)
        u_scr[...] = ub
        g_ref[0] = _dot(ub, wg_ref[...])

    u = u_scr[...]
    z = _dot(u, wc_ref[...]) * _dot(u, wh_ref[...])
    if sample:
        yc = _conv3_sequences(z, wconv_ref[...], s0_ref, s1_ref, z_scr, y_scr, t0_ref, t1_ref, period)
    else:
        yc, zt_ref[0] = _conv3_rows(z, wconv_ref[...], carry_scr, j, lax.rem(m, tpb) == 0)
    yc_ref[0] = (_dot(u, wb_ref[...]) * yc).astype(BF16)
    q_ref[0] = _dot(u, wq_ref[...]).astype(BF16)
    k_ref[0] = (_dot(u, wk_ref[...]) * (HEAD_DIM ** -0.5)).astype(BF16)
    v_ref[0] = _dot(u, wv_ref[...]).astype(BF16)
    o_ref[0] = _dot(u, wo_ref[...]).astype(BF16)


def _inproj(x, mod, w_in, w_gate, w_conv, *, tm, tpb, sample, period=0, s0=None, s1=None):
    rows, d = x.shape
    dc = w_conv.shape[1]
    tn = HEAD_DIM
    nj = dc // tn
    nm = rows // tm
    nseq = nm // tpb
    r = tm if sample else 1

    def wspec(off):
        return pl.BlockSpec((d, tn), lambda m, j, off=off: (0, off * nj + j))

    in_specs = [
        pl.BlockSpec((tm, d), lambda m, j: (m, 0)),
        pl.BlockSpec((1, r, d), lambda m, j: (m // tpb, 0, 0)),
        pl.BlockSpec((1, r, d), lambda m, j: (m // tpb, 0, 1)),
        wspec(0), wspec(1), wspec(2), wspec(3), wspec(4), wspec(5), wspec(6),
        pl.BlockSpec((d, GATE_LANES), lambda m, j: (0, 0)),
        pl.BlockSpec((CONV_K, tn), lambda m, j: (0, j)),
    ]
    args = [x, mod, mod, w_in, w_in, w_in, w_in, w_in, w_in, w_in, w_gate, w_conv]
    scratch = [pltpu.VMEM((tm, d), BF16)]
    act = pl.BlockSpec((1, tm, tn), lambda m, j: (m // tpb, m % tpb, j))
    out_specs = [act, act, act, act, act,
                 pl.BlockSpec((1, tm, GATE_LANES), lambda m, j: (m // tpb, m % tpb, 0))]
    out_shape = [jax.ShapeDtypeStruct((nseq, tpb * tm, dc), BF16)] * 5 + [
        jax.ShapeDtypeStruct((nseq, tpb * tm, GATE_LANES), F32)]
    if sample:
        assert nm == 1
        st = pl.BlockSpec((tm // period, tn), lambda m, j: (0, j))
        in_specs += [st, st]
        args += [s0, s1]
        out_specs += [st, st]
        out_shape += [jax.ShapeDtypeStruct((tm // period, dc), F32)] * 2
        scratch += [pltpu.VMEM((tn // LANES, tm, LANES), F32)] * 2
    else:
        out_specs.append(pl.BlockSpec((1, STATE_ROWS, tn), lambda m, j: (m, 0, j)))
        out_shape.append(jax.ShapeDtypeStruct((nm, STATE_ROWS, dc), F32))
        scratch.append(pltpu.VMEM((nj, STATE_ROWS, tn), F32))
    return pl.pallas_call(
        functools.partial(_inproj_kernel, tpb=tpb, period=period, sample=sample),
        grid=(nm, nj),
        in_specs=in_specs,
        out_specs=out_specs,
        out_shape=out_shape,
        scratch_shapes=scratch,
        compiler_params=_cparams(("arbitrary", "arbitrary")),
        name="inproj_sample" if sample else "inproj_prompt",
    )(*args)


def _split3(x):
    hi = x.astype(BF16)
    r1 = x - hi.astype(F32)
    mid = r1.astype(BF16)
    lo = (r1 - mid.astype(F32)).astype(BF16)
    return hi, mid, lo


def _head_out(hh, o, wmh):
    return (_sigmoid(o.astype(F32)) * (_ln(hh) * wmh)).astype(BF16)


def _mlstm_chunk_kernel(q_ref, k_ref, v_ref, o_ref, g_ref, bg_ref, wmh_ref,
                        y_ref, c_ref, n_ref, m_ref, *, B, L):
    @pl.when(pl.program_id(0) == 0)
    def _():
        c_ref[...] = jnp.zeros_like(c_ref)
        n_ref[...] = jnp.zeros_like(n_ref)
        m_ref[...] = jnp.zeros_like(m_ref)

    row = lax.broadcasted_iota(jnp.int32, (L, L), 0)
    col = lax.broadcasted_iota(jnp.int32, (L, L), 1)
    causal = col <= row
    tril = jnp.where(causal, 1.0, 0.0).astype(BF16)

    heads = [(b, h) for b in range(B) for h in range(N_HEADS)]

    def hsl(h):
        return slice(h * HEAD_DIM, (h + 1) * HEAD_DIM)

    gate = []
    for b in range(B):
        g = g_ref[b] + bg_ref[...]
        hi, mid, lo = _split3(_log_sigmoid(g))
        bcum = _dot(tril, hi) + _dot(tril, mid) + _dot(tril, lo)
        gate.append((g, g.T, bcum, bcum.T, m_ref[b], n_ref[b]))

    st = []
    for b, h in heads:
        g, g_t, bcum, bcum_t, m_all, _ = gate[b]
        bc = bcum[:, N_HEADS + h:N_HEADS + h + 1]
        br = bcum_t[N_HEADS + h:N_HEADS + h + 1, :]
        m_prev = m_all[h:h + 1, 0:1]
        a = bc + m_prev
        dlog = jnp.where(causal, bc - br + g_t[h:h + 1, :], NEG)
        mt = jnp.maximum(a, jnp.max(dlog, axis=1, keepdims=True))
        st.append(dict(bc=bc, m_prev=m_prev, mt=mt, dw=jnp.exp(dlog - mt), inter=jnp.exp(a - mt)))

    for (b, h), e in zip(heads, st):
        e["s"] = lax.dot_general(q_ref[b, :, hsl(h)], k_ref[b, :, hsl(h)], (((1,), (1,)), ((), ())),
                                 preferred_element_type=F32) * e["dw"]

    for (b, h), e in zip(heads, st):
        qh = q_ref[b, :, hsl(h)]
        n_old = gate[b][5][h:h + 1, :]
        num = _dot(e["s"].astype(BF16), v_ref[b, :, hsl(h)]) + e["inter"] * _dot(
            qh, c_ref[b, h].astype(BF16))
        qn = jnp.sum(qh.astype(F32) * n_old, axis=1, keepdims=True)
        den = jnp.sum(e["s"], axis=1, keepdims=True) + e["inter"] * qn
        hh = num / jnp.maximum(jnp.abs(den), jnp.exp(-e["mt"]))
        y_ref[b, :, hsl(h)] = _head_out(hh, o_ref[b, :, hsl(h)], wmh_ref[:, hsl(h)])

    m_rows, n_rows = [], []
    for (b, h), e in zip(heads, st):
        g = gate[b][0]
        kh = k_ref[b, :, hsl(h)]
        m_new = e["mt"][L - 1:L, :]
        b_last = e["bc"][L - 1:L, :]
        wc = jnp.exp(b_last - e["bc"] + g[:, h:h + 1] - m_new)
        dc = jnp.exp(b_last + e["m_prev"] - m_new)
        vw = (v_ref[b, :, hsl(h)].astype(F32) * wc).astype(BF16)
        c_ref[b, h] = dc * c_ref[b, h] + lax.dot_general(
            kh, vw, (((0,), (0,)), ((), ())), preferred_element_type=F32)
        n_rows.append(dc * gate[b][5][h:h + 1, :] + jnp.sum(kh.astype(F32) * wc, axis=0, keepdims=True))
        m_rows.append(jnp.broadcast_to(m_new, (1, GATE_LANES)))

    for b in range(B):
        n_ref[b] = jnp.concatenate(n_rows[b * N_HEADS:(b + 1) * N_HEADS], axis=0)
        m_ref[b] = jnp.concatenate(m_rows[b * N_HEADS:(b + 1) * N_HEADS], axis=0)


def _mlstm_prompt(q, k, v, o, gates, bg, wmh, *, L):
    b, t, dm = q.shape
    act = pl.BlockSpec((b, L, dm), lambda c: (0, c, 0))
    whole = lambda *shape: pl.BlockSpec(shape, lambda c: (0,) * len(shape))
    return pl.pallas_call(
        functools.partial(_mlstm_chunk_kernel, B=b, L=L),
        grid=(t // L,),
        in_specs=[act, act, act, act,
                  pl.BlockSpec((b, L, GATE_LANES), lambda c: (0, c, 0)),
                  whole(1, GATE_LANES), whole(1, dm)],
        out_specs=[act,
                   whole(b, N_HEADS, HEAD_DIM, HEAD_DIM),
                   whole(b, N_HEADS, HEAD_DIM),
                   whole(b, N_HEADS, GATE_LANES)],
        out_shape=[jax.ShapeDtypeStruct((b, t, dm), BF16),
                   jax.ShapeDtypeStruct((b, N_HEADS, HEAD_DIM, HEAD_DIM), F32),
                   jax.ShapeDtypeStruct((b, N_HEADS, HEAD_DIM), F32),
                   jax.ShapeDtypeStruct((b, N_HEADS, GATE_LANES), F32)],
        compiler_params=_cparams(("arbitrary",)),
        name="mlstm_prompt",
    )(q, k, v, o, gates, bg, wmh)


SLAB = 16


def _mlstm_step_kernel(q_ref, k_ref, v_ref, o_ref, g_ref, bg_ref, mrow_ref, nrow_ref,
                       c0_ref, n0_ref, wmh_ref,
                       y_ref, c_ref, n_ref, m_ref, *, bb, T):
    R = bb * T
    per_slab = SLAB // T
    t = lax.rem(lax.broadcasted_iota(jnp.int32, (R, GATE_LANES), 0), T)

    def down(x, d):
        return pltpu.roll(x, d, 0)

    def up(x, d):
        return pltpu.roll(x, x.shape[0] - d, 0)

    def seg_last(x):
        out = x
        for d in range(1, T):
            out = jnp.where(t == T - 1 - d, up(x, d), out)
        return out

    g = g_ref[...] + bg_ref[...]
    li = pltpu.roll(g, N_HEADS, 1)
    lf = _log_sigmoid(g)
    b = lf
    for d in range(1, T):
        b = b + jnp.where(t >= d, down(lf, d), 0.0)
    m_prev = mrow_ref[...]
    a = b + m_prev
    dl = [li] + [jnp.where(t >= d, b - down(b, d) + down(li, d), NEG) for d in range(1, T)]
    mt = a
    for d in range(T):
        mt = jnp.maximum(mt, dl[d])
    dw = [jnp.exp(dl[d] - mt) for d in range(T)]
    inter = jnp.exp(a - mt)
    emt = jnp.exp(-mt)
    m_new = seg_last(mt)
    b_last = seg_last(b)
    wc = jnp.exp(b_last - b + li - m_new)
    dc = jnp.exp(b_last + m_prev - m_new)

    row_s = lax.broadcasted_iota(jnp.int32, (SLAB, HEAD_DIM), 0)
    row_r = lax.broadcasted_iota(jnp.int32, (R, HEAD_DIM), 0)

    for h in range(N_HEADS):
        hs = slice(h * HEAD_DIM, (h + 1) * HEAD_DIM)
        ln = N_HEADS + h

        def col(x):
            return x[:, ln:ln + 1]

        qb = q_ref[:, hs]
        kb = k_ref[:, hs]
        vb = v_ref[:, hs]
        qf = qb.astype(F32)
        kf = kb.astype(F32)
        vf = vb.astype(F32)
        num = jnp.zeros((R, HEAD_DIM), F32)
        den = jnp.zeros((R, 1), F32)
        for d in range(T):
            kd = kf if d == 0 else down(kf, d)
            vd = vf if d == 0 else down(vf, d)
            sw = jnp.sum(qf * kd, axis=1, keepdims=True) * col(dw[d])
            num = num + sw * vd
            den = den + sw

        qc_slabs = []
        for si in range(R // SLAB):
            q16 = qb[si * SLAB:(si + 1) * SLAB]
            acc = jnp.zeros((SLAB, HEAD_DIM), F32)
            for bl in range(per_slab):
                bi = si * per_slab + bl
                r = _dot(q16, c0_ref[bi, h].astype(BF16))
                acc = jnp.where(row_s // T == bl, r, acc)
            qc_slabs.append(acc)
        qc = jnp.concatenate(qc_slabs, axis=0)
        qn = jnp.sum(qf * nrow_ref[:, hs], axis=1, keepdims=True)
        num = num + col(inter) * qc
        den = den + col(inter) * qn
        hh = num / jnp.maximum(jnp.abs(den), col(emt))
        y_ref[:, hs] = _head_out(hh, o_ref[:, hs], wmh_ref[:, hs])

        wk = kf * col(wc)
        vw = vf * col(wc)
        for si in range(R // SLAB):
            k16 = kb[si * SLAB:(si + 1) * SLAB]
            vw16 = vw[si * SLAB:(si + 1) * SLAB]
            for bl in range(per_slab):
                bi = si * per_slab + bl
                last = bi * T + T - 1
                vwb = jnp.where(row_s // T == bl, vw16, 0.0).astype(BF16)
                dcb = dc[last:last + 1, ln:ln + 1]
                c_ref[bi, h] = dcb * c0_ref[bi, h] + lax.dot_general(
                    k16, vwb, (((0,), (0,)), ((), ())), preferred_element_type=F32)
                n_ref[bi, h:h + 1, :] = dcb * n0_ref[bi, h:h + 1, :] + jnp.sum(
                    jnp.where(row_r // T == bi, wk, 0.0), axis=0, keepdims=True)
                m_ref[bi, h:h + 1, :] = jnp.broadcast_to(
                    m_new[last:last + 1, ln:ln + 1], (1, GATE_LANES))


def _mlstm_sample(q, k, v, o, gates, bg, mrow, nrow, c0, n0, wmh, *, bb, T):
    rows, dm = q.shape
    nb = c0.shape[0]
    R = bb * T
    act = pl.BlockSpec((R, dm), lambda i: (i, 0))
    gat = pl.BlockSpec((R, GATE_LANES), lambda i: (i, 0))
    cspec = pl.BlockSpec((bb, N_HEADS, HEAD_DIM, HEAD_DIM), lambda i: (i, 0, 0, 0))
    nspec = pl.BlockSpec((bb, N_HEADS, HEAD_DIM), lambda i: (i, 0, 0))
    return pl.pallas_call(
        functools.partial(_mlstm_step_kernel, bb=bb, T=T),
        grid=(nb // bb,),
        in_specs=[act, act, act, act, gat,
                  pl.BlockSpec((1, GATE_LANES), lambda i: (0, 0)),
                  gat,
                  pl.BlockSpec((R, dm), lambda i: (i, 0)),
                  cspec, nspec,
                  pl.BlockSpec((1, dm), lambda i: (0, 0))],
        out_specs=[act, cspec, nspec,
                   pl.BlockSpec((bb, N_HEADS, GATE_LANES), lambda i: (i, 0, 0))],
        out_shape=[jax.ShapeDtypeStruct((rows, dm), BF16),
                   jax.ShapeDtypeStruct(c0.shape, F32),
                   jax.ShapeDtypeStruct(n0.shape, F32),
                   jax.ShapeDtypeStruct((nb, N_HEADS, GATE_LANES), F32)],
        compiler_params=_cparams(("arbitrary",)),
        name="mlstm_sample",
    )(q, k, v, o, gates, bg, mrow, nrow, c0, n0, wmh)


def _outproj_kernel(yc_ref, ym_ref, w_ref, x_ref, g1_ref, lg_ref, lb_ref, o_ref, *, alpha):
    dc = yc_ref.shape[-1]
    mix = _dot(yc_ref[0], w_ref[0:dc, :]) + _dot(ym_ref[0], w_ref[dc:, :])
    o_ref[...] = _ln(alpha * x_ref[...] + (1.0 + g1_ref[0]) * mix) * lg_ref[...] + lb_ref[...]


def _outproj(yc, ym, w_out, x, mod, ln_g, ln_b, *, tm, tpb, alpha):
    rows, d = x.shape
    dc = yc.shape[-1]
    dm = ym.shape[-1]
    r = 1 if mod.shape[1] == 1 else tm
    vec = pl.BlockSpec((1, d), lambda m: (0, 0))
    return pl.pallas_call(
        functools.partial(_outproj_kernel, alpha=alpha),
        grid=(rows // tm,),
        in_specs=[pl.BlockSpec((1, tm, dc), lambda m: (m // tpb, m % tpb, 0)),
                  pl.BlockSpec((1, tm, dm), lambda m: (m // tpb, m % tpb, 0)),
                  pl.BlockSpec((dc + dm, d), lambda m: (0, 0)),
                  pl.BlockSpec((tm, d), lambda m: (m, 0)),
                  pl.BlockSpec((1, r, d), lambda m: (m // tpb, 0, 2)),
                  vec, vec],
        out_specs=pl.BlockSpec((tm, d), lambda m: (m, 0)),
        out_shape=jax.ShapeDtypeStruct((rows, d), F32),
        compiler_params=_cparams(("arbitrary",)),
        name="outproj",
    )(yc, ym, w_out, x, mod, ln_g, ln_b)


def _ffn_kernel(*refs, tpb, period, sample, alpha):
    if sample:
        (x_ref, sh_ref, sc_ref, g2_ref, wa_ref, wg_ref, wconv_ref, wd_ref, lg_ref, lb_ref,
         s0_ref, s1_ref, y_ref, t0_ref, t1_ref, u_scr, z_scr, y_scr) = refs
    else:
        (x_ref, sh_ref, sc_ref, g2_ref, wa_ref, wg_ref, wconv_ref, wd_ref, lg_ref, lb_ref,
         y_ref, at_ref, u_scr, carry_scr) = refs
    m = pl.program_id(0)
    f = pl.program_id(1)
    nf = pl.num_programs(1)

    @pl.when(f == 0)
    def _():
        u = _ln(x_ref[...]) * (1.0 + sc_ref[0]) + sh_ref[0]
        u_scr[...] = u.astype(BF16)
        y_ref[...] = jnp.zeros_like(y_ref)

    u = u_scr[...]
    a = _dot(u, wa_ref[...])
    if sample:
        ac = _conv3_sequences(a, wconv_ref[...], s0_ref, s1_ref, z_scr, y_scr, t0_ref, t1_ref, period)
    else:
        ac, at_ref[0] = _conv3_rows(a, wconv_ref[...], carry_scr, f, lax.rem(m, tpb) == 0)
    hcur = (ac * _sigmoid(ac) * _dot(u, wg_ref[...])).astype(BF16)
    y_ref[...] += _dot(hcur, wd_ref[...])

    @pl.when(f == nf - 1)
    def _():
        y_ref[...] = (_ln(alpha * x_ref[...] + (1.0 + g2_ref[0]) * y_ref[...])
                      * lg_ref[...] + lb_ref[...])


def _ffn(x, mod, w_up, w_conv, w_down, ln_g, ln_b, *, tm, tpb, tf, sample, alpha, period=0,
         s0=None, s1=None):
    rows, d = x.shape
    ff = w_down.shape[0]
    nf = ff // tf
    nm = rows // tm
    r = tm if sample else 1
    vec = pl.BlockSpec((1, d), lambda m, f: (0, 0))
    in_specs = [
        pl.BlockSpec((tm, d), lambda m, f: (m, 0), pipeline_mode=pl.Buffered(1)),
        pl.BlockSpec((1, r, d), lambda m, f: (m // tpb, 0, 3)),
        pl.BlockSpec((1, r, d), lambda m, f: (m // tpb, 0, 4)),
        pl.BlockSpec((1, r, d), lambda m, f: (m // tpb, 0, 5)),
        pl.BlockSpec((d, tf), lambda m, f: (0, f)),
        pl.BlockSpec((d, tf), lambda m, f: (0, nf + f)),
        pl.BlockSpec((CONV_K, tf), lambda m, f: (0, f)),
        pl.BlockSpec((tf, d), lambda m, f: (f, 0)),
        vec, vec,
    ]
    args = [x, mod, mod, mod, w_up, w_up, w_conv, w_down, ln_g, ln_b]
    scratch = [pltpu.VMEM((tm, d), BF16)]
    out_specs = [pl.BlockSpec((tm, d), lambda m, f: (m, 0))]
    out_shape = [jax.ShapeDtypeStruct((rows, d), F32)]
    if sample:
        assert nm == 1
        st = pl.BlockSpec((tm // period, tf), lambda m, f: (0, f))
        in_specs += [st, st]
        args += [s0, s1]
        out_specs += [st, st]
        out_shape += [jax.ShapeDtypeStruct((tm // period, ff), F32)] * 2
        scratch += [pltpu.VMEM((tf // LANES, tm, LANES), F32)] * 2
    else:
        out_specs.append(pl.BlockSpec((1, STATE_ROWS, tf), lambda m, f: (m, 0, f)))
        out_shape.append(jax.ShapeDtypeStruct((nm, STATE_ROWS, ff), F32))
        scratch.append(pltpu.VMEM((nf, STATE_ROWS, tf), F32))
    return pl.pallas_call(
        functools.partial(_ffn_kernel, tpb=tpb, period=period, sample=sample, alpha=alpha),
        grid=(nm, nf),
        in_specs=in_specs,
        out_specs=out_specs,
        out_shape=out_shape,
        scratch_shapes=scratch,
        compiler_params=_cparams(("arbitrary", "arbitrary")),
        name="ffn_sample" if sample else "ffn_prompt",
    )(*args)


def _layer_prompt(x, mod, wts, *, alpha):
    B, T, D = x.shape
    (w_in, w_gate, bg, w_conv, wmh, w_out, ln1_g, ln1_b, w_up, w_fconv, w_down, ln2_g, ln2_b) = wts
    tm = 1024
    tpb = T // tm
    x2 = x.reshape(B * T, D)
    yc, q, k, v, o, gates, ztail = _inproj(x2, mod, w_in, w_gate, w_conv, tm=tm, tpb=tpb, sample=False)
    ym, C, n, m = _mlstm_prompt(q, k, v, o, gates, bg, wmh, L=128)
    tm2 = 512
    x1 = _outproj(yc, ym, w_out, x2, mod, ln1_g, ln1_b, tm=tm2, tpb=T // tm2, alpha=alpha)
    y, atail = _ffn(x1, mod, w_up, w_fconv, w_down, ln2_g, ln2_b,
                    tm=tm, tpb=tpb, tf=512, sample=False, alpha=alpha)
    return (y.reshape(B, T, D), ztail[tpb - 1::tpb, STATE_ROWS - 2:], C, n, m[..., 0],
            atail[tpb - 1::tpb, STATE_ROWS - 2:])


def _layer_sample(x, mod, conv_buf, C0, n0, m0, ffn_buf, wts, *, alpha):
    B, T, D = x.shape
    (w_in, w_gate, bg, w_conv, wmh, w_out, ln1_g, ln1_b, w_up, w_fconv, w_down, ln2_g, ln2_b) = wts
    rows = B * T
    x2 = x.reshape(rows, D)
    yc, q, k, v, o, gates, z0, z1 = _inproj(x2, mod, w_in, w_gate, w_conv, tm=rows, tpb=1, sample=True,
                                            period=T, s0=conv_buf[:, 0], s1=conv_buf[:, 1])
    mrow = jnp.pad(jnp.repeat(m0, T, axis=0), ((0, 0), (N_HEADS, GATE_LANES - 2 * N_HEADS)))
    nrow = jnp.repeat(n0.reshape(B, N_HEADS * HEAD_DIM), T, axis=0)
    ym, C, n, m = _mlstm_sample(q[0], k[0], v[0], o[0], gates[0], bg, mrow, nrow, C0, n0, wmh, bb=8, T=T)
    x1 = _outproj(yc, ym[None], w_out, x2, mod, ln1_g, ln1_b, tm=rows, tpb=1, alpha=alpha)
    y, a0, a1 = _ffn(x1, mod, w_up, w_fconv, w_down, ln2_g, ln2_b, tm=rows, tpb=1, tf=512,
                     sample=True, alpha=alpha, period=T, s0=ffn_buf[:, 0], s1=ffn_buf[:, 1])
    return (y.reshape(B, T, D), jnp.stack([z0, z1], axis=1), C, n, m[..., 0],
            jnp.stack([a0, a1], axis=1))


def kernel(x_prompt, x_sample, c_prompt, c_sample, state_conv, state_mlstm_C, state_mlstm_n,
           state_mlstm_m, state_ffn_conv, w_ada, b_ada, w_in, b_gate, w_conv, w_mh_norm, w_out,
           ln1_g, ln1_b, w_up, w_ffn_conv, w_down, ln2_g, ln2_b):
    depth = w_in.shape[0]
    alpha = (2 * depth) ** 0.25
    Bp = x_prompt.shape[0]
    Bs, Ts, D = x_sample.shape
    dc = w_conv.shape[-1]
    dm = w_mh_norm.shape[-1]
    n_main = 3 * dc + 4 * dm
    assert dc == dm == N_HEADS * HEAD_DIM and Ts >= CONV_K - 1 and SLAB % Ts == 0

    xp, xs = x_prompt, x_sample
    outs_p = [[] for _ in range(5)]
    outs_s = [[] for _ in range(5)]
    for l in range(depth):
        c_all = jnp.concatenate([jnp.repeat(c_sample, Ts, axis=0), c_prompt], axis=0)
        mod = _ada(c_all, w_ada[l], b_ada[l])
        mod_s = mod.reshape(1, Bs * Ts + Bp, 6 * D)
        mod_p = mod[Bs * Ts:].reshape(Bp, 1, 6 * D)
        w_in_l = w_in[l]
        wts = (
            w_in_l.astype(BF16),
            jnp.pad(w_in_l[:, n_main:], ((0, 0), (0, GATE_LANES - 2 * N_HEADS))).astype(BF16),
            jnp.pad(b_gate[l], (0, GATE_LANES - 2 * N_HEADS)).reshape(1, GATE_LANES),
            w_conv[l],
            w_mh_norm[l].reshape(1, dm),
            w_out[l].astype(BF16),
            ln1_g[l].reshape(1, D), ln1_b[l].reshape(1, D),
            w_up[l].astype(BF16),
            w_ffn_conv[l],
            w_down[l].astype(BF16),
            ln2_g[l].reshape(1, D), ln2_b[l].reshape(1, D),
        )
        xp, *st_p = _layer_prompt(xp, mod_p, wts, alpha=alpha)
        xs, *st_s = _layer_sample(xs, mod_s, state_conv[l], state_mlstm_C[l], state_mlstm_n[l],
                                  state_mlstm_m[l], state_ffn_conv[l], wts, alpha=alpha)
        for acc, val in zip(outs_p, st_p):
            acc.append(val)
        for acc, val in zip(outs_s, st_s):
            acc.append(val)
    return (xp.astype(x_prompt.dtype), xs.astype(x_sample.dtype),
            *[jnp.stack(a) for a in outs_p], *[jnp.stack(a) for a in outs_s])
```

```python
import functools

import jax
import jax.numpy as jnp
from jax import lax
from jax.experimental import pallas as pl
from jax.experimental.pallas import tpu as pltpu

F32 = jnp.float32
BF16 = jnp.bfloat16

N_HEADS = 4
HEAD_DIM = 256
CONV_K = 3
LN_EPS = 1e-5
NEG = -1e30
LANES = 128
GATE_LANES = LANES
STATE_ROWS = 8
VMEM_LIMIT = 56 * 1024 * 1024


def _cparams(sem):
    return pltpu.CompilerParams(dimension_semantics=sem, vmem_limit_bytes=VMEM_LIMIT)


def _ln(x):
    mu = jnp.mean(x, axis=-1, keepdims=True)
    xc = x - mu
    var = jnp.mean(xc * xc, axis=-1, keepdims=True)
    return xc * lax.rsqrt(var + LN_EPS)


def _log_sigmoid(x):
    return jnp.minimum(x, 0.0) - jnp.log1p(jnp.exp(-jnp.abs(x)))


def _sigmoid(x):
    return 1.0 / (1.0 + jnp.exp(-x))


def _dot(a, b):
    return jnp.dot(a, b, preferred_element_type=F32)


def _conv3_rows(z, w, prev):
    p0 = prev[STATE_ROWS - 2:STATE_ROWS - 1]
    p1 = prev[STATE_ROWS - 1:STATE_ROWS]
    t = lax.broadcasted_iota(jnp.int32, z.shape, 0)
    z1 = jnp.where(t >= 1, pltpu.roll(z, 1, 0), p1)
    z2 = jnp.where(t >= 2, pltpu.roll(z, 2, 0), jnp.where(t == 0, p0, p1))
    return w[0:1] * z2 + w[1:2] * z1 + w[2:3] * z, z[z.shape[0] - STATE_ROWS:]


def _carried_rows(carry_ref, idx, first):
    @pl.when(first)
    def _():
        carry_ref[idx] = jnp.zeros(carry_ref.shape[1:], F32)

    return carry_ref[idx]


def _conv3_sequences(z, w, s0_ref, s1_ref, z_scr, y_scr, t0_ref, t1_ref, T):
    nseq = z.shape[0] // T
    y = w[0:1] * pltpu.roll(z, 2, 0) + w[1:2] * pltpu.roll(z, 1, 0) + w[2:3] * z

    def rows(t):
        return pl.ds(t, nseq, stride=T)

    cols = []
    for c in range(z.shape[1] // LANES):
        cs = slice(c * LANES, (c + 1) * LANES)
        w0, w1, w2 = w[0:1, cs], w[1:2, cs], w[2:3, cs]
        z_scr[c] = z[:, cs]
        y_scr[c] = y[:, cs]
        s0 = s0_ref[:, cs]
        s1 = s1_ref[:, cs]
        z0 = z_scr[c, rows(0), :]
        z1 = z_scr[c, rows(1), :]
        y_scr[c, rows(0), :] = w0 * s0 + w1 * s1 + w2 * z0
        y_scr[c, rows(1), :] = w0 * s1 + w1 * z0 + w2 * z1
        t0_ref[:, cs] = z_scr[c, rows(T - 2), :]
        t1_ref[:, cs] = z_scr[c, rows(T - 1), :]
        cols.append(y_scr[c])
    return jnp.concatenate(cols, axis=1)


def _ada_kernel(c_ref, w_ref, b_ref, o_ref):
    c = c_ref[...]
    s = (c * _sigmoid(c)).astype(BF16)
    o_ref[...] = _dot(s, w_ref[...].astype(BF16)) + b_ref[...]


def _ada(c, w, b, tn=1024):
    r, d = c.shape
    n = w.shape[1]
    return pl.pallas_call(
        _ada_kernel,
        grid=(n // tn,),
        in_specs=[
            pl.BlockSpec((r, d), lambda j: (0, 0)),
            pl.BlockSpec((d, tn), lambda j: (0, j)),
            pl.BlockSpec((1, tn), lambda j: (0, j)),
        ],
        out_specs=pl.BlockSpec((r, tn), lambda j: (0, j)),
        out_shape=jax.ShapeDtypeStruct((r, n), F32),
        compiler_params=_cparams(("arbitrary",)),
        name="ada",
    )(c, w, b.reshape(1, n))


def _inproj_kernel(*refs, tpb, period, sample):
    if sample:
        (x_ref, sh_ref, sc_ref, wb_ref, wc_ref, wh_ref, wq_ref, wk_ref, wv_ref, wo_ref, wg_ref,
         wconv_ref, s0_ref, s1_ref,
         yc_ref, q_ref, k_ref, v_ref, o_ref, g_ref, t0_ref, t1_ref, u_scr, z_scr, y_scr) = refs
    else:
        (x_ref, sh_ref, sc_ref, wb_ref, wc_ref, wh_ref, wq_ref, wk_ref, wv_ref, wo_ref, wg_ref,
         wconv_ref,
         yc_ref, q_ref, k_ref, v_ref, o_ref, g_ref, zt_ref, u_scr, carry_scr) = refs
    m = pl.program_id(0)
    j = pl.program_id(1)

    @pl.when(j == 0)
    def _():
        u = _ln(x_ref[...]) * (1.0 + sc_ref[0]) + sh_ref[0]
        ub = u.astype(BF16)
        u_scr[...] = ub
        g_ref[0] = _dot(ub, wg_ref[...])

    u = u_scr[...]
    z = _dot(u, wc_ref[...]) * _dot(u, wh_ref[...])
    if sample:
        yc = _conv3_sequences(z, wconv_ref[...], s0_ref, s1_ref, z_scr, y_scr, t0_ref, t1_ref, period)
    else:
        yc, tail = _conv3_rows(z, wconv_ref[...], _carried_rows(carry_scr, j, lax.rem(m, tpb) == 0))
        carry_scr[j] = tail
        zt_ref[0] = tail
    half = u.shape[0] // 2
    bg = jnp.concatenate([_dot(u[:half], wb_ref[...]), _dot(u[half:], wb_ref[...])], axis=0)
    yc_ref[0] = (bg * yc).astype(BF16)
    q_ref[0] = _dot(u, wq_ref[...]).astype(BF16)
    k_ref[0] = (_dot(u, wk_ref[...]) * (HEAD_DIM ** -0.5)).astype(BF16)
    v_ref[0] = _dot(u, wv_ref[...]).astype(BF16)
    o_ref[0] = _dot(u, wo_ref[...]).astype(BF16)


def _inproj(x, mod, w_in, w_gate, w_conv, *, tm, tpb, sample, period=0, s0=None, s1=None):
    rows, d = x.shape
    dc = w_conv.shape[1]
    tn = HEAD_DIM
    nj = dc // tn
    nm = rows // tm
    nseq = nm // tpb
    r = tm if sample else 1

    def wspec(off):
        return pl.BlockSpec((d, tn), lambda m, j, off=off: (0, off * nj + j))

    in_specs = [
        pl.BlockSpec((tm, d), lambda m, j: (m, 0)),
        pl.BlockSpec((1, r, d), lambda m, j: (m // tpb, 0, 0)),
        pl.BlockSpec((1, r, d), lambda m, j: (m // tpb, 0, 1)),
        wspec(0), wspec(1), wspec(2), wspec(3), wspec(4), wspec(5), wspec(6),
        pl.BlockSpec((d, GATE_LANES), lambda m, j: (0, 0)),
        pl.BlockSpec((CONV_K, tn), lambda m, j: (0, j)),
    ]
    args = [x, mod, mod, w_in, w_in, w_in, w_in, w_in, w_in, w_in, w_gate, w_conv]
    scratch = [pltpu.VMEM((tm, d), BF16)]
    act = pl.BlockSpec((1, tm, tn), lambda m, j: (m // tpb, m % tpb, j))
    out_specs = [act, act, act, act, act,
                 pl.BlockSpec((1, tm, GATE_LANES), lambda m, j: (m // tpb, m % tpb, 0))]
    out_shape = [jax.ShapeDtypeStruct((nseq, tpb * tm, dc), BF16)] * 5 + [
        jax.ShapeDtypeStruct((nseq, tpb * tm, GATE_LANES), F32)]
    if sample:
        assert nm == 1
        st = pl.BlockSpec((tm // period, tn), lambda m, j: (0, j))
        in_specs += [st, st]
        args += [s0, s1]
        out_specs += [st, st]
        out_shape += [jax.ShapeDtypeStruct((tm // period, dc), F32)] * 2
        scratch += [pltpu.VMEM((tn // LANES, tm, LANES), F32)] * 2
    else:
        out_specs.append(pl.BlockSpec((1, STATE_ROWS, tn), lambda m, j: (m, 0, j)))
        out_shape.append(jax.ShapeDtypeStruct((nm, STATE_ROWS, dc), F32))
        scratch.append(pltpu.VMEM((nj, STATE_ROWS, tn), F32))
    return pl.pallas_call(
        functools.partial(_inproj_kernel, tpb=tpb, period=period, sample=sample),
        grid=(nm, nj),
        in_specs=in_specs,
        out_specs=out_specs,
        out_shape=out_shape,
        scratch_shapes=scratch,
        compiler_params=_cparams(("arbitrary", "arbitrary")),
        name="inproj_sample" if sample else "inproj_prompt",
    )(*args)


def _split3(x):
    hi = x.astype(BF16)
    r1 = x - hi.astype(F32)
    mid = r1.astype(BF16)
    lo = (r1 - mid.astype(F32)).astype(BF16)
    return hi, mid, lo


def _head_out(hh, o, wmh):
    return (_sigmoid(o.astype(F32)) * (_ln(hh) * wmh)).astype(BF16)


def _mlstm_chunk_kernel(q_ref, k_ref, v_ref, o_ref, g_ref, bg_ref, wmh_ref,
                        y_ref, c_ref, n_ref, m_ref, *, B, L):
    @pl.when(pl.program_id(0) == 0)
    def _():
        c_ref[...] = jnp.zeros_like(c_ref)
        n_ref[...] = jnp.zeros_like(n_ref)
        m_ref[...] = jnp.zeros_like(m_ref)

    row = lax.broadcasted_iota(jnp.int32, (L, L), 0)
    col = lax.broadcasted_iota(jnp.int32, (L, L), 1)
    causal = col <= row
    tril = jnp.where(causal, 1.0, 0.0).astype(BF16)

    heads = [(b, h) for b in range(B) for h in range(N_HEADS)]

    def hsl(h):
        return slice(h * HEAD_DIM, (h + 1) * HEAD_DIM)

    gate = []
    for b in range(B):
        g = g_ref[b] + bg_ref[...]
        hi, mid, lo = _split3(_log_sigmoid(g))
        bcum = _dot(tril, hi) + _dot(tril, mid) + _dot(tril, lo)
        gate.append((g, g.T, bcum, bcum.T, m_ref[b], n_ref[b]))

    st = []
    for b, h in heads:
        g, g_t, bcum, bcum_t, m_all, _ = gate[b]
        bc = bcum[:, N_HEADS + h:N_HEADS + h + 1]
        br = bcum_t[N_HEADS + h:N_HEADS + h + 1, :]
        m_prev = m_all[h:h + 1, 0:1]
        a = bc + m_prev
        dlog = jnp.where(causal, bc - br + g_t[h:h + 1, :], NEG)
        mt = jnp.maximum(a, jnp.max(dlog, axis=1, keepdims=True))
        st.append(dict(bc=bc, m_prev=m_prev, mt=mt, dw=jnp.exp(dlog - mt), inter=jnp.exp(a - mt)))

    for (b, h), e in zip(heads, st):
        e["s"] = lax.dot_general(q_ref[b, :, hsl(h)], k_ref[b, :, hsl(h)], (((1,), (1,)), ((), ())),
                                 preferred_element_type=F32) * e["dw"]

    for (b, h), e in zip(heads, st):
        qh = q_ref[b, :, hsl(h)]
        n_old = gate[b][5][h:h + 1, :]
        num = _dot(e["s"].astype(BF16), v_ref[b, :, hsl(h)]) + e["inter"] * _dot(
            qh, c_ref[b, h].astype(BF16))
        qn = jnp.sum(qh.astype(F32) * n_old, axis=1, keepdims=True)
        den = jnp.sum(e["s"], axis=1, keepdims=True) + e["inter"] * qn
        hh = num / jnp.maximum(jnp.abs(den), jnp.exp(-e["mt"]))
        y_ref[b, :, hsl(h)] = _head_out(hh, o_ref[b, :, hsl(h)], wmh_ref[:, hsl(h)])

    m_rows, n_rows = [], []
    for (b, h), e in zip(heads, st):
        g = gate[b][0]
        kh = k_ref[b, :, hsl(h)]
        m_new = e["mt"][L - 1:L, :]
        b_last = e["bc"][L - 1:L, :]
        wc = jnp.exp(b_last - e["bc"] + g[:, h:h + 1] - m_new)
        dc = jnp.exp(b_last + e["m_prev"] - m_new)
        vw = (v_ref[b, :, hsl(h)].astype(F32) * wc).astype(BF16)
        c_ref[b, h] = dc * c_ref[b, h] + lax.dot_general(
            kh, vw, (((0,), (0,)), ((), ())), preferred_element_type=F32)
        n_rows.append(dc * gate[b][5][h:h + 1, :] + jnp.sum(kh.astype(F32) * wc, axis=0, keepdims=True))
        m_rows.append(jnp.broadcast_to(m_new, (1, GATE_LANES)))

    for b in range(B):
        n_ref[b] = jnp.concatenate(n_rows[b * N_HEADS:(b + 1) * N_HEADS], axis=0)
        m_ref[b] = jnp.concatenate(m_rows[b * N_HEADS:(b + 1) * N_HEADS], axis=0)


def _mlstm_prompt(q, k, v, o, gates, bg, wmh, *, L):
    b, t, dm = q.shape
    act = pl.BlockSpec((b, L, dm), lambda c: (0, c, 0))
    whole = lambda *shape: pl.BlockSpec(shape, lambda c: (0,) * len(shape))
    return pl.pallas_call(
        functools.partial(_mlstm_chunk_kernel, B=b, L=L),
        grid=(t // L,),
        in_specs=[act, act, act, act,
                  pl.BlockSpec((b, L, GATE_LANES), lambda c: (0, c, 0)),
                  whole(1, GATE_LANES), whole(1, dm)],
        out_specs=[act,
                   whole(b, N_HEADS, HEAD_DIM, HEAD_DIM),
                   whole(b, N_HEADS, HEAD_DIM),
                   whole(b, N_HEADS, GATE_LANES)],
        out_shape=[jax.ShapeDtypeStruct((b, t, dm), BF16),
                   jax.ShapeDtypeStruct((b, N_HEADS, HEAD_DIM, HEAD_DIM), F32),
                   jax.ShapeDtypeStruct((b, N_HEADS, HEAD_DIM), F32),
                   jax.ShapeDtypeStruct((b, N_HEADS, GATE_LANES), F32)],
        compiler_params=_cparams(("arbitrary",)),
        name="mlstm_prompt",
    )(q, k, v, o, gates, bg, wmh)


SLAB = 16


def _mlstm_step_kernel(q_ref, k_ref, v_ref, o_ref, g_ref, bg_ref, mrow_ref, nrow_ref,
                       c0_ref, n0_ref, wmh_ref,
                       y_ref, c_ref, n_ref, m_ref, *, bb, T):
    R = bb * T
    per_slab = SLAB // T
    t = lax.rem(lax.broadcasted_iota(jnp.int32, (R, GATE_LANES), 0), T)

    def down(x, d):
        return pltpu.roll(x, d, 0)

    def up(x, d):
        return pltpu.roll(x, x.shape[0] - d, 0)

    def seg_last(x):
        out = x
        for d in range(1, T):
            out = jnp.where(t == T - 1 - d, up(x, d), out)
        return out

    g = g_ref[...] + bg_ref[...]
    li = pltpu.roll(g, N_HEADS, 1)
    lf = _log_sigmoid(g)
    b = lf
    for d in range(1, T):
        b = b + jnp.where(t >= d, down(lf, d), 0.0)
    m_prev = mrow_ref[...]
    a = b + m_prev
    dl = [li] + [jnp.where(t >= d, b - down(b, d) + down(li, d), NEG) for d in range(1, T)]
    mt = a
    for d in range(T):
        mt = jnp.maximum(mt, dl[d])
    dw = [jnp.exp(dl[d] - mt) for d in range(T)]
    inter = jnp.exp(a - mt)
    emt = jnp.exp(-mt)
    m_new = seg_last(mt)
    b_last = seg_last(b)
    wc = jnp.exp(b_last - b + li - m_new)
    dc = jnp.exp(b_last + m_prev - m_new)

    row_s = lax.broadcasted_iota(jnp.int32, (SLAB, HEAD_DIM), 0)
    row_r = lax.broadcasted_iota(jnp.int32, (R, HEAD_DIM), 0)

    for h in range(N_HEADS):
        hs = slice(h * HEAD_DIM, (h + 1) * HEAD_DIM)
        ln = N_HEADS + h

        def col(x):
            return x[:, ln:ln + 1]

        qb = q_ref[:, hs]
        kb = k_ref[:, hs]
        vb = v_ref[:, hs]
        qf = qb.astype(F32)
        kf = kb.astype(F32)
        vf = vb.astype(F32)
        num = jnp.zeros((R, HEAD_DIM), F32)
        den = jnp.zeros((R, 1), F32)
        for d in range(T):
            kd = kf if d == 0 else down(kf, d)
            vd = vf if d == 0 else down(vf, d)
            sw = jnp.sum(qf * kd, axis=1, keepdims=True) * col(dw[d])
            num = num + sw * vd
            den = den + sw

        qc_slabs = []
        for si in range(R // SLAB):
            q16 = qb[si * SLAB:(si + 1) * SLAB]
            acc = jnp.zeros((SLAB, HEAD_DIM), F32)
            for bl in range(per_slab):
                bi = si * per_slab + bl
                r = _dot(q16, c0_ref[bi, h].astype(BF16))
                acc = jnp.where(row_s // T == bl, r, acc)
            qc_slabs.append(acc)
        qc = jnp.concatenate(qc_slabs, axis=0)
        qn = jnp.sum(qf * nrow_ref[:, hs], axis=1, keepdims=True)
        num = num + col(inter) * qc
        den = den + col(inter) * qn
        hh = num / jnp.maximum(jnp.abs(den), col(emt))
        y_ref[:, hs] = _head_out(hh, o_ref[:, hs], wmh_ref[:, hs])

        wk = kf * col(wc)
        vw = vf * col(wc)
        for si in range(R // SLAB):
            k16 = kb[si * SLAB:(si + 1) * SLAB]
            vw16 = vw[si * SLAB:(si + 1) * SLAB]
            for bl in range(per_slab):
                bi = si * per_slab + bl
                last = bi * T + T - 1
                vwb = jnp.where(row_s // T == bl, vw16, 0.0).astype(BF16)
                dcb = dc[last:last + 1, ln:ln + 1]
                c_ref[bi, h] = dcb * c0_ref[bi, h] + lax.dot_general(
                    k16, vwb, (((0,), (0,)), ((), ())), preferred_element_type=F32)
                n_ref[bi, h:h + 1, :] = dcb * n0_ref[bi, h:h + 1, :] + jnp.sum(
                    jnp.where(row_r // T == bi, wk, 0.0), axis=0, keepdims=True)
                m_ref[bi, h:h + 1, :] = jnp.broadcast_to(
                    m_new[last:last + 1, ln:ln + 1], (1, GATE_LANES))


def _mlstm_sample(q, k, v, o, gates, bg, mrow, nrow, c0, n0, wmh, *, bb, T):
    rows, dm = q.shape
    nb = c0.shape[0]
    R = bb * T
    act = pl.BlockSpec((R, dm), lambda i: (i, 0))
    gat = pl.BlockSpec((R, GATE_LANES), lambda i: (i, 0))
    cspec = pl.BlockSpec((bb, N_HEADS, HEAD_DIM, HEAD_DIM), lambda i: (i, 0, 0, 0))
    nspec = pl.BlockSpec((bb, N_HEADS, HEAD_DIM), lambda i: (i, 0, 0))
    return pl.pallas_call(
        functools.partial(_mlstm_step_kernel, bb=bb, T=T),
        grid=(nb // bb,),
        in_specs=[act, act, act, act, gat,
                  pl.BlockSpec((1, GATE_LANES), lambda i: (0, 0)),
                  gat,
                  pl.BlockSpec((R, dm), lambda i: (i, 0)),
                  cspec, nspec,
                  pl.BlockSpec((1, dm), lambda i: (0, 0))],
        out_specs=[act, cspec, nspec,
                   pl.BlockSpec((bb, N_HEADS, GATE_LANES), lambda i: (i, 0, 0))],
        out_shape=[jax.ShapeDtypeStruct((rows, dm), BF16),
                   jax.ShapeDtypeStruct(c0.shape, F32),
                   jax.ShapeDtypeStruct(n0.shape, F32),
                   jax.ShapeDtypeStruct((nb, N_HEADS, GATE_LANES), F32)],
        compiler_params=_cparams(("arbitrary",)),
        name="mlstm_sample",
    )(q, k, v, o, gates, bg, mrow, nrow, c0, n0, wmh)


def _outproj_kernel(yc_ref, ym_ref, w_ref, x_ref, g1_ref, lg_ref, lb_ref, o_ref, *, alpha, splits):
    dc = yc_ref.shape[-1]
    rs = x_ref.shape[0] // splits
    for i in range(splits):
        r = slice(i * rs, (i + 1) * rs)
        g1 = g1_ref[0] if g1_ref.shape[1] == 1 else g1_ref[0, r, :]
        mix = _dot(yc_ref[0, r, :], w_ref[0:dc, :]) + _dot(ym_ref[0, r, :], w_ref[dc:, :])
        o_ref[r, :] = _ln(alpha * x_ref[r, :] + (1.0 + g1) * mix) * lg_ref[...] + lb_ref[...]


def _outproj(yc, ym, w_out, x, mod, ln_g, ln_b, *, tm, tpb, alpha, splits=2):
    rows, d = x.shape
    dc = yc.shape[-1]
    dm = ym.shape[-1]
    r = 1 if mod.shape[1] == 1 else tm
    vec = pl.BlockSpec((1, d), lambda m: (0, 0))
    return pl.pallas_call(
        functools.partial(_outproj_kernel, alpha=alpha, splits=splits),
        grid=(rows // tm,),
        in_specs=[pl.BlockSpec((1, tm, dc), lambda m: (m // tpb, m % tpb, 0)),
                  pl.BlockSpec((1, tm, dm), lambda m: (m // tpb, m % tpb, 0)),
                  pl.BlockSpec((dc + dm, d), lambda m: (0, 0)),
                  pl.BlockSpec((tm, d), lambda m: (m, 0)),
                  pl.BlockSpec((1, r, d), lambda m: (m // tpb, 0, 2)),
                  vec, vec],
        out_specs=pl.BlockSpec((tm, d), lambda m: (m, 0)),
        out_shape=jax.ShapeDtypeStruct((rows, d), F32),
        compiler_params=_cparams(("arbitrary",)),
        name="outproj",
    )(yc, ym, w_out, x, mod, ln_g, ln_b)


def _ffn_kernel(*refs, tpb, period, sample, alpha, splits):
    if sample:
        (x_ref, sh_ref, sc_ref, g2_ref, wa_ref, wg_ref, wconv_ref, wd_ref, lg_ref, lb_ref,
         s0_ref, s1_ref, y_ref, t0_ref, t1_ref, u_scr, z_scr, y_scr) = refs
    else:
        (x_ref, sh_ref, sc_ref, g2_ref, wa_ref, wg_ref, wconv_ref, wd_ref, lg_ref, lb_ref,
         y_ref, at_ref, u_scr, carry_scr) = refs
    m = pl.program_id(0)
    f = pl.program_id(1)
    nf = pl.num_programs(1)

    @pl.when(f == 0)
    def _():
        u = _ln(x_ref[...]) * (1.0 + sc_ref[0]) + sh_ref[0]
        u_scr[...] = u.astype(BF16)
        y_ref[...] = jnp.zeros_like(y_ref)

    if sample:
        u = u_scr[...]
        a = _dot(u, wa_ref[...])
        ac = _conv3_sequences(a, wconv_ref[...], s0_ref, s1_ref, z_scr, y_scr, t0_ref, t1_ref, period)
        hcur = (ac * _sigmoid(ac) * _dot(u, wg_ref[...])).astype(BF16)
        y_ref[...] += _dot(hcur, wd_ref[...])
    else:
        prev = _carried_rows(carry_scr, f, lax.rem(m, tpb) == 0)
        rs = u_scr.shape[0] // splits
        for i in range(splits):
            r = slice(i * rs, (i + 1) * rs)
            u = u_scr[r, :]
            ac, prev = _conv3_rows(_dot(u, wa_ref[...]), wconv_ref[...], prev)
            hcur = (ac * _sigmoid(ac) * _dot(u, wg_ref[...])).astype(BF16)
            y_ref[r, :] += _dot(hcur, wd_ref[...])
        carry_scr[f] = prev
        at_ref[0] = prev

    @pl.when(f == nf - 1)
    def _():
        y_ref[...] = (_ln(alpha * x_ref[...] + (1.0 + g2_ref[0]) * y_ref[...])
                      * lg_ref[...] + lb_ref[...])


def _ffn(x, mod, w_up, w_conv, w_down, ln_g, ln_b, *, tm, tpb, tf, sample, alpha, period=0,
         splits=1, s0=None, s1=None):
    rows, d = x.shape
    ff = w_down.shape[0]
    nf = ff // tf
    nm = rows // tm
    r = tm if sample else 1
    vec = pl.BlockSpec((1, d), lambda m, f: (0, 0))
    in_specs = [
        pl.BlockSpec((tm, d), lambda m, f: (m, 0), pipeline_mode=pl.Buffered(1)),
        pl.BlockSpec((1, r, d), lambda m, f: (m // tpb, 0, 3)),
        pl.BlockSpec((1, r, d), lambda m, f: (m // tpb, 0, 4)),
        pl.BlockSpec((1, r, d), lambda m, f: (m // tpb, 0, 5)),
        pl.BlockSpec((d, tf), lambda m, f: (0, f)),
        pl.BlockSpec((d, tf), lambda m, f: (0, nf + f)),
        pl.BlockSpec((CONV_K, tf), lambda m, f: (0, f)),
        pl.BlockSpec((tf, d), lambda m, f: (f, 0)),
        vec, vec,
    ]
    args = [x, mod, mod, mod, w_up, w_up, w_conv, w_down, ln_g, ln_b]
    scratch = [pltpu.VMEM((tm, d), BF16)]
    out_specs = [pl.BlockSpec((tm, d), lambda m, f: (m, 0))]
    out_shape = [jax.ShapeDtypeStruct((rows, d), F32)]
    if sample:
        assert nm == 1
        st = pl.BlockSpec((tm // period, tf), lambda m, f: (0, f))
        in_specs += [st, st]
        args += [s0, s1]
        out_specs += [st, st]
        out_shape += [jax.ShapeDtypeStruct((tm // period, ff), F32)] * 2
        scratch += [pltpu.VMEM((tf // LANES, tm, LANES), F32)] * 2
    else:
        out_specs.append(pl.BlockSpec((1, STATE_ROWS, tf), lambda m, f: (m, 0, f)))
        out_shape.append(jax.ShapeDtypeStruct((nm, STATE_ROWS, ff), F32))
        scratch.append(pltpu.VMEM((nf, STATE_ROWS, tf), F32))
    return pl.pallas_call(
        functools.partial(_ffn_kernel, tpb=tpb, period=period, sample=sample, alpha=alpha,
                          splits=splits),
        grid=(nm, nf),
        in_specs=in_specs,
        out_specs=out_specs,
        out_shape=out_shape,
        scratch_shapes=scratch,
        compiler_params=_cparams(("arbitrary", "arbitrary")),
        name="ffn_sample" if sample else "ffn_prompt",
    )(*args)


def _layer_prompt(x, mod, wts, *, alpha):
    B, T, D = x.shape
    (w_in, w_gate, bg, w_conv, wmh, w_out, ln1_g, ln1_b, w_up, w_fconv, w_down, ln2_g, ln2_b) = wts
    tm = 1024
    tpb = T // tm
    x2 = x.reshape(B * T, D)
    yc, q, k, v, o, gates, ztail = _inproj(x2, mod, w_in, w_gate, w_conv, tm=tm, tpb=tpb, sample=False)
    ym, C, n, m = _mlstm_prompt(q, k, v, o, gates, bg, wmh, L=128)
    tm2 = 512
    x1 = _outproj(yc, ym, w_out, x2, mod, ln1_g, ln1_b, tm=tm2, tpb=T // tm2, alpha=alpha)
    y, atail = _ffn(x1, mod, w_up, w_fconv, w_down, ln2_g, ln2_b,
                    tm=tm, tpb=tpb, tf=512, sample=False, alpha=alpha, splits=2)
    return (y.reshape(B, T, D), ztail[tpb - 1::tpb, STATE_ROWS - 2:], C, n, m[..., 0],
            atail[tpb - 1::tpb, STATE_ROWS - 2:])


def _layer_sample(x, mod, conv_buf, C0, n0, m0, ffn_buf, wts, *, alpha):
    B, T, D = x.shape
    (w_in, w_gate, bg, w_conv, wmh, w_out, ln1_g, ln1_b, w_up, w_fconv, w_down, ln2_g, ln2_b) = wts
    rows = B * T
    x2 = x.reshape(rows, D)
    yc, q, k, v, o, gates, z0, z1 = _inproj(x2, mod, w_in, w_gate, w_conv, tm=rows, tpb=1, sample=True,
                                            period=T, s0=conv_buf[:, 0], s1=conv_buf[:, 1])
    mrow = jnp.pad(jnp.repeat(m0, T, axis=0), ((0, 0), (N_HEADS, GATE_LANES - 2 * N_HEADS)))
    nrow = jnp.repeat(n0.reshape(B, N_HEADS * HEAD_DIM), T, axis=0)
    ym, C, n, m = _mlstm_sample(q[0], k[0], v[0], o[0], gates[0], bg, mrow, nrow, C0, n0, wmh, bb=8, T=T)
    x1 = _outproj(yc, ym[None], w_out, x2, mod, ln1_g, ln1_b, tm=rows, tpb=1, alpha=alpha)
    y, a0, a1 = _ffn(x1, mod, w_up, w_fconv, w_down, ln2_g, ln2_b, tm=rows, tpb=1, tf=512,
                     sample=True, alpha=alpha, period=T, s0=ffn_buf[:, 0], s1=ffn_buf[:, 1])
    return (y.reshape(B, T, D), jnp.stack([z0, z1], axis=1), C, n, m[..., 0],
            jnp.stack([a0, a1], axis=1))


def kernel(x_prompt, x_sample, c_prompt, c_sample, state_conv, state_mlstm_C, state_mlstm_n,
           state_mlstm_m, state_ffn_conv, w_ada, b_ada, w_in, b_gate, w_conv, w_mh_norm, w_out,
           ln1_g, ln1_b, w_up, w_ffn_conv, w_down, ln2_g, ln2_b):
    depth = w_in.shape[0]
    alpha = (2 * depth) ** 0.25
    Bp = x_prompt.shape[0]
    Bs, Ts, D = x_sample.shape
    dc = w_conv.shape[-1]
    dm = w_mh_norm.shape[-1]
    n_main = 3 * dc + 4 * dm
    assert dc == dm == N_HEADS * HEAD_DIM and Ts >= CONV_K - 1 and SLAB % Ts == 0

    xp, xs = x_prompt, x_sample
    outs_p = [[] for _ in range(5)]
    outs_s = [[] for _ in range(5)]
    for l in range(depth):
        c_all = jnp.concatenate([jnp.repeat(c_sample, Ts, axis=0), c_prompt], axis=0)
        mod = _ada(c_all, w_ada[l], b_ada[l])
        mod_s = mod.reshape(1, Bs * Ts + Bp, 6 * D)
        mod_p = mod[Bs * Ts:].reshape(Bp, 1, 6 * D)
        w_in_l = w_in[l]
        wts = (
            w_in_l.astype(BF16),
            jnp.pad(w_in_l[:, n_main:], ((0, 0), (0, GATE_LANES - 2 * N_HEADS))).astype(BF16),
            jnp.pad(b_gate[l], (0, GATE_LANES - 2 * N_HEADS)).reshape(1, GATE_LANES),
            w_conv[l],
            w_mh_norm[l].reshape(1, dm),
            w_out[l].astype(BF16),
            ln1_g[l].reshape(1, D), ln1_b[l].reshape(1, D),
            w_up[l].astype(BF16),
            w_ffn_conv[l],
            w_down[l].astype(BF16),
            ln2_g[l].reshape(1, D), ln2_b[l].reshape(1, D),
        )
        xp, *st_p = _layer_prompt(xp, mod_p, wts, alpha=alpha)
        xs, *st_s = _layer_sample(xs, mod_s, state_conv[l], state_mlstm_C[l], state_mlstm_n[l],
                                  state_mlstm_m[l], state_ffn_conv[l], wts, alpha=alpha)
        for acc, val in zip(outs_p, st_p):
            acc.append(val)
        for acc, val in zip(outs_s, st_s):
            acc.append(val)
    return (xp.astype(x_prompt.dtype), xs.astype(x_sample.dtype),
            *[jnp.stack(a) for a in outs_p], *[jnp.stack(a) for a in outs_s])
```

```python
import functools

import jax
import jax.numpy as jnp
from jax import lax
from jax.experimental import pallas as pl
from jax.experimental.pallas import tpu as pltpu

F32 = jnp.float32
BF16 = jnp.bfloat16

N_HEADS = 4
HEAD_DIM = 256
CONV_K = 3
LN_EPS = 1e-5
NEG = -1e30
LANES = 128
GATE_LANES = LANES
STATE_ROWS = 8
VMEM_LIMIT = 56 * 1024 * 1024


def _cparams(sem):
    return pltpu.CompilerParams(dimension_semantics=sem, vmem_limit_bytes=VMEM_LIMIT)


def _ln(x):
    mu = jnp.mean(x, axis=-1, keepdims=True)
    xc = x - mu
    var = jnp.mean(xc * xc, axis=-1, keepdims=True)
    return xc * lax.rsqrt(var + LN_EPS)


def _log_sigmoid(x):
    return jnp.minimum(x, 0.0) - jnp.log1p(jnp.exp(-jnp.abs(x)))


def _sigmoid(x):
    return 1.0 / (1.0 + jnp.exp(-x))


def _dot(a, b):
    return jnp.dot(a, b, preferred_element_type=F32)


def _conv3_rows(z, w, prev):
    p0 = prev[STATE_ROWS - 2:STATE_ROWS - 1]
    p1 = prev[STATE_ROWS - 1:STATE_ROWS]
    t = lax.broadcasted_iota(jnp.int32, z.shape, 0)
    z1 = jnp.where(t >= 1, pltpu.roll(z, 1, 0), p1)
    z2 = jnp.where(t >= 2, pltpu.roll(z, 2, 0), jnp.where(t == 0, p0, p1))
    return w[0:1] * z2 + w[1:2] * z1 + w[2:3] * z, z[z.shape[0] - STATE_ROWS:]


def _carried_rows(carry_ref, idx, first):
    @pl.when(first)
    def _():
        carry_ref[idx] = jnp.zeros(carry_ref.shape[1:], F32)

    return carry_ref[idx]


def _conv3_sequences(z, w, s0_ref, s1_ref, z_scr, y_scr, t0_ref, t1_ref, T):
    nseq = z.shape[0] // T
    y = w[0:1] * pltpu.roll(z, 2, 0) + w[1:2] * pltpu.roll(z, 1, 0) + w[2:3] * z

    def rows(t):
        return pl.ds(t, nseq, stride=T)

    cols = []
    for c in range(z.shape[1] // LANES):
        cs = slice(c * LANES, (c + 1) * LANES)
        w0, w1, w2 = w[0:1, cs], w[1:2, cs], w[2:3, cs]
        z_scr[c] = z[:, cs]
        y_scr[c] = y[:, cs]
        s0 = s0_ref[:, cs]
        s1 = s1_ref[:, cs]
        z0 = z_scr[c, rows(0), :]
        z1 = z_scr[c, rows(1), :]
        y_scr[c, rows(0), :] = w0 * s0 + w1 * s1 + w2 * z0
        y_scr[c, rows(1), :] = w0 * s1 + w1 * z0 + w2 * z1
        t0_ref[:, cs] = z_scr[c, rows(T - 2), :]
        t1_ref[:, cs] = z_scr[c, rows(T - 1), :]
        cols.append(y_scr[c])
    return jnp.concatenate(cols, axis=1)


def _ada_kernel(c_ref, w_ref, b_ref, o_ref):
    c = c_ref[...]
    s = (c * _sigmoid(c)).astype(BF16)
    o_ref[...] = _dot(s, w_ref[...].astype(BF16)) + b_ref[...]


def _ada(c, w, b, tn=1024):
    r, d = c.shape
    n = w.shape[1]
    return pl.pallas_call(
        _ada_kernel,
        grid=(n // tn,),
        in_specs=[
            pl.BlockSpec((r, d), lambda j: (0, 0)),
            pl.BlockSpec((d, tn), lambda j: (0, j)),
            pl.BlockSpec((1, tn), lambda j: (0, j)),
        ],
        out_specs=pl.BlockSpec((r, tn), lambda j: (0, j)),
        out_shape=jax.ShapeDtypeStruct((r, n), F32),
        compiler_params=_cparams(("arbitrary",)),
        name="ada",
    )(c, w, b.reshape(1, n))


def _inproj_kernel(*refs, tpb, period, sample):
    if sample:
        (x_ref, sh_ref, sc_ref, wb_ref, wc_ref, wh_ref, wq_ref, wk_ref, wv_ref, wo_ref, wg_ref,
         wconv_ref, s0_ref, s1_ref,
         yc_ref, q_ref, k_ref, v_ref, o_ref, g_ref, t0_ref, t1_ref, u_scr, z_scr, y_scr) = refs
    else:
        (x_ref, sh_ref, sc_ref, wb_ref, wc_ref, wh_ref, wq_ref, wk_ref, wv_ref, wo_ref, wg_ref,
         wconv_ref,
         yc_ref, q_ref, k_ref, v_ref, o_ref, g_ref, zt_ref, u_scr, carry_scr) = refs
    m = pl.program_id(0)
    j = pl.program_id(1)

    @pl.when(j == 0)
    def _():
        u = _ln(x_ref[...]) * (1.0 + sc_ref[0]) + sh_ref[0]
        ub = u.astype(BF16)
        u_scr[...] = ub
        g_ref[0] = _dot(ub, wg_ref[...])

    u = u_scr[...]
    z = _dot(u, wc_ref[...]) * _dot(u, wh_ref[...])
    if sample:
        yc = _conv3_sequences(z, wconv_ref[...], s0_ref, s1_ref, z_scr, y_scr, t0_ref, t1_ref, period)
    else:
        yc, tail = _conv3_rows(z, wconv_ref[...], _carried_rows(carry_scr, j, lax.rem(m, tpb) == 0))
        carry_scr[j] = tail
        zt_ref[0] = tail
    half = u.shape[0] // 2
    bg = jnp.concatenate([_dot(u[:half], wb_ref[...]), _dot(u[half:], wb_ref[...])], axis=0)
    yc_ref[0] = (bg * yc).astype(BF16)
    q_ref[0] = _dot(u, wq_ref[...]).astype(BF16)
    k_ref[0] = (_dot(u, wk_ref[...]) * (HEAD_DIM ** -0.5)).astype(BF16)
    v_ref[0] = _dot(u, wv_ref[...]).astype(BF16)
    o_ref[0] = _dot(u, wo_ref[...]).astype(BF16)


def _inproj(x, mod, w_in, w_gate, w_conv, *, tm, tpb, sample, period=0, s0=None, s1=None):
    rows, d = x.shape
    dc = w_conv.shape[1]
    tn = HEAD_DIM
    nj = dc // tn
    nm = rows // tm
    nseq = nm // tpb
    r = tm if sample else 1

    def wspec(off):
        return pl.BlockSpec((d, tn), lambda m, j, off=off: (0, off * nj + j))

    in_specs = [
        pl.BlockSpec((tm, d), lambda m, j: (m, 0)),
        pl.BlockSpec((1, r, d), lambda m, j: (m // tpb, 0, 0)),
        pl.BlockSpec((1, r, d), lambda m, j: (m // tpb, 0, 1)),
        wspec(0), wspec(1), wspec(2), wspec(3), wspec(4), wspec(5), wspec(6),
        pl.BlockSpec((d, GATE_LANES), lambda m, j: (0, 0)),
        pl.BlockSpec((CONV_K, tn), lambda m, j: (0, j)),
    ]
    args = [x, mod, mod, w_in, w_in, w_in, w_in, w_in, w_in, w_in, w_gate, w_conv]
    scratch = [pltpu.VMEM((tm, d), BF16)]
    act = pl.BlockSpec((1, tm, tn), lambda m, j: (m // tpb, m % tpb, j))
    out_specs = [act, act, act, act, act,
                 pl.BlockSpec((1, tm, GATE_LANES), lambda m, j: (m // tpb, m % tpb, 0))]
    out_shape = [jax.ShapeDtypeStruct((nseq, tpb * tm, dc), BF16)] * 5 + [
        jax.ShapeDtypeStruct((nseq, tpb * tm, GATE_LANES), F32)]
    if sample:
        assert nm == 1
        st = pl.BlockSpec((tm // period, tn), lambda m, j: (0, j))
        in_specs += [st, st]
        args += [s0, s1]
        out_specs += [st, st]
        out_shape += [jax.ShapeDtypeStruct((tm // period, dc), F32)] * 2
        scratch += [pltpu.VMEM((tn // LANES, tm, LANES), F32)] * 2
    else:
        out_specs.append(pl.BlockSpec((1, STATE_ROWS, tn), lambda m, j: (m, 0, j)))
        out_shape.append(jax.ShapeDtypeStruct((nm, STATE_ROWS, dc), F32))
        scratch.append(pltpu.VMEM((nj, STATE_ROWS, tn), F32))
    return pl.pallas_call(
        functools.partial(_inproj_kernel, tpb=tpb, period=period, sample=sample),
        grid=(nm, nj),
        in_specs=in_specs,
        out_specs=out_specs,
        out_shape=out_shape,
        scratch_shapes=scratch,
        compiler_params=_cparams(("arbitrary", "arbitrary")),
        name="inproj_sample" if sample else "inproj_prompt",
    )(*args)


def _split3(x):
    hi = x.astype(BF16)
    r1 = x - hi.astype(F32)
    mid = r1.astype(BF16)
    lo = (r1 - mid.astype(F32)).astype(BF16)
    return hi, mid, lo


def _head_out(hh, o, wmh):
    return (_sigmoid(o.astype(F32)) * (_ln(hh) * wmh)).astype(BF16)


def _mlstm_chunk_kernel(q_ref, k_ref, v_ref, o_ref, g_ref, bg_ref, wmh_ref,
                        y_ref, c_ref, n_ref, m_ref, *, B, L):
    @pl.when(pl.program_id(0) == 0)
    def _():
        c_ref[...] = jnp.zeros_like(c_ref)
        n_ref[...] = jnp.zeros_like(n_ref)
        m_ref[...] = jnp.zeros_like(m_ref)

    row = lax.broadcasted_iota(jnp.int32, (L, L), 0)
    col = lax.broadcasted_iota(jnp.int32, (L, L), 1)
    causal = col <= row
    tril = jnp.where(causal, 1.0, 0.0).astype(BF16)

    heads = [(b, h) for b in range(B) for h in range(N_HEADS)]

    def hsl(h):
        return slice(h * HEAD_DIM, (h + 1) * HEAD_DIM)

    gate = []
    for b in range(B):
        g = g_ref[b] + bg_ref[...]
        hi, mid, lo = _split3(_log_sigmoid(g))
        bcum = _dot(tril, hi) + _dot(tril, mid) + _dot(tril, lo)
        gate.append((g, g.T, bcum, bcum.T, m_ref[b], n_ref[b]))

    st = []
    for b, h in heads:
        g, g_t, bcum, bcum_t, m_all, _ = gate[b]
        bc = bcum[:, N_HEADS + h:N_HEADS + h + 1]
        br = bcum_t[N_HEADS + h:N_HEADS + h + 1, :]
        m_prev = m_all[h:h + 1, 0:1]
        a = bc + m_prev
        dlog = jnp.where(causal, bc - br + g_t[h:h + 1, :], NEG)
        mt = jnp.maximum(a, jnp.max(dlog, axis=1, keepdims=True))
        st.append(dict(bc=bc, m_prev=m_prev, mt=mt, dw=jnp.exp(dlog - mt), inter=jnp.exp(a - mt)))

    for (b, h), e in zip(heads, st):
        e["s"] = lax.dot_general(q_ref[b, :, hsl(h)], k_ref[b, :, hsl(h)], (((1,), (1,)), ((), ())),
                                 preferred_element_type=F32) * e["dw"]

    for (b, h), e in zip(heads, st):
        qh = q_ref[b, :, hsl(h)]
        n_old = gate[b][5][h:h + 1, :]
        num = _dot(e["s"].astype(BF16), v_ref[b, :, hsl(h)]) + e["inter"] * _dot(
            qh, c_ref[b, h].astype(BF16))
        qn = jnp.sum(qh.astype(F32) * n_old, axis=1, keepdims=True)
        den = jnp.sum(e["s"], axis=1, keepdims=True) + e["inter"] * qn
        hh = num / jnp.maximum(jnp.abs(den), jnp.exp(-e["mt"]))
        y_ref[b, :, hsl(h)] = _head_out(hh, o_ref[b, :, hsl(h)], wmh_ref[:, hsl(h)])

    m_rows, n_rows = [], []
    for (b, h), e in zip(heads, st):
        g = gate[b][0]
        kh = k_ref[b, :, hsl(h)]
        m_new = e["mt"][L - 1:L, :]
        b_last = e["bc"][L - 1:L, :]
        wc = jnp.exp(b_last - e["bc"] + g[:, h:h + 1] - m_new)
        dc = jnp.exp(b_last + e["m_prev"] - m_new)
        vw = (v_ref[b, :, hsl(h)].astype(F32) * wc).astype(BF16)
        c_ref[b, h] = dc * c_ref[b, h] + lax.dot_general(
            kh, vw, (((0,), (0,)), ((), ())), preferred_element_type=F32)
        n_rows.append(dc * gate[b][5][h:h + 1, :] + jnp.sum(kh.astype(F32) * wc, axis=0, keepdims=True))
        m_rows.append(jnp.broadcast_to(m_new, (1, GATE_LANES)))

    for b in range(B):
        n_ref[b] = jnp.concatenate(n_rows[b * N_HEADS:(b + 1) * N_HEADS], axis=0)
        m_ref[b] = jnp.concatenate(m_rows[b * N_HEADS:(b + 1) * N_HEADS], axis=0)


def _mlstm_prompt(q, k, v, o, gates, bg, wmh, *, L):
    b, t, dm = q.shape
    act = pl.BlockSpec((b, L, dm), lambda c: (0, c, 0))
    whole = lambda *shape: pl.BlockSpec(shape, lambda c: (0,) * len(shape))
    return pl.pallas_call(
        functools.partial(_mlstm_chunk_kernel, B=b, L=L),
        grid=(t // L,),
        in_specs=[act, act, act, act,
                  pl.BlockSpec((b, L, GATE_LANES), lambda c: (0, c, 0)),
                  whole(1, GATE_LANES), whole(1, dm)],
        out_specs=[act,
                   whole(b, N_HEADS, HEAD_DIM, HEAD_DIM),
                   whole(b, N_HEADS, HEAD_DIM),
                   whole(b, N_HEADS, GATE_LANES)],
        out_shape=[jax.ShapeDtypeStruct((b, t, dm), BF16),
                   jax.ShapeDtypeStruct((b, N_HEADS, HEAD_DIM, HEAD_DIM), F32),
                   jax.ShapeDtypeStruct((b, N_HEADS, HEAD_DIM), F32),
                   jax.ShapeDtypeStruct((b, N_HEADS, GATE_LANES), F32)],
        compiler_params=_cparams(("arbitrary",)),
        name="mlstm_prompt",
    )(q, k, v, o, gates, bg, wmh)


SLAB = 16


def _mlstm_step_kernel(q_ref, k_ref, v_ref, o_ref, g_ref, bg_ref, mrow_ref, nrow_ref,
                       c0_ref, n0_ref, wmh_ref,
                       y_ref, c_ref, n_ref, m_ref, *, bb, T):
    R = bb * T
    per_slab = SLAB // T
    t = lax.rem(lax.broadcasted_iota(jnp.int32, (R, GATE_LANES), 0), T)

    def down(x, d):
        return pltpu.roll(x, d, 0)

    def up(x, d):
        return pltpu.roll(x, x.shape[0] - d, 0)

    def seg_last(x):
        out = x
        for d in range(1, T):
            out = jnp.where(t == T - 1 - d, up(x, d), out)
        return out

    g = g_ref[...] + bg_ref[...]
    li = pltpu.roll(g, N_HEADS, 1)
    lf = _log_sigmoid(g)
    b = lf
    for d in range(1, T):
        b = b + jnp.where(t >= d, down(lf, d), 0.0)
    m_prev = mrow_ref[...]
    a = b + m_prev
    dl = [li] + [jnp.where(t >= d, b - down(b, d) + down(li, d), NEG) for d in range(1, T)]
    mt = a
    for d in range(T):
        mt = jnp.maximum(mt, dl[d])
    dw = [jnp.exp(dl[d] - mt) for d in range(T)]
    inter = jnp.exp(a - mt)
    emt = jnp.exp(-mt)
    m_new = seg_last(mt)
    b_last = seg_last(b)
    wc = jnp.exp(b_last - b + li - m_new)
    dc = jnp.exp(b_last + m_prev - m_new)

    row_s = lax.broadcasted_iota(jnp.int32, (SLAB, HEAD_DIM), 0)
    row_r = lax.broadcasted_iota(jnp.int32, (R, HEAD_DIM), 0)

    for h in range(N_HEADS):
        hs = slice(h * HEAD_DIM, (h + 1) * HEAD_DIM)
        ln = N_HEADS + h

        def col(x):
            return x[:, ln:ln + 1]

        qb = q_ref[:, hs]
        kb = k_ref[:, hs]
        vb = v_ref[:, hs]
        qf = qb.astype(F32)
        kf = kb.astype(F32)
        vf = vb.astype(F32)
        num = jnp.zeros((R, HEAD_DIM), F32)
        den = jnp.zeros((R, 1), F32)
        for d in range(T):
            kd = kf if d == 0 else down(kf, d)
            vd = vf if d == 0 else down(vf, d)
            sw = jnp.sum(qf * kd, axis=1, keepdims=True) * col(dw[d])
            num = num + sw * vd
            den = den + sw

        qc_slabs = []
        for si in range(R // SLAB):
            q16 = qb[si * SLAB:(si + 1) * SLAB]
            acc = jnp.zeros((SLAB, HEAD_DIM), F32)
            for bl in range(per_slab):
                bi = si * per_slab + bl
                r = _dot(q16, c0_ref[bi, h].astype(BF16))
                acc = jnp.where(row_s // T == bl, r, acc)
            qc_slabs.append(acc)
        qc = jnp.concatenate(qc_slabs, axis=0)
        qn = jnp.sum(qf * nrow_ref[:, hs], axis=1, keepdims=True)
        num = num + col(inter) * qc
        den = den + col(inter) * qn
        hh = num / jnp.maximum(jnp.abs(den), col(emt))
        y_ref[:, hs] = _head_out(hh, o_ref[:, hs], wmh_ref[:, hs])

        wk = kf * col(wc)
        vw = vf * col(wc)
        for si in range(R // SLAB):
            k16 = kb[si * SLAB:(si + 1) * SLAB]
            vw16 = vw[si * SLAB:(si + 1) * SLAB]
            for bl in range(per_slab):
                bi = si * per_slab + bl
                last = bi * T + T - 1
                vwb = jnp.where(row_s // T == bl, vw16, 0.0).astype(BF16)
                dcb = dc[last:last + 1, ln:ln + 1]
                c_ref[bi, h] = dcb * c0_ref[bi, h] + lax.dot_general(
                    k16, vwb, (((0,), (0,)), ((), ())), preferred_element_type=F32)
                n_ref[bi, h:h + 1, :] = dcb * n0_ref[bi, h:h + 1, :] + jnp.sum(
                    jnp.where(row_r // T == bi, wk, 0.0), axis=0, keepdims=True)
                m_ref[bi, h:h + 1, :] = jnp.broadcast_to(
                    m_new[last:last + 1, ln:ln + 1], (1, GATE_LANES))


def _mlstm_sample(q, k, v, o, gates, bg, mrow, nrow, c0, n0, wmh, *, bb, T):
    rows, dm = q.shape
    nb = c0.shape[0]
    R = bb * T
    act = pl.BlockSpec((R, dm), lambda i: (i, 0))
    gat = pl.BlockSpec((R, GATE_LANES), lambda i: (i, 0))
    cspec = pl.BlockSpec((bb, N_HEADS, HEAD_DIM, HEAD_DIM), lambda i: (i, 0, 0, 0))
    nspec = pl.BlockSpec((bb, N_HEADS, HEAD_DIM), lambda i: (i, 0, 0))
    return pl.pallas_call(
        functools.partial(_mlstm_step_kernel, bb=bb, T=T),
        grid=(nb // bb,),
        in_specs=[act, act, act, act, gat,
                  pl.BlockSpec((1, GATE_LANES), lambda i: (0, 0)),
                  gat,
                  pl.BlockSpec((R, dm), lambda i: (i, 0)),
                  cspec, nspec,
                  pl.BlockSpec((1, dm), lambda i: (0, 0))],
        out_specs=[act, cspec, nspec,
                   pl.BlockSpec((bb, N_HEADS, GATE_LANES), lambda i: (i, 0, 0))],
        out_shape=[jax.ShapeDtypeStruct((rows, dm), BF16),
                   jax.ShapeDtypeStruct(c0.shape, F32),
                   jax.ShapeDtypeStruct(n0.shape, F32),
                   jax.ShapeDtypeStruct((nb, N_HEADS, GATE_LANES), F32)],
        compiler_params=_cparams(("arbitrary",)),
        name="mlstm_sample",
    )(q, k, v, o, gates, bg, mrow, nrow, c0, n0, wmh)


def _outproj_kernel(yc_ref, ym_ref, w_ref, x_ref, g1_ref, lg_ref, lb_ref, o_ref, *, alpha, splits):
    dc = yc_ref.shape[-1]
    rs = x_ref.shape[0] // splits
    for i in range(splits):
        r = slice(i * rs, (i + 1) * rs)
        g1 = g1_ref[0] if g1_ref.shape[1] == 1 else g1_ref[0, r, :]
        mix = _dot(yc_ref[0, r, :], w_ref[0:dc, :]) + _dot(ym_ref[0, r, :], w_ref[dc:, :])
        o_ref[r, :] = _ln(alpha * x_ref[r, :] + (1.0 + g1) * mix) * lg_ref[...] + lb_ref[...]


def _outproj(yc, ym, w_out, x, mod, ln_g, ln_b, *, tm, tpb, alpha, splits=2):
    rows, d = x.shape
    dc = yc.shape[-1]
    dm = ym.shape[-1]
    r = 1 if mod.shape[1] == 1 else tm
    vec = pl.BlockSpec((1, d), lambda m: (0, 0))
    return pl.pallas_call(
        functools.partial(_outproj_kernel, alpha=alpha, splits=splits),
        grid=(rows // tm,),
        in_specs=[pl.BlockSpec((1, tm, dc), lambda m: (m // tpb, m % tpb, 0)),
                  pl.BlockSpec((1, tm, dm), lambda m: (m // tpb, m % tpb, 0)),
                  pl.BlockSpec((dc + dm, d), lambda m: (0, 0)),
                  pl.BlockSpec((tm, d), lambda m: (m, 0)),
                  pl.BlockSpec((1, r, d), lambda m: (m // tpb, 0, 2)),
                  vec, vec],
        out_specs=pl.BlockSpec((tm, d), lambda m: (m, 0)),
        out_shape=jax.ShapeDtypeStruct((rows, d), F32),
        compiler_params=_cparams(("arbitrary",)),
        name="outproj",
    )(yc, ym, w_out, x, mod, ln_g, ln_b)


def _ffn_kernel(*refs, tpb, period, sample, alpha, splits):
    if sample:
        (x_ref, sh_ref, sc_ref, g2_ref, wa_ref, wg_ref, wconv_ref, wd_ref, lg_ref, lb_ref,
         s0_ref, s1_ref, y_ref, t0_ref, t1_ref, wa_out, wg_out, wd_out, u_scr, z_scr, y_scr) = refs
    else:
        (x_ref, sh_ref, sc_ref, g2_ref, wa_ref, wg_ref, wconv_ref, wd_ref, lg_ref, lb_ref,
         y_ref, at_ref, u_scr, carry_scr) = refs
    m = pl.program_id(0)
    f = pl.program_id(1)
    nf = pl.num_programs(1)

    @pl.when(f == 0)
    def _():
        u = _ln(x_ref[...]) * (1.0 + sc_ref[0]) + sh_ref[0]
        u_scr[...] = u.astype(BF16)
        y_ref[...] = jnp.zeros_like(y_ref)

    if sample:
        wa = wa_ref[...].astype(BF16)
        wg = wg_ref[...].astype(BF16)
        wd = wd_ref[...].astype(BF16)
        wa_out[...] = wa
        wg_out[...] = wg
        wd_out[...] = wd
        u = u_scr[...]
        a = _dot(u, wa)
        ac = _conv3_sequences(a, wconv_ref[...], s0_ref, s1_ref, z_scr, y_scr, t0_ref, t1_ref, period)
        hcur = (ac * _sigmoid(ac) * _dot(u, wg)).astype(BF16)
        y_ref[...] += _dot(hcur, wd)
    else:
        prev = _carried_rows(carry_scr, f, lax.rem(m, tpb) == 0)
        rs = u_scr.shape[0] // splits
        for i in range(splits):
            r = slice(i * rs, (i + 1) * rs)
            u = u_scr[r, :]
            ac, prev = _conv3_rows(_dot(u, wa_ref[...]), wconv_ref[...], prev)
            hcur = (ac * _sigmoid(ac) * _dot(u, wg_ref[...])).astype(BF16)
            y_ref[r, :] += _dot(hcur, wd_ref[...])
        carry_scr[f] = prev
        at_ref[0] = prev

    @pl.when(f == nf - 1)
    def _():
        y_ref[...] = (_ln(alpha * x_ref[...] + (1.0 + g2_ref[0]) * y_ref[...])
                      * lg_ref[...] + lb_ref[...])


def _ffn(x, mod, w_up, w_conv, w_down, ln_g, ln_b, *, tm, tpb, tf, sample, alpha, period=0,
         splits=1, s0=None, s1=None):
    rows, d = x.shape
    ff = w_down.shape[0]
    nf = ff // tf
    nm = rows // tm
    r = tm if sample else 1
    vec = pl.BlockSpec((1, d), lambda m, f: (0, 0))
    w_a, w_g = (w_up, w_up) if sample else w_up
    up = pl.BlockSpec((d, tf), lambda m, f: (0, f))
    down = pl.BlockSpec((tf, d), lambda m, f: (f, 0))
    in_specs = [
        pl.BlockSpec((tm, d), lambda m, f: (m, 0), pipeline_mode=pl.Buffered(1)),
        pl.BlockSpec((1, r, d), lambda m, f: (m // tpb, 0, 3)),
        pl.BlockSpec((1, r, d), lambda m, f: (m // tpb, 0, 4)),
        pl.BlockSpec((1, r, d), lambda m, f: (m // tpb, 0, 5)),
        up,
        pl.BlockSpec((d, tf), lambda m, f: (0, nf + f)) if sample else up,
        pl.BlockSpec((CONV_K, tf), lambda m, f: (0, f)),
        down,
        vec, vec,
    ]
    args = [x, mod, mod, mod, w_a, w_g, w_conv, w_down, ln_g, ln_b]
    scratch = [pltpu.VMEM((tm, d), BF16)]
    out_specs = [pl.BlockSpec((tm, d), lambda m, f: (m, 0))]
    out_shape = [jax.ShapeDtypeStruct((rows, d), F32)]
    if sample:
        assert nm == 1
        st = pl.BlockSpec((tm // period, tf), lambda m, f: (0, f))
        in_specs += [st, st]
        args += [s0, s1]
        out_specs += [st, st, up, up, down]
        out_shape += [jax.ShapeDtypeStruct((tm // period, ff), F32)] * 2 + [
            jax.ShapeDtypeStruct((d, ff), BF16), jax.ShapeDtypeStruct((d, ff), BF16),
            jax.ShapeDtypeStruct((ff, d), BF16)]
        scratch += [pltpu.VMEM((tf // LANES, tm, LANES), F32)] * 2
    else:
        out_specs.append(pl.BlockSpec((1, STATE_ROWS, tf), lambda m, f: (m, 0, f)))
        out_shape.append(jax.ShapeDtypeStruct((nm, STATE_ROWS, ff), F32))
        scratch.append(pltpu.VMEM((nf, STATE_ROWS, tf), F32))
    return pl.pallas_call(
        functools.partial(_ffn_kernel, tpb=tpb, period=period, sample=sample, alpha=alpha,
                          splits=splits),
        grid=(nm, nf),
        in_specs=in_specs,
        out_specs=out_specs,
        out_shape=out_shape,
        scratch_shapes=scratch,
        compiler_params=_cparams(("arbitrary", "arbitrary")),
        name="ffn_sample" if sample else "ffn_prompt",
    )(*args)


def _layer_prompt(x, mod, wts, ffn_w, *, alpha):
    B, T, D = x.shape
    (w_in, w_gate, bg, w_conv, wmh, w_out, ln1_g, ln1_b, _, w_fconv, _, ln2_g, ln2_b) = wts
    w_a, w_g, w_down = ffn_w
    tm = 1024
    tpb = T // tm
    x2 = x.reshape(B * T, D)
    yc, q, k, v, o, gates, ztail = _inproj(x2, mod, w_in, w_gate, w_conv, tm=tm, tpb=tpb, sample=False)
    ym, C, n, m = _mlstm_prompt(q, k, v, o, gates, bg, wmh, L=128)
    tm2 = 512
    x1 = _outproj(yc, ym, w_out, x2, mod, ln1_g, ln1_b, tm=tm2, tpb=T // tm2, alpha=alpha)
    y, atail = _ffn(x1, mod, (w_a, w_g), w_fconv, w_down, ln2_g, ln2_b,
                    tm=tm, tpb=tpb, tf=512, sample=False, alpha=alpha, splits=2)
    return (y.reshape(B, T, D), ztail[tpb - 1::tpb, STATE_ROWS - 2:], C, n, m[..., 0],
            atail[tpb - 1::tpb, STATE_ROWS - 2:])


def _layer_sample(x, mod, conv_buf, C0, n0, m0, ffn_buf, wts, *, alpha):
    B, T, D = x.shape
    (w_in, w_gate, bg, w_conv, wmh, w_out, ln1_g, ln1_b, w_up, w_fconv, w_down, ln2_g, ln2_b) = wts
    rows = B * T
    x2 = x.reshape(rows, D)
    yc, q, k, v, o, gates, z0, z1 = _inproj(x2, mod, w_in, w_gate, w_conv, tm=rows, tpb=1, sample=True,
                                            period=T, s0=conv_buf[:, 0], s1=conv_buf[:, 1])
    mrow = jnp.pad(jnp.repeat(m0, T, axis=0), ((0, 0), (N_HEADS, GATE_LANES - 2 * N_HEADS)))
    nrow = jnp.repeat(n0.reshape(B, N_HEADS * HEAD_DIM), T, axis=0)
    ym, C, n, m = _mlstm_sample(q[0], k[0], v[0], o[0], gates[0], bg, mrow, nrow, C0, n0, wmh, bb=8, T=T)
    x1 = _outproj(yc, ym[None], w_out, x2, mod, ln1_g, ln1_b, tm=rows, tpb=1, alpha=alpha)
    y, a0, a1, *ffn_w = _ffn(x1, mod, w_up, w_fconv, w_down, ln2_g, ln2_b, tm=rows, tpb=1, tf=256,
                             sample=True, alpha=alpha, period=T, s0=ffn_buf[:, 0], s1=ffn_buf[:, 1])
    return (y.reshape(B, T, D), jnp.stack([z0, z1], axis=1), C, n, m[..., 0],
            jnp.stack([a0, a1], axis=1), ffn_w)


def kernel(x_prompt, x_sample, c_prompt, c_sample, state_conv, state_mlstm_C, state_mlstm_n,
           state_mlstm_m, state_ffn_conv, w_ada, b_ada, w_in, b_gate, w_conv, w_mh_norm, w_out,
           ln1_g, ln1_b, w_up, w_ffn_conv, w_down, ln2_g, ln2_b):
    depth = w_in.shape[0]
    alpha = (2 * depth) ** 0.25
    Bp = x_prompt.shape[0]
    Bs, Ts, D = x_sample.shape
    dc = w_conv.shape[-1]
    dm = w_mh_norm.shape[-1]
    n_main = 3 * dc + 4 * dm
    assert dc == dm == N_HEADS * HEAD_DIM and Ts >= CONV_K - 1 and SLAB % Ts == 0

    xp, xs = x_prompt, x_sample
    outs_p = [[] for _ in range(5)]
    outs_s = [[] for _ in range(5)]
    for l in range(depth):
        c_all = jnp.concatenate([jnp.repeat(c_sample, Ts, axis=0), c_prompt], axis=0)
        mod = _ada(c_all, w_ada[l], b_ada[l])
        mod_s = mod.reshape(1, Bs * Ts + Bp, 6 * D)
        mod_p = mod[Bs * Ts:].reshape(Bp, 1, 6 * D)
        w_in_l = w_in[l]
        wts = (
            w_in_l.astype(BF16),
            jnp.pad(w_in_l[:, n_main:], ((0, 0), (0, GATE_LANES - 2 * N_HEADS))).astype(BF16),
            jnp.pad(b_gate[l], (0, GATE_LANES - 2 * N_HEADS)).reshape(1, GATE_LANES),
            w_conv[l],
            w_mh_norm[l].reshape(1, dm),
            w_out[l].astype(BF16),
            ln1_g[l].reshape(1, D), ln1_b[l].reshape(1, D),
            w_up[l],
            w_ffn_conv[l],
            w_down[l],
            ln2_g[l].reshape(1, D), ln2_b[l].reshape(1, D),
        )
        xs, *st_s, ffn_w = _layer_sample(xs, mod_s, state_conv[l], state_mlstm_C[l], state_mlstm_n[l],
                                         state_mlstm_m[l], state_ffn_conv[l], wts, alpha=alpha)
        xp, *st_p = _layer_prompt(xp, mod_p, wts, ffn_w, alpha=alpha)
        for acc, val in zip(outs_p, st_p):
            acc.append(val)
        for acc, val in zip(outs_s, st_s):
            acc.append(val)
    return (xp.astype(x_prompt.dtype), xs.astype(x_sample.dtype),
            *[jnp.stack(a) for a in outs_p], *[jnp.stack(a) for a in outs_s])
```

```python
import functools

import jax
import jax.numpy as jnp
from jax import lax
from jax.experimental import pallas as pl
from jax.experimental.pallas import tpu as pltpu

F32 = jnp.float32
BF16 = jnp.bfloat16

N_HEADS = 4
HEAD_DIM = 256
CONV_K = 3
LN_EPS = 1e-5
NEG = -1e30
LANES = 128
GATE_LANES = LANES
STATE_ROWS = 8
VMEM_LIMIT = 56 * 1024 * 1024


def _cparams(sem):
    return pltpu.CompilerParams(dimension_semantics=sem, vmem_limit_bytes=VMEM_LIMIT)


def _ln(x):
    mu = jnp.mean(x, axis=-1, keepdims=True)
    xc = x - mu
    var = jnp.mean(xc * xc, axis=-1, keepdims=True)
    return xc * lax.rsqrt(var + LN_EPS)


def _log_sigmoid(x):
    return jnp.minimum(x, 0.0) - jnp.log1p(jnp.exp(-jnp.abs(x)))


def _sigmoid(x):
    return 1.0 / (1.0 + jnp.exp(-x))


def _dot(a, b):
    return jnp.dot(a, b, preferred_element_type=F32)


def _conv3_rows(z, w, prev):
    p0 = prev[STATE_ROWS - 2:STATE_ROWS - 1]
    p1 = prev[STATE_ROWS - 1:STATE_ROWS]
    t = lax.broadcasted_iota(jnp.int32, z.shape, 0)
    z1 = jnp.where(t >= 1, pltpu.roll(z, 1, 0), p1)
    z2 = jnp.where(t >= 2, pltpu.roll(z, 2, 0), jnp.where(t == 0, p0, p1))
    return w[0:1] * z2 + w[1:2] * z1 + w[2:3] * z, z[z.shape[0] - STATE_ROWS:]


def _carried_rows(carry_ref, idx, first):
    @pl.when(first)
    def _():
        carry_ref[idx] = jnp.zeros(carry_ref.shape[1:], F32)

    return carry_ref[idx]


def _conv3_sequences(z, w, s0_ref, s1_ref, z_scr, y_scr, t0_ref, t1_ref, T):
    nseq = z.shape[0] // T
    y = w[0:1] * pltpu.roll(z, 2, 0) + w[1:2] * pltpu.roll(z, 1, 0) + w[2:3] * z

    def rows(t):
        return pl.ds(t, nseq, stride=T)

    cols = []
    for c in range(z.shape[1] // LANES):
        cs = slice(c * LANES, (c + 1) * LANES)
        w0, w1, w2 = w[0:1, cs], w[1:2, cs], w[2:3, cs]
        z_scr[c] = z[:, cs]
        y_scr[c] = y[:, cs]
        s0 = s0_ref[:, cs]
        s1 = s1_ref[:, cs]
        z0 = z_scr[c, rows(0), :]
        z1 = z_scr[c, rows(1), :]
        y_scr[c, rows(0), :] = w0 * s0 + w1 * s1 + w2 * z0
        y_scr[c, rows(1), :] = w0 * s1 + w1 * z0 + w2 * z1
        t0_ref[:, cs] = z_scr[c, rows(T - 2), :]
        t1_ref[:, cs] = z_scr[c, rows(T - 1), :]
        cols.append(y_scr[c])
    return jnp.concatenate(cols, axis=1)


def _ada_kernel(c_ref, w_ref, b_ref, o_ref):
    c = c_ref[...]
    s = (c * _sigmoid(c)).astype(BF16)
    o_ref[...] = _dot(s, w_ref[...].astype(BF16)) + b_ref[...]


def _ada(c, w, b, tn=1024):
    r, d = c.shape
    n = w.shape[1]
    return pl.pallas_call(
        _ada_kernel,
        grid=(n // tn,),
        in_specs=[
            pl.BlockSpec((r, d), lambda j: (0, 0)),
            pl.BlockSpec((d, tn), lambda j: (0, j)),
            pl.BlockSpec((1, tn), lambda j: (0, j)),
        ],
        out_specs=pl.BlockSpec((r, tn), lambda j: (0, j)),
        out_shape=jax.ShapeDtypeStruct((r, n), F32),
        compiler_params=_cparams(("arbitrary",)),
        name="ada",
    )(c, w, b.reshape(1, n))


def _inproj_kernel(*refs, tpb, period, sample):
    if sample:
        (x_ref, sh_ref, sc_ref, wb_ref, wc_ref, wh_ref, wq_ref, wk_ref, wv_ref, wo_ref, wg_ref,
         wconv_ref, s0_ref, s1_ref,
         yc_ref, q_ref, k_ref, v_ref, o_ref, g_ref, t0_ref, t1_ref, u_scr, z_scr, y_scr) = refs
    else:
        (x_ref, sh_ref, sc_ref, wb_ref, wc_ref, wh_ref, wq_ref, wk_ref, wv_ref, wo_ref, wg_ref,
         wconv_ref,
         yc_ref, q_ref, k_ref, v_ref, o_ref, g_ref, zt_ref, u_scr, carry_scr) = refs
    m = pl.program_id(0)
    j = pl.program_id(1)

    if not sample:
        carried = _carried_rows(carry_scr, j, lax.rem(m, tpb) == 0)
        rs = u_scr.shape[0] // 2

        def body(first):
            prev = carried
            for i in range(2):
                r = slice(i * rs, (i + 1) * rs)
                if first:
                    u = (_ln(x_ref[r, :]) * (1.0 + sc_ref[0]) + sh_ref[0]).astype(BF16)
                    u_scr[r, :] = u
                    g_ref[0, r, :] = _dot(u, wg_ref[...])
                else:
                    u = u_scr[r, :]
                z = _dot(u, wc_ref[...]) * _dot(u, wh_ref[...])
                yc, prev = _conv3_rows(z, wconv_ref[...], prev)
                yc_ref[0, r, :] = (_dot(u, wb_ref[...]) * yc).astype(BF16)
                q_ref[0, r, :] = _dot(u, wq_ref[...]).astype(BF16)
                k_ref[0, r, :] = (_dot(u, wk_ref[...]) * (HEAD_DIM ** -0.5)).astype(BF16)
                v_ref[0, r, :] = _dot(u, wv_ref[...]).astype(BF16)
                o_ref[0, r, :] = _dot(u, wo_ref[...]).astype(BF16)
            carry_scr[j] = prev
            zt_ref[0] = prev

        pl.when(j == 0)(lambda: body(True))
        pl.when(j > 0)(lambda: body(False))
        return

    @pl.when(j == 0)
    def _():
        u = _ln(x_ref[...]) * (1.0 + sc_ref[0]) + sh_ref[0]
        ub = u.astype(BF16)
        u_scr[...] = ub
        g_ref[0] = _dot(ub, wg_ref[...])

    u = u_scr[...]
    z = _dot(u, wc_ref[...]) * _dot(u, wh_ref[...])
    yc = _conv3_sequences(z, wconv_ref[...], s0_ref, s1_ref, z_scr, y_scr, t0_ref, t1_ref, period)
    half = u.shape[0] // 2
    bg = jnp.concatenate([_dot(u[:half], wb_ref[...]), _dot(u[half:], wb_ref[...])], axis=0)
    yc_ref[0] = (bg * yc).astype(BF16)
    q_ref[0] = _dot(u, wq_ref[...]).astype(BF16)
    k_ref[0] = (_dot(u, wk_ref[...]) * (HEAD_DIM ** -0.5)).astype(BF16)
    v_ref[0] = _dot(u, wv_ref[...]).astype(BF16)
    o_ref[0] = _dot(u, wo_ref[...]).astype(BF16)


def _inproj(x, mod, w_in, w_gate, w_conv, *, tm, tpb, sample, period=0, s0=None, s1=None):
    rows, d = x.shape
    dc = w_conv.shape[1]
    tn = HEAD_DIM
    nj = dc // tn
    nm = rows // tm
    nseq = nm // tpb
    r = tm if sample else 1

    def wspec(off):
        return pl.BlockSpec((d, tn), lambda m, j, off=off: (0, off * nj + j))

    in_specs = [
        pl.BlockSpec((tm, d), lambda m, j: (m, 0)),
        pl.BlockSpec((1, r, d), lambda m, j: (m // tpb, 0, 0)),
        pl.BlockSpec((1, r, d), lambda m, j: (m // tpb, 0, 1)),
        wspec(0), wspec(1), wspec(2), wspec(3), wspec(4), wspec(5), wspec(6),
        pl.BlockSpec((d, GATE_LANES), lambda m, j: (0, 0)),
        pl.BlockSpec((CONV_K, tn), lambda m, j: (0, j)),
    ]
    args = [x, mod, mod, w_in, w_in, w_in, w_in, w_in, w_in, w_in, w_gate, w_conv]
    scratch = [pltpu.VMEM((tm, d), BF16)]
    act = pl.BlockSpec((1, tm, tn), lambda m, j: (m // tpb, m % tpb, j))
    out_specs = [act, act, act, act, act,
                 pl.BlockSpec((1, tm, GATE_LANES), lambda m, j: (m // tpb, m % tpb, 0))]
    out_shape = [jax.ShapeDtypeStruct((nseq, tpb * tm, dc), BF16)] * 5 + [
        jax.ShapeDtypeStruct((nseq, tpb * tm, GATE_LANES), F32)]
    if sample:
        assert nm == 1
        st = pl.BlockSpec((tm // period, tn), lambda m, j: (0, j))
        in_specs += [st, st]
        args += [s0, s1]
        out_specs += [st, st]
        out_shape += [jax.ShapeDtypeStruct((tm // period, dc), F32)] * 2
        scratch += [pltpu.VMEM((tn // LANES, tm, LANES), F32)] * 2
    else:
        out_specs.append(pl.BlockSpec((1, STATE_ROWS, tn), lambda m, j: (m, 0, j)))
        out_shape.append(jax.ShapeDtypeStruct((nm, STATE_ROWS, dc), F32))
        scratch.append(pltpu.VMEM((nj, STATE_ROWS, tn), F32))
    return pl.pallas_call(
        functools.partial(_inproj_kernel, tpb=tpb, period=period, sample=sample),
        grid=(nm, nj),
        in_specs=in_specs,
        out_specs=out_specs,
        out_shape=out_shape,
        scratch_shapes=scratch,
        compiler_params=_cparams(("arbitrary", "arbitrary")),
        name="inproj_sample" if sample else "inproj_prompt",
    )(*args)


def _split3(x):
    hi = x.astype(BF16)
    r1 = x - hi.astype(F32)
    mid = r1.astype(BF16)
    lo = (r1 - mid.astype(F32)).astype(BF16)
    return hi, mid, lo


def _head_out(hh, o, wmh):
    return (_sigmoid(o.astype(F32)) * (_ln(hh) * wmh)).astype(BF16)


def _mlstm_chunk_kernel(q_ref, k_ref, v_ref, o_ref, g_ref, bg_ref, wmh_ref,
                        y_ref, c_ref, n_ref, m_ref, *, B, L):
    @pl.when(pl.program_id(0) == 0)
    def _():
        c_ref[...] = jnp.zeros_like(c_ref)
        n_ref[...] = jnp.zeros_like(n_ref)
        m_ref[...] = jnp.zeros_like(m_ref)

    row = lax.broadcasted_iota(jnp.int32, (L, L), 0)
    col = lax.broadcasted_iota(jnp.int32, (L, L), 1)
    causal = col <= row
    tril = jnp.where(causal, 1.0, 0.0).astype(BF16)

    heads = [(b, h) for b in range(B) for h in range(N_HEADS)]

    def hsl(h):
        return slice(h * HEAD_DIM, (h + 1) * HEAD_DIM)

    gate = []
    for b in range(B):
        g = g_ref[b] + bg_ref[...]
        hi, mid, lo = _split3(_log_sigmoid(g))
        bcum = _dot(tril, hi) + _dot(tril, mid) + _dot(tril, lo)
        gate.append((g, g.T, bcum, bcum.T, m_ref[b], n_ref[b]))

    st = []
    for b, h in heads:
        g, g_t, bcum, bcum_t, m_all, _ = gate[b]
        bc = bcum[:, N_HEADS + h:N_HEADS + h + 1]
        br = bcum_t[N_HEADS + h:N_HEADS + h + 1, :]
        m_prev = m_all[h:h + 1, 0:1]
        a = bc + m_prev
        dlog = jnp.where(causal, bc - br + g_t[h:h + 1, :], NEG)
        mt = jnp.maximum(a, jnp.max(dlog, axis=1, keepdims=True))
        st.append(dict(bc=bc, m_prev=m_prev, mt=mt, dw=jnp.exp(dlog - mt), inter=jnp.exp(a - mt)))

    for (b, h), e in zip(heads, st):
        e["s"] = lax.dot_general(q_ref[b, :, hsl(h)], k_ref[b, :, hsl(h)], (((1,), (1,)), ((), ())),
                                 preferred_element_type=F32) * e["dw"]

    for (b, h), e in zip(heads, st):
        qh = q_ref[b, :, hsl(h)]
        n_old = gate[b][5][h:h + 1, :]
        num = _dot(e["s"].astype(BF16), v_ref[b, :, hsl(h)]) + e["inter"] * _dot(
            qh, c_ref[b, h].astype(BF16))
        qn = jnp.sum(qh.astype(F32) * n_old, axis=1, keepdims=True)
        den = jnp.sum(e["s"], axis=1, keepdims=True) + e["inter"] * qn
        hh = num / jnp.maximum(jnp.abs(den), jnp.exp(-e["mt"]))
        y_ref[b, :, hsl(h)] = _head_out(hh, o_ref[b, :, hsl(h)], wmh_ref[:, hsl(h)])

    m_rows, n_rows = [], []
    for (b, h), e in zip(heads, st):
        g = gate[b][0]
        kh = k_ref[b, :, hsl(h)]
        m_new = e["mt"][L - 1:L, :]
        b_last = e["bc"][L - 1:L, :]
        wc = jnp.exp(b_last - e["bc"] + g[:, h:h + 1] - m_new)
        dc = jnp.exp(b_last + e["m_prev"] - m_new)
        vw = (v_ref[b, :, hsl(h)].astype(F32) * wc).astype(BF16)
        c_ref[b, h] = dc * c_ref[b, h] + lax.dot_general(
            kh, vw, (((0,), (0,)), ((), ())), preferred_element_type=F32)
        n_rows.append(dc * gate[b][5][h:h + 1, :] + jnp.sum(kh.astype(F32) * wc, axis=0, keepdims=True))
        m_rows.append(jnp.broadcast_to(m_new, (1, GATE_LANES)))

    for b in range(B):
        n_ref[b] = jnp.concatenate(n_rows[b * N_HEADS:(b + 1) * N_HEADS], axis=0)
        m_ref[b] = jnp.concatenate(m_rows[b * N_HEADS:(b + 1) * N_HEADS], axis=0)


def _mlstm_prompt(q, k, v, o, gates, bg, wmh, *, L):
    b, t, dm = q.shape
    act = pl.BlockSpec((b, L, dm), lambda c: (0, c, 0))
    whole = lambda *shape: pl.BlockSpec(shape, lambda c: (0,) * len(shape))
    return pl.pallas_call(
        functools.partial(_mlstm_chunk_kernel, B=b, L=L),
        grid=(t // L,),
        in_specs=[act, act, act, act,
                  pl.BlockSpec((b, L, GATE_LANES), lambda c: (0, c, 0)),
                  whole(1, GATE_LANES), whole(1, dm)],
        out_specs=[act,
                   whole(b, N_HEADS, HEAD_DIM, HEAD_DIM),
                   whole(b, N_HEADS, HEAD_DIM),
                   whole(b, N_HEADS, GATE_LANES)],
        out_shape=[jax.ShapeDtypeStruct((b, t, dm), BF16),
                   jax.ShapeDtypeStruct((b, N_HEADS, HEAD_DIM, HEAD_DIM), F32),
                   jax.ShapeDtypeStruct((b, N_HEADS, HEAD_DIM), F32),
                   jax.ShapeDtypeStruct((b, N_HEADS, GATE_LANES), F32)],
        compiler_params=_cparams(("arbitrary",)),
        name="mlstm_prompt",
    )(q, k, v, o, gates, bg, wmh)


SLAB = 16


def _mlstm_step_kernel(q_ref, k_ref, v_ref, o_ref, g_ref, bg_ref, mrow_ref, nrow_ref,
                       c0_ref, n0_ref, wmh_ref,
                       y_ref, c_ref, n_ref, m_ref, *, bb, T):
    R = bb * T
    per_slab = SLAB // T
    t = lax.rem(lax.broadcasted_iota(jnp.int32, (R, GATE_LANES), 0), T)

    def down(x, d):
        return pltpu.roll(x, d, 0)

    def up(x, d):
        return pltpu.roll(x, x.shape[0] - d, 0)

    def seg_last(x):
        out = x
        for d in range(1, T):
            out = jnp.where(t == T - 1 - d, up(x, d), out)
        return out

    g = g_ref[...] + bg_ref[...]
    li = pltpu.roll(g, N_HEADS, 1)
    lf = _log_sigmoid(g)
    b = lf
    for d in range(1, T):
        b = b + jnp.where(t >= d, down(lf, d), 0.0)
    m_prev = mrow_ref[...]
    a = b + m_prev
    dl = [li] + [jnp.where(t >= d, b - down(b, d) + down(li, d), NEG) for d in range(1, T)]
    mt = a
    for d in range(T):
        mt = jnp.maximum(mt, dl[d])
    dw = [jnp.exp(dl[d] - mt) for d in range(T)]
    inter = jnp.exp(a - mt)
    emt = jnp.exp(-mt)
    m_new = seg_last(mt)
    b_last = seg_last(b)
    wc = jnp.exp(b_last - b + li - m_new)
    dc = jnp.exp(b_last + m_prev - m_new)

    row_s = lax.broadcasted_iota(jnp.int32, (SLAB, HEAD_DIM), 0)
    row_r = lax.broadcasted_iota(jnp.int32, (R, HEAD_DIM), 0)

    for h in range(N_HEADS):
        hs = slice(h * HEAD_DIM, (h + 1) * HEAD_DIM)
        ln = N_HEADS + h

        def col(x):
            return x[:, ln:ln + 1]

        qb = q_ref[:, hs]
        kb = k_ref[:, hs]
        vb = v_ref[:, hs]
        qf = qb.astype(F32)
        kf = kb.astype(F32)
        vf = vb.astype(F32)
        num = jnp.zeros((R, HEAD_DIM), F32)
        den = jnp.zeros((R, 1), F32)
        for d in range(T):
            kd = kf if d == 0 else down(kf, d)
            vd = vf if d == 0 else down(vf, d)
            sw = jnp.sum(qf * kd, axis=1, keepdims=True) * col(dw[d])
            num = num + sw * vd
            den = den + sw

        qc_slabs = []
        for si in range(R // SLAB):
            q16 = qb[si * SLAB:(si + 1) * SLAB]
            acc = jnp.zeros((SLAB, HEAD_DIM), F32)
            for bl in range(per_slab):
                bi = si * per_slab + bl
                r = _dot(q16, c0_ref[bi, h].astype(BF16))
                acc = jnp.where(row_s // T == bl, r, acc)
            qc_slabs.append(acc)
        qc = jnp.concatenate(qc_slabs, axis=0)
        qn = jnp.sum(qf * nrow_ref[:, hs], axis=1, keepdims=True)
        num = num + col(inter) * qc
        den = den + col(inter) * qn
        hh = num / jnp.maximum(jnp.abs(den), col(emt))
        y_ref[:, hs] = _head_out(hh, o_ref[:, hs], wmh_ref[:, hs])

        wk = kf * col(wc)
        vw = vf * col(wc)
        for si in range(R // SLAB):
            k16 = kb[si * SLAB:(si + 1) * SLAB]
            vw16 = vw[si * SLAB:(si + 1) * SLAB]
            for bl in range(per_slab):
                bi = si * per_slab + bl
                last = bi * T + T - 1
                vwb = jnp.where(row_s // T == bl, vw16, 0.0).astype(BF16)
                dcb = dc[last:last + 1, ln:ln + 1]
                c_ref[bi, h] = dcb * c0_ref[bi, h] + lax.dot_general(
                    k16, vwb, (((0,), (0,)), ((), ())), preferred_element_type=F32)
                n_ref[bi, h:h + 1, :] = dcb * n0_ref[bi, h:h + 1, :] + jnp.sum(
                    jnp.where(row_r // T == bi, wk, 0.0), axis=0, keepdims=True)
                m_ref[bi, h:h + 1, :] = jnp.broadcast_to(
                    m_new[last:last + 1, ln:ln + 1], (1, GATE_LANES))


def _mlstm_sample(q, k, v, o, gates, bg, mrow, nrow, c0, n0, wmh, *, bb, T):
    rows, dm = q.shape
    nb = c0.shape[0]
    R = bb * T
    act = pl.BlockSpec((R, dm), lambda i: (i, 0))
    gat = pl.BlockSpec((R, GATE_LANES), lambda i: (i, 0))
    cspec = pl.BlockSpec((bb, N_HEADS, HEAD_DIM, HEAD_DIM), lambda i: (i, 0, 0, 0))
    nspec = pl.BlockSpec((bb, N_HEADS, HEAD_DIM), lambda i: (i, 0, 0))
    return pl.pallas_call(
        functools.partial(_mlstm_step_kernel, bb=bb, T=T),
        grid=(nb // bb,),
        in_specs=[act, act, act, act, gat,
                  pl.BlockSpec((1, GATE_LANES), lambda i: (0, 0)),
                  gat,
                  pl.BlockSpec((R, dm), lambda i: (i, 0)),
                  cspec, nspec,
                  pl.BlockSpec((1, dm), lambda i: (0, 0))],
        out_specs=[act, cspec, nspec,
                   pl.BlockSpec((bb, N_HEADS, GATE_LANES), lambda i: (i, 0, 0))],
        out_shape=[jax.ShapeDtypeStruct((rows, dm), BF16),
                   jax.ShapeDtypeStruct(c0.shape, F32),
                   jax.ShapeDtypeStruct(n0.shape, F32),
                   jax.ShapeDtypeStruct((nb, N_HEADS, GATE_LANES), F32)],
        compiler_params=_cparams(("arbitrary",)),
        name="mlstm_sample",
    )(q, k, v, o, gates, bg, mrow, nrow, c0, n0, wmh)


def _outproj_kernel(yc_ref, ym_ref, w_ref, x_ref, g1_ref, lg_ref, lb_ref, o_ref, *, alpha, splits):
    dc = yc_ref.shape[-1]
    rs = x_ref.shape[0] // splits
    for i in range(splits):
        r = slice(i * rs, (i + 1) * rs)
        g1 = g1_ref[0] if g1_ref.shape[1] == 1 else g1_ref[0, r, :]
        mix = _dot(yc_ref[0, r, :], w_ref[0:dc, :]) + _dot(ym_ref[0, r, :], w_ref[dc:, :])
        o_ref[r, :] = _ln(alpha * x_ref[r, :] + (1.0 + g1) * mix) * lg_ref[...] + lb_ref[...]


def _outproj(yc, ym, w_out, x, mod, ln_g, ln_b, *, tm, tpb, alpha, splits=2):
    rows, d = x.shape
    dc = yc.shape[-1]
    dm = ym.shape[-1]
    r = 1 if mod.shape[1] == 1 else tm
    vec = pl.BlockSpec((1, d), lambda m: (0, 0))
    return pl.pallas_call(
        functools.partial(_outproj_kernel, alpha=alpha, splits=splits),
        grid=(rows // tm,),
        in_specs=[pl.BlockSpec((1, tm, dc), lambda m: (m // tpb, m % tpb, 0)),
                  pl.BlockSpec((1, tm, dm), lambda m: (m // tpb, m % tpb, 0)),
                  pl.BlockSpec((dc + dm, d), lambda m: (0, 0)),
                  pl.BlockSpec((tm, d), lambda m: (m, 0)),
                  pl.BlockSpec((1, r, d), lambda m: (m // tpb, 0, 2)),
                  vec, vec],
        out_specs=pl.BlockSpec((tm, d), lambda m: (m, 0)),
        out_shape=jax.ShapeDtypeStruct((rows, d), F32),
        compiler_params=_cparams(("arbitrary",)),
        name="outproj",
    )(yc, ym, w_out, x, mod, ln_g, ln_b)


def _ffn_kernel(*refs, nf, tpb, period, sample, alpha, splits):
    if sample:
        (x_ref, sh_ref, sc_ref, g2_ref, wa_ref, wg_ref, wconv_ref, wd_ref, lg_ref, lb_ref,
         s0_ref, s1_ref, y_ref, t0_ref, t1_ref, wa_out, wg_out, wd_out, u_scr, z_scr, y_scr) = refs
    else:
        (x_ref, sh_ref, sc_ref, g2_ref, wa_ref, wg_ref, wconv_ref, wd_ref, lg_ref, lb_ref,
         y_ref, at_ref, u_scr, carry_scr) = refs
    m = pl.program_id(0)
    f = pl.program_id(1)

    if not sample:
        carried = _carried_rows(carry_scr, f, lax.rem(m, tpb) == 0)
        rs = u_scr.shape[0] // splits

        def body(first, last):
            prev = carried
            for i in range(splits):
                r = slice(i * rs, (i + 1) * rs)
                if first:
                    u = (_ln(x_ref[r, :]) * (1.0 + sc_ref[0]) + sh_ref[0]).astype(BF16)
                    u_scr[r, :] = u
                else:
                    u = u_scr[r, :]
                ac, prev = _conv3_rows(_dot(u, wa_ref[...]), wconv_ref[...], prev)
                hcur = (ac * _sigmoid(ac) * _dot(u, wg_ref[...])).astype(BF16)
                acc = _dot(hcur, wd_ref[...])
                if not first:
                    acc = y_ref[r, :] + acc
                if last:
                    acc = (_ln(alpha * x_ref[r, :] + (1.0 + g2_ref[0]) * acc)
                           * lg_ref[...] + lb_ref[...])
                y_ref[r, :] = acc
            carry_scr[f] = prev
            at_ref[0] = prev

        if nf == 1:
            body(True, True)
        else:
            pl.when(f == 0)(lambda: body(True, False))
            if nf > 2:
                pl.when(jnp.logical_and(f > 0, f < nf - 1))(lambda: body(False, False))
            pl.when(f == nf - 1)(lambda: body(False, True))
        return

    @pl.when(f == 0)
    def _():
        u = _ln(x_ref[...]) * (1.0 + sc_ref[0]) + sh_ref[0]
        u_scr[...] = u.astype(BF16)
        y_ref[...] = jnp.zeros_like(y_ref)

    wa = wa_ref[...].astype(BF16)
    wg = wg_ref[...].astype(BF16)
    wd = wd_ref[...].astype(BF16)
    wa_out[...] = wa
    wg_out[...] = wg
    wd_out[...] = wd
    u = u_scr[...]
    a = _dot(u, wa)
    ac = _conv3_sequences(a, wconv_ref[...], s0_ref, s1_ref, z_scr, y_scr, t0_ref, t1_ref, period)
    hcur = (ac * _sigmoid(ac) * _dot(u, wg)).astype(BF16)
    y_ref[...] += _dot(hcur, wd)

    @pl.when(f == nf - 1)
    def _():
        y_ref[...] = (_ln(alpha * x_ref[...] + (1.0 + g2_ref[0]) * y_ref[...])
                      * lg_ref[...] + lb_ref[...])


def _ffn(x, mod, w_up, w_conv, w_down, ln_g, ln_b, *, tm, tpb, tf, sample, alpha, period=0,
         splits=1, s0=None, s1=None):
    rows, d = x.shape
    ff = w_down.shape[0]
    nf = ff // tf
    nm = rows // tm
    r = tm if sample else 1
    vec = pl.BlockSpec((1, d), lambda m, f: (0, 0))
    w_a, w_g = (w_up, w_up) if sample else w_up
    up = pl.BlockSpec((d, tf), lambda m, f: (0, f))
    down = pl.BlockSpec((tf, d), lambda m, f: (f, 0))
    in_specs = [
        pl.BlockSpec((tm, d), lambda m, f: (m, 0), pipeline_mode=pl.Buffered(1)),
        pl.BlockSpec((1, r, d), lambda m, f: (m // tpb, 0, 3)),
        pl.BlockSpec((1, r, d), lambda m, f: (m // tpb, 0, 4)),
        pl.BlockSpec((1, r, d), lambda m, f: (m // tpb, 0, 5)),
        up,
        pl.BlockSpec((d, tf), lambda m, f: (0, nf + f)) if sample else up,
        pl.BlockSpec((CONV_K, tf), lambda m, f: (0, f)),
        down,
        vec, vec,
    ]
    args = [x, mod, mod, mod, w_a, w_g, w_conv, w_down, ln_g, ln_b]
    scratch = [pltpu.VMEM((tm, d), BF16)]
    out_specs = [pl.BlockSpec((tm, d), lambda m, f: (m, 0))]
    out_shape = [jax.ShapeDtypeStruct((rows, d), F32)]
    if sample:
        assert nm == 1
        st = pl.BlockSpec((tm // period, tf), lambda m, f: (0, f))
        in_specs += [st, st]
        args += [s0, s1]
        out_specs += [st, st, up, up, down]
        out_shape += [jax.ShapeDtypeStruct((tm // period, ff), F32)] * 2 + [
            jax.ShapeDtypeStruct((d, ff), BF16), jax.ShapeDtypeStruct((d, ff), BF16),
            jax.ShapeDtypeStruct((ff, d), BF16)]
        scratch += [pltpu.VMEM((tf // LANES, tm, LANES), F32)] * 2
    else:
        out_specs.append(pl.BlockSpec((1, STATE_ROWS, tf), lambda m, f: (m, 0, f)))
        out_shape.append(jax.ShapeDtypeStruct((nm, STATE_ROWS, ff), F32))
        scratch.append(pltpu.VMEM((nf, STATE_ROWS, tf), F32))
    return pl.pallas_call(
        functools.partial(_ffn_kernel, nf=nf, tpb=tpb, period=period, sample=sample, alpha=alpha,
                          splits=splits),
        grid=(nm, nf),
        in_specs=in_specs,
        out_specs=out_specs,
        out_shape=out_shape,
        scratch_shapes=scratch,
        compiler_params=_cparams(("arbitrary", "arbitrary")),
        name="ffn_sample" if sample else "ffn_prompt",
    )(*args)


def _layer_prompt(x, mod, wts, ffn_w, *, alpha):
    B, T, D = x.shape
    (w_in, w_gate, bg, w_conv, wmh, w_out, ln1_g, ln1_b, _, w_fconv, _, ln2_g, ln2_b) = wts
    w_a, w_g, w_down = ffn_w
    tm = 1024
    tpb = T // tm
    x2 = x.reshape(B * T, D)
    yc, q, k, v, o, gates, ztail = _inproj(x2, mod, w_in, w_gate, w_conv, tm=tm, tpb=tpb, sample=False)
    ym, C, n, m = _mlstm_prompt(q, k, v, o, gates, bg, wmh, L=128)
    tm2 = 512
    x1 = _outproj(yc, ym, w_out, x2, mod, ln1_g, ln1_b, tm=tm2, tpb=T // tm2, alpha=alpha)
    y, atail = _ffn(x1, mod, (w_a, w_g), w_fconv, w_down, ln2_g, ln2_b,
                    tm=tm, tpb=tpb, tf=512, sample=False, alpha=alpha, splits=2)
    return (y.reshape(B, T, D), ztail[tpb - 1::tpb, STATE_ROWS - 2:], C, n, m[..., 0],
            atail[tpb - 1::tpb, STATE_ROWS - 2:])


def _layer_sample(x, mod, conv_buf, C0, n0, m0, ffn_buf, wts, *, alpha):
    B, T, D = x.shape
    (w_in, w_gate, bg, w_conv, wmh, w_out, ln1_g, ln1_b, w_up, w_fconv, w_down, ln2_g, ln2_b) = wts
    rows = B * T
    x2 = x.reshape(rows, D)
    yc, q, k, v, o, gates, z0, z1 = _inproj(x2, mod, w_in, w_gate, w_conv, tm=rows, tpb=1, sample=True,
                                            period=T, s0=conv_buf[:, 0], s1=conv_buf[:, 1])
    mrow = jnp.pad(jnp.repeat(m0, T, axis=0), ((0, 0), (N_HEADS, GATE_LANES - 2 * N_HEADS)))
    nrow = jnp.repeat(n0.reshape(B, N_HEADS * HEAD_DIM), T, axis=0)
    ym, C, n, m = _mlstm_sample(q[0], k[0], v[0], o[0], gates[0], bg, mrow, nrow, C0, n0, wmh, bb=8, T=T)
    x1 = _outproj(yc, ym[None], w_out, x2, mod, ln1_g, ln1_b, tm=rows, tpb=1, alpha=alpha)
    y, a0, a1, *ffn_w = _ffn(x1, mod, w_up, w_fconv, w_down, ln2_g, ln2_b, tm=rows, tpb=1, tf=256,
                             sample=True, alpha=alpha, period=T, s0=ffn_buf[:, 0], s1=ffn_buf[:, 1])
    return (y.reshape(B, T, D), jnp.stack([z0, z1], axis=1), C, n, m[..., 0],
            jnp.stack([a0, a1], axis=1), ffn_w)


def kernel(x_prompt, x_sample, c_prompt, c_sample, state_conv, state_mlstm_C, state_mlstm_n,
           state_mlstm_m, state_ffn_conv, w_ada, b_ada, w_in, b_gate, w_conv, w_mh_norm, w_out,
           ln1_g, ln1_b, w_up, w_ffn_conv, w_down, ln2_g, ln2_b):
    depth = w_in.shape[0]
    alpha = (2 * depth) ** 0.25
    Bp = x_prompt.shape[0]
    Bs, Ts, D = x_sample.shape
    dc = w_conv.shape[-1]
    dm = w_mh_norm.shape[-1]
    n_main = 3 * dc + 4 * dm
    assert dc == dm == N_HEADS * HEAD_DIM and Ts >= CONV_K - 1 and SLAB % Ts == 0

    xp, xs = x_prompt, x_sample
    outs_p = [[] for _ in range(5)]
    outs_s = [[] for _ in range(5)]
    for l in range(depth):
        c_all = jnp.concatenate([jnp.repeat(c_sample, Ts, axis=0), c_prompt], axis=0)
        mod = _ada(c_all, w_ada[l], b_ada[l])
        mod_s = mod.reshape(1, Bs * Ts + Bp, 6 * D)
        mod_p = mod[Bs * Ts:].reshape(Bp, 1, 6 * D)
        w_in_l = w_in[l]
        wts = (
            w_in_l.astype(BF16),
            jnp.pad(w_in_l[:, n_main:], ((0, 0), (0, GATE_LANES - 2 * N_HEADS))).astype(BF16),
            jnp.pad(b_gate[l], (0, GATE_LANES - 2 * N_HEADS)).reshape(1, GATE_LANES),
            w_conv[l],
            w_mh_norm[l].reshape(1, dm),
            w_out[l].astype(BF16),
            ln1_g[l].reshape(1, D), ln1_b[l].reshape(1, D),
            w_up[l],
            w_ffn_conv[l],
            w_down[l],
            ln2_g[l].reshape(1, D), ln2_b[l].reshape(1, D),
        )
        xs, *st_s, ffn_w = _layer_sample(xs, mod_s, state_conv[l], state_mlstm_C[l], state_mlstm_n[l],
                                         state_mlstm_m[l], state_ffn_conv[l], wts, alpha=alpha)
        xp, *st_p = _layer_prompt(xp, mod_p, wts, ffn_w, alpha=alpha)
        for acc, val in zip(outs_p, st_p):
            acc.append(val)
        for acc, val in zip(outs_s, st_s):
            acc.append(val)
    return (xp.astype(x_prompt.dtype), xs.astype(x_sample.dtype),
            *[jnp.stack(a) for a in outs_p], *[jnp.stack(a) for a in outs_s])
```

```python
import functools

import jax
import jax.numpy as jnp
from jax import lax
from jax.experimental import pallas as pl
from jax.experimental.pallas import tpu as pltpu

F32 = jnp.float32
BF16 = jnp.bfloat16

N_HEADS = 4
HEAD_DIM = 256
CONV_K = 3
LN_EPS = 1e-5
NEG = -1e30
LANES = 128
GATE_LANES = LANES
STATE_ROWS = 8
VMEM_LIMIT = 56 * 1024 * 1024


def _cparams(sem):
    return pltpu.CompilerParams(dimension_semantics=sem, vmem_limit_bytes=VMEM_LIMIT)


def _ln(x):
    mu = jnp.mean(x, axis=-1, keepdims=True)
    xc = x - mu
    var = jnp.mean(xc * xc, axis=-1, keepdims=True)
    return xc * lax.rsqrt(var + LN_EPS)


def _log_sigmoid(x):
    return jnp.minimum(x, 0.0) - jnp.log1p(jnp.exp(-jnp.abs(x)))


def _sigmoid(x):
    return 1.0 / (1.0 + jnp.exp(-x))


def _dot(a, b):
    return jnp.dot(a, b, preferred_element_type=F32)


def _conv3_rows(z, w, prev):
    p0 = prev[STATE_ROWS - 2:STATE_ROWS - 1]
    p1 = prev[STATE_ROWS - 1:STATE_ROWS]
    t = lax.broadcasted_iota(jnp.int32, z.shape, 0)
    z1 = jnp.where(t >= 1, pltpu.roll(z, 1, 0), p1)
    z2 = jnp.where(t >= 2, pltpu.roll(z, 2, 0), jnp.where(t == 0, p0, p1))
    return w[0:1] * z2 + w[1:2] * z1 + w[2:3] * z, z[z.shape[0] - STATE_ROWS:]


def _carried_rows(carry_ref, idx, first):
    @pl.when(first)
    def _():
        carry_ref[idx] = jnp.zeros(carry_ref.shape[1:], F32)

    return carry_ref[idx]


def _conv3_sequences(z, w, s0_ref, s1_ref, z_scr, y_scr, t0_ref, t1_ref, T):
    nseq = z.shape[0] // T
    y = w[0:1] * pltpu.roll(z, 2, 0) + w[1:2] * pltpu.roll(z, 1, 0) + w[2:3] * z

    def rows(t):
        return pl.ds(t, nseq, stride=T)

    cols = []
    for c in range(z.shape[1] // LANES):
        cs = slice(c * LANES, (c + 1) * LANES)
        w0, w1, w2 = w[0:1, cs], w[1:2, cs], w[2:3, cs]
        z_scr[c] = z[:, cs]
        y_scr[c] = y[:, cs]
        s0 = s0_ref[:, cs]
        s1 = s1_ref[:, cs]
        z0 = z_scr[c, rows(0), :]
        z1 = z_scr[c, rows(1), :]
        y_scr[c, rows(0), :] = w0 * s0 + w1 * s1 + w2 * z0
        y_scr[c, rows(1), :] = w0 * s1 + w1 * z0 + w2 * z1
        t0_ref[:, cs] = z_scr[c, rows(T - 2), :]
        t1_ref[:, cs] = z_scr[c, rows(T - 1), :]
        cols.append(y_scr[c])
    return jnp.concatenate(cols, axis=1)


def _ada_kernel(c_ref, w_ref, b_ref, o_ref):
    c = c_ref[...]
    s = (c * _sigmoid(c)).astype(BF16)
    o_ref[...] = _dot(s, w_ref[...].astype(BF16)) + b_ref[...]


def _ada(c, w, b, tn=1024):
    r, d = c.shape
    n = w.shape[1]
    return pl.pallas_call(
        _ada_kernel,
        grid=(n // tn,),
        in_specs=[
            pl.BlockSpec((r, d), lambda j: (0, 0)),
            pl.BlockSpec((d, tn), lambda j: (0, j)),
            pl.BlockSpec((1, tn), lambda j: (0, j)),
        ],
        out_specs=pl.BlockSpec((r, tn), lambda j: (0, j)),
        out_shape=jax.ShapeDtypeStruct((r, n), F32),
        compiler_params=_cparams(("arbitrary",)),
        name="ada",
    )(c, w, b.reshape(1, n))


def _inproj_kernel(*refs, tpb, period, sample):
    if sample:
        (x_ref, sh_ref, sc_ref, wb_ref, wc_ref, wh_ref, wq_ref, wk_ref, wv_ref, wo_ref, wg_ref,
         wconv_ref, s0_ref, s1_ref,
         yc_ref, q_ref, k_ref, v_ref, o_ref, g_ref, t0_ref, t1_ref, u_scr, z_scr, y_scr) = refs
    else:
        (x_ref, sh_ref, sc_ref, wb_ref, wc_ref, wh_ref, wq_ref, wk_ref, wv_ref, wo_ref, wg_ref,
         wconv_ref,
         yc_ref, q_ref, k_ref, v_ref, o_ref, g_ref, zt_ref, u_scr, carry_scr) = refs
    m = pl.program_id(0)
    j = pl.program_id(1)

    if not sample:
        carried = _carried_rows(carry_scr, j, lax.rem(m, tpb) == 0)
        rs = u_scr.shape[0] // 2

        def body(first):
            prev = carried
            for i in range(2):
                r = slice(i * rs, (i + 1) * rs)
                if first:
                    u = (_ln(x_ref[r, :]) * (1.0 + sc_ref[0]) + sh_ref[0]).astype(BF16)
                    u_scr[r, :] = u
                    g_ref[0, r, :] = _dot(u, wg_ref[...])
                else:
                    u = u_scr[r, :]
                z = _dot(u, wc_ref[...]) * _dot(u, wh_ref[...])
                yc, prev = _conv3_rows(z, wconv_ref[...], prev)
                yc_ref[0, r, :] = (_dot(u, wb_ref[...]) * yc).astype(BF16)
                q_ref[0, r, :] = _dot(u, wq_ref[...]).astype(BF16)
                k_ref[0, r, :] = (_dot(u, wk_ref[...]) * (HEAD_DIM ** -0.5)).astype(BF16)
                v_ref[0, r, :] = _dot(u, wv_ref[...]).astype(BF16)
                o_ref[0, r, :] = _dot(u, wo_ref[...]).astype(BF16)
            carry_scr[j] = prev
            zt_ref[0] = prev

        pl.when(j == 0)(lambda: body(True))
        pl.when(j > 0)(lambda: body(False))
        return

    @pl.when(j == 0)
    def _():
        u = _ln(x_ref[...]) * (1.0 + sc_ref[0]) + sh_ref[0]
        ub = u.astype(BF16)
        u_scr[...] = ub
        g_ref[0] = _dot(ub, wg_ref[...])

    u = u_scr[...]
    z = _dot(u, wc_ref[...]) * _dot(u, wh_ref[...])
    yc = _conv3_sequences(z, wconv_ref[...], s0_ref, s1_ref, z_scr, y_scr, t0_ref, t1_ref, period)
    half = u.shape[0] // 2
    bg = jnp.concatenate([_dot(u[:half], wb_ref[...]), _dot(u[half:], wb_ref[...])], axis=0)
    yc_ref[0] = (bg * yc).astype(BF16)
    q_ref[0] = _dot(u, wq_ref[...]).astype(BF16)
    k_ref[0] = (_dot(u, wk_ref[...]) * (HEAD_DIM ** -0.5)).astype(BF16)
    v_ref[0] = _dot(u, wv_ref[...]).astype(BF16)
    o_ref[0] = _dot(u, wo_ref[...]).astype(BF16)


def _inproj(x, mod, w_in, w_gate, w_conv, *, tm, tpb, sample, period=0, s0=None, s1=None):
    rows, d = x.shape
    dc = w_conv.shape[1]
    tn = HEAD_DIM
    nj = dc // tn
    nm = rows // tm
    nseq = nm // tpb
    r = tm if sample else 1

    def wspec(off):
        return pl.BlockSpec((d, tn), lambda m, j, off=off: (0, off * nj + j))

    in_specs = [
        pl.BlockSpec((tm, d), lambda m, j: (m, 0)),
        pl.BlockSpec((1, r, d), lambda m, j: (m // tpb, 0, 0)),
        pl.BlockSpec((1, r, d), lambda m, j: (m // tpb, 0, 1)),
        wspec(0), wspec(1), wspec(2), wspec(3), wspec(4), wspec(5), wspec(6),
        pl.BlockSpec((d, GATE_LANES), lambda m, j: (0, 0)),
        pl.BlockSpec((CONV_K, tn), lambda m, j: (0, j)),
    ]
    args = [x, mod, mod, w_in, w_in, w_in, w_in, w_in, w_in, w_in, w_gate, w_conv]
    scratch = [pltpu.VMEM((tm, d), BF16)]
    act = pl.BlockSpec((1, tm, tn), lambda m, j: (m // tpb, m % tpb, j))
    out_specs = [act, act, act, act, act,
                 pl.BlockSpec((1, tm, GATE_LANES), lambda m, j: (m // tpb, m % tpb, 0))]
    out_shape = [jax.ShapeDtypeStruct((nseq, tpb * tm, dc), BF16)] * 5 + [
        jax.ShapeDtypeStruct((nseq, tpb * tm, GATE_LANES), F32)]
    if sample:
        assert nm == 1
        st = pl.BlockSpec((tm // period, tn), lambda m, j: (0, j))
        in_specs += [st, st]
        args += [s0, s1]
        out_specs += [st, st]
        out_shape += [jax.ShapeDtypeStruct((tm // period, dc), F32)] * 2
        scratch += [pltpu.VMEM((tn // LANES, tm, LANES), F32)] * 2
    else:
        out_specs.append(pl.BlockSpec((1, STATE_ROWS, tn), lambda m, j: (m, 0, j)))
        out_shape.append(jax.ShapeDtypeStruct((nm, STATE_ROWS, dc), F32))
        scratch.append(pltpu.VMEM((nj, STATE_ROWS, tn), F32))
    return pl.pallas_call(
        functools.partial(_inproj_kernel, tpb=tpb, period=period, sample=sample),
        grid=(nm, nj),
        in_specs=in_specs,
        out_specs=out_specs,
        out_shape=out_shape,
        scratch_shapes=scratch,
        compiler_params=_cparams(("arbitrary", "arbitrary")),
        name="inproj_sample" if sample else "inproj_prompt",
    )(*args)


def _split3(x):
    hi = x.astype(BF16)
    r1 = x - hi.astype(F32)
    mid = r1.astype(BF16)
    lo = (r1 - mid.astype(F32)).astype(BF16)
    return hi, mid, lo


def _head_out(hh, o, wmh):
    return (_sigmoid(o.astype(F32)) * (_ln(hh) * wmh)).astype(BF16)


def _split2(x):
    hi = x.astype(BF16)
    return hi, (x - hi.astype(F32)).astype(BF16)


def _rowsum(x, ones):
    hi, lo = _split2(x)
    return _dot(hi, ones) + _dot(lo, ones)


def _rep2(x):
    return jnp.concatenate([x, x], axis=1)


def _mlstm_chunk_kernel(q_ref, k_ref, v_ref, o_ref, g_ref, bg_ref, wmh_ref,
                        y_ref, c_ref, n_ref, m_ref, nrep_scr, *, B, L):
    step = pl.program_id(0)

    @pl.when(step == 0)
    def _():
        c_ref[...] = jnp.zeros_like(c_ref)
        m_ref[...] = jnp.zeros_like(m_ref)
        nrep_scr[...] = jnp.zeros_like(nrep_scr)

    row = lax.broadcasted_iota(jnp.int32, (L, L), 0)
    col = lax.broadcasted_iota(jnp.int32, (L, L), 1)
    causal = col <= row
    tril = jnp.where(causal, 1.0, 0.0).astype(BF16)
    ones_l = jnp.ones((L, LANES), BF16)
    ones_d = jnp.ones((HEAD_DIM, LANES), BF16)
    inv_d = 1.0 / HEAD_DIM
    tn = (((0,), (0,)), ((), ()))

    heads = [(b, h) for b in range(B) for h in range(N_HEADS)]

    def hsl(h):
        return slice(h * HEAD_DIM, (h + 1) * HEAD_DIM)

    gate = []
    for b in range(B):
        g = g_ref[b] + bg_ref[...]
        hi, mid, lo = _split3(_log_sigmoid(g))
        bcum = _dot(tril, hi) + _dot(tril, mid) + _dot(tril, lo)
        gate.append((g, g.T, bcum, bcum.T, m_ref[b]))

    st = []
    for b, h in heads:
        g, g_t, bcum, bcum_t, m_all = gate[b]
        bc = jnp.broadcast_to(bcum[:, N_HEADS + h:N_HEADS + h + 1], (L, LANES))
        li = jnp.broadcast_to(g[:, h:h + 1], (L, LANES))
        br = bcum_t[N_HEADS + h:N_HEADS + h + 1, :]
        m_prev = m_all[h:h + 1, :]
        a = bc + m_prev
        dlog = jnp.where(causal, bc - br + g_t[h:h + 1, :], NEG)
        mt = jnp.maximum(a, jnp.max(dlog, axis=1, keepdims=True))
        st.append(dict(bc=bc, li=li, m_prev=m_prev, mt=mt, dw=jnp.exp(dlog - mt),
                       inter=jnp.exp(a - mt)))

    for (b, h), e in zip(heads, st):
        e["s"] = lax.dot_general(q_ref[b, :, hsl(h)], k_ref[b, :, hsl(h)], (((1,), (1,)), ((), ())),
                                 preferred_element_type=F32) * e["dw"]

    for i, ((b, h), e) in enumerate(zip(heads, st)):
        s_hi, s_lo = _split2(e["s"])
        cn = jnp.concatenate([c_ref[b, h], nrep_scr[i]], axis=1).astype(BF16)
        qc = _dot(q_ref[b, :, hsl(h)], cn)
        num = _dot(s_hi, v_ref[b, :, hsl(h)]) + _rep2(e["inter"]) * qc[:, :HEAD_DIM]
        den = _dot(s_hi, ones_l) + _dot(s_lo, ones_l) + e["inter"] * qc[:, HEAD_DIM:]
        rden = 1.0 / jnp.maximum(jnp.abs(den), jnp.exp(-e["mt"]))
        e["hh"] = num * _rep2(rden)

    for (b, h), e in zip(heads, st):
        hh = e["hh"]
        xc = hh - _rep2(_rowsum(hh, ones_d) * inv_d)
        rstd = lax.rsqrt(_rowsum(xc * xc, ones_d) * inv_d + LN_EPS)
        y = _sigmoid(o_ref[b, :, hsl(h)].astype(F32)) * (xc * _rep2(rstd) * wmh_ref[:, hsl(h)])
        y_ref[b, :, hsl(h)] = y.astype(BF16)

    m_rows = []
    for i, ((b, h), e) in enumerate(zip(heads, st)):
        kh = k_ref[b, :, hsl(h)]
        m_new = e["mt"][L - 1:L, :]
        b_last = e["bc"][L - 1:L, :]
        wc = jnp.exp(b_last - e["bc"] + e["li"] - m_new)
        dc = jnp.exp(b_last + e["m_prev"] - m_new)
        vw = (v_ref[b, :, hsl(h)].astype(F32) * _rep2(wc)).astype(BF16)
        c_ref[b, h] = _rep2(dc) * c_ref[b, h] + lax.dot_general(kh, vw, tn, preferred_element_type=F32)
        wc_hi, wc_lo = _split2(wc)
        nrep_scr[i] = (dc * nrep_scr[i] + lax.dot_general(kh, wc_hi, tn, preferred_element_type=F32)
                       + lax.dot_general(kh, wc_lo, tn, preferred_element_type=F32))
        m_rows.append(m_new)

    for b in range(B):
        m_ref[b] = jnp.concatenate(m_rows[b * N_HEADS:(b + 1) * N_HEADS], axis=0)

    @pl.when(step == pl.num_programs(0) - 1)
    def _():
        for b in range(B):
            n_ref[b] = jnp.concatenate(
                [nrep_scr[b * N_HEADS + h].T[0:1, :] for h in range(N_HEADS)], axis=0)


def _mlstm_prompt(q, k, v, o, gates, bg, wmh, *, L):
    b, t, dm = q.shape
    assert L == LANES
    act = pl.BlockSpec((b, L, dm), lambda c: (0, c, 0))
    whole = lambda *shape: pl.BlockSpec(shape, lambda c: (0,) * len(shape))
    return pl.pallas_call(
        functools.partial(_mlstm_chunk_kernel, B=b, L=L),
        grid=(t // L,),
        in_specs=[act, act, act, act,
                  pl.BlockSpec((b, L, GATE_LANES), lambda c: (0, c, 0)),
                  whole(1, GATE_LANES), whole(1, dm)],
        out_specs=[act,
                   whole(b, N_HEADS, HEAD_DIM, HEAD_DIM),
                   whole(b, N_HEADS, HEAD_DIM),
                   whole(b, N_HEADS, GATE_LANES)],
        out_shape=[jax.ShapeDtypeStruct((b, t, dm), BF16),
                   jax.ShapeDtypeStruct((b, N_HEADS, HEAD_DIM, HEAD_DIM), F32),
                   jax.ShapeDtypeStruct((b, N_HEADS, HEAD_DIM), F32),
                   jax.ShapeDtypeStruct((b, N_HEADS, GATE_LANES), F32)],
        scratch_shapes=[pltpu.VMEM((b * N_HEADS, HEAD_DIM, LANES), F32)],
        compiler_params=_cparams(("arbitrary",)),
        name="mlstm_prompt",
    )(q, k, v, o, gates, bg, wmh)


SLAB = 16


def _mlstm_step_kernel(q_ref, k_ref, v_ref, o_ref, g_ref, bg_ref, mrow_ref, nrow_ref,
                       c0_ref, n0_ref, wmh_ref,
                       y_ref, c_ref, n_ref, m_ref, *, bb, T):
    R = bb * T
    per_slab = SLAB // T
    t = lax.rem(lax.broadcasted_iota(jnp.int32, (R, GATE_LANES), 0), T)

    def down(x, d):
        return pltpu.roll(x, d, 0)

    def up(x, d):
        return pltpu.roll(x, x.shape[0] - d, 0)

    def seg_last(x):
        out = x
        for d in range(1, T):
            out = jnp.where(t == T - 1 - d, up(x, d), out)
        return out

    g = g_ref[...] + bg_ref[...]
    li = pltpu.roll(g, N_HEADS, 1)
    lf = _log_sigmoid(g)
    b = lf
    for d in range(1, T):
        b = b + jnp.where(t >= d, down(lf, d), 0.0)
    m_prev = mrow_ref[...]
    a = b + m_prev
    dl = [li] + [jnp.where(t >= d, b - down(b, d) + down(li, d), NEG) for d in range(1, T)]
    mt = a
    for d in range(T):
        mt = jnp.maximum(mt, dl[d])
    dw = [jnp.exp(dl[d] - mt) for d in range(T)]
    inter = jnp.exp(a - mt)
    emt = jnp.exp(-mt)
    m_new = seg_last(mt)
    b_last = seg_last(b)
    wc = jnp.exp(b_last - b + li - m_new)
    dc = jnp.exp(b_last + m_prev - m_new)

    row_s = lax.broadcasted_iota(jnp.int32, (SLAB, HEAD_DIM), 0)
    row_r = lax.broadcasted_iota(jnp.int32, (R, HEAD_DIM), 0)

    for h in range(N_HEADS):
        hs = slice(h * HEAD_DIM, (h + 1) * HEAD_DIM)
        ln = N_HEADS + h

        def col(x):
            return x[:, ln:ln + 1]

        qb = q_ref[:, hs]
        kb = k_ref[:, hs]
        vb = v_ref[:, hs]
        qf = qb.astype(F32)
        kf = kb.astype(F32)
        vf = vb.astype(F32)
        num = jnp.zeros((R, HEAD_DIM), F32)
        den = jnp.zeros((R, 1), F32)
        for d in range(T):
            kd = kf if d == 0 else down(kf, d)
            vd = vf if d == 0 else down(vf, d)
            sw = jnp.sum(qf * kd, axis=1, keepdims=True) * col(dw[d])
            num = num + sw * vd
            den = den + sw

        qc_slabs = []
        for si in range(R // SLAB):
            q16 = qb[si * SLAB:(si + 1) * SLAB]
            acc = jnp.zeros((SLAB, HEAD_DIM), F32)
            for bl in range(per_slab):
                bi = si * per_slab + bl
                r = _dot(q16, c0_ref[bi, h].astype(BF16))
                acc = jnp.where(row_s // T == bl, r, acc)
            qc_slabs.append(acc)
        qc = jnp.concatenate(qc_slabs, axis=0)
        qn = jnp.sum(qf * nrow_ref[:, hs], axis=1, keepdims=True)
        num = num + col(inter) * qc
        den = den + col(inter) * qn
        hh = num / jnp.maximum(jnp.abs(den), col(emt))
        y_ref[:, hs] = _head_out(hh, o_ref[:, hs], wmh_ref[:, hs])

        wk = kf * col(wc)
        vw = vf * col(wc)
        for si in range(R // SLAB):
            k16 = kb[si * SLAB:(si + 1) * SLAB]
            vw16 = vw[si * SLAB:(si + 1) * SLAB]
            for bl in range(per_slab):
                bi = si * per_slab + bl
                last = bi * T + T - 1
                vwb = jnp.where(row_s // T == bl, vw16, 0.0).astype(BF16)
                dcb = dc[last:last + 1, ln:ln + 1]
                c_ref[bi, h] = dcb * c0_ref[bi, h] + lax.dot_general(
                    k16, vwb, (((0,), (0,)), ((), ())), preferred_element_type=F32)
                n_ref[bi, h:h + 1, :] = dcb * n0_ref[bi, h:h + 1, :] + jnp.sum(
                    jnp.where(row_r // T == bi, wk, 0.0), axis=0, keepdims=True)
                m_ref[bi, h:h + 1, :] = jnp.broadcast_to(
                    m_new[last:last + 1, ln:ln + 1], (1, GATE_LANES))


def _mlstm_sample(q, k, v, o, gates, bg, mrow, nrow, c0, n0, wmh, *, bb, T):
    rows, dm = q.shape
    nb = c0.shape[0]
    R = bb * T
    act = pl.BlockSpec((R, dm), lambda i: (i, 0))
    gat = pl.BlockSpec((R, GATE_LANES), lambda i: (i, 0))
    cspec = pl.BlockSpec((bb, N_HEADS, HEAD_DIM, HEAD_DIM), lambda i: (i, 0, 0, 0))
    nspec = pl.BlockSpec((bb, N_HEADS, HEAD_DIM), lambda i: (i, 0, 0))
    return pl.pallas_call(
        functools.partial(_mlstm_step_kernel, bb=bb, T=T),
        grid=(nb // bb,),
        in_specs=[act, act, act, act, gat,
                  pl.BlockSpec((1, GATE_LANES), lambda i: (0, 0)),
                  gat,
                  pl.BlockSpec((R, dm), lambda i: (i, 0)),
                  cspec, nspec,
                  pl.BlockSpec((1, dm), lambda i: (0, 0))],
        out_specs=[act, cspec, nspec,
                   pl.BlockSpec((bb, N_HEADS, GATE_LANES), lambda i: (i, 0, 0))],
        out_shape=[jax.ShapeDtypeStruct((rows, dm), BF16),
                   jax.ShapeDtypeStruct(c0.shape, F32),
                   jax.ShapeDtypeStruct(n0.shape, F32),
                   jax.ShapeDtypeStruct((nb, N_HEADS, GATE_LANES), F32)],
        compiler_params=_cparams(("arbitrary",)),
        name="mlstm_sample",
    )(q, k, v, o, gates, bg, mrow, nrow, c0, n0, wmh)


def _outproj_kernel(yc_ref, ym_ref, w_ref, x_ref, g1_ref, lg_ref, lb_ref, o_ref, *, alpha, splits):
    dc = yc_ref.shape[-1]
    rs = x_ref.shape[0] // splits
    for i in range(splits):
        r = slice(i * rs, (i + 1) * rs)
        g1 = g1_ref[0] if g1_ref.shape[1] == 1 else g1_ref[0, r, :]
        mix = _dot(yc_ref[0, r, :], w_ref[0:dc, :]) + _dot(ym_ref[0, r, :], w_ref[dc:, :])
        o_ref[r, :] = _ln(alpha * x_ref[r, :] + (1.0 + g1) * mix) * lg_ref[...] + lb_ref[...]


def _outproj(yc, ym, w_out, x, mod, ln_g, ln_b, *, tm, tpb, alpha, splits=2):
    rows, d = x.shape
    dc = yc.shape[-1]
    dm = ym.shape[-1]
    r = 1 if mod.shape[1] == 1 else tm
    vec = pl.BlockSpec((1, d), lambda m: (0, 0))
    return pl.pallas_call(
        functools.partial(_outproj_kernel, alpha=alpha, splits=splits),
        grid=(rows // tm,),
        in_specs=[pl.BlockSpec((1, tm, dc), lambda m: (m // tpb, m % tpb, 0)),
                  pl.BlockSpec((1, tm, dm), lambda m: (m // tpb, m % tpb, 0)),
                  pl.BlockSpec((dc + dm, d), lambda m: (0, 0)),
                  pl.BlockSpec((tm, d), lambda m: (m, 0)),
                  pl.BlockSpec((1, r, d), lambda m: (m // tpb, 0, 2)),
                  vec, vec],
        out_specs=pl.BlockSpec((tm, d), lambda m: (m, 0)),
        out_shape=jax.ShapeDtypeStruct((rows, d), F32),
        compiler_params=_cparams(("arbitrary",)),
        name="outproj",
    )(yc, ym, w_out, x, mod, ln_g, ln_b)


def _ffn_kernel(*refs, nf, tpb, period, sample, alpha, splits):
    if sample:
        (x_ref, sh_ref, sc_ref, g2_ref, wa_ref, wg_ref, wconv_ref, wd_ref, lg_ref, lb_ref,
         s0_ref, s1_ref, y_ref, t0_ref, t1_ref, wa_out, wg_out, wd_out, u_scr, z_scr, y_scr) = refs
    else:
        (x_ref, sh_ref, sc_ref, g2_ref, wa_ref, wg_ref, wconv_ref, wd_ref, lg_ref, lb_ref,
         y_ref, at_ref, u_scr, carry_scr) = refs
    m = pl.program_id(0)
    f = pl.program_id(1)

    if not sample:
        carried = _carried_rows(carry_scr, f, lax.rem(m, tpb) == 0)
        rs = u_scr.shape[0] // splits

        def body(first, last):
            prev = carried
            for i in range(splits):
                r = slice(i * rs, (i + 1) * rs)
                if first:
                    u = (_ln(x_ref[r, :]) * (1.0 + sc_ref[0]) + sh_ref[0]).astype(BF16)
                    u_scr[r, :] = u
                else:
                    u = u_scr[r, :]
                ac, prev = _conv3_rows(_dot(u, wa_ref[...]), wconv_ref[...], prev)
                hcur = (ac * _sigmoid(ac) * _dot(u, wg_ref[...])).astype(BF16)
                acc = _dot(hcur, wd_ref[...])
                if not first:
                    acc = y_ref[r, :] + acc
                if last:
                    acc = (_ln(alpha * x_ref[r, :] + (1.0 + g2_ref[0]) * acc)
                           * lg_ref[...] + lb_ref[...])
                y_ref[r, :] = acc
            carry_scr[f] = prev
            at_ref[0] = prev

        if nf == 1:
            body(True, True)
        else:
            pl.when(f == 0)(lambda: body(True, False))
            if nf > 2:
                pl.when(jnp.logical_and(f > 0, f < nf - 1))(lambda: body(False, False))
            pl.when(f == nf - 1)(lambda: body(False, True))
        return

    @pl.when(f == 0)
    def _():
        u = _ln(x_ref[...]) * (1.0 + sc_ref[0]) + sh_ref[0]
        u_scr[...] = u.astype(BF16)
        y_ref[...] = jnp.zeros_like(y_ref)

    wa = wa_ref[...].astype(BF16)
    wg = wg_ref[...].astype(BF16)
    wd = wd_ref[...].astype(BF16)
    wa_out[...] = wa
    wg_out[...] = wg
    wd_out[...] = wd
    u = u_scr[...]
    a = _dot(u, wa)
    ac = _conv3_sequences(a, wconv_ref[...], s0_ref, s1_ref, z_scr, y_scr, t0_ref, t1_ref, period)
    hcur = (ac * _sigmoid(ac) * _dot(u, wg)).astype(BF16)
    y_ref[...] += _dot(hcur, wd)

    @pl.when(f == nf - 1)
    def _():
        y_ref[...] = (_ln(alpha * x_ref[...] + (1.0 + g2_ref[0]) * y_ref[...])
                      * lg_ref[...] + lb_ref[...])


def _ffn(x, mod, w_up, w_conv, w_down, ln_g, ln_b, *, tm, tpb, tf, sample, alpha, period=0,
         splits=1, s0=None, s1=None):
    rows, d = x.shape
    ff = w_down.shape[0]
    nf = ff // tf
    nm = rows // tm
    r = tm if sample else 1
    vec = pl.BlockSpec((1, d), lambda m, f: (0, 0))
    w_a, w_g = (w_up, w_up) if sample else w_up
    up = pl.BlockSpec((d, tf), lambda m, f: (0, f))
    down = pl.BlockSpec((tf, d), lambda m, f: (f, 0))
    in_specs = [
        pl.BlockSpec((tm, d), lambda m, f: (m, 0), pipeline_mode=pl.Buffered(1)),
        pl.BlockSpec((1, r, d), lambda m, f: (m // tpb, 0, 3)),
        pl.BlockSpec((1, r, d), lambda m, f: (m // tpb, 0, 4)),
        pl.BlockSpec((1, r, d), lambda m, f: (m // tpb, 0, 5)),
        up,
        pl.BlockSpec((d, tf), lambda m, f: (0, nf + f)) if sample else up,
        pl.BlockSpec((CONV_K, tf), lambda m, f: (0, f)),
        down,
        vec, vec,
    ]
    args = [x, mod, mod, mod, w_a, w_g, w_conv, w_down, ln_g, ln_b]
    scratch = [pltpu.VMEM((tm, d), BF16)]
    out_specs = [pl.BlockSpec((tm, d), lambda m, f: (m, 0))]
    out_shape = [jax.ShapeDtypeStruct((rows, d), F32)]
    if sample:
        assert nm == 1
        st = pl.BlockSpec((tm // period, tf), lambda m, f: (0, f))
        in_specs += [st, st]
        args += [s0, s1]
        out_specs += [st, st, up, up, down]
        out_shape += [jax.ShapeDtypeStruct((tm // period, ff), F32)] * 2 + [
            jax.ShapeDtypeStruct((d, ff), BF16), jax.ShapeDtypeStruct((d, ff), BF16),
            jax.ShapeDtypeStruct((ff, d), BF16)]
        scratch += [pltpu.VMEM((tf // LANES, tm, LANES), F32)] * 2
    else:
        out_specs.append(pl.BlockSpec((1, STATE_ROWS, tf), lambda m, f: (m, 0, f)))
        out_shape.append(jax.ShapeDtypeStruct((nm, STATE_ROWS, ff), F32))
        scratch.append(pltpu.VMEM((nf, STATE_ROWS, tf), F32))
    return pl.pallas_call(
        functools.partial(_ffn_kernel, nf=nf, tpb=tpb, period=period, sample=sample, alpha=alpha,
                          splits=splits),
        grid=(nm, nf),
        in_specs=in_specs,
        out_specs=out_specs,
        out_shape=out_shape,
        scratch_shapes=scratch,
        compiler_params=_cparams(("arbitrary", "arbitrary")),
        name="ffn_sample" if sample else "ffn_prompt",
    )(*args)


def _layer_prompt(x, mod, wts, ffn_w, *, alpha):
    B, T, D = x.shape
    (w_in, w_gate, bg, w_conv, wmh, w_out, ln1_g, ln1_b, _, w_fconv, _, ln2_g, ln2_b) = wts
    w_a, w_g, w_down = ffn_w
    tm = 1024
    tpb = T // tm
    x2 = x.reshape(B * T, D)
    yc, q, k, v, o, gates, ztail = _inproj(x2, mod, w_in, w_gate, w_conv, tm=tm, tpb=tpb, sample=False)
    ym, C, n, m = _mlstm_prompt(q, k, v, o, gates, bg, wmh, L=128)
    tm2 = 512
    x1 = _outproj(yc, ym, w_out, x2, mod, ln1_g, ln1_b, tm=tm2, tpb=T // tm2, alpha=alpha)
    y, atail = _ffn(x1, mod, (w_a, w_g), w_fconv, w_down, ln2_g, ln2_b,
                    tm=tm, tpb=tpb, tf=512, sample=False, alpha=alpha, splits=2)
    return (y.reshape(B, T, D), ztail[tpb - 1::tpb, STATE_ROWS - 2:], C, n, m[..., 0],
            atail[tpb - 1::tpb, STATE_ROWS - 2:])


def _layer_sample(x, mod, conv_buf, C0, n0, m0, ffn_buf, wts, *, alpha):
    B, T, D = x.shape
    (w_in, w_gate, bg, w_conv, wmh, w_out, ln1_g, ln1_b, w_up, w_fconv, w_down, ln2_g, ln2_b) = wts
    rows = B * T
    x2 = x.reshape(rows, D)
    yc, q, k, v, o, gates, z0, z1 = _inproj(x2, mod, w_in, w_gate, w_conv, tm=rows, tpb=1, sample=True,
                                            period=T, s0=conv_buf[:, 0], s1=conv_buf[:, 1])
    mrow = jnp.pad(jnp.repeat(m0, T, axis=0), ((0, 0), (N_HEADS, GATE_LANES - 2 * N_HEADS)))
    nrow = jnp.repeat(n0.reshape(B, N_HEADS * HEAD_DIM), T, axis=0)
    ym, C, n, m = _mlstm_sample(q[0], k[0], v[0], o[0], gates[0], bg, mrow, nrow, C0, n0, wmh, bb=8, T=T)
    x1 = _outproj(yc, ym[None], w_out, x2, mod, ln1_g, ln1_b, tm=rows, tpb=1, alpha=alpha)
    y, a0, a1, *ffn_w = _ffn(x1, mod, w_up, w_fconv, w_down, ln2_g, ln2_b, tm=rows, tpb=1, tf=256,
                             sample=True, alpha=alpha, period=T, s0=ffn_buf[:, 0], s1=ffn_buf[:, 1])
    return (y.reshape(B, T, D), jnp.stack([z0, z1], axis=1), C, n, m[..., 0],
            jnp.stack([a0, a1], axis=1), ffn_w)


def kernel(x_prompt, x_sample, c_prompt, c_sample, state_conv, state_mlstm_C, state_mlstm_n,
           state_mlstm_m, state_ffn_conv, w_ada, b_ada, w_in, b_gate, w_conv, w_mh_norm, w_out,
           ln1_g, ln1_b, w_up, w_ffn_conv, w_down, ln2_g, ln2_b):
    depth = w_in.shape[0]
    alpha = (2 * depth) ** 0.25
    Bp = x_prompt.shape[0]
    Bs, Ts, D = x_sample.shape
    dc = w_conv.shape[-1]
    dm = w_mh_norm.shape[-1]
    n_main = 3 * dc + 4 * dm
    assert dc == dm == N_HEADS * HEAD_DIM and Ts >= CONV_K - 1 and SLAB % Ts == 0

    xp, xs = x_prompt, x_sample
    outs_p = [[] for _ in range(5)]
    outs_s = [[] for _ in range(5)]
    for l in range(depth):
        c_all = jnp.concatenate([jnp.repeat(c_sample, Ts, axis=0), c_prompt], axis=0)
        mod = _ada(c_all, w_ada[l], b_ada[l])
        mod_s = mod.reshape(1, Bs * Ts + Bp, 6 * D)
        mod_p = mod[Bs * Ts:].reshape(Bp, 1, 6 * D)
        w_in_l = w_in[l]
        wts = (
            w_in_l.astype(BF16),
            jnp.pad(w_in_l[:, n_main:], ((0, 0), (0, GATE_LANES - 2 * N_HEADS))).astype(BF16),
            jnp.pad(b_gate[l], (0, GATE_LANES - 2 * N_HEADS)).reshape(1, GATE_LANES),
            w_conv[l],
            w_mh_norm[l].reshape(1, dm),
            w_out[l].astype(BF16),
            ln1_g[l].reshape(1, D), ln1_b[l].reshape(1, D),
            w_up[l],
            w_ffn_conv[l],
            w_down[l],
            ln2_g[l].reshape(1, D), ln2_b[l].reshape(1, D),
        )
        xs, *st_s, ffn_w = _layer_sample(xs, mod_s, state_conv[l], state_mlstm_C[l], state_mlstm_n[l],
                                         state_mlstm_m[l], state_ffn_conv[l], wts, alpha=alpha)
        xp, *st_p = _layer_prompt(xp, mod_p, wts, ffn_w, alpha=alpha)
        for acc, val in zip(outs_p, st_p):
            acc.append(val)
        for acc, val in zip(outs_s, st_s):
            acc.append(val)
    return (xp.astype(x_prompt.dtype), xs.astype(x_sample.dtype),
            *[jnp.stack(a) for a in outs_p], *[jnp.stack(a) for a in outs_s])
```

```python
import functools

import jax
import jax.numpy as jnp
from jax import lax
from jax.experimental import pallas as pl
from jax.experimental.pallas import tpu as pltpu

F32 = jnp.float32
BF16 = jnp.bfloat16

N_HEADS = 4
HEAD_DIM = 256
CONV_K = 3
LN_EPS = 1e-5
NEG = -1e30
LANES = 128
GATE_LANES = LANES
STATE_ROWS = 8
VMEM_LIMIT = 56 * 1024 * 1024


def _cparams(sem):
    return pltpu.CompilerParams(dimension_semantics=sem, vmem_limit_bytes=VMEM_LIMIT)


def _ln(x):
    mu = jnp.mean(x, axis=-1, keepdims=True)
    xc = x - mu
    var = jnp.mean(xc * xc, axis=-1, keepdims=True)
    return xc * lax.rsqrt(var + LN_EPS)


def _log_sigmoid(x):
    return jnp.minimum(x, 0.0) - jnp.log1p(jnp.exp(-jnp.abs(x)))


def _sigmoid(x):
    return 1.0 / (1.0 + jnp.exp(-x))


def _dot(a, b):
    return jnp.dot(a, b, preferred_element_type=F32)


def _conv3_rows(z, w, prev):
    p0 = prev[STATE_ROWS - 2:STATE_ROWS - 1]
    p1 = prev[STATE_ROWS - 1:STATE_ROWS]
    t = lax.broadcasted_iota(jnp.int32, z.shape, 0)
    z1 = jnp.where(t >= 1, pltpu.roll(z, 1, 0), p1)
    z2 = jnp.where(t >= 2, pltpu.roll(z, 2, 0), jnp.where(t == 0, p0, p1))
    return w[0:1] * z2 + w[1:2] * z1 + w[2:3] * z, z[z.shape[0] - STATE_ROWS:]


def _carried_rows(carry_ref, idx, first):
    @pl.when(first)
    def _():
        carry_ref[idx] = jnp.zeros(carry_ref.shape[1:], F32)

    return carry_ref[idx]


def _conv3_sequences(z, w, s0_ref, s1_ref, z_scr, y_scr, t0_ref, t1_ref, T):
    nseq = z.shape[0] // T
    y = w[0:1] * pltpu.roll(z, 2, 0) + w[1:2] * pltpu.roll(z, 1, 0) + w[2:3] * z

    def rows(t):
        return pl.ds(t, nseq, stride=T)

    cols = []
    for c in range(z.shape[1] // LANES):
        cs = slice(c * LANES, (c + 1) * LANES)
        w0, w1, w2 = w[0:1, cs], w[1:2, cs], w[2:3, cs]
        z_scr[c] = z[:, cs]
        y_scr[c] = y[:, cs]
        s0 = s0_ref[:, cs]
        s1 = s1_ref[:, cs]
        z0 = z_scr[c, rows(0), :]
        z1 = z_scr[c, rows(1), :]
        y_scr[c, rows(0), :] = w0 * s0 + w1 * s1 + w2 * z0
        y_scr[c, rows(1), :] = w0 * s1 + w1 * z0 + w2 * z1
        t0_ref[:, cs] = z_scr[c, rows(T - 2), :]
        t1_ref[:, cs] = z_scr[c, rows(T - 1), :]
        cols.append(y_scr[c])
    return jnp.concatenate(cols, axis=1)


def _ada_kernel(c_ref, w_ref, b_ref, o_ref, *, n_seq, reps):
    c = c_ref[...]
    s = (c * _sigmoid(c)).astype(BF16)
    mod = _dot(s, w_ref[...].astype(BF16)) + b_ref[...]
    rows, u = o_ref.shape[0], c.shape[0]
    r = lax.broadcasted_iota(jnp.int32, (rows, u), 0)
    src = jnp.where(r < n_seq * reps, r // reps, r - n_seq * (reps - 1))
    sel = jnp.where(lax.broadcasted_iota(jnp.int32, (rows, u), 1) == src, 1.0, 0.0).astype(BF16)
    hi, mid, lo = _split3(mod)
    o_ref[...] = _dot(sel, hi) + _dot(sel, mid) + _dot(sel, lo)


def _ada(c, w, b, *, n_seq, reps, rows, tn=1024):
    u, d = c.shape
    n = w.shape[1]
    return pl.pallas_call(
        functools.partial(_ada_kernel, n_seq=n_seq, reps=reps),
        grid=(n // tn,),
        in_specs=[
            pl.BlockSpec((u, d), lambda j: (0, 0)),
            pl.BlockSpec((d, tn), lambda j: (0, j)),
            pl.BlockSpec((1, tn), lambda j: (0, j)),
        ],
        out_specs=pl.BlockSpec((rows, tn), lambda j: (0, j)),
        out_shape=jax.ShapeDtypeStruct((rows, n), F32),
        compiler_params=_cparams(("arbitrary",)),
        name="ada",
    )(c, w, b.reshape(1, n))


def _cast_transposed_kernel(w_ref, o_ref):
    o_ref[...] = w_ref[...].T.astype(BF16)


def _cast_transposed(wt, n, tn=512):
    k = wt.shape[1]
    return pl.pallas_call(
        _cast_transposed_kernel,
        grid=(n // tn,),
        in_specs=[pl.BlockSpec((tn, k), lambda j: (j, 0))],
        out_specs=pl.BlockSpec((k, tn), lambda j: (0, j)),
        out_shape=jax.ShapeDtypeStruct((k, n), BF16),
        compiler_params=_cparams(("arbitrary",)),
        name="cast_transposed",
    )(wt)


def _inproj_kernel(*refs, tpb, period, sample):
    if sample:
        (x_ref, sh_ref, sc_ref, wb_ref, wc_ref, wh_ref, wq_ref, wk_ref, wv_ref, wo_ref, wg_ref,
         wconv_ref, s0_ref, s1_ref,
         yc_ref, q_ref, k_ref, v_ref, o_ref, g_ref, t0_ref, t1_ref, u_scr, z_scr, y_scr) = refs
    else:
        (x_ref, sh_ref, sc_ref, wb_ref, wc_ref, wh_ref, wq_ref, wk_ref, wv_ref, wo_ref, wg_ref,
         wconv_ref,
         yc_ref, q_ref, k_ref, v_ref, o_ref, g_ref, zt_ref, u_scr, carry_scr) = refs
    m = pl.program_id(0)
    j = pl.program_id(1)

    if not sample:
        carried = _carried_rows(carry_scr, j, lax.rem(m, tpb) == 0)
        rs = u_scr.shape[0] // 2

        def body(first):
            prev = carried
            for i in range(2):
                r = slice(i * rs, (i + 1) * rs)
                if first:
                    u = (_ln(x_ref[r, :]) * (1.0 + sc_ref[0]) + sh_ref[0]).astype(BF16)
                    u_scr[r, :] = u
                    g_ref[0, r, :] = _dot(u, wg_ref[...])
                else:
                    u = u_scr[r, :]
                z = _dot(u, wc_ref[...]) * _dot(u, wh_ref[...])
                yc, prev = _conv3_rows(z, wconv_ref[...], prev)
                yc_ref[0, r, :] = (_dot(u, wb_ref[...]) * yc).astype(BF16)
                q_ref[0, r, :] = _dot(u, wq_ref[...]).astype(BF16)
                k_ref[0, r, :] = (_dot(u, wk_ref[...]) * (HEAD_DIM ** -0.5)).astype(BF16)
                v_ref[0, r, :] = _dot(u, wv_ref[...]).astype(BF16)
                o_ref[0, r, :] = _dot(u, wo_ref[...]).astype(BF16)
            carry_scr[j] = prev
            zt_ref[0] = prev

        pl.when(j == 0)(lambda: body(True))
        pl.when(j > 0)(lambda: body(False))
        return

    @pl.when(j == 0)
    def _():
        u = _ln(x_ref[...]) * (1.0 + sc_ref[0]) + sh_ref[0]
        ub = u.astype(BF16)
        u_scr[...] = ub
        g_ref[0] = _dot(ub, wg_ref[...])

    u = u_scr[...]
    z = _dot(u, wc_ref[...]) * _dot(u, wh_ref[...])
    yc = _conv3_sequences(z, wconv_ref[...], s0_ref, s1_ref, z_scr, y_scr, t0_ref, t1_ref, period)
    half = u.shape[0] // 2
    bg = jnp.concatenate([_dot(u[:half], wb_ref[...]), _dot(u[half:], wb_ref[...])], axis=0)
    yc_ref[0] = (bg * yc).astype(BF16)
    q_ref[0] = _dot(u, wq_ref[...]).astype(BF16)
    k_ref[0] = (_dot(u, wk_ref[...]) * (HEAD_DIM ** -0.5)).astype(BF16)
    v_ref[0] = _dot(u, wv_ref[...]).astype(BF16)
    o_ref[0] = _dot(u, wo_ref[...]).astype(BF16)


def _inproj(x, mod, w_in, w_gate, w_conv, *, tm, tpb, sample, period=0, s0=None, s1=None):
    rows, d = x.shape
    dc = w_conv.shape[1]
    tn = HEAD_DIM
    nj = dc // tn
    nm = rows // tm
    nseq = nm // tpb
    r = tm if sample else 1

    def wspec(off):
        return pl.BlockSpec((d, tn), lambda m, j, off=off: (0, off * nj + j))

    in_specs = [
        pl.BlockSpec((tm, d), lambda m, j: (m, 0)),
        pl.BlockSpec((1, r, d), lambda m, j: (m // tpb, 0, 0)),
        pl.BlockSpec((1, r, d), lambda m, j: (m // tpb, 0, 1)),
        wspec(0), wspec(1), wspec(2), wspec(3), wspec(4), wspec(5), wspec(6),
        pl.BlockSpec((d, GATE_LANES), lambda m, j: (0, 0)),
        pl.BlockSpec((CONV_K, tn), lambda m, j: (0, j)),
    ]
    args = [x, mod, mod, w_in, w_in, w_in, w_in, w_in, w_in, w_in, w_gate, w_conv]
    scratch = [pltpu.VMEM((tm, d), BF16)]
    act = pl.BlockSpec((1, tm, tn), lambda m, j: (m // tpb, m % tpb, j))
    out_specs = [act, act, act, act, act,
                 pl.BlockSpec((1, tm, GATE_LANES), lambda m, j: (m // tpb, m % tpb, 0))]
    out_shape = [jax.ShapeDtypeStruct((nseq, tpb * tm, dc), BF16)] * 5 + [
        jax.ShapeDtypeStruct((nseq, tpb * tm, GATE_LANES), F32)]
    if sample:
        assert nm == 1
        st = pl.BlockSpec((tm // period, tn), lambda m, j: (0, j))
        in_specs += [st, st]
        args += [s0, s1]
        out_specs += [st, st]
        out_shape += [jax.ShapeDtypeStruct((tm // period, dc), F32)] * 2
        scratch += [pltpu.VMEM((tn // LANES, tm, LANES), F32)] * 2
    else:
        out_specs.append(pl.BlockSpec((1, STATE_ROWS, tn), lambda m, j: (m, 0, j)))
        out_shape.append(jax.ShapeDtypeStruct((nm, STATE_ROWS, dc), F32))
        scratch.append(pltpu.VMEM((nj, STATE_ROWS, tn), F32))
    return pl.pallas_call(
        functools.partial(_inproj_kernel, tpb=tpb, period=period, sample=sample),
        grid=(nm, nj),
        in_specs=in_specs,
        out_specs=out_specs,
        out_shape=out_shape,
        scratch_shapes=scratch,
        compiler_params=_cparams(("arbitrary", "arbitrary")),
        name="inproj_sample" if sample else "inproj_prompt",
    )(*args)


def _split3(x):
    hi = x.astype(BF16)
    r1 = x - hi.astype(F32)
    mid = r1.astype(BF16)
    lo = (r1 - mid.astype(F32)).astype(BF16)
    return hi, mid, lo


def _head_out(hh, o, wmh):
    return (_sigmoid(o.astype(F32)) * (_ln(hh) * wmh)).astype(BF16)


def _split2(x):
    hi = x.astype(BF16)
    return hi, (x - hi.astype(F32)).astype(BF16)


def _rowsum(x, ones):
    hi, lo = _split2(x)
    return _dot(hi, ones) + _dot(lo, ones)


def _rep2(x):
    return jnp.concatenate([x, x], axis=1)


def _mlstm_chunk_kernel(q_ref, k_ref, v_ref, o_ref, g_ref, bg_ref, wmh_ref,
                        y_ref, c_ref, n_ref, m_ref, nrep_scr, *, B, L):
    step = pl.program_id(0)

    @pl.when(step == 0)
    def _():
        c_ref[...] = jnp.zeros_like(c_ref)
        m_ref[...] = jnp.zeros_like(m_ref)
        nrep_scr[...] = jnp.zeros_like(nrep_scr)

    row = lax.broadcasted_iota(jnp.int32, (L, L), 0)
    col = lax.broadcasted_iota(jnp.int32, (L, L), 1)
    causal = col <= row
    tril = jnp.where(causal, 1.0, 0.0).astype(BF16)
    ones_l = jnp.ones((L, LANES), BF16)
    ones_d = jnp.ones((HEAD_DIM, LANES), BF16)
    inv_d = 1.0 / HEAD_DIM
    tn = (((0,), (0,)), ((), ()))

    heads = [(b, h) for b in range(B) for h in range(N_HEADS)]

    def hsl(h):
        return slice(h * HEAD_DIM, (h + 1) * HEAD_DIM)

    gate = []
    for b in range(B):
        g = g_ref[b] + bg_ref[...]
        hi, mid, lo = _split3(_log_sigmoid(g))
        bcum = _dot(tril, hi) + _dot(tril, mid) + _dot(tril, lo)
        gate.append((g, g.T, bcum, bcum.T, m_ref[b]))

    st = []
    for b, h in heads:
        g, g_t, bcum, bcum_t, m_all = gate[b]
        bc = jnp.broadcast_to(bcum[:, N_HEADS + h:N_HEADS + h + 1], (L, LANES))
        li = jnp.broadcast_to(g[:, h:h + 1], (L, LANES))
        br = bcum_t[N_HEADS + h:N_HEADS + h + 1, :]
        m_prev = m_all[h:h + 1, :]
        a = bc + m_prev
        dlog = jnp.where(causal, bc - br + g_t[h:h + 1, :], NEG)
        mt = jnp.maximum(a, jnp.max(dlog, axis=1, keepdims=True))
        st.append(dict(bc=bc, li=li, m_prev=m_prev, mt=mt, dw=jnp.exp(dlog - mt),
                       inter=jnp.exp(a - mt)))

    for (b, h), e in zip(heads, st):
        e["s"] = lax.dot_general(q_ref[b, :, hsl(h)], k_ref[b, :, hsl(h)], (((1,), (1,)), ((), ())),
                                 preferred_element_type=F32) * e["dw"]

    for i, ((b, h), e) in enumerate(zip(heads, st)):
        s_hi, s_lo = _split2(e["s"])
        cn = jnp.concatenate([c_ref[b, h], nrep_scr[i]], axis=1).astype(BF16)
        qc = _dot(q_ref[b, :, hsl(h)], cn)
        num = _dot(s_hi, v_ref[b, :, hsl(h)]) + _rep2(e["inter"]) * qc[:, :HEAD_DIM]
        den = _dot(s_hi, ones_l) + _dot(s_lo, ones_l) + e["inter"] * qc[:, HEAD_DIM:]
        rden = 1.0 / jnp.maximum(jnp.abs(den), jnp.exp(-e["mt"]))
        e["hh"] = num * _rep2(rden)

    for (b, h), e in zip(heads, st):
        hh = e["hh"]
        xc = hh - _rep2(_rowsum(hh, ones_d) * inv_d)
        rstd = lax.rsqrt(_rowsum(xc * xc, ones_d) * inv_d + LN_EPS)
        y = _sigmoid(o_ref[b, :, hsl(h)].astype(F32)) * (xc * _rep2(rstd) * wmh_ref[:, hsl(h)])
        y_ref[b, :, hsl(h)] = y.astype(BF16)

    m_rows = []
    for i, ((b, h), e) in enumerate(zip(heads, st)):
        kh = k_ref[b, :, hsl(h)]
        m_new = e["mt"][L - 1:L, :]
        b_last = e["bc"][L - 1:L, :]
        wc = jnp.exp(b_last - e["bc"] + e["li"] - m_new)
        dc = jnp.exp(b_last + e["m_prev"] - m_new)
        vw = (v_ref[b, :, hsl(h)].astype(F32) * _rep2(wc)).astype(BF16)
        c_ref[b, h] = _rep2(dc) * c_ref[b, h] + lax.dot_general(kh, vw, tn, preferred_element_type=F32)
        wc_hi, wc_lo = _split2(wc)
        nrep_scr[i] = (dc * nrep_scr[i] + lax.dot_general(kh, wc_hi, tn, preferred_element_type=F32)
                       + lax.dot_general(kh, wc_lo, tn, preferred_element_type=F32))
        m_rows.append(m_new)

    for b in range(B):
        m_ref[b] = jnp.concatenate(m_rows[b * N_HEADS:(b + 1) * N_HEADS], axis=0)

    @pl.when(step == pl.num_programs(0) - 1)
    def _():
        for b in range(B):
            n_ref[b] = jnp.concatenate(
                [nrep_scr[b * N_HEADS + h].T[0:1, :] for h in range(N_HEADS)], axis=0)


def _mlstm_prompt(q, k, v, o, gates, bg, wmh, *, L):
    b, t, dm = q.shape
    assert L == LANES
    act = pl.BlockSpec((b, L, dm), lambda c: (0, c, 0))
    whole = lambda *shape: pl.BlockSpec(shape, lambda c: (0,) * len(shape))
    return pl.pallas_call(
        functools.partial(_mlstm_chunk_kernel, B=b, L=L),
        grid=(t // L,),
        in_specs=[act, act, act, act,
                  pl.BlockSpec((b, L, GATE_LANES), lambda c: (0, c, 0)),
                  whole(1, GATE_LANES), whole(1, dm)],
        out_specs=[act,
                   whole(b, N_HEADS, HEAD_DIM, HEAD_DIM),
                   whole(b, N_HEADS, HEAD_DIM),
                   whole(b, N_HEADS, GATE_LANES)],
        out_shape=[jax.ShapeDtypeStruct((b, t, dm), BF16),
                   jax.ShapeDtypeStruct((b, N_HEADS, HEAD_DIM, HEAD_DIM), F32),
                   jax.ShapeDtypeStruct((b, N_HEADS, HEAD_DIM), F32),
                   jax.ShapeDtypeStruct((b, N_HEADS, GATE_LANES), F32)],
        scratch_shapes=[pltpu.VMEM((b * N_HEADS, HEAD_DIM, LANES), F32)],
        compiler_params=_cparams(("arbitrary",)),
        name="mlstm_prompt",
    )(q, k, v, o, gates, bg, wmh)


SLAB = 16


def _mlstm_step_kernel(q_ref, k_ref, v_ref, o_ref, g_ref, bg_ref, mrow_ref, nrow_ref,
                       c0_ref, n0_ref, wmh_ref,
                       y_ref, c_ref, n_ref, m_ref, *, bb, T):
    R = bb * T
    per_slab = SLAB // T
    t = lax.rem(lax.broadcasted_iota(jnp.int32, (R, GATE_LANES), 0), T)

    def down(x, d):
        return pltpu.roll(x, d, 0)

    def up(x, d):
        return pltpu.roll(x, x.shape[0] - d, 0)

    def seg_last(x):
        out = x
        for d in range(1, T):
            out = jnp.where(t == T - 1 - d, up(x, d), out)
        return out

    g = g_ref[...] + bg_ref[...]
    li = pltpu.roll(g, N_HEADS, 1)
    lf = _log_sigmoid(g)
    b = lf
    for d in range(1, T):
        b = b + jnp.where(t >= d, down(lf, d), 0.0)
    m_prev = mrow_ref[...]
    a = b + m_prev
    dl = [li] + [jnp.where(t >= d, b - down(b, d) + down(li, d), NEG) for d in range(1, T)]
    mt = a
    for d in range(T):
        mt = jnp.maximum(mt, dl[d])
    dw = [jnp.exp(dl[d] - mt) for d in range(T)]
    inter = jnp.exp(a - mt)
    emt = jnp.exp(-mt)
    m_new = seg_last(mt)
    b_last = seg_last(b)
    wc = jnp.exp(b_last - b + li - m_new)
    dc = jnp.exp(b_last + m_prev - m_new)

    row_s = lax.broadcasted_iota(jnp.int32, (SLAB, HEAD_DIM), 0)
    row_r = lax.broadcasted_iota(jnp.int32, (R, HEAD_DIM), 0)

    for h in range(N_HEADS):
        hs = slice(h * HEAD_DIM, (h + 1) * HEAD_DIM)
        ln = N_HEADS + h

        def col(x):
            return x[:, ln:ln + 1]

        qb = q_ref[:, hs]
        kb = k_ref[:, hs]
        vb = v_ref[:, hs]
        qf = qb.astype(F32)
        kf = kb.astype(F32)
        vf = vb.astype(F32)
        num = jnp.zeros((R, HEAD_DIM), F32)
        den = jnp.zeros((R, 1), F32)
        for d in range(T):
            kd = kf if d == 0 else down(kf, d)
            vd = vf if d == 0 else down(vf, d)
            sw = jnp.sum(qf * kd, axis=1, keepdims=True) * col(dw[d])
            num = num + sw * vd
            den = den + sw

        qc_slabs = []
        for si in range(R // SLAB):
            q16 = qb[si * SLAB:(si + 1) * SLAB]
            acc = jnp.zeros((SLAB, HEAD_DIM), F32)
            for bl in range(per_slab):
                bi = si * per_slab + bl
                r = _dot(q16, c0_ref[bi, h].astype(BF16))
                acc = jnp.where(row_s // T == bl, r, acc)
            qc_slabs.append(acc)
        qc = jnp.concatenate(qc_slabs, axis=0)
        qn = jnp.sum(qf * nrow_ref[:, hs], axis=1, keepdims=True)
        num = num + col(inter) * qc
        den = den + col(inter) * qn
        hh = num / jnp.maximum(jnp.abs(den), col(emt))
        y_ref[:, hs] = _head_out(hh, o_ref[:, hs], wmh_ref[:, hs])

        wk = kf * col(wc)
        vw = vf * col(wc)
        for si in range(R // SLAB):
            k16 = kb[si * SLAB:(si + 1) * SLAB]
            vw16 = vw[si * SLAB:(si + 1) * SLAB]
            for bl in range(per_slab):
                bi = si * per_slab + bl
                last = bi * T + T - 1
                vwb = jnp.where(row_s // T == bl, vw16, 0.0).astype(BF16)
                dcb = dc[last:last + 1, ln:ln + 1]
                c_ref[bi, h] = dcb * c0_ref[bi, h] + lax.dot_general(
                    k16, vwb, (((0,), (0,)), ((), ())), preferred_element_type=F32)
                n_ref[bi, h:h + 1, :] = dcb * n0_ref[bi, h:h + 1, :] + jnp.sum(
                    jnp.where(row_r // T == bi, wk, 0.0), axis=0, keepdims=True)
                m_ref[bi, h:h + 1, :] = jnp.broadcast_to(
                    m_new[last:last + 1, ln:ln + 1], (1, GATE_LANES))


def _mlstm_sample(q, k, v, o, gates, bg, mrow, nrow, c0, n0, wmh, *, bb, T):
    rows, dm = q.shape
    nb = c0.shape[0]
    R = bb * T
    act = pl.BlockSpec((R, dm), lambda i: (i, 0))
    gat = pl.BlockSpec((R, GATE_LANES), lambda i: (i, 0))
    cspec = pl.BlockSpec((bb, N_HEADS, HEAD_DIM, HEAD_DIM), lambda i: (i, 0, 0, 0))
    nspec = pl.BlockSpec((bb, N_HEADS, HEAD_DIM), lambda i: (i, 0, 0))
    return pl.pallas_call(
        functools.partial(_mlstm_step_kernel, bb=bb, T=T),
        grid=(nb // bb,),
        in_specs=[act, act, act, act, gat,
                  pl.BlockSpec((1, GATE_LANES), lambda i: (0, 0)),
                  gat,
                  pl.BlockSpec((R, dm), lambda i: (i, 0)),
                  cspec, nspec,
                  pl.BlockSpec((1, dm), lambda i: (0, 0))],
        out_specs=[act, cspec, nspec,
                   pl.BlockSpec((bb, N_HEADS, GATE_LANES), lambda i: (i, 0, 0))],
        out_shape=[jax.ShapeDtypeStruct((rows, dm), BF16),
                   jax.ShapeDtypeStruct(c0.shape, F32),
                   jax.ShapeDtypeStruct(n0.shape, F32),
                   jax.ShapeDtypeStruct((nb, N_HEADS, GATE_LANES), F32)],
        compiler_params=_cparams(("arbitrary",)),
        name="mlstm_sample",
    )(q, k, v, o, gates, bg, mrow, nrow, c0, n0, wmh)


def _outproj_kernel(yc_ref, ym_ref, w_ref, x_ref, g1_ref, lg_ref, lb_ref, o_ref, *, alpha, splits):
    dc = yc_ref.shape[-1]
    rs = x_ref.shape[0] // splits
    for i in range(splits):
        r = slice(i * rs, (i + 1) * rs)
        g1 = g1_ref[0] if g1_ref.shape[1] == 1 else g1_ref[0, r, :]
        mix = _dot(yc_ref[0, r, :], w_ref[0:dc, :]) + _dot(ym_ref[0, r, :], w_ref[dc:, :])
        o_ref[r, :] = _ln(alpha * x_ref[r, :] + (1.0 + g1) * mix) * lg_ref[...] + lb_ref[...]


def _outproj(yc, ym, w_out, x, mod, ln_g, ln_b, *, tm, tpb, alpha, splits=2):
    rows, d = x.shape
    dc = yc.shape[-1]
    dm = ym.shape[-1]
    r = 1 if mod.shape[1] == 1 else tm
    vec = pl.BlockSpec((1, d), lambda m: (0, 0))
    return pl.pallas_call(
        functools.partial(_outproj_kernel, alpha=alpha, splits=splits),
        grid=(rows // tm,),
        in_specs=[pl.BlockSpec((1, tm, dc), lambda m: (m // tpb, m % tpb, 0)),
                  pl.BlockSpec((1, tm, dm), lambda m: (m // tpb, m % tpb, 0)),
                  pl.BlockSpec((dc + dm, d), lambda m: (0, 0)),
                  pl.BlockSpec((tm, d), lambda m: (m, 0)),
                  pl.BlockSpec((1, r, d), lambda m: (m // tpb, 0, 2)),
                  vec, vec],
        out_specs=pl.BlockSpec((tm, d), lambda m: (m, 0)),
        out_shape=jax.ShapeDtypeStruct((rows, d), F32),
        compiler_params=_cparams(("arbitrary",)),
        name="outproj",
    )(yc, ym, w_out, x, mod, ln_g, ln_b)


def _ffn_kernel(*refs, nf, tpb, period, sample, alpha, splits):
    if sample:
        (x_ref, sh_ref, sc_ref, g2_ref, wa_ref, wg_ref, wconv_ref, wd_ref, lg_ref, lb_ref,
         s0_ref, s1_ref, y_ref, t0_ref, t1_ref, wa_out, wg_out, wd_out, u_scr, z_scr, y_scr) = refs
    else:
        (x_ref, sh_ref, sc_ref, g2_ref, wa_ref, wg_ref, wconv_ref, wd_ref, lg_ref, lb_ref,
         y_ref, at_ref, u_scr, carry_scr) = refs
    m = pl.program_id(0)
    f = pl.program_id(1)

    if not sample:
        carried = _carried_rows(carry_scr, f, lax.rem(m, tpb) == 0)
        rs = u_scr.shape[0] // splits

        def body(first, last):
            prev = carried
            for i in range(splits):
                r = slice(i * rs, (i + 1) * rs)
                if first:
                    u = (_ln(x_ref[r, :]) * (1.0 + sc_ref[0]) + sh_ref[0]).astype(BF16)
                    u_scr[r, :] = u
                else:
                    u = u_scr[r, :]
                ac, prev = _conv3_rows(_dot(u, wa_ref[...]), wconv_ref[...], prev)
                hcur = (ac * _sigmoid(ac) * _dot(u, wg_ref[...])).astype(BF16)
                acc = _dot(hcur, wd_ref[...])
                if not first:
                    acc = y_ref[r, :] + acc
                if last:
                    acc = (_ln(alpha * x_ref[r, :] + (1.0 + g2_ref[0]) * acc)
                           * lg_ref[...] + lb_ref[...])
                y_ref[r, :] = acc
            carry_scr[f] = prev
            at_ref[0] = prev

        if nf == 1:
            body(True, True)
        else:
            pl.when(f == 0)(lambda: body(True, False))
            if nf > 2:
                pl.when(jnp.logical_and(f > 0, f < nf - 1))(lambda: body(False, False))
            pl.when(f == nf - 1)(lambda: body(False, True))
        return

    @pl.when(f == 0)
    def _():
        u = _ln(x_ref[...]) * (1.0 + sc_ref[0]) + sh_ref[0]
        u_scr[...] = u.astype(BF16)
        y_ref[...] = jnp.zeros_like(y_ref)

    wa = wa_ref[...].astype(BF16)
    wg = wg_ref[...].astype(BF16)
    wd = wd_ref[...].astype(BF16)
    wa_out[...] = wa
    wg_out[...] = wg
    wd_out[...] = wd
    u = u_scr[...]
    a = _dot(u, wa)
    ac = _conv3_sequences(a, wconv_ref[...], s0_ref, s1_ref, z_scr, y_scr, t0_ref, t1_ref, period)
    hcur = (ac * _sigmoid(ac) * _dot(u, wg)).astype(BF16)
    y_ref[...] += _dot(hcur, wd)

    @pl.when(f == nf - 1)
    def _():
        y_ref[...] = (_ln(alpha * x_ref[...] + (1.0 + g2_ref[0]) * y_ref[...])
                      * lg_ref[...] + lb_ref[...])


def _ffn(x, mod, w_up, w_conv, w_down, ln_g, ln_b, *, tm, tpb, tf, sample, alpha, period=0,
         splits=1, s0=None, s1=None):
    rows, d = x.shape
    ff = w_down.shape[0]
    nf = ff // tf
    nm = rows // tm
    r = tm if sample else 1
    vec = pl.BlockSpec((1, d), lambda m, f: (0, 0))
    w_a, w_g = (w_up, w_up) if sample else w_up
    up = pl.BlockSpec((d, tf), lambda m, f: (0, f))
    down = pl.BlockSpec((tf, d), lambda m, f: (f, 0))
    in_specs = [
        pl.BlockSpec((tm, d), lambda m, f: (m, 0), pipeline_mode=pl.Buffered(1)),
        pl.BlockSpec((1, r, d), lambda m, f: (m // tpb, 0, 3)),
        pl.BlockSpec((1, r, d), lambda m, f: (m // tpb, 0, 4)),
        pl.BlockSpec((1, r, d), lambda m, f: (m // tpb, 0, 5)),
        up,
        pl.BlockSpec((d, tf), lambda m, f: (0, nf + f)) if sample else up,
        pl.BlockSpec((CONV_K, tf), lambda m, f: (0, f)),
        down,
        vec, vec,
    ]
    args = [x, mod, mod, mod, w_a, w_g, w_conv, w_down, ln_g, ln_b]
    scratch = [pltpu.VMEM((tm, d), BF16)]
    out_specs = [pl.BlockSpec((tm, d), lambda m, f: (m, 0))]
    out_shape = [jax.ShapeDtypeStruct((rows, d), F32)]
    if sample:
        assert nm == 1
        st = pl.BlockSpec((tm // period, tf), lambda m, f: (0, f))
        in_specs += [st, st]
        args += [s0, s1]
        out_specs += [st, st, up, up, down]
        out_shape += [jax.ShapeDtypeStruct((tm // period, ff), F32)] * 2 + [
            jax.ShapeDtypeStruct((d, ff), BF16), jax.ShapeDtypeStruct((d, ff), BF16),
            jax.ShapeDtypeStruct((ff, d), BF16)]
        scratch += [pltpu.VMEM((tf // LANES, tm, LANES), F32)] * 2
    else:
        out_specs.append(pl.BlockSpec((1, STATE_ROWS, tf), lambda m, f: (m, 0, f)))
        out_shape.append(jax.ShapeDtypeStruct((nm, STATE_ROWS, ff), F32))
        scratch.append(pltpu.VMEM((nf, STATE_ROWS, tf), F32))
    return pl.pallas_call(
        functools.partial(_ffn_kernel, nf=nf, tpb=tpb, period=period, sample=sample, alpha=alpha,
                          splits=splits),
        grid=(nm, nf),
        in_specs=in_specs,
        out_specs=out_specs,
        out_shape=out_shape,
        scratch_shapes=scratch,
        compiler_params=_cparams(("arbitrary", "arbitrary")),
        name="ffn_sample" if sample else "ffn_prompt",
    )(*args)


def _layer_prompt(x, mod, wts, ffn_w, *, alpha):
    B, T, D = x.shape
    (w_in, w_gate, bg, w_conv, wmh, w_out, ln1_g, ln1_b, _, w_fconv, _, ln2_g, ln2_b) = wts
    w_a, w_g, w_down = ffn_w
    tm = 1024
    tpb = T // tm
    x2 = x.reshape(B * T, D)
    yc, q, k, v, o, gates, ztail = _inproj(x2, mod, w_in, w_gate, w_conv, tm=tm, tpb=tpb, sample=False)
    ym, C, n, m = _mlstm_prompt(q, k, v, o, gates, bg, wmh, L=128)
    tm2 = 512
    x1 = _outproj(yc, ym, w_out, x2, mod, ln1_g, ln1_b, tm=tm2, tpb=T // tm2, alpha=alpha)
    y, atail = _ffn(x1, mod, (w_a, w_g), w_fconv, w_down, ln2_g, ln2_b,
                    tm=tm, tpb=tpb, tf=512, sample=False, alpha=alpha, splits=2)
    return (y.reshape(B, T, D), ztail[tpb - 1::tpb, STATE_ROWS - 2:], C, n, m[..., 0],
            atail[tpb - 1::tpb, STATE_ROWS - 2:])


def _layer_sample(x, mod, conv_buf, C0, n0, m0, ffn_buf, wts, *, alpha):
    B, T, D = x.shape
    (w_in, w_gate, bg, w_conv, wmh, w_out, ln1_g, ln1_b, w_up, w_fconv, w_down, ln2_g, ln2_b) = wts
    rows = B * T
    x2 = x.reshape(rows, D)
    yc, q, k, v, o, gates, z0, z1 = _inproj(x2, mod, w_in, w_gate, w_conv, tm=rows, tpb=1, sample=True,
                                            period=T, s0=conv_buf[:, 0], s1=conv_buf[:, 1])
    mrow = jnp.pad(jnp.repeat(m0, T, axis=0), ((0, 0), (N_HEADS, GATE_LANES - 2 * N_HEADS)))
    nrow = jnp.repeat(n0.reshape(B, N_HEADS * HEAD_DIM), T, axis=0)
    ym, C, n, m = _mlstm_sample(q[0], k[0], v[0], o[0], gates[0], bg, mrow, nrow, C0, n0, wmh, bb=8, T=T)
    x1 = _outproj(yc, ym[None], w_out, x2, mod, ln1_g, ln1_b, tm=rows, tpb=1, alpha=alpha)
    y, a0, a1, *ffn_w = _ffn(x1, mod, w_up, w_fconv, w_down, ln2_g, ln2_b, tm=rows, tpb=1, tf=256,
                             sample=True, alpha=alpha, period=T, s0=ffn_buf[:, 0], s1=ffn_buf[:, 1])
    return (y.reshape(B, T, D), jnp.stack([z0, z1], axis=1), C, n, m[..., 0],
            jnp.stack([a0, a1], axis=1), ffn_w)


def kernel(x_prompt, x_sample, c_prompt, c_sample, state_conv, state_mlstm_C, state_mlstm_n,
           state_mlstm_m, state_ffn_conv, w_ada, b_ada, w_in, b_gate, w_conv, w_mh_norm, w_out,
           ln1_g, ln1_b, w_up, w_ffn_conv, w_down, ln2_g, ln2_b):
    depth = w_in.shape[0]
    alpha = (2 * depth) ** 0.25
    Bp = x_prompt.shape[0]
    Bs, Ts, D = x_sample.shape
    dc = w_conv.shape[-1]
    dm = w_mh_norm.shape[-1]
    n_main = 3 * dc + 4 * dm
    assert dc == dm == N_HEADS * HEAD_DIM and Ts >= CONV_K - 1 and SLAB % Ts == 0

    xp, xs = x_prompt, x_sample
    outs_p = [[] for _ in range(5)]
    outs_s = [[] for _ in range(5)]
    for l in range(depth):
        n_c = Bs + Bp
        c_all = jnp.pad(jnp.concatenate([c_sample, c_prompt], axis=0), ((0, -n_c % LANES), (0, 0)))
        mod = _ada(c_all, w_ada[l], b_ada[l], n_seq=Bs, reps=Ts, rows=Bs * Ts + Bp)
        mod_s = mod.reshape(1, Bs * Ts + Bp, 6 * D)
        mod_p = mod[Bs * Ts:].reshape(Bp, 1, 6 * D)
        w_in_l = w_in[l]
        wts = (
            _cast_transposed(jnp.swapaxes(w_in_l, 0, 1), n_main),
            jnp.pad(w_in_l[:, n_main:], ((0, 0), (0, GATE_LANES - 2 * N_HEADS))).astype(BF16),
            jnp.pad(b_gate[l], (0, GATE_LANES - 2 * N_HEADS)).reshape(1, GATE_LANES),
            w_conv[l],
            w_mh_norm[l].reshape(1, dm),
            w_out[l].astype(BF16),
            ln1_g[l].reshape(1, D), ln1_b[l].reshape(1, D),
            w_up[l],
            w_ffn_conv[l],
            w_down[l],
            ln2_g[l].reshape(1, D), ln2_b[l].reshape(1, D),
        )
        xs, *st_s, ffn_w = _layer_sample(xs, mod_s, state_conv[l], state_mlstm_C[l], state_mlstm_n[l],
                                         state_mlstm_m[l], state_ffn_conv[l], wts, alpha=alpha)
        xp, *st_p = _layer_prompt(xp, mod_p, wts, ffn_w, alpha=alpha)
        for acc, val in zip(outs_p, st_p):
            acc.append(val)
        for acc, val in zip(outs_s, st_s):
            acc.append(val)
    return (xp.astype(x_prompt.dtype), xs.astype(x_sample.dtype),
            *[jnp.stack(a) for a in outs_p], *[jnp.stack(a) for a in outs_s])
```

```python
import functools

import jax
import jax.numpy as jnp
from jax import lax
from jax.experimental import pallas as pl
from jax.experimental.pallas import tpu as pltpu

F32 = jnp.float32
BF16 = jnp.bfloat16

N_HEADS = 4
HEAD_DIM = 256
CONV_K = 3
LN_EPS = 1e-5
NEG = -1e30
LANES = 128
GATE_LANES = LANES
STATE_ROWS = 8
VMEM_LIMIT = 56 * 1024 * 1024


def _cparams(sem):
    return pltpu.CompilerParams(dimension_semantics=sem, vmem_limit_bytes=VMEM_LIMIT)


def _ln(x):
    mu = jnp.mean(x, axis=-1, keepdims=True)
    xc = x - mu
    var = jnp.mean(xc * xc, axis=-1, keepdims=True)
    return xc * lax.rsqrt(var + LN_EPS)


def _log_sigmoid(x):
    return jnp.minimum(x, 0.0) - jnp.log1p(jnp.exp(-jnp.abs(x)))


def _sigmoid(x):
    return 1.0 / (1.0 + jnp.exp(-x))


def _dot(a, b):
    return jnp.dot(a, b, preferred_element_type=F32)


def _conv3_rows(z, w, prev):
    p0 = prev[STATE_ROWS - 2:STATE_ROWS - 1]
    p1 = prev[STATE_ROWS - 1:STATE_ROWS]
    t = lax.broadcasted_iota(jnp.int32, z.shape, 0)
    z1 = jnp.where(t >= 1, pltpu.roll(z, 1, 0), p1)
    z2 = jnp.where(t >= 2, pltpu.roll(z, 2, 0), jnp.where(t == 0, p0, p1))
    return w[0:1] * z2 + w[1:2] * z1 + w[2:3] * z, z[z.shape[0] - STATE_ROWS:]


def _carried_rows(carry_ref, idx, first):
    @pl.when(first)
    def _():
        carry_ref[idx] = jnp.zeros(carry_ref.shape[1:], F32)

    return carry_ref[idx]


def _conv3_sequences(z, w, s0_ref, s1_ref, z_scr, y_scr, t0_ref, t1_ref, T):
    nseq = z.shape[0] // T
    y = w[0:1] * pltpu.roll(z, 2, 0) + w[1:2] * pltpu.roll(z, 1, 0) + w[2:3] * z

    def rows(t):
        return pl.ds(t, nseq, stride=T)

    cols = []
    for c in range(z.shape[1] // LANES):
        cs = slice(c * LANES, (c + 1) * LANES)
        w0, w1, w2 = w[0:1, cs], w[1:2, cs], w[2:3, cs]
        z_scr[c] = z[:, cs]
        y_scr[c] = y[:, cs]
        s0 = s0_ref[:, cs]
        s1 = s1_ref[:, cs]
        z0 = z_scr[c, rows(0), :]
        z1 = z_scr[c, rows(1), :]
        y_scr[c, rows(0), :] = w0 * s0 + w1 * s1 + w2 * z0
        y_scr[c, rows(1), :] = w0 * s1 + w1 * z0 + w2 * z1
        t0_ref[:, cs] = z_scr[c, rows(T - 2), :]
        t1_ref[:, cs] = z_scr[c, rows(T - 1), :]
        cols.append(y_scr[c])
    return jnp.concatenate(cols, axis=1)


def _ada_kernel(c_ref, w_ref, b_ref, o_ref):
    c = c_ref[...]
    s = (c * _sigmoid(c)).astype(BF16)
    o_ref[...] = _dot(s, w_ref[...].astype(BF16)) + b_ref[...]


def _ada(c, w, b, tn=1024):
    r, d = c.shape
    n = w.shape[1]
    return pl.pallas_call(
        _ada_kernel,
        grid=(n // tn,),
        in_specs=[
            pl.BlockSpec((r, d), lambda j: (0, 0)),
            pl.BlockSpec((d, tn), lambda j: (0, j)),
            pl.BlockSpec((1, tn), lambda j: (0, j)),
        ],
        out_specs=pl.BlockSpec((r, tn), lambda j: (0, j)),
        out_shape=jax.ShapeDtypeStruct((r, n), F32),
        compiler_params=_cparams(("arbitrary",)),
        name="ada",
    )(c, w, b.reshape(1, n))


def _cast_transposed_kernel(w_ref, o_ref):
    o_ref[...] = w_ref[...].T.astype(BF16)


def _cast_transposed(wt, n, tn=512):
    k = wt.shape[1]
    return pl.pallas_call(
        _cast_transposed_kernel,
        grid=(n // tn,),
        in_specs=[pl.BlockSpec((tn, k), lambda j: (j, 0))],
        out_specs=pl.BlockSpec((k, tn), lambda j: (0, j)),
        out_shape=jax.ShapeDtypeStruct((k, n), BF16),
        compiler_params=_cparams(("arbitrary",)),
        name="cast_transposed",
    )(wt)


def _inproj_kernel(*refs, tpb, period, sample):
    if sample:
        (x_ref, sh_ref, sc_ref, wb_ref, wc_ref, wh_ref, wq_ref, wk_ref, wv_ref, wo_ref, wg_ref,
         wconv_ref, s0_ref, s1_ref,
         yc_ref, q_ref, k_ref, v_ref, o_ref, g_ref, t0_ref, t1_ref, u_scr, z_scr, y_scr) = refs
    else:
        (x_ref, sh_ref, sc_ref, wb_ref, wc_ref, wh_ref, wq_ref, wk_ref, wv_ref, wo_ref, wg_ref,
         wconv_ref,
         yc_ref, q_ref, k_ref, v_ref, o_ref, g_ref, zt_ref, u_scr, carry_scr) = refs
    m = pl.program_id(0)
    j = pl.program_id(1)

    if not sample:
        carried = _carried_rows(carry_scr, j, lax.rem(m, tpb) == 0)
        rs = u_scr.shape[0] // 2

        def body(first):
            prev = carried
            for i in range(2):
                r = slice(i * rs, (i + 1) * rs)
                if first:
                    u = (_ln(x_ref[r, :]) * (1.0 + sc_ref[0]) + sh_ref[0]).astype(BF16)
                    u_scr[r, :] = u
                    g_ref[0, r, :] = _dot(u, wg_ref[...])
                else:
                    u = u_scr[r, :]
                z = _dot(u, wc_ref[...]) * _dot(u, wh_ref[...])
                yc, prev = _conv3_rows(z, wconv_ref[...], prev)
                yc_ref[0, r, :] = (_dot(u, wb_ref[...]) * yc).astype(BF16)
                q_ref[0, r, :] = _dot(u, wq_ref[...]).astype(BF16)
                k_ref[0, r, :] = (_dot(u, wk_ref[...]) * (HEAD_DIM ** -0.5)).astype(BF16)
                v_ref[0, r, :] = _dot(u, wv_ref[...]).astype(BF16)
                o_ref[0, r, :] = _dot(u, wo_ref[...]).astype(BF16)
            carry_scr[j] = prev
            zt_ref[0] = prev

        pl.when(j == 0)(lambda: body(True))
        pl.when(j > 0)(lambda: body(False))
        return

    @pl.when(j == 0)
    def _():
        u = _ln(x_ref[...]) * (1.0 + sc_ref[0]) + sh_ref[0]
        ub = u.astype(BF16)
        u_scr[...] = ub
        g_ref[0] = _dot(ub, wg_ref[...])

    u = u_scr[...]
    z = _dot(u, wc_ref[...]) * _dot(u, wh_ref[...])
    yc = _conv3_sequences(z, wconv_ref[...], s0_ref, s1_ref, z_scr, y_scr, t0_ref, t1_ref, period)
    half = u.shape[0] // 2
    bg = jnp.concatenate([_dot(u[:half], wb_ref[...]), _dot(u[half:], wb_ref[...])], axis=0)
    yc_ref[0] = (bg * yc).astype(BF16)
    q_ref[0] = _dot(u, wq_ref[...]).astype(BF16)
    k_ref[0] = (_dot(u, wk_ref[...]) * (HEAD_DIM ** -0.5)).astype(BF16)
    v_ref[0] = _dot(u, wv_ref[...]).astype(BF16)
    o_ref[0] = _dot(u, wo_ref[...]).astype(BF16)


def _inproj(x, mod, w_in, w_gate, w_conv, *, tm, tpb, sample, period=0, s0=None, s1=None):
    rows, d = x.shape
    dc = w_conv.shape[1]
    tn = HEAD_DIM
    nj = dc // tn
    nm = rows // tm
    nseq = nm // tpb
    r = tm if sample else 1

    def wspec(off):
        return pl.BlockSpec((d, tn), lambda m, j, off=off: (0, off * nj + j))

    in_specs = [
        pl.BlockSpec((tm, d), lambda m, j: (m, 0)),
        pl.BlockSpec((1, r, d), lambda m, j: (m // tpb, 0, 0)),
        pl.BlockSpec((1, r, d), lambda m, j: (m // tpb, 0, 1)),
        wspec(0), wspec(1), wspec(2), wspec(3), wspec(4), wspec(5), wspec(6),
        pl.BlockSpec((d, GATE_LANES), lambda m, j: (0, 0)),
        pl.BlockSpec((CONV_K, tn), lambda m, j: (0, j)),
    ]
    args = [x, mod, mod, w_in, w_in, w_in, w_in, w_in, w_in, w_in, w_gate, w_conv]
    scratch = [pltpu.VMEM((tm, d), BF16)]
    act = pl.BlockSpec((1, tm, tn), lambda m, j: (m // tpb, m % tpb, j))
    out_specs = [act, act, act, act, act,
                 pl.BlockSpec((1, tm, GATE_LANES), lambda m, j: (m // tpb, m % tpb, 0))]
    out_shape = [jax.ShapeDtypeStruct((nseq, tpb * tm, dc), BF16)] * 5 + [
        jax.ShapeDtypeStruct((nseq, tpb * tm, GATE_LANES), F32)]
    if sample:
        assert nm == 1
        st = pl.BlockSpec((tm // period, tn), lambda m, j: (0, j))
        in_specs += [st, st]
        args += [s0, s1]
        out_specs += [st, st]
        out_shape += [jax.ShapeDtypeStruct((tm // period, dc), F32)] * 2
        scratch += [pltpu.VMEM((tn // LANES, tm, LANES), F32)] * 2
    else:
        out_specs.append(pl.BlockSpec((1, STATE_ROWS, tn), lambda m, j: (m, 0, j)))
        out_shape.append(jax.ShapeDtypeStruct((nm, STATE_ROWS, dc), F32))
        scratch.append(pltpu.VMEM((nj, STATE_ROWS, tn), F32))
    return pl.pallas_call(
        functools.partial(_inproj_kernel, tpb=tpb, period=period, sample=sample),
        grid=(nm, nj),
        in_specs=in_specs,
        out_specs=out_specs,
        out_shape=out_shape,
        scratch_shapes=scratch,
        compiler_params=_cparams(("arbitrary", "arbitrary")),
        name="inproj_sample" if sample else "inproj_prompt",
    )(*args)


def _split3(x):
    hi = x.astype(BF16)
    r1 = x - hi.astype(F32)
    mid = r1.astype(BF16)
    lo = (r1 - mid.astype(F32)).astype(BF16)
    return hi, mid, lo


def _head_out(hh, o, wmh):
    return (_sigmoid(o.astype(F32)) * (_ln(hh) * wmh)).astype(BF16)


def _split2(x):
    hi = x.astype(BF16)
    return hi, (x - hi.astype(F32)).astype(BF16)


def _rowsum(x, ones):
    hi, lo = _split2(x)
    return _dot(hi, ones) + _dot(lo, ones)


def _rep2(x):
    return jnp.concatenate([x, x], axis=1)


def _mlstm_chunk_kernel(q_ref, k_ref, v_ref, o_ref, g_ref, bg_ref, wmh_ref,
                        y_ref, c_ref, n_ref, m_ref, nrep_scr, *, B, L):
    step = pl.program_id(0)

    @pl.when(step == 0)
    def _():
        c_ref[...] = jnp.zeros_like(c_ref)
        m_ref[...] = jnp.zeros_like(m_ref)
        nrep_scr[...] = jnp.zeros_like(nrep_scr)

    row = lax.broadcasted_iota(jnp.int32, (L, L), 0)
    col = lax.broadcasted_iota(jnp.int32, (L, L), 1)
    causal = col <= row
    tril = jnp.where(causal, 1.0, 0.0).astype(BF16)
    ones_l = jnp.ones((L, LANES), BF16)
    ones_d = jnp.ones((HEAD_DIM, LANES), BF16)
    inv_d = 1.0 / HEAD_DIM
    tn = (((0,), (0,)), ((), ()))

    heads = [(b, h) for b in range(B) for h in range(N_HEADS)]

    def hsl(h):
        return slice(h * HEAD_DIM, (h + 1) * HEAD_DIM)

    gate = []
    for b in range(B):
        g = g_ref[b] + bg_ref[...]
        hi, mid, lo = _split3(_log_sigmoid(g))
        bcum = _dot(tril, hi) + _dot(tril, mid) + _dot(tril, lo)
        gate.append((g, g.T, bcum, bcum.T, m_ref[b]))

    st = []
    for b, h in heads:
        g, g_t, bcum, bcum_t, m_all = gate[b]
        bc = jnp.broadcast_to(bcum[:, N_HEADS + h:N_HEADS + h + 1], (L, LANES))
        li = jnp.broadcast_to(g[:, h:h + 1], (L, LANES))
        br = bcum_t[N_HEADS + h:N_HEADS + h + 1, :]
        m_prev = m_all[h:h + 1, :]
        a = bc + m_prev
        dlog = jnp.where(causal, bc - br + g_t[h:h + 1, :], NEG)
        mt = jnp.maximum(a, jnp.max(dlog, axis=1, keepdims=True))
        st.append(dict(bc=bc, li=li, m_prev=m_prev, mt=mt, dw=jnp.exp(dlog - mt),
                       inter=jnp.exp(a - mt)))

    for (b, h), e in zip(heads, st):
        e["s"] = lax.dot_general(q_ref[b, :, hsl(h)], k_ref[b, :, hsl(h)], (((1,), (1,)), ((), ())),
                                 preferred_element_type=F32) * e["dw"]

    for i, ((b, h), e) in enumerate(zip(heads, st)):
        s_hi, s_lo = _split2(e["s"])
        cn = jnp.concatenate([c_ref[b, h], nrep_scr[i]], axis=1).astype(BF16)
        qc = _dot(q_ref[b, :, hsl(h)], cn)
        num = _dot(s_hi, v_ref[b, :, hsl(h)]) + _rep2(e["inter"]) * qc[:, :HEAD_DIM]
        den = _dot(s_hi, ones_l) + _dot(s_lo, ones_l) + e["inter"] * qc[:, HEAD_DIM:]
        rden = 1.0 / jnp.maximum(jnp.abs(den), jnp.exp(-e["mt"]))
        e["hh"] = num * _rep2(rden)

    for (b, h), e in zip(heads, st):
        hh = e["hh"]
        xc = hh - _rep2(_rowsum(hh, ones_d) * inv_d)
        rstd = lax.rsqrt(_rowsum(xc * xc, ones_d) * inv_d + LN_EPS)
        y = _sigmoid(o_ref[b, :, hsl(h)].astype(F32)) * (xc * _rep2(rstd) * wmh_ref[:, hsl(h)])
        y_ref[b, :, hsl(h)] = y.astype(BF16)

    m_rows = []
    for i, ((b, h), e) in enumerate(zip(heads, st)):
        kh = k_ref[b, :, hsl(h)]
        m_new = e["mt"][L - 1:L, :]
        b_last = e["bc"][L - 1:L, :]
        wc = jnp.exp(b_last - e["bc"] + e["li"] - m_new)
        dc = jnp.exp(b_last + e["m_prev"] - m_new)
        vw = (v_ref[b, :, hsl(h)].astype(F32) * _rep2(wc)).astype(BF16)
        c_ref[b, h] = _rep2(dc) * c_ref[b, h] + lax.dot_general(kh, vw, tn, preferred_element_type=F32)
        wc_hi, wc_lo = _split2(wc)
        nrep_scr[i] = (dc * nrep_scr[i] + lax.dot_general(kh, wc_hi, tn, preferred_element_type=F32)
                       + lax.dot_general(kh, wc_lo, tn, preferred_element_type=F32))
        m_rows.append(m_new)

    for b in range(B):
        m_ref[b] = jnp.concatenate(m_rows[b * N_HEADS:(b + 1) * N_HEADS], axis=0)

    @pl.when(step == pl.num_programs(0) - 1)
    def _():
        for b in range(B):
            n_ref[b] = jnp.concatenate(
                [nrep_scr[b * N_HEADS + h].T[0:1, :] for h in range(N_HEADS)], axis=0)


def _mlstm_prompt(q, k, v, o, gates, bg, wmh, *, L):
    b, t, dm = q.shape
    assert L == LANES
    act = pl.BlockSpec((b, L, dm), lambda c: (0, c, 0))
    whole = lambda *shape: pl.BlockSpec(shape, lambda c: (0,) * len(shape))
    return pl.pallas_call(
        functools.partial(_mlstm_chunk_kernel, B=b, L=L),
        grid=(t // L,),
        in_specs=[act, act, act, act,
                  pl.BlockSpec((b, L, GATE_LANES), lambda c: (0, c, 0)),
                  whole(1, GATE_LANES), whole(1, dm)],
        out_specs=[act,
                   whole(b, N_HEADS, HEAD_DIM, HEAD_DIM),
                   whole(b, N_HEADS, HEAD_DIM),
                   whole(b, N_HEADS, GATE_LANES)],
        out_shape=[jax.ShapeDtypeStruct((b, t, dm), BF16),
                   jax.ShapeDtypeStruct((b, N_HEADS, HEAD_DIM, HEAD_DIM), F32),
                   jax.ShapeDtypeStruct((b, N_HEADS, HEAD_DIM), F32),
                   jax.ShapeDtypeStruct((b, N_HEADS, GATE_LANES), F32)],
        scratch_shapes=[pltpu.VMEM((b * N_HEADS, HEAD_DIM, LANES), F32)],
        compiler_params=_cparams(("arbitrary",)),
        name="mlstm_prompt",
    )(q, k, v, o, gates, bg, wmh)


SLAB = 16


def _mlstm_step_kernel(q_ref, k_ref, v_ref, o_ref, g_ref, bg_ref, mrow_ref, nrow_ref,
                       c0_ref, n0_ref, wmh_ref,
                       y_ref, c_ref, n_ref, m_ref, *, bb, T):
    R = bb * T
    per_slab = SLAB // T
    t = lax.rem(lax.broadcasted_iota(jnp.int32, (R, GATE_LANES), 0), T)

    def down(x, d):
        return pltpu.roll(x, d, 0)

    def up(x, d):
        return pltpu.roll(x, x.shape[0] - d, 0)

    def seg_last(x):
        out = x
        for d in range(1, T):
            out = jnp.where(t == T - 1 - d, up(x, d), out)
        return out

    g = g_ref[...] + bg_ref[...]
    li = pltpu.roll(g, N_HEADS, 1)
    lf = _log_sigmoid(g)
    b = lf
    for d in range(1, T):
        b = b + jnp.where(t >= d, down(lf, d), 0.0)
    m_prev = mrow_ref[...]
    a = b + m_prev
    dl = [li] + [jnp.where(t >= d, b - down(b, d) + down(li, d), NEG) for d in range(1, T)]
    mt = a
    for d in range(T):
        mt = jnp.maximum(mt, dl[d])
    dw = [jnp.exp(dl[d] - mt) for d in range(T)]
    inter = jnp.exp(a - mt)
    emt = jnp.exp(-mt)
    m_new = seg_last(mt)
    b_last = seg_last(b)
    wc = jnp.exp(b_last - b + li - m_new)
    dc = jnp.exp(b_last + m_prev - m_new)

    row_s = lax.broadcasted_iota(jnp.int32, (SLAB, HEAD_DIM), 0)
    row_r = lax.broadcasted_iota(jnp.int32, (R, HEAD_DIM), 0)

    for h in range(N_HEADS):
        hs = slice(h * HEAD_DIM, (h + 1) * HEAD_DIM)
        ln = N_HEADS + h

        def col(x):
            return x[:, ln:ln + 1]

        qb = q_ref[:, hs]
        kb = k_ref[:, hs]
        vb = v_ref[:, hs]
        qf = qb.astype(F32)
        kf = kb.astype(F32)
        vf = vb.astype(F32)
        num = jnp.zeros((R, HEAD_DIM), F32)
        den = jnp.zeros((R, 1), F32)
        for d in range(T):
            kd = kf if d == 0 else down(kf, d)
            vd = vf if d == 0 else down(vf, d)
            sw = jnp.sum(qf * kd, axis=1, keepdims=True) * col(dw[d])
            num = num + sw * vd
            den = den + sw

        qc_slabs = []
        for si in range(R // SLAB):
            q16 = qb[si * SLAB:(si + 1) * SLAB]
            acc = jnp.zeros((SLAB, HEAD_DIM), F32)
            for bl in range(per_slab):
                bi = si * per_slab + bl
                r = _dot(q16, c0_ref[bi, h].astype(BF16))
                acc = jnp.where(row_s // T == bl, r, acc)
            qc_slabs.append(acc)
        qc = jnp.concatenate(qc_slabs, axis=0)
        qn = jnp.sum(qf * nrow_ref[:, hs], axis=1, keepdims=True)
        num = num + col(inter) * qc
        den = den + col(inter) * qn
        hh = num / jnp.maximum(jnp.abs(den), col(emt))
        y_ref[:, hs] = _head_out(hh, o_ref[:, hs], wmh_ref[:, hs])

        wk = kf * col(wc)
        vw = vf * col(wc)
        for si in range(R // SLAB):
            k16 = kb[si * SLAB:(si + 1) * SLAB]
            vw16 = vw[si * SLAB:(si + 1) * SLAB]
            for bl in range(per_slab):
                bi = si * per_slab + bl
                last = bi * T + T - 1
                vwb = jnp.where(row_s // T == bl, vw16, 0.0).astype(BF16)
                dcb = dc[last:last + 1, ln:ln + 1]
                c_ref[bi, h] = dcb * c0_ref[bi, h] + lax.dot_general(
                    k16, vwb, (((0,), (0,)), ((), ())), preferred_element_type=F32)
                n_ref[bi, h:h + 1, :] = dcb * n0_ref[bi, h:h + 1, :] + jnp.sum(
                    jnp.where(row_r // T == bi, wk, 0.0), axis=0, keepdims=True)
                m_ref[bi, h:h + 1, :] = jnp.broadcast_to(
                    m_new[last:last + 1, ln:ln + 1], (1, GATE_LANES))


def _mlstm_sample(q, k, v, o, gates, bg, mrow, nrow, c0, n0, wmh, *, bb, T):
    rows, dm = q.shape
    nb = c0.shape[0]
    R = bb * T
    act = pl.BlockSpec((R, dm), lambda i: (i, 0))
    gat = pl.BlockSpec((R, GATE_LANES), lambda i: (i, 0))
    cspec = pl.BlockSpec((bb, N_HEADS, HEAD_DIM, HEAD_DIM), lambda i: (i, 0, 0, 0))
    nspec = pl.BlockSpec((bb, N_HEADS, HEAD_DIM), lambda i: (i, 0, 0))
    return pl.pallas_call(
        functools.partial(_mlstm_step_kernel, bb=bb, T=T),
        grid=(nb // bb,),
        in_specs=[act, act, act, act, gat,
                  pl.BlockSpec((1, GATE_LANES), lambda i: (0, 0)),
                  gat,
                  pl.BlockSpec((R, dm), lambda i: (i, 0)),
                  cspec, nspec,
                  pl.BlockSpec((1, dm), lambda i: (0, 0))],
        out_specs=[act, cspec, nspec,
                   pl.BlockSpec((bb, N_HEADS, GATE_LANES), lambda i: (i, 0, 0))],
        out_shape=[jax.ShapeDtypeStruct((rows, dm), BF16),
                   jax.ShapeDtypeStruct(c0.shape, F32),
                   jax.ShapeDtypeStruct(n0.shape, F32),
                   jax.ShapeDtypeStruct((nb, N_HEADS, GATE_LANES), F32)],
        compiler_params=_cparams(("arbitrary",)),
        name="mlstm_sample",
    )(q, k, v, o, gates, bg, mrow, nrow, c0, n0, wmh)


def _outproj_kernel(yc_ref, ym_ref, w_ref, x_ref, g1_ref, lg_ref, lb_ref, o_ref, *, alpha, splits):
    dc = yc_ref.shape[-1]
    rs = x_ref.shape[0] // splits
    for i in range(splits):
        r = slice(i * rs, (i + 1) * rs)
        g1 = g1_ref[0] if g1_ref.shape[1] == 1 else g1_ref[0, r, :]
        mix = _dot(yc_ref[0, r, :], w_ref[0:dc, :]) + _dot(ym_ref[0, r, :], w_ref[dc:, :])
        o_ref[r, :] = _ln(alpha * x_ref[r, :] + (1.0 + g1) * mix) * lg_ref[...] + lb_ref[...]


def _outproj(yc, ym, w_out, x, mod, ln_g, ln_b, *, tm, tpb, alpha, splits=2):
    rows, d = x.shape
    dc = yc.shape[-1]
    dm = ym.shape[-1]
    r = 1 if mod.shape[1] == 1 else tm
    vec = pl.BlockSpec((1, d), lambda m: (0, 0))
    return pl.pallas_call(
        functools.partial(_outproj_kernel, alpha=alpha, splits=splits),
        grid=(rows // tm,),
        in_specs=[pl.BlockSpec((1, tm, dc), lambda m: (m // tpb, m % tpb, 0)),
                  pl.BlockSpec((1, tm, dm), lambda m: (m // tpb, m % tpb, 0)),
                  pl.BlockSpec((dc + dm, d), lambda m: (0, 0)),
                  pl.BlockSpec((tm, d), lambda m: (m, 0)),
                  pl.BlockSpec((1, r, d), lambda m: (m // tpb, 0, 2)),
                  vec, vec],
        out_specs=pl.BlockSpec((tm, d), lambda m: (m, 0)),
        out_shape=jax.ShapeDtypeStruct((rows, d), F32),
        compiler_params=_cparams(("arbitrary",)),
        name="outproj",
    )(yc, ym, w_out, x, mod, ln_g, ln_b)


def _ffn_kernel(*refs, nf, tpb, period, sample, alpha, splits):
    if sample:
        (x_ref, sh_ref, sc_ref, g2_ref, wa_ref, wg_ref, wconv_ref, wd_ref, lg_ref, lb_ref,
         s0_ref, s1_ref, y_ref, t0_ref, t1_ref, wa_out, wg_out, wd_out, u_scr, z_scr, y_scr) = refs
    else:
        (x_ref, sh_ref, sc_ref, g2_ref, wa_ref, wg_ref, wconv_ref, wd_ref, lg_ref, lb_ref,
         y_ref, at_ref, u_scr, carry_scr) = refs
    m = pl.program_id(0)
    f = pl.program_id(1)

    if not sample:
        carried = _carried_rows(carry_scr, f, lax.rem(m, tpb) == 0)
        rs = u_scr.shape[0] // splits

        def body(first, last):
            prev = carried
            for i in range(splits):
                r = slice(i * rs, (i + 1) * rs)
                if first:
                    u = (_ln(x_ref[r, :]) * (1.0 + sc_ref[0]) + sh_ref[0]).astype(BF16)
                    u_scr[r, :] = u
                else:
                    u = u_scr[r, :]
                ac, prev = _conv3_rows(_dot(u, wa_ref[...]), wconv_ref[...], prev)
                hcur = (ac * _sigmoid(ac) * _dot(u, wg_ref[...])).astype(BF16)
                acc = _dot(hcur, wd_ref[...])
                if not first:
                    acc = y_ref[r, :] + acc
                if last:
                    acc = (_ln(alpha * x_ref[r, :] + (1.0 + g2_ref[0]) * acc)
                           * lg_ref[...] + lb_ref[...])
                y_ref[r, :] = acc
            carry_scr[f] = prev
            at_ref[0] = prev

        if nf == 1:
            body(True, True)
        else:
            pl.when(f == 0)(lambda: body(True, False))
            if nf > 2:
                pl.when(jnp.logical_and(f > 0, f < nf - 1))(lambda: body(False, False))
            pl.when(f == nf - 1)(lambda: body(False, True))
        return

    @pl.when(f == 0)
    def _():
        u = _ln(x_ref[...]) * (1.0 + sc_ref[0]) + sh_ref[0]
        u_scr[...] = u.astype(BF16)
        y_ref[...] = jnp.zeros_like(y_ref)

    wa = wa_ref[...].astype(BF16)
    wg = wg_ref[...].astype(BF16)
    wd = wd_ref[...].astype(BF16)
    wa_out[...] = wa
    wg_out[...] = wg
    wd_out[...] = wd
    u = u_scr[...]
    a = _dot(u, wa)
    ac = _conv3_sequences(a, wconv_ref[...], s0_ref, s1_ref, z_scr, y_scr, t0_ref, t1_ref, period)
    hcur = (ac * _sigmoid(ac) * _dot(u, wg)).astype(BF16)
    y_ref[...] += _dot(hcur, wd)

    @pl.when(f == nf - 1)
    def _():
        y_ref[...] = (_ln(alpha * x_ref[...] + (1.0 + g2_ref[0]) * y_ref[...])
                      * lg_ref[...] + lb_ref[...])


def _ffn(x, mod, w_up, w_conv, w_down, ln_g, ln_b, *, tm, tpb, tf, sample, alpha, period=0,
         splits=1, s0=None, s1=None):
    rows, d = x.shape
    ff = w_down.shape[0]
    nf = ff // tf
    nm = rows // tm
    r = tm if sample else 1
    vec = pl.BlockSpec((1, d), lambda m, f: (0, 0))
    w_a, w_g = (w_up, w_up) if sample else w_up
    up = pl.BlockSpec((d, tf), lambda m, f: (0, f))
    down = pl.BlockSpec((tf, d), lambda m, f: (f, 0))
    in_specs = [
        pl.BlockSpec((tm, d), lambda m, f: (m, 0), pipeline_mode=pl.Buffered(1)),
        pl.BlockSpec((1, r, d), lambda m, f: (m // tpb, 0, 3)),
        pl.BlockSpec((1, r, d), lambda m, f: (m // tpb, 0, 4)),
        pl.BlockSpec((1, r, d), lambda m, f: (m // tpb, 0, 5)),
        up,
        pl.BlockSpec((d, tf), lambda m, f: (0, nf + f)) if sample else up,
        pl.BlockSpec((CONV_K, tf), lambda m, f: (0, f)),
        down,
        vec, vec,
    ]
    args = [x, mod, mod, mod, w_a, w_g, w_conv, w_down, ln_g, ln_b]
    scratch = [pltpu.VMEM((tm, d), BF16)]
    out_specs = [pl.BlockSpec((tm, d), lambda m, f: (m, 0))]
    out_shape = [jax.ShapeDtypeStruct((rows, d), F32)]
    if sample:
        assert nm == 1
        st = pl.BlockSpec((tm // period, tf), lambda m, f: (0, f))
        in_specs += [st, st]
        args += [s0, s1]
        out_specs += [st, st, up, up, down]
        out_shape += [jax.ShapeDtypeStruct((tm // period, ff), F32)] * 2 + [
            jax.ShapeDtypeStruct((d, ff), BF16), jax.ShapeDtypeStruct((d, ff), BF16),
            jax.ShapeDtypeStruct((ff, d), BF16)]
        scratch += [pltpu.VMEM((tf // LANES, tm, LANES), F32)] * 2
    else:
        out_specs.append(pl.BlockSpec((1, STATE_ROWS, tf), lambda m, f: (m, 0, f)))
        out_shape.append(jax.ShapeDtypeStruct((nm, STATE_ROWS, ff), F32))
        scratch.append(pltpu.VMEM((nf, STATE_ROWS, tf), F32))
    return pl.pallas_call(
        functools.partial(_ffn_kernel, nf=nf, tpb=tpb, period=period, sample=sample, alpha=alpha,
                          splits=splits),
        grid=(nm, nf),
        in_specs=in_specs,
        out_specs=out_specs,
        out_shape=out_shape,
        scratch_shapes=scratch,
        compiler_params=_cparams(("arbitrary", "arbitrary")),
        name="ffn_sample" if sample else "ffn_prompt",
    )(*args)


def _layer_prompt(x, mod, wts, ffn_w, *, alpha):
    B, T, D = x.shape
    (w_in, w_gate, bg, w_conv, wmh, w_out, ln1_g, ln1_b, _, w_fconv, _, ln2_g, ln2_b) = wts
    w_a, w_g, w_down = ffn_w
    tm = 1024
    tpb = T // tm
    x2 = x.reshape(B * T, D)
    yc, q, k, v, o, gates, ztail = _inproj(x2, mod, w_in, w_gate, w_conv, tm=tm, tpb=tpb, sample=False)
    ym, C, n, m = _mlstm_prompt(q, k, v, o, gates, bg, wmh, L=128)
    tm2 = 512
    x1 = _outproj(yc, ym, w_out, x2, mod, ln1_g, ln1_b, tm=tm2, tpb=T // tm2, alpha=alpha)
    y, atail = _ffn(x1, mod, (w_a, w_g), w_fconv, w_down, ln2_g, ln2_b,
                    tm=tm, tpb=tpb, tf=512, sample=False, alpha=alpha, splits=2)
    return (y.reshape(B, T, D), ztail[tpb - 1::tpb, STATE_ROWS - 2:], C, n, m[..., 0],
            atail[tpb - 1::tpb, STATE_ROWS - 2:])


def _layer_sample(x, mod, conv_buf, C0, n0, m0, ffn_buf, wts, *, alpha):
    B, T, D = x.shape
    (w_in, w_gate, bg, w_conv, wmh, w_out, ln1_g, ln1_b, w_up, w_fconv, w_down, ln2_g, ln2_b) = wts
    rows = B * T
    x2 = x.reshape(rows, D)
    yc, q, k, v, o, gates, z0, z1 = _inproj(x2, mod, w_in, w_gate, w_conv, tm=rows, tpb=1, sample=True,
                                            period=T, s0=conv_buf[:, 0], s1=conv_buf[:, 1])
    mrow = jnp.pad(jnp.repeat(m0, T, axis=0), ((0, 0), (N_HEADS, GATE_LANES - 2 * N_HEADS)))
    nrow = jnp.repeat(n0.reshape(B, N_HEADS * HEAD_DIM), T, axis=0)
    ym, C, n, m = _mlstm_sample(q[0], k[0], v[0], o[0], gates[0], bg, mrow, nrow, C0, n0, wmh, bb=8, T=T)
    x1 = _outproj(yc, ym[None], w_out, x2, mod, ln1_g, ln1_b, tm=rows, tpb=1, alpha=alpha)
    y, a0, a1, *ffn_w = _ffn(x1, mod, w_up, w_fconv, w_down, ln2_g, ln2_b, tm=rows, tpb=1, tf=256,
                             sample=True, alpha=alpha, period=T, s0=ffn_buf[:, 0], s1=ffn_buf[:, 1])
    return (y.reshape(B, T, D), jnp.stack([z0, z1], axis=1), C, n, m[..., 0],
            jnp.stack([a0, a1], axis=1), ffn_w)


def kernel(x_prompt, x_sample, c_prompt, c_sample, state_conv, state_mlstm_C, state_mlstm_n,
           state_mlstm_m, state_ffn_conv, w_ada, b_ada, w_in, b_gate, w_conv, w_mh_norm, w_out,
           ln1_g, ln1_b, w_up, w_ffn_conv, w_down, ln2_g, ln2_b):
    depth = w_in.shape[0]
    alpha = (2 * depth) ** 0.25
    Bp = x_prompt.shape[0]
    Bs, Ts, D = x_sample.shape
    dc = w_conv.shape[-1]
    dm = w_mh_norm.shape[-1]
    n_main = 3 * dc + 4 * dm
    assert dc == dm == N_HEADS * HEAD_DIM and Ts >= CONV_K - 1 and SLAB % Ts == 0

    xp, xs = x_prompt, x_sample
    outs_p = [[] for _ in range(5)]
    outs_s = [[] for _ in range(5)]
    for l in range(depth):
        c_all = jnp.concatenate([jnp.repeat(c_sample, Ts, axis=0), c_prompt], axis=0)
        mod = _ada(c_all, w_ada[l], b_ada[l])
        mod_s = mod.reshape(1, Bs * Ts + Bp, 6 * D)
        mod_p = mod[Bs * Ts:].reshape(Bp, 1, 6 * D)
        w_in_t = jnp.swapaxes(w_in[l], 0, 1)
        w_gate_t = jnp.pad(w_in_t[n_main:], ((0, GATE_LANES - 2 * N_HEADS), (0, 0)))
        wts = (
            _cast_transposed(w_in_t, n_main),
            jnp.swapaxes(w_gate_t, 0, 1).astype(BF16),
            jnp.pad(b_gate[l], (0, GATE_LANES - 2 * N_HEADS)).reshape(1, GATE_LANES),
            w_conv[l],
            w_mh_norm[l].reshape(1, dm),
            w_out[l].astype(BF16),
            ln1_g[l].reshape(1, D), ln1_b[l].reshape(1, D),
            w_up[l],
            w_ffn_conv[l],
            w_down[l],
            ln2_g[l].reshape(1, D), ln2_b[l].reshape(1, D),
        )
        xs, *st_s, ffn_w = _layer_sample(xs, mod_s, state_conv[l], state_mlstm_C[l], state_mlstm_n[l],
                                         state_mlstm_m[l], state_ffn_conv[l], wts, alpha=alpha)
        xp, *st_p = _layer_prompt(xp, mod_p, wts, ffn_w, alpha=alpha)
        for acc, val in zip(outs_p, st_p):
            acc.append(val)
        for acc, val in zip(outs_s, st_s):
            acc.append(val)
    return (xp.astype(x_prompt.dtype), xs.astype(x_sample.dtype),
            *[jnp.stack(a) for a in outs_p], *[jnp.stack(a) for a in outs_s])
```

```python
import functools

import jax
import jax.numpy as jnp
from jax import lax
from jax.experimental import pallas as pl
from jax.experimental.pallas import tpu as pltpu

F32 = jnp.float32
BF16 = jnp.bfloat16

N_HEADS = 4
HEAD_DIM = 256
CONV_K = 3
LN_EPS = 1e-5
NEG = -1e30
LANES = 128
GATE_LANES = LANES
STATE_ROWS = 8
VMEM_LIMIT = 56 * 1024 * 1024


def _cparams(sem):
    return pltpu.CompilerParams(dimension_semantics=sem, vmem_limit_bytes=VMEM_LIMIT)


def _ln(x):
    mu = jnp.mean(x, axis=-1, keepdims=True)
    xc = x - mu
    var = jnp.mean(xc * xc, axis=-1, keepdims=True)
    return xc * lax.rsqrt(var + LN_EPS)


def _log_sigmoid(x):
    return jnp.minimum(x, 0.0) - jnp.log1p(jnp.exp(-jnp.abs(x)))


def _sigmoid(x):
    return 1.0 / (1.0 + jnp.exp(-x))


def _dot(a, b):
    return jnp.dot(a, b, preferred_element_type=F32)


def _dot_nt(a, b):
    return lax.dot_general(a, b, (((1,), (1,)), ((), ())), preferred_element_type=F32)


def _conv3_rows(z, w, prev):
    p0 = prev[STATE_ROWS - 2:STATE_ROWS - 1]
    p1 = prev[STATE_ROWS - 1:STATE_ROWS]
    t = lax.broadcasted_iota(jnp.int32, z.shape, 0)
    z1 = jnp.where(t >= 1, pltpu.roll(z, 1, 0), p1)
    z2 = jnp.where(t >= 2, pltpu.roll(z, 2, 0), jnp.where(t == 0, p0, p1))
    return w[0:1] * z2 + w[1:2] * z1 + w[2:3] * z, z[z.shape[0] - STATE_ROWS:]


def _carried_rows(carry_ref, idx, first):
    @pl.when(first)
    def _():
        carry_ref[idx] = jnp.zeros(carry_ref.shape[1:], F32)

    return carry_ref[idx]


def _conv3_sequences(z, w, s0_ref, s1_ref, z_scr, y_scr, t0_ref, t1_ref, T):
    nseq = z.shape[0] // T
    y = w[0:1] * pltpu.roll(z, 2, 0) + w[1:2] * pltpu.roll(z, 1, 0) + w[2:3] * z

    def rows(t):
        return pl.ds(t, nseq, stride=T)

    cols = []
    for c in range(z.shape[1] // LANES):
        cs = slice(c * LANES, (c + 1) * LANES)
        w0, w1, w2 = w[0:1, cs], w[1:2, cs], w[2:3, cs]
        z_scr[c] = z[:, cs]
        y_scr[c] = y[:, cs]
        s0 = s0_ref[:, cs]
        s1 = s1_ref[:, cs]
        z0 = z_scr[c, rows(0), :]
        z1 = z_scr[c, rows(1), :]
        y_scr[c, rows(0), :] = w0 * s0 + w1 * s1 + w2 * z0
        y_scr[c, rows(1), :] = w0 * s1 + w1 * z0 + w2 * z1
        t0_ref[:, cs] = z_scr[c, rows(T - 2), :]
        t1_ref[:, cs] = z_scr[c, rows(T - 1), :]
        cols.append(y_scr[c])
    return jnp.concatenate(cols, axis=1)


def _ada_kernel(c_ref, w_ref, b_ref, o_ref):
    c = c_ref[...]
    s = (c * _sigmoid(c)).astype(BF16)
    o_ref[...] = _dot(s, w_ref[...].astype(BF16)) + b_ref[...]


def _ada(c, w, b, tn=1024):
    r, d = c.shape
    n = w.shape[1]
    return pl.pallas_call(
        _ada_kernel,
        grid=(n // tn,),
        in_specs=[
            pl.BlockSpec((r, d), lambda j: (0, 0)),
            pl.BlockSpec((d, tn), lambda j: (0, j)),
            pl.BlockSpec((1, tn), lambda j: (0, j)),
        ],
        out_specs=pl.BlockSpec((r, tn), lambda j: (0, j)),
        out_shape=jax.ShapeDtypeStruct((r, n), F32),
        compiler_params=_cparams(("arbitrary",)),
        name="ada",
    )(c, w, b.reshape(1, n))


def _cast_transposed_kernel(w_ref, o_ref):
    o_ref[...] = w_ref[...].T.astype(BF16)


def _cast_transposed(wt, n, tn=512):
    k = wt.shape[1]
    return pl.pallas_call(
        _cast_transposed_kernel,
        grid=(n // tn,),
        in_specs=[pl.BlockSpec((tn, k), lambda j: (j, 0))],
        out_specs=pl.BlockSpec((k, tn), lambda j: (0, j)),
        out_shape=jax.ShapeDtypeStruct((k, n), BF16),
        compiler_params=_cparams(("arbitrary",)),
        name="cast_transposed",
    )(wt)


def _inproj_kernel(*refs, tpb, period, sample):
    if sample:
        (x_ref, sh_ref, sc_ref, wb_ref, wc_ref, wh_ref, wq_ref, wk_ref, wv_ref, wo_ref, wg_ref,
         wconv_ref, s0_ref, s1_ref,
         yc_ref, q_ref, k_ref, v_ref, o_ref, g_ref, t0_ref, t1_ref, u_scr, z_scr, y_scr) = refs
    else:
        (x_ref, sh_ref, sc_ref, wb_ref, wc_ref, wh_ref, wq_ref, wk_ref, wv_ref, wo_ref, wg_ref,
         wconv_ref,
         yc_ref, q_ref, k_ref, v_ref, o_ref, g_ref, zt_ref, u_scr, carry_scr) = refs
    m = pl.program_id(0)
    j = pl.program_id(1)

    if not sample:
        carried = _carried_rows(carry_scr, j, lax.rem(m, tpb) == 0)
        rs = u_scr.shape[0] // 2

        def body(first):
            prev = carried
            for i in range(2):
                r = slice(i * rs, (i + 1) * rs)
                if first:
                    u = (_ln(x_ref[r, :]) * (1.0 + sc_ref[0]) + sh_ref[0]).astype(BF16)
                    u_scr[r, :] = u
                    g_ref[0, r, :] = _dot_nt(u, wg_ref[...])
                else:
                    u = u_scr[r, :]
                z = _dot(u, wc_ref[...]) * _dot(u, wh_ref[...])
                yc, prev = _conv3_rows(z, wconv_ref[...], prev)
                yc_ref[0, r, :] = (_dot(u, wb_ref[...]) * yc).astype(BF16)
                q_ref[0, r, :] = _dot(u, wq_ref[...]).astype(BF16)
                k_ref[0, r, :] = (_dot(u, wk_ref[...]) * (HEAD_DIM ** -0.5)).astype(BF16)
                v_ref[0, r, :] = _dot(u, wv_ref[...]).astype(BF16)
                o_ref[0, r, :] = _dot(u, wo_ref[...]).astype(BF16)
            carry_scr[j] = prev
            zt_ref[0] = prev

        pl.when(j == 0)(lambda: body(True))
        pl.when(j > 0)(lambda: body(False))
        return

    @pl.when(j == 0)
    def _():
        u = _ln(x_ref[...]) * (1.0 + sc_ref[0]) + sh_ref[0]
        ub = u.astype(BF16)
        u_scr[...] = ub
        g_ref[0] = _dot_nt(ub, wg_ref[...])

    u = u_scr[...]
    z = _dot(u, wc_ref[...]) * _dot(u, wh_ref[...])
    yc = _conv3_sequences(z, wconv_ref[...], s0_ref, s1_ref, z_scr, y_scr, t0_ref, t1_ref, period)
    half = u.shape[0] // 2
    bg = jnp.concatenate([_dot(u[:half], wb_ref[...]), _dot(u[half:], wb_ref[...])], axis=0)
    yc_ref[0] = (bg * yc).astype(BF16)
    q_ref[0] = _dot(u, wq_ref[...]).astype(BF16)
    k_ref[0] = (_dot(u, wk_ref[...]) * (HEAD_DIM ** -0.5)).astype(BF16)
    v_ref[0] = _dot(u, wv_ref[...]).astype(BF16)
    o_ref[0] = _dot(u, wo_ref[...]).astype(BF16)


def _inproj(x, mod, w_in, w_gate, w_conv, *, tm, tpb, sample, period=0, s0=None, s1=None):
    rows, d = x.shape
    dc = w_conv.shape[1]
    tn = HEAD_DIM
    nj = dc // tn
    nm = rows // tm
    nseq = nm // tpb
    r = tm if sample else 1

    def wspec(off):
        return pl.BlockSpec((d, tn), lambda m, j, off=off: (0, off * nj + j))

    in_specs = [
        pl.BlockSpec((tm, d), lambda m, j: (m, 0)),
        pl.BlockSpec((1, r, d), lambda m, j: (m // tpb, 0, 0)),
        pl.BlockSpec((1, r, d), lambda m, j: (m // tpb, 0, 1)),
        wspec(0), wspec(1), wspec(2), wspec(3), wspec(4), wspec(5), wspec(6),
        pl.BlockSpec((GATE_LANES, d), lambda m, j: (0, 0)),
        pl.BlockSpec((CONV_K, tn), lambda m, j: (0, j)),
    ]
    args = [x, mod, mod, w_in, w_in, w_in, w_in, w_in, w_in, w_in, w_gate, w_conv]
    scratch = [pltpu.VMEM((tm, d), BF16)]
    act = pl.BlockSpec((1, tm, tn), lambda m, j: (m // tpb, m % tpb, j))
    out_specs = [act, act, act, act, act,
                 pl.BlockSpec((1, tm, GATE_LANES), lambda m, j: (m // tpb, m % tpb, 0))]
    out_shape = [jax.ShapeDtypeStruct((nseq, tpb * tm, dc), BF16)] * 5 + [
        jax.ShapeDtypeStruct((nseq, tpb * tm, GATE_LANES), F32)]
    if sample:
        assert nm == 1
        st = pl.BlockSpec((tm // period, tn), lambda m, j: (0, j))
        in_specs += [st, st]
        args += [s0, s1]
        out_specs += [st, st]
        out_shape += [jax.ShapeDtypeStruct((tm // period, dc), F32)] * 2
        scratch += [pltpu.VMEM((tn // LANES, tm, LANES), F32)] * 2
    else:
        out_specs.append(pl.BlockSpec((1, STATE_ROWS, tn), lambda m, j: (m, 0, j)))
        out_shape.append(jax.ShapeDtypeStruct((nm, STATE_ROWS, dc), F32))
        scratch.append(pltpu.VMEM((nj, STATE_ROWS, tn), F32))
    return pl.pallas_call(
        functools.partial(_inproj_kernel, tpb=tpb, period=period, sample=sample),
        grid=(nm, nj),
        in_specs=in_specs,
        out_specs=out_specs,
        out_shape=out_shape,
        scratch_shapes=scratch,
        compiler_params=_cparams(("arbitrary", "arbitrary")),
        name="inproj_sample" if sample else "inproj_prompt",
    )(*args)


def _split3(x):
    hi = x.astype(BF16)
    r1 = x - hi.astype(F32)
    mid = r1.astype(BF16)
    lo = (r1 - mid.astype(F32)).astype(BF16)
    return hi, mid, lo


def _head_out(hh, o, wmh):
    return (_sigmoid(o.astype(F32)) * (_ln(hh) * wmh)).astype(BF16)


def _split2(x):
    hi = x.astype(BF16)
    return hi, (x - hi.astype(F32)).astype(BF16)


def _rowsum(x, ones):
    hi, lo = _split2(x)
    return _dot(hi, ones) + _dot(lo, ones)


def _rep2(x):
    return jnp.concatenate([x, x], axis=1)


def _mlstm_chunk_kernel(q_ref, k_ref, v_ref, o_ref, g_ref, bg_ref, wmh_ref,
                        y_ref, c_ref, n_ref, m_ref, nrep_scr, *, B, L):
    step = pl.program_id(0)

    @pl.when(step == 0)
    def _():
        c_ref[...] = jnp.zeros_like(c_ref)
        m_ref[...] = jnp.zeros_like(m_ref)
        nrep_scr[...] = jnp.zeros_like(nrep_scr)

    row = lax.broadcasted_iota(jnp.int32, (L, L), 0)
    col = lax.broadcasted_iota(jnp.int32, (L, L), 1)
    causal = col <= row
    tril = jnp.where(causal, 1.0, 0.0).astype(BF16)
    ones_l = jnp.ones((L, LANES), BF16)
    ones_d = jnp.ones((HEAD_DIM, LANES), BF16)
    inv_d = 1.0 / HEAD_DIM
    tn = (((0,), (0,)), ((), ()))

    heads = [(b, h) for b in range(B) for h in range(N_HEADS)]

    def hsl(h):
        return slice(h * HEAD_DIM, (h + 1) * HEAD_DIM)

    gate = []
    for b in range(B):
        g = g_ref[b] + bg_ref[...]
        hi, mid, lo = _split3(_log_sigmoid(g))
        bcum = _dot(tril, hi) + _dot(tril, mid) + _dot(tril, lo)
        gate.append((g, g.T, bcum, bcum.T, m_ref[b]))

    st = []
    for b, h in heads:
        g, g_t, bcum, bcum_t, m_all = gate[b]
        bc = jnp.broadcast_to(bcum[:, N_HEADS + h:N_HEADS + h + 1], (L, LANES))
        li = jnp.broadcast_to(g[:, h:h + 1], (L, LANES))
        br = bcum_t[N_HEADS + h:N_HEADS + h + 1, :]
        m_prev = m_all[h:h + 1, :]
        a = bc + m_prev
        dlog = jnp.where(causal, bc - br + g_t[h:h + 1, :], NEG)
        mt = jnp.maximum(a, jnp.max(dlog, axis=1, keepdims=True))
        st.append(dict(bc=bc, li=li, m_prev=m_prev, mt=mt, dw=jnp.exp(dlog - mt),
                       inter=jnp.exp(a - mt)))

    for (b, h), e in zip(heads, st):
        e["s"] = lax.dot_general(q_ref[b, :, hsl(h)], k_ref[b, :, hsl(h)], (((1,), (1,)), ((), ())),
                                 preferred_element_type=F32) * e["dw"]

    for i, ((b, h), e) in enumerate(zip(heads, st)):
        s_hi, s_lo = _split2(e["s"])
        cn = jnp.concatenate([c_ref[b, h], nrep_scr[i]], axis=1).astype(BF16)
        qc = _dot(q_ref[b, :, hsl(h)], cn)
        num = _dot(s_hi, v_ref[b, :, hsl(h)]) + _rep2(e["inter"]) * qc[:, :HEAD_DIM]
        den = _dot(s_hi, ones_l) + _dot(s_lo, ones_l) + e["inter"] * qc[:, HEAD_DIM:]
        rden = 1.0 / jnp.maximum(jnp.abs(den), jnp.exp(-e["mt"]))
        e["hh"] = num * _rep2(rden)

    for (b, h), e in zip(heads, st):
        hh = e["hh"]
        xc = hh - _rep2(_rowsum(hh, ones_d) * inv_d)
        rstd = lax.rsqrt(_rowsum(xc * xc, ones_d) * inv_d + LN_EPS)
        y = _sigmoid(o_ref[b, :, hsl(h)].astype(F32)) * (xc * _rep2(rstd) * wmh_ref[:, hsl(h)])
        y_ref[b, :, hsl(h)] = y.astype(BF16)

    m_rows = []
    for i, ((b, h), e) in enumerate(zip(heads, st)):
        kh = k_ref[b, :, hsl(h)]
        m_new = e["mt"][L - 1:L, :]
        b_last = e["bc"][L - 1:L, :]
        wc = jnp.exp(b_last - e["bc"] + e["li"] - m_new)
        dc = jnp.exp(b_last + e["m_prev"] - m_new)
        vw = (v_ref[b, :, hsl(h)].astype(F32) * _rep2(wc)).astype(BF16)
        c_ref[b, h] = _rep2(dc) * c_ref[b, h] + lax.dot_general(kh, vw, tn, preferred_element_type=F32)
        wc_hi, wc_lo = _split2(wc)
        nrep_scr[i] = (dc * nrep_scr[i] + lax.dot_general(kh, wc_hi, tn, preferred_element_type=F32)
                       + lax.dot_general(kh, wc_lo, tn, preferred_element_type=F32))
        m_rows.append(m_new)

    for b in range(B):
        m_ref[b] = jnp.concatenate(m_rows[b * N_HEADS:(b + 1) * N_HEADS], axis=0)

    @pl.when(step == pl.num_programs(0) - 1)
    def _():
        for b in range(B):
            n_ref[b] = jnp.concatenate(
                [nrep_scr[b * N_HEADS + h].T[0:1, :] for h in range(N_HEADS)], axis=0)


def _mlstm_prompt(q, k, v, o, gates, bg, wmh, *, L):
    b, t, dm = q.shape
    assert L == LANES
    act = pl.BlockSpec((b, L, dm), lambda c: (0, c, 0))
    whole = lambda *shape: pl.BlockSpec(shape, lambda c: (0,) * len(shape))
    return pl.pallas_call(
        functools.partial(_mlstm_chunk_kernel, B=b, L=L),
        grid=(t // L,),
        in_specs=[act, act, act, act,
                  pl.BlockSpec((b, L, GATE_LANES), lambda c: (0, c, 0)),
                  whole(1, GATE_LANES), whole(1, dm)],
        out_specs=[act,
                   whole(b, N_HEADS, HEAD_DIM, HEAD_DIM),
                   whole(b, N_HEADS, HEAD_DIM),
                   whole(b, N_HEADS, GATE_LANES)],
        out_shape=[jax.ShapeDtypeStruct((b, t, dm), BF16),
                   jax.ShapeDtypeStruct((b, N_HEADS, HEAD_DIM, HEAD_DIM), F32),
                   jax.ShapeDtypeStruct((b, N_HEADS, HEAD_DIM), F32),
                   jax.ShapeDtypeStruct((b, N_HEADS, GATE_LANES), F32)],
        scratch_shapes=[pltpu.VMEM((b * N_HEADS, HEAD_DIM, LANES), F32)],
        compiler_params=_cparams(("arbitrary",)),
        name="mlstm_prompt",
    )(q, k, v, o, gates, bg, wmh)


SLAB = 16


def _mlstm_step_kernel(q_ref, k_ref, v_ref, o_ref, g_ref, bg_ref, mrow_ref, nrow_ref,
                       c0_ref, n0_ref, wmh_ref,
                       y_ref, c_ref, n_ref, m_ref, *, bb, T):
    R = bb * T
    per_slab = SLAB // T
    t = lax.rem(lax.broadcasted_iota(jnp.int32, (R, GATE_LANES), 0), T)

    def down(x, d):
        return pltpu.roll(x, d, 0)

    def up(x, d):
        return pltpu.roll(x, x.shape[0] - d, 0)

    def seg_last(x):
        out = x
        for d in range(1, T):
            out = jnp.where(t == T - 1 - d, up(x, d), out)
        return out

    g = g_ref[...] + bg_ref[...]
    li = pltpu.roll(g, N_HEADS, 1)
    lf = _log_sigmoid(g)
    b = lf
    for d in range(1, T):
        b = b + jnp.where(t >= d, down(lf, d), 0.0)
    m_prev = mrow_ref[...]
    a = b + m_prev
    dl = [li] + [jnp.where(t >= d, b - down(b, d) + down(li, d), NEG) for d in range(1, T)]
    mt = a
    for d in range(T):
        mt = jnp.maximum(mt, dl[d])
    dw = [jnp.exp(dl[d] - mt) for d in range(T)]
    inter = jnp.exp(a - mt)
    emt = jnp.exp(-mt)
    m_new = seg_last(mt)
    b_last = seg_last(b)
    wc = jnp.exp(b_last - b + li - m_new)
    dc = jnp.exp(b_last + m_prev - m_new)

    row_s = lax.broadcasted_iota(jnp.int32, (SLAB, HEAD_DIM), 0)
    row_r = lax.broadcasted_iota(jnp.int32, (R, HEAD_DIM), 0)

    for h in range(N_HEADS):
        hs = slice(h * HEAD_DIM, (h + 1) * HEAD_DIM)
        ln = N_HEADS + h

        def col(x):
            return x[:, ln:ln + 1]

        qb = q_ref[:, hs]
        kb = k_ref[:, hs]
        vb = v_ref[:, hs]
        qf = qb.astype(F32)
        kf = kb.astype(F32)
        vf = vb.astype(F32)
        num = jnp.zeros((R, HEAD_DIM), F32)
        den = jnp.zeros((R, 1), F32)
        for d in range(T):
            kd = kf if d == 0 else down(kf, d)
            vd = vf if d == 0 else down(vf, d)
            sw = jnp.sum(qf * kd, axis=1, keepdims=True) * col(dw[d])
            num = num + sw * vd
            den = den + sw

        qc_slabs = []
        for si in range(R // SLAB):
            q16 = qb[si * SLAB:(si + 1) * SLAB]
            acc = jnp.zeros((SLAB, HEAD_DIM), F32)
            for bl in range(per_slab):
                bi = si * per_slab + bl
                r = _dot(q16, c0_ref[bi, h].astype(BF16))
                acc = jnp.where(row_s // T == bl, r, acc)
            qc_slabs.append(acc)
        qc = jnp.concatenate(qc_slabs, axis=0)
        qn = jnp.sum(qf * nrow_ref[:, hs], axis=1, keepdims=True)
        num = num + col(inter) * qc
        den = den + col(inter) * qn
        hh = num / jnp.maximum(jnp.abs(den), col(emt))
        y_ref[:, hs] = _head_out(hh, o_ref[:, hs], wmh_ref[:, hs])

        wk = kf * col(wc)
        vw = vf * col(wc)
        for si in range(R // SLAB):
            k16 = kb[si * SLAB:(si + 1) * SLAB]
            vw16 = vw[si * SLAB:(si + 1) * SLAB]
            for bl in range(per_slab):
                bi = si * per_slab + bl
                last = bi * T + T - 1
                vwb = jnp.where(row_s // T == bl, vw16, 0.0).astype(BF16)
                dcb = dc[last:last + 1, ln:ln + 1]
                c_ref[bi, h] = dcb * c0_ref[bi, h] + lax.dot_general(
                    k16, vwb, (((0,), (0,)), ((), ())), preferred_element_type=F32)
                n_ref[bi, h:h + 1, :] = dcb * n0_ref[bi, h:h + 1, :] + jnp.sum(
                    jnp.where(row_r // T == bi, wk, 0.0), axis=0, keepdims=True)
                m_ref[bi, h:h + 1, :] = jnp.broadcast_to(
                    m_new[last:last + 1, ln:ln + 1], (1, GATE_LANES))


def _mlstm_sample(q, k, v, o, gates, bg, mrow, nrow, c0, n0, wmh, *, bb, T):
    rows, dm = q.shape
    nb = c0.shape[0]
    R = bb * T
    act = pl.BlockSpec((R, dm), lambda i: (i, 0))
    gat = pl.BlockSpec((R, GATE_LANES), lambda i: (i, 0))
    cspec = pl.BlockSpec((bb, N_HEADS, HEAD_DIM, HEAD_DIM), lambda i: (i, 0, 0, 0))
    nspec = pl.BlockSpec((bb, N_HEADS, HEAD_DIM), lambda i: (i, 0, 0))
    return pl.pallas_call(
        functools.partial(_mlstm_step_kernel, bb=bb, T=T),
        grid=(nb // bb,),
        in_specs=[act, act, act, act, gat,
                  pl.BlockSpec((1, GATE_LANES), lambda i: (0, 0)),
                  gat,
                  pl.BlockSpec((R, dm), lambda i: (i, 0)),
                  cspec, nspec,
                  pl.BlockSpec((1, dm), lambda i: (0, 0))],
        out_specs=[act, cspec, nspec,
                   pl.BlockSpec((bb, N_HEADS, GATE_LANES), lambda i: (i, 0, 0))],
        out_shape=[jax.ShapeDtypeStruct((rows, dm), BF16),
                   jax.ShapeDtypeStruct(c0.shape, F32),
                   jax.ShapeDtypeStruct(n0.shape, F32),
                   jax.ShapeDtypeStruct((nb, N_HEADS, GATE_LANES), F32)],
        compiler_params=_cparams(("arbitrary",)),
        name="mlstm_sample",
    )(q, k, v, o, gates, bg, mrow, nrow, c0, n0, wmh)


def _outproj_kernel(yc_ref, ym_ref, w_ref, x_ref, g1_ref, lg_ref, lb_ref, o_ref, *, alpha, splits):
    dc = yc_ref.shape[-1]
    rs = x_ref.shape[0] // splits
    for i in range(splits):
        r = slice(i * rs, (i + 1) * rs)
        g1 = g1_ref[0] if g1_ref.shape[1] == 1 else g1_ref[0, r, :]
        mix = _dot(yc_ref[0, r, :], w_ref[0:dc, :]) + _dot(ym_ref[0, r, :], w_ref[dc:, :])
        o_ref[r, :] = _ln(alpha * x_ref[r, :] + (1.0 + g1) * mix) * lg_ref[...] + lb_ref[...]


def _outproj(yc, ym, w_out, x, mod, ln_g, ln_b, *, tm, tpb, alpha, splits=2):
    rows, d = x.shape
    dc = yc.shape[-1]
    dm = ym.shape[-1]
    r = 1 if mod.shape[1] == 1 else tm
    vec = pl.BlockSpec((1, d), lambda m: (0, 0))
    return pl.pallas_call(
        functools.partial(_outproj_kernel, alpha=alpha, splits=splits),
        grid=(rows // tm,),
        in_specs=[pl.BlockSpec((1, tm, dc), lambda m: (m // tpb, m % tpb, 0)),
                  pl.BlockSpec((1, tm, dm), lambda m: (m // tpb, m % tpb, 0)),
                  pl.BlockSpec((dc + dm, d), lambda m: (0, 0)),
                  pl.BlockSpec((tm, d), lambda m: (m, 0)),
                  pl.BlockSpec((1, r, d), lambda m: (m // tpb, 0, 2)),
                  vec, vec],
        out_specs=pl.BlockSpec((tm, d), lambda m: (m, 0)),
        out_shape=jax.ShapeDtypeStruct((rows, d), F32),
        compiler_params=_cparams(("arbitrary",)),
        name="outproj",
    )(yc, ym, w_out, x, mod, ln_g, ln_b)


def _ffn_kernel(*refs, nf, tpb, period, sample, alpha, splits):
    if sample:
        (x_ref, sh_ref, sc_ref, g2_ref, wa_ref, wg_ref, wconv_ref, wd_ref, lg_ref, lb_ref,
         s0_ref, s1_ref, y_ref, t0_ref, t1_ref, wa_out, wg_out, wd_out, u_scr, z_scr, y_scr) = refs
    else:
        (x_ref, sh_ref, sc_ref, g2_ref, wa_ref, wg_ref, wconv_ref, wd_ref, lg_ref, lb_ref,
         y_ref, at_ref, u_scr, carry_scr) = refs
    m = pl.program_id(0)
    f = pl.program_id(1)

    if not sample:
        carried = _carried_rows(carry_scr, f, lax.rem(m, tpb) == 0)
        rs = u_scr.shape[0] // splits

        def body(first, last):
            prev = carried
            for i in range(splits):
                r = slice(i * rs, (i + 1) * rs)
                if first:
                    u = (_ln(x_ref[r, :]) * (1.0 + sc_ref[0]) + sh_ref[0]).astype(BF16)
                    u_scr[r, :] = u
                else:
                    u = u_scr[r, :]
                ac, prev = _conv3_rows(_dot(u, wa_ref[...]), wconv_ref[...], prev)
                hcur = (ac * _sigmoid(ac) * _dot(u, wg_ref[...])).astype(BF16)
                acc = _dot(hcur, wd_ref[...])
                if not first:
                    acc = y_ref[r, :] + acc
                if last:
                    acc = (_ln(alpha * x_ref[r, :] + (1.0 + g2_ref[0]) * acc)
                           * lg_ref[...] + lb_ref[...])
                y_ref[r, :] = acc
            carry_scr[f] = prev
            at_ref[0] = prev

        if nf == 1:
            body(True, True)
        else:
            pl.when(f == 0)(lambda: body(True, False))
            if nf > 2:
                pl.when(jnp.logical_and(f > 0, f < nf - 1))(lambda: body(False, False))
            pl.when(f == nf - 1)(lambda: body(False, True))
        return

    @pl.when(f == 0)
    def _():
        u = _ln(x_ref[...]) * (1.0 + sc_ref[0]) + sh_ref[0]
        u_scr[...] = u.astype(BF16)
        y_ref[...] = jnp.zeros_like(y_ref)

    wa = wa_ref[...].astype(BF16)
    wg = wg_ref[...].astype(BF16)
    wd = wd_ref[...].astype(BF16)
    wa_out[...] = wa
    wg_out[...] = wg
    wd_out[...] = wd
    u = u_scr[...]
    a = _dot(u, wa)
    ac = _conv3_sequences(a, wconv_ref[...], s0_ref, s1_ref, z_scr, y_scr, t0_ref, t1_ref, period)
    hcur = (ac * _sigmoid(ac) * _dot(u, wg)).astype(BF16)
    y_ref[...] += _dot(hcur, wd)

    @pl.when(f == nf - 1)
    def _():
        y_ref[...] = (_ln(alpha * x_ref[...] + (1.0 + g2_ref[0]) * y_ref[...])
                      * lg_ref[...] + lb_ref[...])


def _ffn(x, mod, w_up, w_conv, w_down, ln_g, ln_b, *, tm, tpb, tf, sample, alpha, period=0,
         splits=1, s0=None, s1=None):
    rows, d = x.shape
    ff = w_down.shape[0]
    nf = ff // tf
    nm = rows // tm
    r = tm if sample else 1
    vec = pl.BlockSpec((1, d), lambda m, f: (0, 0))
    w_a, w_g = (w_up, w_up) if sample else w_up
    up = pl.BlockSpec((d, tf), lambda m, f: (0, f))
    down = pl.BlockSpec((tf, d), lambda m, f: (f, 0))
    in_specs = [
        pl.BlockSpec((tm, d), lambda m, f: (m, 0), pipeline_mode=pl.Buffered(1)),
        pl.BlockSpec((1, r, d), lambda m, f: (m // tpb, 0, 3)),
        pl.BlockSpec((1, r, d), lambda m, f: (m // tpb, 0, 4)),
        pl.BlockSpec((1, r, d), lambda m, f: (m // tpb, 0, 5)),
        up,
        pl.BlockSpec((d, tf), lambda m, f: (0, nf + f)) if sample else up,
        pl.BlockSpec((CONV_K, tf), lambda m, f: (0, f)),
        down,
        vec, vec,
    ]
    args = [x, mod, mod, mod, w_a, w_g, w_conv, w_down, ln_g, ln_b]
    scratch = [pltpu.VMEM((tm, d), BF16)]
    out_specs = [pl.BlockSpec((tm, d), lambda m, f: (m, 0))]
    out_shape = [jax.ShapeDtypeStruct((rows, d), F32)]
    if sample:
        assert nm == 1
        st = pl.BlockSpec((tm // period, tf), lambda m, f: (0, f))
        in_specs += [st, st]
        args += [s0, s1]
        out_specs += [st, st, up, up, down]
        out_shape += [jax.ShapeDtypeStruct((tm // period, ff), F32)] * 2 + [
            jax.ShapeDtypeStruct((d, ff), BF16), jax.ShapeDtypeStruct((d, ff), BF16),
            jax.ShapeDtypeStruct((ff, d), BF16)]
        scratch += [pltpu.VMEM((tf // LANES, tm, LANES), F32)] * 2
    else:
        out_specs.append(pl.BlockSpec((1, STATE_ROWS, tf), lambda m, f: (m, 0, f)))
        out_shape.append(jax.ShapeDtypeStruct((nm, STATE_ROWS, ff), F32))
        scratch.append(pltpu.VMEM((nf, STATE_ROWS, tf), F32))
    return pl.pallas_call(
        functools.partial(_ffn_kernel, nf=nf, tpb=tpb, period=period, sample=sample, alpha=alpha,
                          splits=splits),
        grid=(nm, nf),
        in_specs=in_specs,
        out_specs=out_specs,
        out_shape=out_shape,
        scratch_shapes=scratch,
        compiler_params=_cparams(("arbitrary", "arbitrary")),
        name="ffn_sample" if sample else "ffn_prompt",
    )(*args)


def _layer_prompt(x, mod, wts, ffn_w, *, alpha):
    B, T, D = x.shape
    (w_in, w_gate, bg, w_conv, wmh, w_out, ln1_g, ln1_b, _, w_fconv, _, ln2_g, ln2_b) = wts
    w_a, w_g, w_down = ffn_w
    tm = 1024
    tpb = T // tm
    x2 = x.reshape(B * T, D)
    yc, q, k, v, o, gates, ztail = _inproj(x2, mod, w_in, w_gate, w_conv, tm=tm, tpb=tpb, sample=False)
    ym, C, n, m = _mlstm_prompt(q, k, v, o, gates, bg, wmh, L=128)
    tm2 = 512
    x1 = _outproj(yc, ym, w_out, x2, mod, ln1_g, ln1_b, tm=tm2, tpb=T // tm2, alpha=alpha)
    y, atail = _ffn(x1, mod, (w_a, w_g), w_fconv, w_down, ln2_g, ln2_b,
                    tm=tm, tpb=tpb, tf=512, sample=False, alpha=alpha, splits=2)
    return (y.reshape(B, T, D), ztail[tpb - 1::tpb, STATE_ROWS - 2:], C, n, m[..., 0],
            atail[tpb - 1::tpb, STATE_ROWS - 2:])


def _layer_sample(x, mod, conv_buf, C0, n0, m0, ffn_buf, wts, *, alpha):
    B, T, D = x.shape
    (w_in, w_gate, bg, w_conv, wmh, w_out, ln1_g, ln1_b, w_up, w_fconv, w_down, ln2_g, ln2_b) = wts
    rows = B * T
    x2 = x.reshape(rows, D)
    yc, q, k, v, o, gates, z0, z1 = _inproj(x2, mod, w_in, w_gate, w_conv, tm=rows, tpb=1, sample=True,
                                            period=T, s0=conv_buf[:, 0], s1=conv_buf[:, 1])
    mrow = jnp.pad(jnp.repeat(m0, T, axis=0), ((0, 0), (N_HEADS, GATE_LANES - 2 * N_HEADS)))
    nrow = jnp.repeat(n0.reshape(B, N_HEADS * HEAD_DIM), T, axis=0)
    ym, C, n, m = _mlstm_sample(q[0], k[0], v[0], o[0], gates[0], bg, mrow, nrow, C0, n0, wmh, bb=8, T=T)
    x1 = _outproj(yc, ym[None], w_out, x2, mod, ln1_g, ln1_b, tm=rows, tpb=1, alpha=alpha)
    y, a0, a1, *ffn_w = _ffn(x1, mod, w_up, w_fconv, w_down, ln2_g, ln2_b, tm=rows, tpb=1, tf=256,
                             sample=True, alpha=alpha, period=T, s0=ffn_buf[:, 0], s1=ffn_buf[:, 1])
    return (y.reshape(B, T, D), jnp.stack([z0, z1], axis=1), C, n, m[..., 0],
            jnp.stack([a0, a1], axis=1), ffn_w)


def kernel(x_prompt, x_sample, c_prompt, c_sample, state_conv, state_mlstm_C, state_mlstm_n,
           state_mlstm_m, state_ffn_conv, w_ada, b_ada, w_in, b_gate, w_conv, w_mh_norm, w_out,
           ln1_g, ln1_b, w_up, w_ffn_conv, w_down, ln2_g, ln2_b):
    depth = w_in.shape[0]
    alpha = (2 * depth) ** 0.25
    Bp = x_prompt.shape[0]
    Bs, Ts, D = x_sample.shape
    dc = w_conv.shape[-1]
    dm = w_mh_norm.shape[-1]
    n_main = 3 * dc + 4 * dm
    assert dc == dm == N_HEADS * HEAD_DIM and Ts >= CONV_K - 1 and SLAB % Ts == 0

    xp, xs = x_prompt, x_sample
    outs_p = [[] for _ in range(5)]
    outs_s = [[] for _ in range(5)]
    for l in range(depth):
        c_all = jnp.concatenate([jnp.repeat(c_sample, Ts, axis=0), c_prompt], axis=0)
        mod = _ada(c_all, w_ada[l], b_ada[l])
        mod_s = mod.reshape(1, Bs * Ts + Bp, 6 * D)
        mod_p = mod[Bs * Ts:].reshape(Bp, 1, 6 * D)
        w_in_t = jnp.swapaxes(w_in[l], 0, 1)
        wts = (
            _cast_transposed(w_in_t, n_main),
            jnp.pad(w_in_t[n_main:], ((0, GATE_LANES - 2 * N_HEADS), (0, 0))).astype(BF16),
            jnp.pad(b_gate[l], (0, GATE_LANES - 2 * N_HEADS)).reshape(1, GATE_LANES),
            w_conv[l],
            w_mh_norm[l].reshape(1, dm),
            w_out[l].astype(BF16),
            ln1_g[l].reshape(1, D), ln1_b[l].reshape(1, D),
            w_up[l],
            w_ffn_conv[l],
            w_down[l],
            ln2_g[l].reshape(1, D), ln2_b[l].reshape(1, D),
        )
        xs, *st_s, ffn_w = _layer_sample(xs, mod_s, state_conv[l], state_mlstm_C[l], state_mlstm_n[l],
                                         state_mlstm_m[l], state_ffn_conv[l], wts, alpha=alpha)
        xp, *st_p = _layer_prompt(xp, mod_p, wts, ffn_w, alpha=alpha)
        for acc, val in zip(outs_p, st_p):
            acc.append(val)
        for acc, val in zip(outs_s, st_s):
            acc.append(val)
    return (xp.astype(x_prompt.dtype), xs.astype(x_sample.dtype),
            *[jnp.stack(a) for a in outs_p], *[jnp.stack(a) for a in outs_s])
```

```python
import functools

import jax
import jax.numpy as jnp
from jax import lax
from jax.experimental import pallas as pl
from jax.experimental.pallas import tpu as pltpu

F32 = jnp.float32
BF16 = jnp.bfloat16

N_HEADS = 4
HEAD_DIM = 256
CONV_K = 3
LN_EPS = 1e-5
NEG = -1e30
LANES = 128
GATE_LANES = LANES
STATE_ROWS = 8
VMEM_LIMIT = 56 * 1024 * 1024


def _cparams(sem):
    return pltpu.CompilerParams(dimension_semantics=sem, vmem_limit_bytes=VMEM_LIMIT)


def _ln(x):
    mu = jnp.mean(x, axis=-1, keepdims=True)
    xc = x - mu
    var = jnp.mean(xc * xc, axis=-1, keepdims=True)
    return xc * lax.rsqrt(var + LN_EPS)


def _log_sigmoid(x):
    return jnp.minimum(x, 0.0) - jnp.log1p(jnp.exp(-jnp.abs(x)))


def _sigmoid(x):
    return 1.0 / (1.0 + jnp.exp(-x))


def _dot(a, b):
    return jnp.dot(a, b, preferred_element_type=F32)


def _dot_nt(a, b):
    return lax.dot_general(a, b, (((1,), (1,)), ((), ())), preferred_element_type=F32)


def _conv3_rows(z, w, prev):
    p0 = prev[STATE_ROWS - 2:STATE_ROWS - 1]
    p1 = prev[STATE_ROWS - 1:STATE_ROWS]
    t = lax.broadcasted_iota(jnp.int32, z.shape, 0)
    z1 = jnp.where(t >= 1, pltpu.roll(z, 1, 0), p1)
    z2 = jnp.where(t >= 2, pltpu.roll(z, 2, 0), jnp.where(t == 0, p0, p1))
    return w[0:1] * z2 + w[1:2] * z1 + w[2:3] * z, z[z.shape[0] - STATE_ROWS:]


def _carried_rows(carry_ref, idx, first):
    @pl.when(first)
    def _():
        carry_ref[idx] = jnp.zeros(carry_ref.shape[1:], F32)

    return carry_ref[idx]


def _conv3_sequences(z, w, s0_ref, s1_ref, z_scr, y_scr, t0_ref, t1_ref, T):
    nseq = z.shape[0] // T
    y = w[0:1] * pltpu.roll(z, 2, 0) + w[1:2] * pltpu.roll(z, 1, 0) + w[2:3] * z

    def rows(t):
        return pl.ds(t, nseq, stride=T)

    cols = []
    for c in range(z.shape[1] // LANES):
        cs = slice(c * LANES, (c + 1) * LANES)
        w0, w1, w2 = w[0:1, cs], w[1:2, cs], w[2:3, cs]
        z_scr[c] = z[:, cs]
        y_scr[c] = y[:, cs]
        s0 = s0_ref[:, cs]
        s1 = s1_ref[:, cs]
        z0 = z_scr[c, rows(0), :]
        z1 = z_scr[c, rows(1), :]
        y_scr[c, rows(0), :] = w0 * s0 + w1 * s1 + w2 * z0
        y_scr[c, rows(1), :] = w0 * s1 + w1 * z0 + w2 * z1
        t0_ref[:, cs] = z_scr[c, rows(T - 2), :]
        t1_ref[:, cs] = z_scr[c, rows(T - 1), :]
        cols.append(y_scr[c])
    return jnp.concatenate(cols, axis=1)


def _ada_kernel(c_ref, w_ref, b_ref, o_ref):
    c = c_ref[...]
    s = (c * _sigmoid(c)).astype(BF16)
    o_ref[...] = _dot(s, w_ref[...].astype(BF16)) + b_ref[...]


def _ada(c, w, b, tn=2048):
    r, d = c.shape
    n = w.shape[1]
    return pl.pallas_call(
        _ada_kernel,
        grid=(n // tn,),
        in_specs=[
            pl.BlockSpec((r, d), lambda j: (0, 0)),
            pl.BlockSpec((d, tn), lambda j: (0, j)),
            pl.BlockSpec((1, tn), lambda j: (0, j)),
        ],
        out_specs=pl.BlockSpec((r, tn), lambda j: (0, j)),
        out_shape=jax.ShapeDtypeStruct((r, n), F32),
        compiler_params=_cparams(("arbitrary",)),
        name="ada",
    )(c, w, b.reshape(1, n))


def _cast_transposed_kernel(w_ref, o_ref):
    o_ref[...] = w_ref[...].T.astype(BF16)


def _cast_transposed(wt, n, tn=512):
    k = wt.shape[1]
    return pl.pallas_call(
        _cast_transposed_kernel,
        grid=(n // tn,),
        in_specs=[pl.BlockSpec((tn, k), lambda j: (j, 0))],
        out_specs=pl.BlockSpec((k, tn), lambda j: (0, j)),
        out_shape=jax.ShapeDtypeStruct((k, n), BF16),
        compiler_params=_cparams(("arbitrary",)),
        name="cast_transposed",
    )(wt)


def _inproj_kernel(*refs, tpb, period, sample):
    if sample:
        (x_ref, sh_ref, sc_ref, wb_ref, wc_ref, wh_ref, wq_ref, wk_ref, wv_ref, wo_ref, wg_ref,
         wconv_ref, s0_ref, s1_ref,
         yc_ref, q_ref, k_ref, v_ref, o_ref, g_ref, t0_ref, t1_ref, u_scr, z_scr, y_scr) = refs
    else:
        (x_ref, sh_ref, sc_ref, wb_ref, wc_ref, wh_ref, wq_ref, wk_ref, wv_ref, wo_ref, wg_ref,
         wconv_ref,
         yc_ref, q_ref, k_ref, v_ref, o_ref, g_ref, zt_ref, u_scr, carry_scr) = refs
    m = pl.program_id(0)
    j = pl.program_id(1)

    if not sample:
        carried = _carried_rows(carry_scr, j, lax.rem(m, tpb) == 0)
        rs = u_scr.shape[0] // 2

        def body(first):
            prev = carried
            for i in range(2):
                r = slice(i * rs, (i + 1) * rs)
                if first:
                    u = (_ln(x_ref[r, :]) * (1.0 + sc_ref[0]) + sh_ref[0]).astype(BF16)
                    u_scr[r, :] = u
                    g_ref[0, r, :] = _dot_nt(u, wg_ref[...])
                else:
                    u = u_scr[r, :]
                z = _dot(u, wc_ref[...]) * _dot(u, wh_ref[...])
                yc, prev = _conv3_rows(z, wconv_ref[...], prev)
                yc_ref[0, r, :] = (_dot(u, wb_ref[...]) * yc).astype(BF16)
                q_ref[0, r, :] = _dot(u, wq_ref[...]).astype(BF16)
                k_ref[0, r, :] = (_dot(u, wk_ref[...]) * (HEAD_DIM ** -0.5)).astype(BF16)
                v_ref[0, r, :] = _dot(u, wv_ref[...]).astype(BF16)
                o_ref[0, r, :] = _dot(u, wo_ref[...]).astype(BF16)
            carry_scr[j] = prev
            zt_ref[0] = prev

        pl.when(j == 0)(lambda: body(True))
        pl.when(j > 0)(lambda: body(False))
        return

    @pl.when(j == 0)
    def _():
        u = _ln(x_ref[...]) * (1.0 + sc_ref[0]) + sh_ref[0]
        ub = u.astype(BF16)
        u_scr[...] = ub
        g_ref[0] = _dot_nt(ub, wg_ref[...])

    u = u_scr[...]
    z = _dot(u, wc_ref[...]) * _dot(u, wh_ref[...])
    yc = _conv3_sequences(z, wconv_ref[...], s0_ref, s1_ref, z_scr, y_scr, t0_ref, t1_ref, period)
    half = u.shape[0] // 2
    bg = jnp.concatenate([_dot(u[:half], wb_ref[...]), _dot(u[half:], wb_ref[...])], axis=0)
    yc_ref[0] = (bg * yc).astype(BF16)
    q_ref[0] = _dot(u, wq_ref[...]).astype(BF16)
    k_ref[0] = (_dot(u, wk_ref[...]) * (HEAD_DIM ** -0.5)).astype(BF16)
    v_ref[0] = _dot(u, wv_ref[...]).astype(BF16)
    o_ref[0] = _dot(u, wo_ref[...]).astype(BF16)


def _inproj(x, mod, w_in, w_gate, w_conv, *, tm, tpb, sample, period=0, s0=None, s1=None):
    rows, d = x.shape
    dc = w_conv.shape[1]
    tn = HEAD_DIM
    nj = dc // tn
    nm = rows // tm
    nseq = nm // tpb
    r = tm if sample else 1

    def wspec(off):
        return pl.BlockSpec((d, tn), lambda m, j, off=off: (0, off * nj + j))

    in_specs = [
        pl.BlockSpec((tm, d), lambda m, j: (m, 0)),
        pl.BlockSpec((1, r, d), lambda m, j: (m // tpb, 0, 0)),
        pl.BlockSpec((1, r, d), lambda m, j: (m // tpb, 0, 1)),
        wspec(0), wspec(1), wspec(2), wspec(3), wspec(4), wspec(5), wspec(6),
        pl.BlockSpec((GATE_LANES, d), lambda m, j: (0, 0)),
        pl.BlockSpec((CONV_K, tn), lambda m, j: (0, j)),
    ]
    args = [x, mod, mod, w_in, w_in, w_in, w_in, w_in, w_in, w_in, w_gate, w_conv]
    scratch = [pltpu.VMEM((tm, d), BF16)]
    act = pl.BlockSpec((1, tm, tn), lambda m, j: (m // tpb, m % tpb, j))
    out_specs = [act, act, act, act, act,
                 pl.BlockSpec((1, tm, GATE_LANES), lambda m, j: (m // tpb, m % tpb, 0))]
    out_shape = [jax.ShapeDtypeStruct((nseq, tpb * tm, dc), BF16)] * 5 + [
        jax.ShapeDtypeStruct((nseq, tpb * tm, GATE_LANES), F32)]
    if sample:
        assert nm == 1
        st = pl.BlockSpec((tm // period, tn), lambda m, j: (0, j))
        in_specs += [st, st]
        args += [s0, s1]
        out_specs += [st, st]
        out_shape += [jax.ShapeDtypeStruct((tm // period, dc), F32)] * 2
        scratch += [pltpu.VMEM((tn // LANES, tm, LANES), F32)] * 2
    else:
        out_specs.append(pl.BlockSpec((1, STATE_ROWS, tn), lambda m, j: (m, 0, j)))
        out_shape.append(jax.ShapeDtypeStruct((nm, STATE_ROWS, dc), F32))
        scratch.append(pltpu.VMEM((nj, STATE_ROWS, tn), F32))
    return pl.pallas_call(
        functools.partial(_inproj_kernel, tpb=tpb, period=period, sample=sample),
        grid=(nm, nj),
        in_specs=in_specs,
        out_specs=out_specs,
        out_shape=out_shape,
        scratch_shapes=scratch,
        compiler_params=_cparams(("arbitrary", "arbitrary")),
        name="inproj_sample" if sample else "inproj_prompt",
    )(*args)


def _split3(x):
    hi = x.astype(BF16)
    r1 = x - hi.astype(F32)
    mid = r1.astype(BF16)
    lo = (r1 - mid.astype(F32)).astype(BF16)
    return hi, mid, lo


def _head_out(hh, o, wmh):
    return (_sigmoid(o.astype(F32)) * (_ln(hh) * wmh)).astype(BF16)


def _split2(x):
    hi = x.astype(BF16)
    return hi, (x - hi.astype(F32)).astype(BF16)


def _rowsum(x, ones):
    hi, lo = _split2(x)
    return _dot(hi, ones) + _dot(lo, ones)


def _rep2(x):
    return jnp.concatenate([x, x], axis=1)


def _mlstm_chunk_kernel(q_ref, k_ref, v_ref, o_ref, g_ref, bg_ref, wmh_ref,
                        y_ref, c_ref, n_ref, m_ref, nrep_scr, *, B, L):
    step = pl.program_id(0)

    @pl.when(step == 0)
    def _():
        c_ref[...] = jnp.zeros_like(c_ref)
        m_ref[...] = jnp.zeros_like(m_ref)
        nrep_scr[...] = jnp.zeros_like(nrep_scr)

    row = lax.broadcasted_iota(jnp.int32, (L, L), 0)
    col = lax.broadcasted_iota(jnp.int32, (L, L), 1)
    causal = col <= row
    tril = jnp.where(causal, 1.0, 0.0).astype(BF16)
    ones_l = jnp.ones((L, LANES), BF16)
    ones_d = jnp.ones((HEAD_DIM, LANES), BF16)
    inv_d = 1.0 / HEAD_DIM
    tn = (((0,), (0,)), ((), ()))

    heads = [(b, h) for b in range(B) for h in range(N_HEADS)]

    def hsl(h):
        return slice(h * HEAD_DIM, (h + 1) * HEAD_DIM)

    gate = []
    for b in range(B):
        g = g_ref[b] + bg_ref[...]
        hi, mid, lo = _split3(_log_sigmoid(g))
        bcum = _dot(tril, hi) + _dot(tril, mid) + _dot(tril, lo)
        gate.append((g, g.T, bcum, bcum.T, m_ref[b]))

    st = []
    for b, h in heads:
        g, g_t, bcum, bcum_t, m_all = gate[b]
        bc = jnp.broadcast_to(bcum[:, N_HEADS + h:N_HEADS + h + 1], (L, LANES))
        li = jnp.broadcast_to(g[:, h:h + 1], (L, LANES))
        br = bcum_t[N_HEADS + h:N_HEADS + h + 1, :]
        m_prev = m_all[h:h + 1, :]
        a = bc + m_prev
        dlog = jnp.where(causal, bc - br + g_t[h:h + 1, :], NEG)
        mt = jnp.maximum(a, jnp.max(dlog, axis=1, keepdims=True))
        st.append(dict(bc=bc, li=li, m_prev=m_prev, mt=mt, dw=jnp.exp(dlog - mt),
                       inter=jnp.exp(a - mt)))

    for (b, h), e in zip(heads, st):
        e["s"] = lax.dot_general(q_ref[b, :, hsl(h)], k_ref[b, :, hsl(h)], (((1,), (1,)), ((), ())),
                                 preferred_element_type=F32) * e["dw"]

    for i, ((b, h), e) in enumerate(zip(heads, st)):
        s_hi, s_lo = _split2(e["s"])
        cn = jnp.concatenate([c_ref[b, h], nrep_scr[i]], axis=1).astype(BF16)
        qc = _dot(q_ref[b, :, hsl(h)], cn)
        num = _dot(s_hi, v_ref[b, :, hsl(h)]) + _rep2(e["inter"]) * qc[:, :HEAD_DIM]
        den = _dot(s_hi, ones_l) + _dot(s_lo, ones_l) + e["inter"] * qc[:, HEAD_DIM:]
        rden = 1.0 / jnp.maximum(jnp.abs(den), jnp.exp(-e["mt"]))
        e["hh"] = num * _rep2(rden)

    for (b, h), e in zip(heads, st):
        hh = e["hh"]
        xc = hh - _rep2(_rowsum(hh, ones_d) * inv_d)
        rstd = lax.rsqrt(_rowsum(xc * xc, ones_d) * inv_d + LN_EPS)
        y = _sigmoid(o_ref[b, :, hsl(h)].astype(F32)) * (xc * _rep2(rstd) * wmh_ref[:, hsl(h)])
        y_ref[b, :, hsl(h)] = y.astype(BF16)

    m_rows = []
    for i, ((b, h), e) in enumerate(zip(heads, st)):
        kh = k_ref[b, :, hsl(h)]
        m_new = e["mt"][L - 1:L, :]
        b_last = e["bc"][L - 1:L, :]
        wc = jnp.exp(b_last - e["bc"] + e["li"] - m_new)
        dc = jnp.exp(b_last + e["m_prev"] - m_new)
        vw = (v_ref[b, :, hsl(h)].astype(F32) * _rep2(wc)).astype(BF16)
        c_ref[b, h] = _rep2(dc) * c_ref[b, h] + lax.dot_general(kh, vw, tn, preferred_element_type=F32)
        wc_hi, wc_lo = _split2(wc)
        nrep_scr[i] = (dc * nrep_scr[i] + lax.dot_general(kh, wc_hi, tn, preferred_element_type=F32)
                       + lax.dot_general(kh, wc_lo, tn, preferred_element_type=F32))
        m_rows.append(m_new)

    for b in range(B):
        m_ref[b] = jnp.concatenate(m_rows[b * N_HEADS:(b + 1) * N_HEADS], axis=0)

    @pl.when(step == pl.num_programs(0) - 1)
    def _():
        for b in range(B):
            n_ref[b] = jnp.concatenate(
                [nrep_scr[b * N_HEADS + h].T[0:1, :] for h in range(N_HEADS)], axis=0)


def _mlstm_prompt(q, k, v, o, gates, bg, wmh, *, L):
    b, t, dm = q.shape
    assert L == LANES
    act = pl.BlockSpec((b, L, dm), lambda c: (0, c, 0))
    whole = lambda *shape: pl.BlockSpec(shape, lambda c: (0,) * len(shape))
    return pl.pallas_call(
        functools.partial(_mlstm_chunk_kernel, B=b, L=L),
        grid=(t // L,),
        in_specs=[act, act, act, act,
                  pl.BlockSpec((b, L, GATE_LANES), lambda c: (0, c, 0)),
                  whole(1, GATE_LANES), whole(1, dm)],
        out_specs=[act,
                   whole(b, N_HEADS, HEAD_DIM, HEAD_DIM),
                   whole(b, N_HEADS, HEAD_DIM),
                   whole(b, N_HEADS, GATE_LANES)],
        out_shape=[jax.ShapeDtypeStruct((b, t, dm), BF16),
                   jax.ShapeDtypeStruct((b, N_HEADS, HEAD_DIM, HEAD_DIM), F32),
                   jax.ShapeDtypeStruct((b, N_HEADS, HEAD_DIM), F32),
                   jax.ShapeDtypeStruct((b, N_HEADS, GATE_LANES), F32)],
        scratch_shapes=[pltpu.VMEM((b * N_HEADS, HEAD_DIM, LANES), F32)],
        compiler_params=_cparams(("arbitrary",)),
        name="mlstm_prompt",
    )(q, k, v, o, gates, bg, wmh)


SLAB = 16


def _mlstm_step_kernel(q_ref, k_ref, v_ref, o_ref, g_ref, bg_ref, mrow_ref, nrow_ref,
                       c0_ref, n0_ref, wmh_ref,
                       y_ref, c_ref, n_ref, m_ref, *, bb, T):
    R = bb * T
    per_slab = SLAB // T
    t = lax.rem(lax.broadcasted_iota(jnp.int32, (R, GATE_LANES), 0), T)

    def down(x, d):
        return pltpu.roll(x, d, 0)

    def up(x, d):
        return pltpu.roll(x, x.shape[0] - d, 0)

    def seg_last(x):
        out = x
        for d in range(1, T):
            out = jnp.where(t == T - 1 - d, up(x, d), out)
        return out

    g = g_ref[...] + bg_ref[...]
    li = pltpu.roll(g, N_HEADS, 1)
    lf = _log_sigmoid(g)
    b = lf
    for d in range(1, T):
        b = b + jnp.where(t >= d, down(lf, d), 0.0)
    m_prev = mrow_ref[...]
    a = b + m_prev
    dl = [li] + [jnp.where(t >= d, b - down(b, d) + down(li, d), NEG) for d in range(1, T)]
    mt = a
    for d in range(T):
        mt = jnp.maximum(mt, dl[d])
    dw = [jnp.exp(dl[d] - mt) for d in range(T)]
    inter = jnp.exp(a - mt)
    emt = jnp.exp(-mt)
    m_new = seg_last(mt)
    b_last = seg_last(b)
    wc = jnp.exp(b_last - b + li - m_new)
    dc = jnp.exp(b_last + m_prev - m_new)

    row_s = lax.broadcasted_iota(jnp.int32, (SLAB, HEAD_DIM), 0)
    row_r = lax.broadcasted_iota(jnp.int32, (R, HEAD_DIM), 0)

    for h in range(N_HEADS):
        hs = slice(h * HEAD_DIM, (h + 1) * HEAD_DIM)
        ln = N_HEADS + h

        def col(x):
            return x[:, ln:ln + 1]

        qb = q_ref[:, hs]
        kb = k_ref[:, hs]
        vb = v_ref[:, hs]
        qf = qb.astype(F32)
        kf = kb.astype(F32)
        vf = vb.astype(F32)
        num = jnp.zeros((R, HEAD_DIM), F32)
        den = jnp.zeros((R, 1), F32)
        for d in range(T):
            kd = kf if d == 0 else down(kf, d)
            vd = vf if d == 0 else down(vf, d)
            sw = jnp.sum(qf * kd, axis=1, keepdims=True) * col(dw[d])
            num = num + sw * vd
            den = den + sw

        qc_slabs = []
        for si in range(R // SLAB):
            q16 = qb[si * SLAB:(si + 1) * SLAB]
            acc = jnp.zeros((SLAB, HEAD_DIM), F32)
            for bl in range(per_slab):
                bi = si * per_slab + bl
                r = _dot(q16, c0_ref[bi, h].astype(BF16))
                acc = jnp.where(row_s // T == bl, r, acc)
            qc_slabs.append(acc)
        qc = jnp.concatenate(qc_slabs, axis=0)
        qn = jnp.sum(qf * nrow_ref[:, hs], axis=1, keepdims=True)
        num = num + col(inter) * qc
        den = den + col(inter) * qn
        hh = num / jnp.maximum(jnp.abs(den), col(emt))
        y_ref[:, hs] = _head_out(hh, o_ref[:, hs], wmh_ref[:, hs])

        wk = kf * col(wc)
        vw = vf * col(wc)
        for si in range(R // SLAB):
            k16 = kb[si * SLAB:(si + 1) * SLAB]
            vw16 = vw[si * SLAB:(si + 1) * SLAB]
            for bl in range(per_slab):
                bi = si * per_slab + bl
                last = bi * T + T - 1
                vwb = jnp.where(row_s // T == bl, vw16, 0.0).astype(BF16)
                dcb = dc[last:last + 1, ln:ln + 1]
                c_ref[bi, h] = dcb * c0_ref[bi, h] + lax.dot_general(
                    k16, vwb, (((0,), (0,)), ((), ())), preferred_element_type=F32)
                n_ref[bi, h:h + 1, :] = dcb * n0_ref[bi, h:h + 1, :] + jnp.sum(
                    jnp.where(row_r // T == bi, wk, 0.0), axis=0, keepdims=True)
                m_ref[bi, h:h + 1, :] = jnp.broadcast_to(
                    m_new[last:last + 1, ln:ln + 1], (1, GATE_LANES))


def _mlstm_sample(q, k, v, o, gates, bg, mrow, nrow, c0, n0, wmh, *, bb, T):
    rows, dm = q.shape
    nb = c0.shape[0]
    R = bb * T
    act = pl.BlockSpec((R, dm), lambda i: (i, 0))
    gat = pl.BlockSpec((R, GATE_LANES), lambda i: (i, 0))
    cspec = pl.BlockSpec((bb, N_HEADS, HEAD_DIM, HEAD_DIM), lambda i: (i, 0, 0, 0))
    nspec = pl.BlockSpec((bb, N_HEADS, HEAD_DIM), lambda i: (i, 0, 0))
    return pl.pallas_call(
        functools.partial(_mlstm_step_kernel, bb=bb, T=T),
        grid=(nb // bb,),
        in_specs=[act, act, act, act, gat,
                  pl.BlockSpec((1, GATE_LANES), lambda i: (0, 0)),
                  gat,
                  pl.BlockSpec((R, dm), lambda i: (i, 0)),
                  cspec, nspec,
                  pl.BlockSpec((1, dm), lambda i: (0, 0))],
        out_specs=[act, cspec, nspec,
                   pl.BlockSpec((bb, N_HEADS, GATE_LANES), lambda i: (i, 0, 0))],
        out_shape=[jax.ShapeDtypeStruct((rows, dm), BF16),
                   jax.ShapeDtypeStruct(c0.shape, F32),
                   jax.ShapeDtypeStruct(n0.shape, F32),
                   jax.ShapeDtypeStruct((nb, N_HEADS, GATE_LANES), F32)],
        compiler_params=_cparams(("arbitrary",)),
        name="mlstm_sample",
    )(q, k, v, o, gates, bg, mrow, nrow, c0, n0, wmh)


def _outproj_kernel(yc_ref, ym_ref, w_ref, x_ref, g1_ref, lg_ref, lb_ref, o_ref, *, alpha, splits):
    dc = yc_ref.shape[-1]
    rs = x_ref.shape[0] // splits
    for i in range(splits):
        r = slice(i * rs, (i + 1) * rs)
        g1 = g1_ref[0] if g1_ref.shape[1] == 1 else g1_ref[0, r, :]
        mix = _dot(yc_ref[0, r, :], w_ref[0:dc, :]) + _dot(ym_ref[0, r, :], w_ref[dc:, :])
        o_ref[r, :] = _ln(alpha * x_ref[r, :] + (1.0 + g1) * mix) * lg_ref[...] + lb_ref[...]


def _outproj(yc, ym, w_out, x, mod, ln_g, ln_b, *, tm, tpb, alpha, splits=2):
    rows, d = x.shape
    dc = yc.shape[-1]
    dm = ym.shape[-1]
    r = 1 if mod.shape[1] == 1 else tm
    vec = pl.BlockSpec((1, d), lambda m: (0, 0))
    return pl.pallas_call(
        functools.partial(_outproj_kernel, alpha=alpha, splits=splits),
        grid=(rows // tm,),
        in_specs=[pl.BlockSpec((1, tm, dc), lambda m: (m // tpb, m % tpb, 0)),
                  pl.BlockSpec((1, tm, dm), lambda m: (m // tpb, m % tpb, 0)),
                  pl.BlockSpec((dc + dm, d), lambda m: (0, 0)),
                  pl.BlockSpec((tm, d), lambda m: (m, 0)),
                  pl.BlockSpec((1, r, d), lambda m: (m // tpb, 0, 2)),
                  vec, vec],
        out_specs=pl.BlockSpec((tm, d), lambda m: (m, 0)),
        out_shape=jax.ShapeDtypeStruct((rows, d), F32),
        compiler_params=_cparams(("arbitrary",)),
        name="outproj",
    )(yc, ym, w_out, x, mod, ln_g, ln_b)


def _ffn_kernel(*refs, nf, tpb, period, sample, alpha, splits):
    if sample:
        (x_ref, sh_ref, sc_ref, g2_ref, wa_ref, wg_ref, wconv_ref, wd_ref, lg_ref, lb_ref,
         s0_ref, s1_ref, y_ref, t0_ref, t1_ref, wa_out, wg_out, wd_out, u_scr, z_scr, y_scr) = refs
    else:
        (x_ref, sh_ref, sc_ref, g2_ref, wa_ref, wg_ref, wconv_ref, wd_ref, lg_ref, lb_ref,
         y_ref, at_ref, u_scr, carry_scr) = refs
    m = pl.program_id(0)
    f = pl.program_id(1)

    if not sample:
        carried = _carried_rows(carry_scr, f, lax.rem(m, tpb) == 0)
        rs = u_scr.shape[0] // splits

        def body(first, last):
            prev = carried
            for i in range(splits):
                r = slice(i * rs, (i + 1) * rs)
                if first:
                    u = (_ln(x_ref[r, :]) * (1.0 + sc_ref[0]) + sh_ref[0]).astype(BF16)
                    u_scr[r, :] = u
                else:
                    u = u_scr[r, :]
                ac, prev = _conv3_rows(_dot(u, wa_ref[...]), wconv_ref[...], prev)
                hcur = (ac * _sigmoid(ac) * _dot(u, wg_ref[...])).astype(BF16)
                acc = _dot(hcur, wd_ref[...])
                if not first:
                    acc = y_ref[r, :] + acc
                if last:
                    acc = (_ln(alpha * x_ref[r, :] + (1.0 + g2_ref[0]) * acc)
                           * lg_ref[...] + lb_ref[...])
                y_ref[r, :] = acc
            carry_scr[f] = prev
            at_ref[0] = prev

        if nf == 1:
            body(True, True)
        else:
            pl.when(f == 0)(lambda: body(True, False))
            if nf > 2:
                pl.when(jnp.logical_and(f > 0, f < nf - 1))(lambda: body(False, False))
            pl.when(f == nf - 1)(lambda: body(False, True))
        return

    @pl.when(f == 0)
    def _():
        u = _ln(x_ref[...]) * (1.0 + sc_ref[0]) + sh_ref[0]
        u_scr[...] = u.astype(BF16)
        y_ref[...] = jnp.zeros_like(y_ref)

    wa = wa_ref[...].astype(BF16)
    wg = wg_ref[...].astype(BF16)
    wd = wd_ref[...].astype(BF16)
    wa_out[...] = wa
    wg_out[...] = wg
    wd_out[...] = wd
    u = u_scr[...]
    a = _dot(u, wa)
    ac = _conv3_sequences(a, wconv_ref[...], s0_ref, s1_ref, z_scr, y_scr, t0_ref, t1_ref, period)
    hcur = (ac * _sigmoid(ac) * _dot(u, wg)).astype(BF16)
    y_ref[...] += _dot(hcur, wd)

    @pl.when(f == nf - 1)
    def _():
        y_ref[...] = (_ln(alpha * x_ref[...] + (1.0 + g2_ref[0]) * y_ref[...])
                      * lg_ref[...] + lb_ref[...])


def _ffn(x, mod, w_up, w_conv, w_down, ln_g, ln_b, *, tm, tpb, tf, sample, alpha, period=0,
         splits=1, s0=None, s1=None):
    rows, d = x.shape
    ff = w_down.shape[0]
    nf = ff // tf
    nm = rows // tm
    r = tm if sample else 1
    vec = pl.BlockSpec((1, d), lambda m, f: (0, 0))
    w_a, w_g = (w_up, w_up) if sample else w_up
    up = pl.BlockSpec((d, tf), lambda m, f: (0, f))
    down = pl.BlockSpec((tf, d), lambda m, f: (f, 0))
    in_specs = [
        pl.BlockSpec((tm, d), lambda m, f: (m, 0), pipeline_mode=pl.Buffered(1)),
        pl.BlockSpec((1, r, d), lambda m, f: (m // tpb, 0, 3)),
        pl.BlockSpec((1, r, d), lambda m, f: (m // tpb, 0, 4)),
        pl.BlockSpec((1, r, d), lambda m, f: (m // tpb, 0, 5)),
        up,
        pl.BlockSpec((d, tf), lambda m, f: (0, nf + f)) if sample else up,
        pl.BlockSpec((CONV_K, tf), lambda m, f: (0, f)),
        down,
        vec, vec,
    ]
    args = [x, mod, mod, mod, w_a, w_g, w_conv, w_down, ln_g, ln_b]
    scratch = [pltpu.VMEM((tm, d), BF16)]
    out_specs = [pl.BlockSpec((tm, d), lambda m, f: (m, 0))]
    out_shape = [jax.ShapeDtypeStruct((rows, d), F32)]
    if sample:
        assert nm == 1
        st = pl.BlockSpec((tm // period, tf), lambda m, f: (0, f))
        in_specs += [st, st]
        args += [s0, s1]
        out_specs += [st, st, up, up, down]
        out_shape += [jax.ShapeDtypeStruct((tm // period, ff), F32)] * 2 + [
            jax.ShapeDtypeStruct((d, ff), BF16), jax.ShapeDtypeStruct((d, ff), BF16),
            jax.ShapeDtypeStruct((ff, d), BF16)]
        scratch += [pltpu.VMEM((tf // LANES, tm, LANES), F32)] * 2
    else:
        out_specs.append(pl.BlockSpec((1, STATE_ROWS, tf), lambda m, f: (m, 0, f)))
        out_shape.append(jax.ShapeDtypeStruct((nm, STATE_ROWS, ff), F32))
        scratch.append(pltpu.VMEM((nf, STATE_ROWS, tf), F32))
    return pl.pallas_call(
        functools.partial(_ffn_kernel, nf=nf, tpb=tpb, period=period, sample=sample, alpha=alpha,
                          splits=splits),
        grid=(nm, nf),
        in_specs=in_specs,
        out_specs=out_specs,
        out_shape=out_shape,
        scratch_shapes=scratch,
        compiler_params=_cparams(("arbitrary", "arbitrary")),
        name="ffn_sample" if sample else "ffn_prompt",
    )(*args)


def _layer_prompt(x, mod, wts, ffn_w, *, alpha):
    B, T, D = x.shape
    (w_in, w_gate, bg, w_conv, wmh, w_out, ln1_g, ln1_b, _, w_fconv, _, ln2_g, ln2_b) = wts
    w_a, w_g, w_down = ffn_w
    tm = 1024
    tpb = T // tm
    x2 = x.reshape(B * T, D)
    yc, q, k, v, o, gates, ztail = _inproj(x2, mod, w_in, w_gate, w_conv, tm=tm, tpb=tpb, sample=False)
    ym, C, n, m = _mlstm_prompt(q, k, v, o, gates, bg, wmh, L=128)
    tm2 = 512
    x1 = _outproj(yc, ym, w_out, x2, mod, ln1_g, ln1_b, tm=tm2, tpb=T // tm2, alpha=alpha)
    y, atail = _ffn(x1, mod, (w_a, w_g), w_fconv, w_down, ln2_g, ln2_b,
                    tm=tm, tpb=tpb, tf=512, sample=False, alpha=alpha, splits=2)
    return (y.reshape(B, T, D), ztail[tpb - 1::tpb, STATE_ROWS - 2:], C, n, m[..., 0],
            atail[tpb - 1::tpb, STATE_ROWS - 2:])


def _layer_sample(x, mod, conv_buf, C0, n0, m0, ffn_buf, wts, *, alpha):
    B, T, D = x.shape
    (w_in, w_gate, bg, w_conv, wmh, w_out, ln1_g, ln1_b, w_up, w_fconv, w_down, ln2_g, ln2_b) = wts
    rows = B * T
    x2 = x.reshape(rows, D)
    yc, q, k, v, o, gates, z0, z1 = _inproj(x2, mod, w_in, w_gate, w_conv, tm=rows, tpb=1, sample=True,
                                            period=T, s0=conv_buf[:, 0], s1=conv_buf[:, 1])
    mrow = jnp.pad(jnp.repeat(m0, T, axis=0), ((0, 0), (N_HEADS, GATE_LANES - 2 * N_HEADS)))
    nrow = jnp.repeat(n0.reshape(B, N_HEADS * HEAD_DIM), T, axis=0)
    ym, C, n, m = _mlstm_sample(q[0], k[0], v[0], o[0], gates[0], bg, mrow, nrow, C0, n0, wmh, bb=8, T=T)
    x1 = _outproj(yc, ym[None], w_out, x2, mod, ln1_g, ln1_b, tm=rows, tpb=1, alpha=alpha)
    y, a0, a1, *ffn_w = _ffn(x1, mod, w_up, w_fconv, w_down, ln2_g, ln2_b, tm=rows, tpb=1, tf=256,
                             sample=True, alpha=alpha, period=T, s0=ffn_buf[:, 0], s1=ffn_buf[:, 1])
    return (y.reshape(B, T, D), jnp.stack([z0, z1], axis=1), C, n, m[..., 0],
            jnp.stack([a0, a1], axis=1), ffn_w)


def kernel(x_prompt, x_sample, c_prompt, c_sample, state_conv, state_mlstm_C, state_mlstm_n,
           state_mlstm_m, state_ffn_conv, w_ada, b_ada, w_in, b_gate, w_conv, w_mh_norm, w_out,
           ln1_g, ln1_b, w_up, w_ffn_conv, w_down, ln2_g, ln2_b):
    depth = w_in.shape[0]
    alpha = (2 * depth) ** 0.25
    Bp = x_prompt.shape[0]
    Bs, Ts, D = x_sample.shape
    dc = w_conv.shape[-1]
    dm = w_mh_norm.shape[-1]
    n_main = 3 * dc + 4 * dm
    assert dc == dm == N_HEADS * HEAD_DIM and Ts >= CONV_K - 1 and SLAB % Ts == 0

    xp, xs = x_prompt, x_sample
    outs_p = [[] for _ in range(5)]
    outs_s = [[] for _ in range(5)]
    for l in range(depth):
        c_all = jnp.concatenate([jnp.repeat(c_sample, Ts, axis=0), c_prompt], axis=0)
        mod = _ada(c_all, w_ada[l], b_ada[l])
        mod_s = mod.reshape(1, Bs * Ts + Bp, 6 * D)
        mod_p = mod[Bs * Ts:].reshape(Bp, 1, 6 * D)
        w_in_t = jnp.swapaxes(w_in[l], 0, 1)
        wts = (
            _cast_transposed(w_in_t, n_main),
            jnp.pad(w_in_t[n_main:], ((0, GATE_LANES - 2 * N_HEADS), (0, 0))).astype(BF16),
            jnp.pad(b_gate[l], (0, GATE_LANES - 2 * N_HEADS)).reshape(1, GATE_LANES),
            w_conv[l],
            w_mh_norm[l].reshape(1, dm),
            w_out[l].astype(BF16),
            ln1_g[l].reshape(1, D), ln1_b[l].reshape(1, D),
            w_up[l],
            w_ffn_conv[l],
            w_down[l],
            ln2_g[l].reshape(1, D), ln2_b[l].reshape(1, D),
        )
        xs, *st_s, ffn_w = _layer_sample(xs, mod_s, state_conv[l], state_mlstm_C[l], state_mlstm_n[l],
                                         state_mlstm_m[l], state_ffn_conv[l], wts, alpha=alpha)
        xp, *st_p = _layer_prompt(xp, mod_p, wts, ffn_w, alpha=alpha)
        for acc, val in zip(outs_p, st_p):
            acc.append(val)
        for acc, val in zip(outs_s, st_s):
            acc.append(val)
    return (xp.astype(x_prompt.dtype), xs.astype(x_sample.dtype),
            *[jnp.stack(a) for a in outs_p], *[jnp.stack(a) for a in outs_s])
```

```python
import functools

import jax
import jax.numpy as jnp
from jax import lax
from jax.experimental import pallas as pl
from jax.experimental.pallas import tpu as pltpu

F32 = jnp.float32
BF16 = jnp.bfloat16

N_HEADS = 4
HEAD_DIM = 256
CONV_K = 3
LN_EPS = 1e-5
NEG = -1e30
LANES = 128
GATE_LANES = LANES
STATE_ROWS = 8
VMEM_LIMIT = 56 * 1024 * 1024


def _cparams(sem):
    return pltpu.CompilerParams(dimension_semantics=sem, vmem_limit_bytes=VMEM_LIMIT)


def _ln(x):
    mu = jnp.mean(x, axis=-1, keepdims=True)
    xc = x - mu
    var = jnp.mean(xc * xc, axis=-1, keepdims=True)
    return xc * lax.rsqrt(var + LN_EPS)


def _log_sigmoid(x):
    return jnp.minimum(x, 0.0) - jnp.log1p(jnp.exp(-jnp.abs(x)))


def _sigmoid(x):
    return 1.0 / (1.0 + jnp.exp(-x))


def _dot(a, b):
    return jnp.dot(a, b, preferred_element_type=F32)


def _dot_nt(a, b):
    return lax.dot_general(a, b, (((1,), (1,)), ((), ())), preferred_element_type=F32)


def _conv3_rows(z, w, prev):
    p0 = prev[STATE_ROWS - 2:STATE_ROWS - 1]
    p1 = prev[STATE_ROWS - 1:STATE_ROWS]
    t = lax.broadcasted_iota(jnp.int32, z.shape, 0)
    z1 = jnp.where(t >= 1, pltpu.roll(z, 1, 0), p1)
    z2 = jnp.where(t >= 2, pltpu.roll(z, 2, 0), jnp.where(t == 0, p0, p1))
    return w[0:1] * z2 + w[1:2] * z1 + w[2:3] * z, z[z.shape[0] - STATE_ROWS:]


def _carried_rows(carry_ref, idx, first):
    @pl.when(first)
    def _():
        carry_ref[idx] = jnp.zeros(carry_ref.shape[1:], F32)

    return carry_ref[idx]


def _conv3_sequences(z, w, s0_ref, s1_ref, z_scr, y_scr, t0_ref, t1_ref, T):
    nseq = z.shape[0] // T
    y = w[0:1] * pltpu.roll(z, 2, 0) + w[1:2] * pltpu.roll(z, 1, 0) + w[2:3] * z

    def rows(t):
        return pl.ds(t, nseq, stride=T)

    cols = []
    for c in range(z.shape[1] // LANES):
        cs = slice(c * LANES, (c + 1) * LANES)
        w0, w1, w2 = w[0:1, cs], w[1:2, cs], w[2:3, cs]
        z_scr[c] = z[:, cs]
        y_scr[c] = y[:, cs]
        s0 = s0_ref[:, cs]
        s1 = s1_ref[:, cs]
        z0 = z_scr[c, rows(0), :]
        z1 = z_scr[c, rows(1), :]
        y_scr[c, rows(0), :] = w0 * s0 + w1 * s1 + w2 * z0
        y_scr[c, rows(1), :] = w0 * s1 + w1 * z0 + w2 * z1
        t0_ref[:, cs] = z_scr[c, rows(T - 2), :]
        t1_ref[:, cs] = z_scr[c, rows(T - 1), :]
        cols.append(y_scr[c])
    return jnp.concatenate(cols, axis=1)


def _ada_kernel(c_ref, w_ref, b_ref, o_ref):
    c = c_ref[...]
    s = (c * _sigmoid(c)).astype(BF16)
    o_ref[...] = _dot(s, w_ref[...].astype(BF16)) + b_ref[...]


def _ada(c, w, b, tn=1024):
    r, d = c.shape
    n = w.shape[1]
    return pl.pallas_call(
        _ada_kernel,
        grid=(n // tn,),
        in_specs=[
            pl.BlockSpec((r, d), lambda j: (0, 0)),
            pl.BlockSpec((d, tn), lambda j: (0, j)),
            pl.BlockSpec((1, tn), lambda j: (0, j)),
        ],
        out_specs=pl.BlockSpec((r, tn), lambda j: (0, j)),
        out_shape=jax.ShapeDtypeStruct((r, n), F32),
        compiler_params=_cparams(("arbitrary",)),
        name="ada",
    )(c, w, b.reshape(1, n))


SLAB = 16


def _cast_side_job(mats, n_steps, index_map):
    specs, shapes = [], []
    for w in mats:
        slab = w.shape[0] // n_steps
        assert slab * n_steps == w.shape[0] and slab % SLAB == 0
        specs.append(pl.BlockSpec((slab, w.shape[1]), index_map))
        shapes.append(jax.ShapeDtypeStruct(w.shape, BF16))
    return specs, shapes


def _cast_slabs(srcs, dsts):
    for src, dst in zip(srcs, dsts):
        dst[...] = src[...].astype(BF16)


def _cast_transposed_kernel(w_ref, o_ref):
    o_ref[...] = w_ref[...].T.astype(BF16)


def _cast_transposed(wt, n, tn=512):
    k = wt.shape[1]
    return pl.pallas_call(
        _cast_transposed_kernel,
        grid=(n // tn,),
        in_specs=[pl.BlockSpec((tn, k), lambda j: (j, 0))],
        out_specs=pl.BlockSpec((k, tn), lambda j: (0, j)),
        out_shape=jax.ShapeDtypeStruct((k, n), BF16),
        compiler_params=_cparams(("arbitrary",)),
        name="cast_transposed",
    )(wt)


def _inproj_kernel(*refs, tpb, period, sample, n_cast=0):
    if sample:
        (x_ref, sh_ref, sc_ref, wb_ref, wc_ref, wh_ref, wq_ref, wk_ref, wv_ref, wo_ref, wg_ref,
         wconv_ref, s0_ref, s1_ref,
         yc_ref, q_ref, k_ref, v_ref, o_ref, g_ref, t0_ref, t1_ref, u_scr, z_scr, y_scr) = refs
    else:
        (x_ref, sh_ref, sc_ref, wb_ref, wc_ref, wh_ref, wq_ref, wk_ref, wv_ref, wo_ref, wg_ref,
         wconv_ref) = refs[:12]
        cast_in = refs[12:12 + n_cast]
        (yc_ref, q_ref, k_ref, v_ref, o_ref, g_ref, zt_ref) = refs[12 + n_cast:19 + n_cast]
        cast_out = refs[19 + n_cast:19 + 2 * n_cast]
        u_scr, carry_scr = refs[19 + 2 * n_cast:]
    m = pl.program_id(0)
    j = pl.program_id(1)

    if not sample:
        _cast_slabs(cast_in, cast_out)

        carried = _carried_rows(carry_scr, j, lax.rem(m, tpb) == 0)
        rs = u_scr.shape[0] // 2

        def body(first):
            prev = carried
            for i in range(2):
                r = slice(i * rs, (i + 1) * rs)
                if first:
                    u = (_ln(x_ref[r, :]) * (1.0 + sc_ref[0]) + sh_ref[0]).astype(BF16)
                    u_scr[r, :] = u
                    g_ref[0, r, :] = _dot_nt(u, wg_ref[...])
                else:
                    u = u_scr[r, :]
                z = _dot(u, wc_ref[...]) * _dot(u, wh_ref[...])
                yc, prev = _conv3_rows(z, wconv_ref[...], prev)
                yc_ref[0, r, :] = (_dot(u, wb_ref[...]) * yc).astype(BF16)
                q_ref[0, r, :] = _dot(u, wq_ref[...]).astype(BF16)
                k_ref[0, r, :] = (_dot(u, wk_ref[...]) * (HEAD_DIM ** -0.5)).astype(BF16)
                v_ref[0, r, :] = _dot(u, wv_ref[...]).astype(BF16)
                o_ref[0, r, :] = _dot(u, wo_ref[...]).astype(BF16)
            carry_scr[j] = prev
            zt_ref[0] = prev

        pl.when(j == 0)(lambda: body(True))
        pl.when(j > 0)(lambda: body(False))
        return

    @pl.when(j == 0)
    def _():
        u = _ln(x_ref[...]) * (1.0 + sc_ref[0]) + sh_ref[0]
        ub = u.astype(BF16)
        u_scr[...] = ub
        g_ref[0] = _dot_nt(ub, wg_ref[...])

    u = u_scr[...]
    z = _dot(u, wc_ref[...]) * _dot(u, wh_ref[...])
    yc = _conv3_sequences(z, wconv_ref[...], s0_ref, s1_ref, z_scr, y_scr, t0_ref, t1_ref, period)
    half = u.shape[0] // 2
    bg = jnp.concatenate([_dot(u[:half], wb_ref[...]), _dot(u[half:], wb_ref[...])], axis=0)
    yc_ref[0] = (bg * yc).astype(BF16)
    q_ref[0] = _dot(u, wq_ref[...]).astype(BF16)
    k_ref[0] = (_dot(u, wk_ref[...]) * (HEAD_DIM ** -0.5)).astype(BF16)
    v_ref[0] = _dot(u, wv_ref[...]).astype(BF16)
    o_ref[0] = _dot(u, wo_ref[...]).astype(BF16)


def _inproj(x, mod, w_in, w_gate, w_conv, *, tm, tpb, sample, period=0, s0=None, s1=None, cast=()):
    rows, d = x.shape
    dc = w_conv.shape[1]
    tn = HEAD_DIM
    nj = dc // tn
    nm = rows // tm
    nseq = nm // tpb
    r = tm if sample else 1

    def wspec(off):
        return pl.BlockSpec((d, tn), lambda m, j, off=off: (0, off * nj + j))

    in_specs = [
        pl.BlockSpec((tm, d), lambda m, j: (m, 0)),
        pl.BlockSpec((1, r, d), lambda m, j: (m // tpb, 0, 0)),
        pl.BlockSpec((1, r, d), lambda m, j: (m // tpb, 0, 1)),
        wspec(0), wspec(1), wspec(2), wspec(3), wspec(4), wspec(5), wspec(6),
        pl.BlockSpec((GATE_LANES, d), lambda m, j: (0, 0)),
        pl.BlockSpec((CONV_K, tn), lambda m, j: (0, j)),
    ]
    args = [x, mod, mod, w_in, w_in, w_in, w_in, w_in, w_in, w_in, w_gate, w_conv]
    scratch = [pltpu.VMEM((tm, d), BF16)]
    act = pl.BlockSpec((1, tm, tn), lambda m, j: (m // tpb, m % tpb, j))
    out_specs = [act, act, act, act, act,
                 pl.BlockSpec((1, tm, GATE_LANES), lambda m, j: (m // tpb, m % tpb, 0))]
    out_shape = [jax.ShapeDtypeStruct((nseq, tpb * tm, dc), BF16)] * 5 + [
        jax.ShapeDtypeStruct((nseq, tpb * tm, GATE_LANES), F32)]
    if sample:
        assert nm == 1
        st = pl.BlockSpec((tm // period, tn), lambda m, j: (0, j))
        in_specs += [st, st]
        args += [s0, s1]
        out_specs += [st, st]
        out_shape += [jax.ShapeDtypeStruct((tm // period, dc), F32)] * 2
        scratch += [pltpu.VMEM((tn // LANES, tm, LANES), F32)] * 2
    else:
        out_specs.append(pl.BlockSpec((1, STATE_ROWS, tn), lambda m, j: (m, 0, j)))
        out_shape.append(jax.ShapeDtypeStruct((nm, STATE_ROWS, dc), F32))
        scratch.append(pltpu.VMEM((nj, STATE_ROWS, tn), F32))
        cast_specs, cast_shapes = _cast_side_job(cast, nm * nj, lambda m, j: (m * nj + j, 0))
        in_specs += cast_specs
        args += list(cast)
        out_specs += cast_specs
        out_shape += cast_shapes
    return pl.pallas_call(
        functools.partial(_inproj_kernel, tpb=tpb, period=period, sample=sample, n_cast=len(cast)),
        grid=(nm, nj),
        in_specs=in_specs,
        out_specs=out_specs,
        out_shape=out_shape,
        scratch_shapes=scratch,
        compiler_params=_cparams(("arbitrary", "arbitrary")),
        name="inproj_sample" if sample else "inproj_prompt",
    )(*args)


def _split3(x):
    hi = x.astype(BF16)
    r1 = x - hi.astype(F32)
    mid = r1.astype(BF16)
    lo = (r1 - mid.astype(F32)).astype(BF16)
    return hi, mid, lo


def _head_out(hh, o, wmh):
    return (_sigmoid(o.astype(F32)) * (_ln(hh) * wmh)).astype(BF16)


def _split2(x):
    hi = x.astype(BF16)
    return hi, (x - hi.astype(F32)).astype(BF16)


def _rowsum(x, ones):
    hi, lo = _split2(x)
    return _dot(hi, ones) + _dot(lo, ones)


def _rep2(x):
    return jnp.concatenate([x, x], axis=1)


def _mlstm_chunk_kernel(*refs, B, L, n_cast):
    (q_ref, k_ref, v_ref, o_ref, g_ref, bg_ref, wmh_ref) = refs[:7]
    (y_ref, c_ref, n_ref, m_ref) = refs[7 + n_cast:11 + n_cast]
    nrep_scr = refs[-1]
    _cast_slabs(refs[7:7 + n_cast], refs[11 + n_cast:11 + 2 * n_cast])
    _mlstm_chunk(q_ref, k_ref, v_ref, o_ref, g_ref, bg_ref, wmh_ref, y_ref, c_ref, n_ref, m_ref,
                 nrep_scr, B=B, L=L)


def _mlstm_chunk(q_ref, k_ref, v_ref, o_ref, g_ref, bg_ref, wmh_ref,
                 y_ref, c_ref, n_ref, m_ref, nrep_scr, *, B, L):
    step = pl.program_id(0)

    @pl.when(step == 0)
    def _():
        c_ref[...] = jnp.zeros_like(c_ref)
        m_ref[...] = jnp.zeros_like(m_ref)
        nrep_scr[...] = jnp.zeros_like(nrep_scr)

    row = lax.broadcasted_iota(jnp.int32, (L, L), 0)
    col = lax.broadcasted_iota(jnp.int32, (L, L), 1)
    causal = col <= row
    tril = jnp.where(causal, 1.0, 0.0).astype(BF16)
    ones_l = jnp.ones((L, LANES), BF16)
    ones_d = jnp.ones((HEAD_DIM, LANES), BF16)
    inv_d = 1.0 / HEAD_DIM
    tn = (((0,), (0,)), ((), ()))

    heads = [(b, h) for b in range(B) for h in range(N_HEADS)]

    def hsl(h):
        return slice(h * HEAD_DIM, (h + 1) * HEAD_DIM)

    gate = []
    for b in range(B):
        g = g_ref[b] + bg_ref[...]
        hi, mid, lo = _split3(_log_sigmoid(g))
        bcum = _dot(tril, hi) + _dot(tril, mid) + _dot(tril, lo)
        gate.append((g, g.T, bcum, bcum.T, m_ref[b]))

    st = []
    for b, h in heads:
        g, g_t, bcum, bcum_t, m_all = gate[b]
        bc = jnp.broadcast_to(bcum[:, N_HEADS + h:N_HEADS + h + 1], (L, LANES))
        li = jnp.broadcast_to(g[:, h:h + 1], (L, LANES))
        br = bcum_t[N_HEADS + h:N_HEADS + h + 1, :]
        m_prev = m_all[h:h + 1, :]
        a = bc + m_prev
        dlog = jnp.where(causal, bc - br + g_t[h:h + 1, :], NEG)
        mt = jnp.maximum(a, jnp.max(dlog, axis=1, keepdims=True))
        st.append(dict(bc=bc, li=li, m_prev=m_prev, mt=mt, dw=jnp.exp(dlog - mt),
                       inter=jnp.exp(a - mt)))

    for (b, h), e in zip(heads, st):
        e["s"] = lax.dot_general(q_ref[b, :, hsl(h)], k_ref[b, :, hsl(h)], (((1,), (1,)), ((), ())),
                                 preferred_element_type=F32) * e["dw"]

    for i, ((b, h), e) in enumerate(zip(heads, st)):
        s_hi, s_lo = _split2(e["s"])
        cn = jnp.concatenate([c_ref[b, h], nrep_scr[i]], axis=1).astype(BF16)
        qc = _dot(q_ref[b, :, hsl(h)], cn)
        num = _dot(s_hi, v_ref[b, :, hsl(h)]) + _rep2(e["inter"]) * qc[:, :HEAD_DIM]
        den = _dot(s_hi, ones_l) + _dot(s_lo, ones_l) + e["inter"] * qc[:, HEAD_DIM:]
        rden = 1.0 / jnp.maximum(jnp.abs(den), jnp.exp(-e["mt"]))
        e["hh"] = num * _rep2(rden)

    for (b, h), e in zip(heads, st):
        hh = e["hh"]
        xc = hh - _rep2(_rowsum(hh, ones_d) * inv_d)
        rstd = lax.rsqrt(_rowsum(xc * xc, ones_d) * inv_d + LN_EPS)
        y = _sigmoid(o_ref[b, :, hsl(h)].astype(F32)) * (xc * _rep2(rstd) * wmh_ref[:, hsl(h)])
        y_ref[b, :, hsl(h)] = y.astype(BF16)

    m_rows = []
    for i, ((b, h), e) in enumerate(zip(heads, st)):
        kh = k_ref[b, :, hsl(h)]
        m_new = e["mt"][L - 1:L, :]
        b_last = e["bc"][L - 1:L, :]
        wc = jnp.exp(b_last - e["bc"] + e["li"] - m_new)
        dc = jnp.exp(b_last + e["m_prev"] - m_new)
        vw = (v_ref[b, :, hsl(h)].astype(F32) * _rep2(wc)).astype(BF16)
        c_ref[b, h] = _rep2(dc) * c_ref[b, h] + lax.dot_general(kh, vw, tn, preferred_element_type=F32)
        wc_hi, wc_lo = _split2(wc)
        nrep_scr[i] = (dc * nrep_scr[i] + lax.dot_general(kh, wc_hi, tn, preferred_element_type=F32)
                       + lax.dot_general(kh, wc_lo, tn, preferred_element_type=F32))
        m_rows.append(m_new)

    for b in range(B):
        m_ref[b] = jnp.concatenate(m_rows[b * N_HEADS:(b + 1) * N_HEADS], axis=0)

    @pl.when(step == pl.num_programs(0) - 1)
    def _():
        for b in range(B):
            n_ref[b] = jnp.concatenate(
                [nrep_scr[b * N_HEADS + h].T[0:1, :] for h in range(N_HEADS)], axis=0)


def _mlstm_prompt(q, k, v, o, gates, bg, wmh, *, L, cast=()):
    b, t, dm = q.shape
    assert L == LANES
    act = pl.BlockSpec((b, L, dm), lambda c: (0, c, 0))
    whole = lambda *shape: pl.BlockSpec(shape, lambda c: (0,) * len(shape))
    cast_specs, cast_shapes = _cast_side_job(cast, t // L, lambda c: (c, 0))
    return pl.pallas_call(
        functools.partial(_mlstm_chunk_kernel, B=b, L=L, n_cast=len(cast)),
        grid=(t // L,),
        in_specs=[act, act, act, act,
                  pl.BlockSpec((b, L, GATE_LANES), lambda c: (0, c, 0)),
                  whole(1, GATE_LANES), whole(1, dm)] + cast_specs,
        out_specs=[act,
                   whole(b, N_HEADS, HEAD_DIM, HEAD_DIM),
                   whole(b, N_HEADS, HEAD_DIM),
                   whole(b, N_HEADS, GATE_LANES)] + cast_specs,
        out_shape=[jax.ShapeDtypeStruct((b, t, dm), BF16),
                   jax.ShapeDtypeStruct((b, N_HEADS, HEAD_DIM, HEAD_DIM), F32),
                   jax.ShapeDtypeStruct((b, N_HEADS, HEAD_DIM), F32),
                   jax.ShapeDtypeStruct((b, N_HEADS, GATE_LANES), F32)] + cast_shapes,
        scratch_shapes=[pltpu.VMEM((b * N_HEADS, HEAD_DIM, LANES), F32)],
        compiler_params=_cparams(("arbitrary",)),
        name="mlstm_prompt",
    )(q, k, v, o, gates, bg, wmh, *cast)


def _mlstm_step_kernel(q_ref, k_ref, v_ref, o_ref, g_ref, bg_ref, mrow_ref, nrow_ref,
                       c0_ref, n0_ref, wmh_ref,
                       y_ref, c_ref, n_ref, m_ref, *, bb, T):
    R = bb * T
    per_slab = SLAB // T
    t = lax.rem(lax.broadcasted_iota(jnp.int32, (R, GATE_LANES), 0), T)

    def down(x, d):
        return pltpu.roll(x, d, 0)

    def up(x, d):
        return pltpu.roll(x, x.shape[0] - d, 0)

    def seg_last(x):
        out = x
        for d in range(1, T):
            out = jnp.where(t == T - 1 - d, up(x, d), out)
        return out

    g = g_ref[...] + bg_ref[...]
    li = pltpu.roll(g, N_HEADS, 1)
    lf = _log_sigmoid(g)
    b = lf
    for d in range(1, T):
        b = b + jnp.where(t >= d, down(lf, d), 0.0)
    m_prev = mrow_ref[...]
    a = b + m_prev
    dl = [li] + [jnp.where(t >= d, b - down(b, d) + down(li, d), NEG) for d in range(1, T)]
    mt = a
    for d in range(T):
        mt = jnp.maximum(mt, dl[d])
    dw = [jnp.exp(dl[d] - mt) for d in range(T)]
    inter = jnp.exp(a - mt)
    emt = jnp.exp(-mt)
    m_new = seg_last(mt)
    b_last = seg_last(b)
    wc = jnp.exp(b_last - b + li - m_new)
    dc = jnp.exp(b_last + m_prev - m_new)

    row_s = lax.broadcasted_iota(jnp.int32, (SLAB, HEAD_DIM), 0)
    row_r = lax.broadcasted_iota(jnp.int32, (R, HEAD_DIM), 0)

    for h in range(N_HEADS):
        hs = slice(h * HEAD_DIM, (h + 1) * HEAD_DIM)
        ln = N_HEADS + h

        def col(x):
            return x[:, ln:ln + 1]

        qb = q_ref[:, hs]
        kb = k_ref[:, hs]
        vb = v_ref[:, hs]
        qf = qb.astype(F32)
        kf = kb.astype(F32)
        vf = vb.astype(F32)
        num = jnp.zeros((R, HEAD_DIM), F32)
        den = jnp.zeros((R, 1), F32)
        for d in range(T):
            kd = kf if d == 0 else down(kf, d)
            vd = vf if d == 0 else down(vf, d)
            sw = jnp.sum(qf * kd, axis=1, keepdims=True) * col(dw[d])
            num = num + sw * vd
            den = den + sw

        qc_slabs = []
        for si in range(R // SLAB):
            q16 = qb[si * SLAB:(si + 1) * SLAB]
            acc = jnp.zeros((SLAB, HEAD_DIM), F32)
            for bl in range(per_slab):
                bi = si * per_slab + bl
                r = _dot(q16, c0_ref[bi, h].astype(BF16))
                acc = jnp.where(row_s // T == bl, r, acc)
            qc_slabs.append(acc)
        qc = jnp.concatenate(qc_slabs, axis=0)
        qn = jnp.sum(qf * nrow_ref[:, hs], axis=1, keepdims=True)
        num = num + col(inter) * qc
        den = den + col(inter) * qn
        hh = num / jnp.maximum(jnp.abs(den), col(emt))
        y_ref[:, hs] = _head_out(hh, o_ref[:, hs], wmh_ref[:, hs])

        wk = kf * col(wc)
        vw = vf * col(wc)
        for si in range(R // SLAB):
            k16 = kb[si * SLAB:(si + 1) * SLAB]
            vw16 = vw[si * SLAB:(si + 1) * SLAB]
            for bl in range(per_slab):
                bi = si * per_slab + bl
                last = bi * T + T - 1
                vwb = jnp.where(row_s // T == bl, vw16, 0.0).astype(BF16)
                dcb = dc[last:last + 1, ln:ln + 1]
                c_ref[bi, h] = dcb * c0_ref[bi, h] + lax.dot_general(
                    k16, vwb, (((0,), (0,)), ((), ())), preferred_element_type=F32)
                n_ref[bi, h:h + 1, :] = dcb * n0_ref[bi, h:h + 1, :] + jnp.sum(
                    jnp.where(row_r // T == bi, wk, 0.0), axis=0, keepdims=True)
                m_ref[bi, h:h + 1, :] = jnp.broadcast_to(
                    m_new[last:last + 1, ln:ln + 1], (1, GATE_LANES))


def _mlstm_sample(q, k, v, o, gates, bg, mrow, nrow, c0, n0, wmh, *, bb, T):
    rows, dm = q.shape
    nb = c0.shape[0]
    R = bb * T
    act = pl.BlockSpec((R, dm), lambda i: (i, 0))
    gat = pl.BlockSpec((R, GATE_LANES), lambda i: (i, 0))
    cspec = pl.BlockSpec((bb, N_HEADS, HEAD_DIM, HEAD_DIM), lambda i: (i, 0, 0, 0))
    nspec = pl.BlockSpec((bb, N_HEADS, HEAD_DIM), lambda i: (i, 0, 0))
    return pl.pallas_call(
        functools.partial(_mlstm_step_kernel, bb=bb, T=T),
        grid=(nb // bb,),
        in_specs=[act, act, act, act, gat,
                  pl.BlockSpec((1, GATE_LANES), lambda i: (0, 0)),
                  gat,
                  pl.BlockSpec((R, dm), lambda i: (i, 0)),
                  cspec, nspec,
                  pl.BlockSpec((1, dm), lambda i: (0, 0))],
        out_specs=[act, cspec, nspec,
                   pl.BlockSpec((bb, N_HEADS, GATE_LANES), lambda i: (i, 0, 0))],
        out_shape=[jax.ShapeDtypeStruct((rows, dm), BF16),
                   jax.ShapeDtypeStruct(c0.shape, F32),
                   jax.ShapeDtypeStruct(n0.shape, F32),
                   jax.ShapeDtypeStruct((nb, N_HEADS, GATE_LANES), F32)],
        compiler_params=_cparams(("arbitrary",)),
        name="mlstm_sample",
    )(q, k, v, o, gates, bg, mrow, nrow, c0, n0, wmh)


def _outproj_kernel(yc_ref, ym_ref, w_ref, x_ref, g1_ref, lg_ref, lb_ref, o_ref, *, alpha, splits):
    dc = yc_ref.shape[-1]
    rs = x_ref.shape[0] // splits
    for i in range(splits):
        r = slice(i * rs, (i + 1) * rs)
        g1 = g1_ref[0] if g1_ref.shape[1] == 1 else g1_ref[0, r, :]
        mix = _dot(yc_ref[0, r, :], w_ref[0:dc, :]) + _dot(ym_ref[0, r, :], w_ref[dc:, :])
        o_ref[r, :] = _ln(alpha * x_ref[r, :] + (1.0 + g1) * mix) * lg_ref[...] + lb_ref[...]


def _outproj(yc, ym, w_out, x, mod, ln_g, ln_b, *, tm, tpb, alpha, splits=2):
    rows, d = x.shape
    dc = yc.shape[-1]
    dm = ym.shape[-1]
    r = 1 if mod.shape[1] == 1 else tm
    vec = pl.BlockSpec((1, d), lambda m: (0, 0))
    return pl.pallas_call(
        functools.partial(_outproj_kernel, alpha=alpha, splits=splits),
        grid=(rows // tm,),
        in_specs=[pl.BlockSpec((1, tm, dc), lambda m: (m // tpb, m % tpb, 0)),
                  pl.BlockSpec((1, tm, dm), lambda m: (m // tpb, m % tpb, 0)),
                  pl.BlockSpec((dc + dm, d), lambda m: (0, 0)),
                  pl.BlockSpec((tm, d), lambda m: (m, 0)),
                  pl.BlockSpec((1, r, d), lambda m: (m // tpb, 0, 2)),
                  vec, vec],
        out_specs=pl.BlockSpec((tm, d), lambda m: (m, 0)),
        out_shape=jax.ShapeDtypeStruct((rows, d), F32),
        compiler_params=_cparams(("arbitrary",)),
        name="outproj",
    )(yc, ym, w_out, x, mod, ln_g, ln_b)


def _ffn_kernel(*refs, nf, tpb, period, sample, alpha, splits):
    if sample:
        (x_ref, sh_ref, sc_ref, g2_ref, wa_ref, wg_ref, wconv_ref, wd_ref, lg_ref, lb_ref,
         s0_ref, s1_ref, y_ref, t0_ref, t1_ref, u_scr, z_scr, y_scr) = refs
    else:
        (x_ref, sh_ref, sc_ref, g2_ref, wa_ref, wg_ref, wconv_ref, wd_ref, lg_ref, lb_ref,
         y_ref, at_ref, u_scr, carry_scr) = refs
    m = pl.program_id(0)
    f = pl.program_id(1)

    if not sample:
        carried = _carried_rows(carry_scr, f, lax.rem(m, tpb) == 0)
        rs = u_scr.shape[0] // splits

        def body(first, last):
            prev = carried
            for i in range(splits):
                r = slice(i * rs, (i + 1) * rs)
                if first:
                    u = (_ln(x_ref[r, :]) * (1.0 + sc_ref[0]) + sh_ref[0]).astype(BF16)
                    u_scr[r, :] = u
                else:
                    u = u_scr[r, :]
                ac, prev = _conv3_rows(_dot(u, wa_ref[...]), wconv_ref[...], prev)
                hcur = (ac * _sigmoid(ac) * _dot(u, wg_ref[...])).astype(BF16)
                acc = _dot(hcur, wd_ref[...])
                if not first:
                    acc = y_ref[r, :] + acc
                if last:
                    acc = (_ln(alpha * x_ref[r, :] + (1.0 + g2_ref[0]) * acc)
                           * lg_ref[...] + lb_ref[...])
                y_ref[r, :] = acc
            carry_scr[f] = prev
            at_ref[0] = prev

        if nf == 1:
            body(True, True)
        else:
            pl.when(f == 0)(lambda: body(True, False))
            if nf > 2:
                pl.when(jnp.logical_and(f > 0, f < nf - 1))(lambda: body(False, False))
            pl.when(f == nf - 1)(lambda: body(False, True))
        return

    @pl.when(f == 0)
    def _():
        u = _ln(x_ref[...]) * (1.0 + sc_ref[0]) + sh_ref[0]
        u_scr[...] = u.astype(BF16)
        y_ref[...] = jnp.zeros_like(y_ref)

    u = u_scr[...]
    a = _dot(u, wa_ref[...])
    ac = _conv3_sequences(a, wconv_ref[...], s0_ref, s1_ref, z_scr, y_scr, t0_ref, t1_ref, period)
    hcur = (ac * _sigmoid(ac) * _dot(u, wg_ref[...])).astype(BF16)
    y_ref[...] += _dot(hcur, wd_ref[...])

    @pl.when(f == nf - 1)
    def _():
        y_ref[...] = (_ln(alpha * x_ref[...] + (1.0 + g2_ref[0]) * y_ref[...])
                      * lg_ref[...] + lb_ref[...])


def _ffn(x, mod, w_up, w_conv, w_down, ln_g, ln_b, *, tm, tpb, tf, sample, alpha, period=0,
         splits=1, s0=None, s1=None):
    rows, d = x.shape
    ff = w_down.shape[0]
    nf = ff // tf
    nm = rows // tm
    r = tm if sample else 1
    vec = pl.BlockSpec((1, d), lambda m, f: (0, 0))
    in_specs = [
        pl.BlockSpec((tm, d), lambda m, f: (m, 0), pipeline_mode=pl.Buffered(1)),
        pl.BlockSpec((1, r, d), lambda m, f: (m // tpb, 0, 3)),
        pl.BlockSpec((1, r, d), lambda m, f: (m // tpb, 0, 4)),
        pl.BlockSpec((1, r, d), lambda m, f: (m // tpb, 0, 5)),
        pl.BlockSpec((d, tf), lambda m, f: (0, f)),
        pl.BlockSpec((d, tf), lambda m, f: (0, nf + f)),
        pl.BlockSpec((CONV_K, tf), lambda m, f: (0, f)),
        pl.BlockSpec((tf, d), lambda m, f: (f, 0)),
        vec, vec,
    ]
    args = [x, mod, mod, mod, w_up, w_up, w_conv, w_down, ln_g, ln_b]
    scratch = [pltpu.VMEM((tm, d), BF16)]
    out_specs = [pl.BlockSpec((tm, d), lambda m, f: (m, 0))]
    out_shape = [jax.ShapeDtypeStruct((rows, d), F32)]
    if sample:
        assert nm == 1
        st = pl.BlockSpec((tm // period, tf), lambda m, f: (0, f))
        in_specs += [st, st]
        args += [s0, s1]
        out_specs += [st, st]
        out_shape += [jax.ShapeDtypeStruct((tm // period, ff), F32)] * 2
        scratch += [pltpu.VMEM((tf // LANES, tm, LANES), F32)] * 2
    else:
        out_specs.append(pl.BlockSpec((1, STATE_ROWS, tf), lambda m, f: (m, 0, f)))
        out_shape.append(jax.ShapeDtypeStruct((nm, STATE_ROWS, ff), F32))
        scratch.append(pltpu.VMEM((nf, STATE_ROWS, tf), F32))
    return pl.pallas_call(
        functools.partial(_ffn_kernel, nf=nf, tpb=tpb, period=period, sample=sample, alpha=alpha,
                          splits=splits),
        grid=(nm, nf),
        in_specs=in_specs,
        out_specs=out_specs,
        out_shape=out_shape,
        scratch_shapes=scratch,
        compiler_params=_cparams(("arbitrary", "arbitrary")),
        name="ffn_sample" if sample else "ffn_prompt",
    )(*args)


PROMPT_TM = 1024


def _layer_prompt(x, mod, wts, yc, ztail, mlstm_out, *, alpha):
    B, T, D = x.shape
    (_, _, _, _, _, w_out, ln1_g, ln1_b, w_up, w_fconv, w_down, ln2_g, ln2_b) = wts
    tm = PROMPT_TM
    tpb = T // tm
    x2 = x.reshape(B * T, D)
    ym, C, n, m = mlstm_out
    tm2 = 512
    x1 = _outproj(yc, ym, w_out, x2, mod, ln1_g, ln1_b, tm=tm2, tpb=T // tm2, alpha=alpha)
    y, atail = _ffn(x1, mod, w_up, w_fconv, w_down, ln2_g, ln2_b,
                    tm=tm, tpb=tpb, tf=512, sample=False, alpha=alpha, splits=2)
    return (y.reshape(B, T, D), ztail[tpb - 1::tpb, STATE_ROWS - 2:], C, n, m[..., 0],
            atail[tpb - 1::tpb, STATE_ROWS - 2:])


def _layer_sample(x, mod, conv_buf, C0, n0, m0, ffn_buf, wts, *, alpha):
    B, T, D = x.shape
    (w_in, w_gate, bg, w_conv, wmh, w_out, ln1_g, ln1_b, w_up, w_fconv, w_down, ln2_g, ln2_b) = wts
    rows = B * T
    x2 = x.reshape(rows, D)
    yc, q, k, v, o, gates, z0, z1 = _inproj(x2, mod, w_in, w_gate, w_conv, tm=rows, tpb=1, sample=True,
                                            period=T, s0=conv_buf[:, 0], s1=conv_buf[:, 1])
    mrow = jnp.pad(jnp.repeat(m0, T, axis=0), ((0, 0), (N_HEADS, GATE_LANES - 2 * N_HEADS)))
    nrow = jnp.repeat(n0.reshape(B, N_HEADS * HEAD_DIM), T, axis=0)
    ym, C, n, m = _mlstm_sample(q[0], k[0], v[0], o[0], gates[0], bg, mrow, nrow, C0, n0, wmh, bb=8, T=T)
    x1 = _outproj(yc, ym[None], w_out, x2, mod, ln1_g, ln1_b, tm=rows, tpb=1, alpha=alpha)
    y, a0, a1 = _ffn(x1, mod, w_up, w_fconv, w_down, ln2_g, ln2_b, tm=rows, tpb=1, tf=512,
                     sample=True, alpha=alpha, period=T, s0=ffn_buf[:, 0], s1=ffn_buf[:, 1])
    return (y.reshape(B, T, D), jnp.stack([z0, z1], axis=1), C, n, m[..., 0],
            jnp.stack([a0, a1], axis=1))


def kernel(x_prompt, x_sample, c_prompt, c_sample, state_conv, state_mlstm_C, state_mlstm_n,
           state_mlstm_m, state_ffn_conv, w_ada, b_ada, w_in, b_gate, w_conv, w_mh_norm, w_out,
           ln1_g, ln1_b, w_up, w_ffn_conv, w_down, ln2_g, ln2_b):
    depth = w_in.shape[0]
    alpha = (2 * depth) ** 0.25
    Bp = x_prompt.shape[0]
    Bs, Ts, D = x_sample.shape
    dc = w_conv.shape[-1]
    dm = w_mh_norm.shape[-1]
    n_main = 3 * dc + 4 * dm
    assert dc == dm == N_HEADS * HEAD_DIM and Ts >= CONV_K - 1 and SLAB % Ts == 0

    xp, xs = x_prompt, x_sample
    outs_p = [[] for _ in range(5)]
    outs_s = [[] for _ in range(5)]
    for l in range(depth):
        c_all = jnp.concatenate([jnp.repeat(c_sample, Ts, axis=0), c_prompt], axis=0)
        mod = _ada(c_all, w_ada[l], b_ada[l])
        mod_s = mod.reshape(1, Bs * Ts + Bp, 6 * D)
        mod_p = mod[Bs * Ts:].reshape(Bp, 1, 6 * D)
        w_in_t = jnp.swapaxes(w_in[l], 0, 1)
        w_in_b = _cast_transposed(w_in_t, n_main)
        w_gate = jnp.pad(w_in_t[n_main:], ((0, GATE_LANES - 2 * N_HEADS), (0, 0))).astype(BF16)
        bg = jnp.pad(b_gate[l], (0, GATE_LANES - 2 * N_HEADS)).reshape(1, GATE_LANES)
        wmh = w_mh_norm[l].reshape(1, dm)
        Tp = xp.shape[1]
        yc, q, k, v, o, gates, ztail, w_up_b = _inproj(
            xp.reshape(Bp * Tp, D), mod_p, w_in_b, w_gate, w_conv[l], tm=PROMPT_TM,
            tpb=Tp // PROMPT_TM, sample=False, cast=(w_up[l],))
        *mlstm_out, w_down_b, w_out_b = _mlstm_prompt(q, k, v, o, gates, bg, wmh, L=LANES,
                                                      cast=(w_down[l], w_out[l]))
        wts = (
            w_in_b,
            w_gate,
            bg,
            w_conv[l],
            wmh,
            w_out_b,
            ln1_g[l].reshape(1, D), ln1_b[l].reshape(1, D),
            w_up_b,
            w_ffn_conv[l],
            w_down_b,
            ln2_g[l].reshape(1, D), ln2_b[l].reshape(1, D),
        )
        xs, *st_s = _layer_sample(xs, mod_s, state_conv[l], state_mlstm_C[l], state_mlstm_n[l],
                                  state_mlstm_m[l], state_ffn_conv[l], wts, alpha=alpha)
        xp, *st_p = _layer_prompt(xp, mod_p, wts, yc, ztail, mlstm_out, alpha=alpha)
        for acc, val in zip(outs_p, st_p):
            acc.append(val)
        for acc, val in zip(outs_s, st_s):
            acc.append(val)
    return (xp.astype(x_prompt.dtype), xs.astype(x_sample.dtype),
            *[jnp.stack(a) for a in outs_p], *[jnp.stack(a) for a in outs_s])
```

```python
import functools

import jax
import jax.numpy as jnp
from jax import lax
from jax.experimental import pallas as pl
from jax.experimental.pallas import tpu as pltpu

F32 = jnp.float32
BF16 = jnp.bfloat16

N_HEADS = 4
HEAD_DIM = 256
CONV_K = 3
LN_EPS = 1e-5
NEG = -1e30
LANES = 128
GATE_LANES = LANES
STATE_ROWS = 8
SLAB = 16
VMEM_LIMIT = 56 * 1024 * 1024

PROMPT_TM = 1024
OUTPROJ_TM = 512
FFN_TF = 512
MLSTM_CHUNK = LANES
SAMPLE_SEQS = 8


def _cparams(sem):
    return pltpu.CompilerParams(dimension_semantics=sem, vmem_limit_bytes=VMEM_LIMIT)


def _ln(x):
    mu = jnp.mean(x, axis=-1, keepdims=True)
    xc = x - mu
    var = jnp.mean(xc * xc, axis=-1, keepdims=True)
    return xc * lax.rsqrt(var + LN_EPS)


def _log_sigmoid(x):
    return jnp.minimum(x, 0.0) - jnp.log1p(jnp.exp(-jnp.abs(x)))


def _sigmoid(x):
    return 1.0 / (1.0 + jnp.exp(-x))


def _dot(a, b):
    return jnp.dot(a, b, preferred_element_type=F32)


def _dot_nt(a, b):
    return lax.dot_general(a, b, (((1,), (1,)), ((), ())), preferred_element_type=F32)


def _conv3_rows(z, w, prev):
    p0 = prev[STATE_ROWS - 2:STATE_ROWS - 1]
    p1 = prev[STATE_ROWS - 1:STATE_ROWS]
    t = lax.broadcasted_iota(jnp.int32, z.shape, 0)
    z1 = jnp.where(t >= 1, pltpu.roll(z, 1, 0), p1)
    z2 = jnp.where(t >= 2, pltpu.roll(z, 2, 0), jnp.where(t == 0, p0, p1))
    return w[0:1] * z2 + w[1:2] * z1 + w[2:3] * z, z[z.shape[0] - STATE_ROWS:]


def _carried_rows(carry_ref, idx, first):
    @pl.when(first)
    def _():
        carry_ref[idx] = jnp.zeros(carry_ref.shape[1:], F32)

    return carry_ref[idx]


def _conv3_sequences(z, w, s_ref, z_scr, y_scr, t_ref, T):
    nseq = z.shape[0] // T
    y = w[0:1] * pltpu.roll(z, 2, 0) + w[1:2] * pltpu.roll(z, 1, 0) + w[2:3] * z

    def rows(t):
        return pl.ds(t, nseq, stride=T)

    cols = []
    for c in range(z.shape[1] // LANES):
        cs = slice(c * LANES, (c + 1) * LANES)
        w0, w1, w2 = w[0:1, cs], w[1:2, cs], w[2:3, cs]
        z_scr[c] = z[:, cs]
        y_scr[c] = y[:, cs]
        s0 = s_ref[:, 0, cs]
        s1 = s_ref[:, 1, cs]
        z0 = z_scr[c, rows(0), :]
        z1 = z_scr[c, rows(1), :]
        y_scr[c, rows(0), :] = w0 * s0 + w1 * s1 + w2 * z0
        y_scr[c, rows(1), :] = w0 * s1 + w1 * z0 + w2 * z1
        t_ref[:, 0, cs] = z_scr[c, rows(T - 2), :]
        t_ref[:, 1, cs] = z_scr[c, rows(T - 1), :]
        cols.append(y_scr[c])
    return jnp.concatenate(cols, axis=1)


def _ada_kernel(c_ref, w_ref, b_ref, wt_ref, o_ref, wb_ref):
    c = c_ref[...]
    s = (c * _sigmoid(c)).astype(BF16)
    o_ref[...] = _dot(s, w_ref[...].astype(BF16)) + b_ref[...]
    wb_ref[...] = wt_ref[...].T.astype(BF16)


def _ada(c, w, b, wt, n_t, *, tn=768, tt=512):
    r, d = c.shape
    n = w.shape[1]
    k = wt.shape[1]
    steps, t_blocks = n // tn, n_t // tt
    assert steps * tn == n and t_blocks * tt == n_t and t_blocks <= steps

    def t_block(j):
        return jnp.minimum(j, t_blocks - 1)

    return pl.pallas_call(
        _ada_kernel,
        grid=(steps,),
        in_specs=[
            pl.BlockSpec((r, d), lambda j: (0, 0)),
            pl.BlockSpec((d, tn), lambda j: (0, j)),
            pl.BlockSpec((1, tn), lambda j: (0, j)),
            pl.BlockSpec((tt, k), lambda j: (t_block(j), 0)),
        ],
        out_specs=[pl.BlockSpec((r, tn), lambda j: (0, j)),
                   pl.BlockSpec((k, tt), lambda j: (0, t_block(j)))],
        out_shape=[jax.ShapeDtypeStruct((r, n), F32), jax.ShapeDtypeStruct((k, n_t), BF16)],
        compiler_params=_cparams(("arbitrary",)),
        name="ada",
    )(c, w, b.reshape(1, n), wt)


def _cast_side_job(mats, n_steps, index_map):
    specs, shapes = [], []
    for w in mats:
        slab = w.shape[0] // n_steps
        assert slab * n_steps == w.shape[0] and slab % SLAB == 0
        specs.append(pl.BlockSpec((slab, w.shape[1]), index_map))
        shapes.append(jax.ShapeDtypeStruct(w.shape, BF16))
    return specs, shapes


def _cast_slabs(srcs, dsts):
    for src, dst in zip(srcs, dsts):
        dst[...] = src[...].astype(BF16)


def _inproj_kernel(*refs, tpb, period, sample, n_cast=0):
    if sample:
        (x_ref, sh_ref, sc_ref, wb_ref, wc_ref, wh_ref, wq_ref, wk_ref, wv_ref, wo_ref, wg_ref,
         wconv_ref, s_ref,
         yc_ref, q_ref, k_ref, v_ref, o_ref, g_ref, t_ref, u_scr, z_scr, y_scr) = refs
    else:
        (x_ref, sh_ref, sc_ref, wb_ref, wc_ref, wh_ref, wq_ref, wk_ref, wv_ref, wo_ref, wg_ref,
         wconv_ref) = refs[:12]
        cast_in = refs[12:12 + n_cast]
        (yc_ref, q_ref, k_ref, v_ref, o_ref, g_ref, zt_ref) = refs[12 + n_cast:19 + n_cast]
        cast_out = refs[19 + n_cast:19 + 2 * n_cast]
        u_scr, carry_scr = refs[19 + 2 * n_cast:]
    m = pl.program_id(0)
    j = pl.program_id(1)

    if not sample:
        _cast_slabs(cast_in, cast_out)

        carried = _carried_rows(carry_scr, j, lax.rem(m, tpb) == 0)
        rs = u_scr.shape[0] // 2

        def body(first):
            prev = carried
            for i in range(2):
                r = slice(i * rs, (i + 1) * rs)
                if first:
                    u = (_ln(x_ref[r, :]) * (1.0 + sc_ref[0]) + sh_ref[0]).astype(BF16)
                    u_scr[r, :] = u
                    g_ref[0, r, :] = _dot_nt(u, wg_ref[...])
                else:
                    u = u_scr[r, :]
                z = _dot(u, wc_ref[...]) * _dot(u, wh_ref[...])
                yc, prev = _conv3_rows(z, wconv_ref[...], prev)
                yc_ref[0, r, :] = (_dot(u, wb_ref[...]) * yc).astype(BF16)
                q_ref[0, r, :] = _dot(u, wq_ref[...]).astype(BF16)
                k_ref[0, r, :] = (_dot(u, wk_ref[...]) * (HEAD_DIM ** -0.5)).astype(BF16)
                v_ref[0, r, :] = _dot(u, wv_ref[...]).astype(BF16)
                o_ref[0, r, :] = _dot(u, wo_ref[...]).astype(BF16)
            carry_scr[j] = prev
            zt_ref[0] = prev

        pl.when(j == 0)(lambda: body(True))
        pl.when(j > 0)(lambda: body(False))
        return

    @pl.when(j == 0)
    def _():
        u = _ln(x_ref[...]) * (1.0 + sc_ref[0]) + sh_ref[0]
        ub = u.astype(BF16)
        u_scr[...] = ub
        g_ref[0] = _dot_nt(ub, wg_ref[...])

    u = u_scr[...]
    z = _dot(u, wc_ref[...]) * _dot(u, wh_ref[...])
    yc = _conv3_sequences(z, wconv_ref[...], s_ref, z_scr, y_scr, t_ref, period)
    half = u.shape[0] // 2
    bg = jnp.concatenate([_dot(u[:half], wb_ref[...]), _dot(u[half:], wb_ref[...])], axis=0)
    yc_ref[0] = (bg * yc).astype(BF16)
    q_ref[0] = _dot(u, wq_ref[...]).astype(BF16)
    k_ref[0] = (_dot(u, wk_ref[...]) * (HEAD_DIM ** -0.5)).astype(BF16)
    v_ref[0] = _dot(u, wv_ref[...]).astype(BF16)
    o_ref[0] = _dot(u, wo_ref[...]).astype(BF16)


def _inproj(x, mod, w_in, w_gate, w_conv, *, tm, tpb, sample, period=0, state=None, cast=()):
    rows, d = x.shape
    dc = w_conv.shape[1]
    tn = HEAD_DIM
    nj = dc // tn
    nm = rows // tm
    nseq = nm // tpb
    r = tm if sample else 1

    def wspec(off):
        return pl.BlockSpec((d, tn), lambda m, j, off=off: (0, off * nj + j))

    in_specs = [
        pl.BlockSpec((tm, d), lambda m, j: (m, 0)),
        pl.BlockSpec((1, r, d), lambda m, j: (m // tpb, 0, 0)),
        pl.BlockSpec((1, r, d), lambda m, j: (m // tpb, 0, 1)),
        wspec(0), wspec(1), wspec(2), wspec(3), wspec(4), wspec(5), wspec(6),
        pl.BlockSpec((GATE_LANES, d), lambda m, j: (0, 0)),
        pl.BlockSpec((CONV_K, tn), lambda m, j: (0, j)),
    ]
    args = [x, mod, mod, w_in, w_in, w_in, w_in, w_in, w_in, w_in, w_gate, w_conv]
    scratch = [pltpu.VMEM((tm, d), BF16)]
    act = pl.BlockSpec((1, tm, tn), lambda m, j: (m // tpb, m % tpb, j))
    out_specs = [act, act, act, act, act,
                 pl.BlockSpec((1, tm, GATE_LANES), lambda m, j: (m // tpb, m % tpb, 0))]
    out_shape = [jax.ShapeDtypeStruct((nseq, tpb * tm, dc), BF16)] * 5 + [
        jax.ShapeDtypeStruct((nseq, tpb * tm, GATE_LANES), F32)]
    if sample:
        assert nm == 1
        st = pl.BlockSpec((tm // period, CONV_K - 1, tn), lambda m, j: (0, 0, j))
        in_specs.append(st)
        args.append(state)
        out_specs.append(st)
        out_shape.append(jax.ShapeDtypeStruct((tm // period, CONV_K - 1, dc), F32))
        scratch += [pltpu.VMEM((tn // LANES, tm, LANES), F32)] * 2
    else:
        out_specs.append(pl.BlockSpec((1, STATE_ROWS, tn), lambda m, j: (m, 0, j)))
        out_shape.append(jax.ShapeDtypeStruct((nm, STATE_ROWS, dc), F32))
        scratch.append(pltpu.VMEM((nj, STATE_ROWS, tn), F32))
        cast_specs, cast_shapes = _cast_side_job(cast, nm * nj, lambda m, j: (m * nj + j, 0))
        in_specs += cast_specs
        args += list(cast)
        out_specs += cast_specs
        out_shape += cast_shapes
    return pl.pallas_call(
        functools.partial(_inproj_kernel, tpb=tpb, period=period, sample=sample, n_cast=len(cast)),
        grid=(nm, nj),
        in_specs=in_specs,
        out_specs=out_specs,
        out_shape=out_shape,
        scratch_shapes=scratch,
        compiler_params=_cparams(("arbitrary", "arbitrary")),
        name="inproj_sample" if sample else "inproj_prompt",
    )(*args)


def _split3(x):
    hi = x.astype(BF16)
    r1 = x - hi.astype(F32)
    mid = r1.astype(BF16)
    lo = (r1 - mid.astype(F32)).astype(BF16)
    return hi, mid, lo


def _head_out(hh, o, wmh):
    return (_sigmoid(o.astype(F32)) * (_ln(hh) * wmh)).astype(BF16)


def _split2(x):
    hi = x.astype(BF16)
    return hi, (x - hi.astype(F32)).astype(BF16)


def _rowsum(x, ones):
    hi, lo = _split2(x)
    return _dot(hi, ones) + _dot(lo, ones)


def _rep2(x):
    return jnp.concatenate([x, x], axis=1)


def _mlstm_chunk_kernel(*refs, B, L, n_cast):
    (q_ref, k_ref, v_ref, o_ref, g_ref, bg_ref, wmh_ref) = refs[:7]
    (y_ref, c_ref, n_ref, m_ref) = refs[7 + n_cast:11 + n_cast]
    nrep_scr = refs[-1]
    _cast_slabs(refs[7:7 + n_cast], refs[11 + n_cast:11 + 2 * n_cast])
    _mlstm_chunk(q_ref, k_ref, v_ref, o_ref, g_ref, bg_ref, wmh_ref, y_ref, c_ref, n_ref, m_ref,
                 nrep_scr, B=B, L=L)


def _mlstm_chunk(q_ref, k_ref, v_ref, o_ref, g_ref, bg_ref, wmh_ref,
                 y_ref, c_ref, n_ref, m_ref, nrep_scr, *, B, L):
    step = pl.program_id(0)

    @pl.when(step == 0)
    def _():
        c_ref[...] = jnp.zeros_like(c_ref)
        m_ref[...] = jnp.zeros_like(m_ref)
        nrep_scr[...] = jnp.zeros_like(nrep_scr)

    row = lax.broadcasted_iota(jnp.int32, (L, L), 0)
    col = lax.broadcasted_iota(jnp.int32, (L, L), 1)
    causal = col <= row
    tril = jnp.where(causal, 1.0, 0.0).astype(BF16)
    ones_l = jnp.ones((L, LANES), BF16)
    ones_d = jnp.ones((HEAD_DIM, LANES), BF16)
    inv_d = 1.0 / HEAD_DIM
    tn = (((0,), (0,)), ((), ()))

    heads =[(b, h) for b in range(B) for h in range(N_HEADS)]

    def hsl(h):
        return slice(h * HEAD_DIM, (h + 1) * HEAD_DIM)

    gate = []
    for b in range(B):
        g = g_ref[b] + bg_ref[...]
        hi, mid, lo = _split3(_log_sigmoid(g))
        bcum = _dot(tril, hi) + _dot(tril, mid) + _dot(tril, lo)
        gate.append((g, g.T, bcum, bcum.T, m_ref[b]))

    st = []
    for b, h in heads:
        g, g_t, bcum, bcum_t, m_all = gate[b]
        bc = jnp.broadcast_to(bcum[:, N_HEADS + h:N_HEADS + h + 1], (L, LANES))
        li = jnp.broadcast_to(g[:, h:h + 1], (L, LANES))
        br = bcum_t[N_HEADS + h:N_HEADS + h + 1, :]
        m_prev = m_all[h:h + 1, :]
        a = bc + m_prev
        dlog = jnp.where(causal, bc - br + g_t[h:h + 1, :], NEG)
        mt = jnp.maximum(a, jnp.max(dlog, axis=1, keepdims=True))
        st.append(dict(bc=bc, li=li, m_prev=m_prev, mt=mt, dw=jnp.exp(dlog - mt),
                       inter=jnp.exp(a - mt)))

    for (b, h), e in zip(heads, st):
        e["s"] = lax.dot_general(q_ref[b, :, hsl(h)], k_ref[b, :, hsl(h)], (((1,), (1,)), ((), ())),
                                 preferred_element_type=F32) * e["dw"]

    for i, ((b, h), e) in enumerate(zip(heads, st)):
        s_hi, s_lo = _split2(e["s"])
        cn = jnp.concatenate([c_ref[b, h], nrep_scr[i]], axis=1).astype(BF16)
        qc = _dot(q_ref[b, :, hsl(h)], cn)
        num = _dot(s_hi, v_ref[b, :, hsl(h)]) + _rep2(e["inter"]) * qc[:, :HEAD_DIM]
        den = _dot(s_hi, ones_l) + _dot(s_lo, ones_l) + e["inter"] * qc[:, HEAD_DIM:]
        rden = 1.0 / jnp.maximum(jnp.abs(den), jnp.exp(-e["mt"]))
        e["hh"] = num * _rep2(rden)

    for (b, h), e in zip(heads, st):
        hh = e["hh"]
        xc = hh - _rep2(_rowsum(hh, ones_d) * inv_d)
        rstd = lax.rsqrt(_rowsum(xc * xc, ones_d) * inv_d + LN_EPS)
        y = _sigmoid(o_ref[b, :, hsl(h)].astype(F32)) * (xc * _rep2(rstd) * wmh_ref[:, hsl(h)])
        y_ref[b, :, hsl(h)] = y.astype(BF16)

    m_rows = []
    for i, ((b, h), e) in enumerate(zip(heads, st)):
        kh = k_ref[b, :, hsl(h)]
        m_new = e["mt"][L - 1:L, :]
        b_last = e["bc"][L - 1:L, :]
        wc = jnp.exp(b_last - e["bc"] + e["li"] - m_new)
        dc = jnp.exp(b_last + e["m_prev"] - m_new)
        vw = (v_ref[b, :, hsl(h)].astype(F32) * _rep2(wc)).astype(BF16)
        c_ref[b, h] = _rep2(dc) * c_ref[b, h] + lax.dot_general(kh, vw, tn, preferred_element_type=F32)
        wc_hi, wc_lo = _split2(wc)
        nrep_scr[i] = (dc * nrep_scr[i] + lax.dot_general(kh, wc_hi, tn, preferred_element_type=F32)
                       + lax.dot_general(kh, wc_lo, tn, preferred_element_type=F32))
        m_rows.append(m_new)

    for b in range(B):
        m_ref[b] = jnp.concatenate(m_rows[b * N_HEADS:(b + 1) * N_HEADS], axis=0)

    @pl.when(step == pl.num_programs(0) - 1)
    def _():
        for b in range(B):
            n_ref[b] = jnp.concatenate(
                [nrep_scr[b * N_HEADS + h].T[0:1, :] for h in range(N_HEADS)], axis=0)


def _mlstm_prompt(q, k, v, o, gates, bg, wmh, *, L, cast=()):
    b, t, dm = q.shape
    assert L == LANES
    act = pl.BlockSpec((b, L, dm), lambda c: (0, c, 0))
    whole = lambda *shape: pl.BlockSpec(shape, lambda c: (0,) * len(shape))
    cast_specs, cast_shapes = _cast_side_job(cast, t // L, lambda c: (c, 0))
    return pl.pallas_call(
        functools.partial(_mlstm_chunk_kernel, B=b, L=L, n_cast=len(cast)),
        grid=(t // L,),
        in_specs=[act, act, act, act,
                  pl.BlockSpec((b, L, GATE_LANES), lambda c: (0, c, 0)),
                  whole(1, GATE_LANES), whole(1, dm)] + cast_specs,
        out_specs=[act,
                   whole(b, N_HEADS, HEAD_DIM, HEAD_DIM),
                   whole(b, N_HEADS, HEAD_DIM),
                   whole(b, N_HEADS, GATE_LANES)] + cast_specs,
        out_shape=[jax.ShapeDtypeStruct((b, t, dm), BF16),
                   jax.ShapeDtypeStruct((b, N_HEADS, HEAD_DIM, HEAD_DIM), F32),
                   jax.ShapeDtypeStruct((b, N_HEADS, HEAD_DIM), F32),
                   jax.ShapeDtypeStruct((b, N_HEADS, GATE_LANES), F32)] + cast_shapes,
        scratch_shapes=[pltpu.VMEM((b * N_HEADS, HEAD_DIM, LANES), F32)],
        compiler_params=_cparams(("arbitrary",)),
        name="mlstm_prompt",
    )(q, k, v, o, gates, bg, wmh, *cast)


def _mlstm_step_kernel(q_ref, k_ref, v_ref, o_ref, g_ref, bg_ref, mrow_ref, nrow_ref,
                       c0_ref, n0_ref, wmh_ref,
                       y_ref, c_ref, n_ref, m_ref, *, bb, T):
    R = bb * T
    per_slab = SLAB // T
    t = lax.rem(lax.broadcasted_iota(jnp.int32, (R, GATE_LANES), 0), T)

    def down(x, d):
        return pltpu.roll(x, d, 0)

    def up(x, d):
        return pltpu.roll(x, x.shape[0] - d, 0)

    def seg_last(x):
        out = x
        for d in range(1, T):
            out = jnp.where(t == T - 1 - d, up(x, d), out)
        return out

    g = g_ref[...] + bg_ref[...]
    li = pltpu.roll(g, N_HEADS, 1)
    lf = _log_sigmoid(g)
    b = lf
    for d in range(1, T):
        b = b + jnp.where(t >= d, down(lf, d), 0.0)
    m_prev = mrow_ref[...]
    a = b + m_prev
    dl = [li] + [jnp.where(t >= d, b - down(b, d) + down(li, d), NEG) for d in range(1, T)]
    mt = a
    for d in range(T):
        mt = jnp.maximum(mt, dl[d])
    dw = [jnp.exp(dl[d] - mt) for d in range(T)]
    inter = jnp.exp(a - mt)
    emt = jnp.exp(-mt)
    m_new = seg_last(mt)
    b_last = seg_last(b)
    wc = jnp.exp(b_last - b + li - m_new)
    dc = jnp.exp(b_last + m_prev - m_new)

    row_s = lax.broadcasted_iota(jnp.int32, (SLAB, HEAD_DIM), 0)
    row_r = lax.broadcasted_iota(jnp.int32, (R, HEAD_DIM), 0)

    for h in range(N_HEADS):
        hs = slice(h * HEAD_DIM, (h + 1) * HEAD_DIM)
        ln = N_HEADS + h

        def col(x):
            return x[:, ln:ln + 1]

        qb = q_ref[:, hs]
        kb = k_ref[:, hs]
        vb = v_ref[:, hs]
        qf = qb.astype(F32)
        kf = kb.astype(F32)
        vf = vb.astype(F32)
        num = jnp.zeros((R, HEAD_DIM), F32)
        den = jnp.zeros((R, 1), F32)
        for d in range(T):
            kd = kf if d == 0 else down(kf, d)
            vd = vf if d == 0 else down(vf, d)
            sw = jnp.sum(qf * kd, axis=1, keepdims=True) * col(dw[d])
            num = num + sw * vd
            den = den + sw

        qc_slabs = []
        for si in range(R // SLAB):
            q16 = qb[si * SLAB:(si + 1) * SLAB]
            acc = jnp.zeros((SLAB, HEAD_DIM), F32)
            for bl in range(per_slab):
                bi = si * per_slab + bl
                r = _dot(q16, c0_ref[bi, h].astype(BF16))
                acc = jnp.where(row_s // T == bl, r, acc)
            qc_slabs.append(acc)
        qc = jnp.concatenate(qc_slabs, axis=0)
        qn = jnp.sum(qf * nrow_ref[:, hs], axis=1, keepdims=True)
        num = num + col(inter) * qc
        den = den + col(inter) * qn
        hh = num / jnp.maximum(jnp.abs(den), col(emt))
        y_ref[:, hs] = _head_out(hh, o_ref[:, hs], wmh_ref[:, hs])

        wk = kf * col(wc)
        vw = vf * col(wc)
        for si in range(R // SLAB):
            k16 = kb[si * SLAB:(si + 1) * SLAB]
            vw16 = vw[si * SLAB:(si + 1) * SLAB]
            for bl in range(per_slab):
                bi = si * per_slab + bl
                last = bi * T + T - 1
                vwb = jnp.where(row_s // T == bl, vw16, 0.0).astype(BF16)
                dcb = dc[last:last + 1, ln:ln + 1]
                c_ref[bi, h] = dcb * c0_ref[bi, h] + lax.dot_general(
                    k16, vwb, (((0,), (0,)), ((), ())), preferred_element_type=F32)
                n_ref[bi, h:h + 1, :] = dcb * n0_ref[bi, h:h + 1, :] + jnp.sum(
                    jnp.where(row_r // T == bi, wk, 0.0), axis=0, keepdims=True)
                m_ref[bi, h:h + 1, :] = jnp.broadcast_to(
                    m_new[last:last + 1, ln:ln + 1], (1, GATE_LANES))


def _mlstm_sample(q, k, v, o, gates, bg, mrow, nrow, c0, n0, wmh, *, bb, T):
    rows, dm = q.shape
    nb = c0.shape[0]
    R = bb * T
    act = pl.BlockSpec((R, dm), lambda i: (i, 0))
    gat = pl.BlockSpec((R, GATE_LANES), lambda i: (i, 0))
    cspec = pl.BlockSpec((bb, N_HEADS, HEAD_DIM, HEAD_DIM), lambda i: (i, 0, 0, 0))
    nspec = pl.BlockSpec((bb, N_HEADS, HEAD_DIM), lambda i: (i, 0, 0))
    return pl.pallas_call(
        functools.partial(_mlstm_step_kernel, bb=bb, T=T),
        grid=(nb // bb,),
        in_specs=[act, act, act, act, gat,
                  pl.BlockSpec((1, GATE_LANES), lambda i: (0, 0)),
                  gat,
                  pl.BlockSpec((R, dm), lambda i: (i, 0)),
                  cspec, nspec,
                  pl.BlockSpec((1, dm), lambda i: (0, 0))],
        out_specs=[act, cspec, nspec,
                   pl.BlockSpec((bb, N_HEADS, GATE_LANES), lambda i: (i, 0, 0))],
        out_shape=[jax.ShapeDtypeStruct((rows, dm), BF16),
                   jax.ShapeDtypeStruct(c0.shape, F32),
                   jax.ShapeDtypeStruct(n0.shape, F32),
                   jax.ShapeDtypeStruct((nb, N_HEADS, GATE_LANES), F32)],
        compiler_params=_cparams(("arbitrary",)),
        name="mlstm_sample",
    )(q, k, v, o, gates, bg, mrow, nrow, c0, n0, wmh)


def _outproj_kernel(yc_ref, ym_ref, w_ref, x_ref, g1_ref, lg_ref, lb_ref, o_ref, *, alpha, splits):
    dc = yc_ref.shape[-1]
    rs = x_ref.shape[0] // splits
    for i in range(splits):
        r = slice(i * rs, (i + 1) * rs)
        g1 = g1_ref[0] if g1_ref.shape[1] == 1 else g1_ref[0, r, :]
        mix = _dot(yc_ref[0, r, :], w_ref[0:dc, :]) + _dot(ym_ref[0, r, :], w_ref[dc:, :])
        o_ref[r, :] = _ln(alpha * x_ref[r, :] + (1.0 + g1) * mix) * lg_ref[...] + lb_ref[...]


def _outproj(yc, ym, w_out, x, mod, ln_g, ln_b, *, tm, tpb, alpha, splits=2):
    rows, d = x.shape
    dc = yc.shape[-1]
    dm = ym.shape[-1]
    r = 1 if mod.shape[1] == 1 else tm
    vec = pl.BlockSpec((1, d), lambda m: (0, 0))
    return pl.pallas_call(
        functools.partial(_outproj_kernel, alpha=alpha, splits=splits),
        grid=(rows // tm,),
        in_specs=[pl.BlockSpec((1, tm, dc), lambda m: (m // tpb, m % tpb, 0)),
                  pl.BlockSpec((1, tm, dm), lambda m: (m // tpb, m % tpb, 0)),
                  pl.BlockSpec((dc + dm, d), lambda m: (0, 0)),
                  pl.BlockSpec((tm, d), lambda m: (m, 0)),
                  pl.BlockSpec((1, r, d), lambda m: (m // tpb, 0, 2)),
                  vec, vec],
        out_specs=pl.BlockSpec((tm, d), lambda m: (m, 0)),
        out_shape=jax.ShapeDtypeStruct((rows, d), F32),
        compiler_params=_cparams(("arbitrary",)),
        name="outproj",
    )(yc, ym, w_out, x, mod, ln_g, ln_b)


def _ffn_kernel(*refs, nf, tpb, period, sample, alpha, splits):
    if sample:
        (x_ref, sh_ref, sc_ref, g2_ref, wa_ref, wg_ref, wconv_ref, wd_ref, lg_ref, lb_ref,
         s_ref, y_ref, t_ref, u_scr, z_scr, y_scr) = refs
    else:
        (x_ref, sh_ref, sc_ref, g2_ref, wa_ref, wg_ref, wconv_ref, wd_ref, lg_ref, lb_ref,
         y_ref, at_ref, u_scr, carry_scr) = refs
    m = pl.program_id(0)
    f = pl.program_id(1)

    if not sample:
        carried = _carried_rows(carry_scr, f, lax.rem(m, tpb) == 0)
        rs = u_scr.shape[0] // splits

        def body(first, last):
            prev = carried
            for i in range(splits):
                r = slice(i * rs, (i + 1) * rs)
                if first:
                    u = (_ln(x_ref[r, :]) * (1.0 + sc_ref[0]) + sh_ref[0]).astype(BF16)
                    u_scr[r, :] = u
                else:
                    u = u_scr[r, :]
                ac, prev = _conv3_rows(_dot(u, wa_ref[...]), wconv_ref[...], prev)
                hcur = (ac * _sigmoid(ac) * _dot(u, wg_ref[...])).astype(BF16)
                acc = _dot(hcur, wd_ref[...])
                if not first:
                    acc = y_ref[r, :] + acc
                if last:
                    acc = (_ln(alpha * x_ref[r, :] + (1.0 + g2_ref[0]) * acc)
                           * lg_ref[...] + lb_ref[...])
                y_ref[r, :] = acc
            carry_scr[f] = prev
            at_ref[0] = prev

        if nf == 1:
            body(True, True)
        else:
            pl.when(f == 0)(lambda: body(True, False))
            if nf > 2:
                pl.when(jnp.logical_and(f > 0, f < nf - 1))(lambda: body(False, False))
            pl.when(f == nf - 1)(lambda: body(False, True))
        return

    @pl.when(f == 0)
    def _():
        u = _ln(x_ref[...]) * (1.0 + sc_ref[0]) + sh_ref[0]
        u_scr[...] = u.astype(BF16)
        y_ref[...] = jnp.zeros_like(y_ref)

    u = u_scr[...]
    a = _dot(u, wa_ref[...])
    ac = _conv3_sequences(a, wconv_ref[...], s_ref, z_scr, y_scr, t_ref, period)
    hcur = (ac * _sigmoid(ac) * _dot(u, wg_ref[...])).astype(BF16)
    y_ref[...] += _dot(hcur, wd_ref[...])

    @pl.when(f == nf - 1)
    def _():
        y_ref[...] = (_ln(alpha * x_ref[...] + (1.0 + g2_ref[0]) * y_ref[...])
                      * lg_ref[...] + lb_ref[...])


def _ffn(x, mod, w_up, w_conv, w_down, ln_g, ln_b, *, tm, tpb, tf, sample, alpha, period=0,
         splits=1, state=None):
    rows, d = x.shape
    ff = w_down.shape[0]
    nf = ff // tf
    nm = rows // tm
    r = tm if sample else 1
    vec = pl.BlockSpec((1, d), lambda m, f: (0, 0))
    in_specs = [
        pl.BlockSpec((tm, d), lambda m, f: (m, 0), pipeline_mode=pl.Buffered(1)),
        pl.BlockSpec((1, r, d), lambda m, f: (m // tpb, 0, 3)),
        pl.BlockSpec((1, r, d), lambda m, f: (m // tpb, 0, 4)),
        pl.BlockSpec((1, r, d), lambda m, f: (m // tpb, 0, 5)),
        pl.BlockSpec((d, tf), lambda m, f: (0, f)),
        pl.BlockSpec((d, tf), lambda m, f: (0, nf + f)),
        pl.BlockSpec((CONV_K, tf), lambda m, f: (0, f)),
        pl.BlockSpec((tf, d), lambda m, f: (f, 0)),
        vec, vec,
    ]
    args = [x, mod, mod, mod, w_up, w_up, w_conv, w_down, ln_g, ln_b]
    scratch = [pltpu.VMEM((tm, d), BF16)]
    out_specs = [pl.BlockSpec((tm, d), lambda m, f: (m, 0))]
    out_shape = [jax.ShapeDtypeStruct((rows, d), F32)]
    if sample:
        assert nm == 1
        st = pl.BlockSpec((tm // period, CONV_K - 1, tf), lambda m, f: (0, 0, f))
        in_specs.append(st)
        args.append(state)
        out_specs.append(st)
        out_shape.append(jax.ShapeDtypeStruct((tm // period, CONV_K - 1, ff), F32))
        scratch += [pltpu.VMEM((tf // LANES, tm, LANES), F32)] * 2
    else:
        out_specs.append(pl.BlockSpec((1, STATE_ROWS, tf), lambda m, f: (m, 0, f)))
        out_shape.append(jax.ShapeDtypeStruct((nm, STATE_ROWS, ff), F32))
        scratch.append(pltpu.VMEM((nf, STATE_ROWS, tf), F32))
    return pl.pallas_call(
        functools.partial(_ffn_kernel, nf=nf, tpb=tpb, period=period, sample=sample, alpha=alpha,
                          splits=splits),
        grid=(nm, nf),
        in_specs=in_specs,
        out_specs=out_specs,
        out_shape=out_shape,
        scratch_shapes=scratch,
        compiler_params=_cparams(("arbitrary", "arbitrary")),
        name="ffn_sample" if sample else "ffn_prompt",
    )(*args)


def _layer_prompt(x, mod, wts, yc, ztail, mlstm_out, *, alpha):
    B, T, D = x.shape
    (_, _, _, _, _, w_out, ln1_g, ln1_b, w_up, w_fconv, w_down, ln2_g, ln2_b) = wts
    tm = PROMPT_TM
    tpb = T // tm
    x2 = x.reshape(B * T, D)
    ym, C, n, m = mlstm_out
    x1 = _outproj(yc, ym, w_out, x2, mod, ln1_g, ln1_b, tm=OUTPROJ_TM, tpb=T // OUTPROJ_TM,
                  alpha=alpha)
    y, atail = _ffn(x1, mod, w_up, w_fconv, w_down, ln2_g, ln2_b,
                    tm=tm, tpb=tpb, tf=FFN_TF, sample=False, alpha=alpha, splits=2)
    return (y.reshape(B, T, D), ztail[tpb - 1::tpb, STATE_ROWS - 2:], C, n, m[..., 0],
            atail[tpb - 1::tpb, STATE_ROWS - 2:])


def _layer_sample(x, mod, conv_buf, C0, n0, m0, ffn_buf, wts, *, alpha):
    B, T, D = x.shape
    (w_in, w_gate, bg, w_conv, wmh, w_out, ln1_g, ln1_b, w_up, w_fconv, w_down, ln2_g, ln2_b) = wts
    rows = B * T
    x2 = x.reshape(rows, D)
    yc, q, k, v, o, gates, conv_new = _inproj(x2, mod, w_in, w_gate, w_conv, tm=rows, tpb=1,
                                              sample=True, period=T, state=conv_buf)
    mrow = jnp.pad(jnp.repeat(m0, T, axis=0), ((0, 0), (N_HEADS, GATE_LANES - 2 * N_HEADS)))
    nrow = jnp.repeat(n0.reshape(B, N_HEADS * HEAD_DIM), T, axis=0)
    ym, C, n, m = _mlstm_sample(q[0], k[0], v[0], o[0], gates[0], bg, mrow, nrow, C0, n0, wmh,
                                bb=SAMPLE_SEQS, T=T)
    x1 = _outproj(yc, ym[None], w_out, x2, mod, ln1_g, ln1_b, tm=rows, tpb=1, alpha=alpha)
    y, ffn_new = _ffn(x1, mod, w_up, w_fconv, w_down, ln2_g, ln2_b, tm=rows, tpb=1, tf=FFN_TF,
                      sample=True, alpha=alpha, period=T, state=ffn_buf)
    return y.reshape(B, T, D), conv_new, C, n, m[..., 0], ffn_new


def kernel(x_prompt, x_sample, c_prompt, c_sample, state_conv, state_mlstm_C, state_mlstm_n,
           state_mlstm_m, state_ffn_conv, w_ada, b_ada, w_in, b_gate, w_conv, w_mh_norm, w_out,
           ln1_g, ln1_b, w_up, w_ffn_conv, w_down, ln2_g, ln2_b):
    depth = w_in.shape[0]
    alpha = (2 * depth) ** 0.25
    Bp = x_prompt.shape[0]
    Bs, Ts, D = x_sample.shape
    dc = w_conv.shape[-1]
    dm = w_mh_norm.shape[-1]
    n_main = 3 * dc + 4 * dm
    assert dc == dm == N_HEADS * HEAD_DIM and Ts >= CONV_K - 1 and SLAB % Ts == 0

    xp, xs = x_prompt, x_sample
    outs_p = [[] for _ in range(5)]
    outs_s = [[] for _ in range(5)]
    for l in range(depth):
        c_all = jnp.concatenate([jnp.repeat(c_sample, Ts, axis=0), c_prompt], axis=0)
        w_in_t = jnp.swapaxes(w_in[l], 0, 1)
        mod, w_in_b = _ada(c_all, w_ada[l], b_ada[l], w_in_t, n_main)
        mod_s = mod.reshape(1, Bs * Ts + Bp, 6 * D)
        mod_p = mod[Bs * Ts:].reshape(Bp, 1, 6 * D)
        w_gate = jnp.pad(w_in_t[n_main:], ((0, GATE_LANES - 2 * N_HEADS), (0, 0))).astype(BF16)
        bg = jnp.pad(b_gate[l], (0, GATE_LANES - 2 * N_HEADS)).reshape(1, GATE_LANES)
        wmh = w_mh_norm[l].reshape(1, dm)
        Tp = xp.shape[1]
        yc, q, k, v, o, gates, ztail, w_up_b = _inproj(
            xp.reshape(Bp * Tp, D), mod_p, w_in_b, w_gate, w_conv[l], tm=PROMPT_TM,
            tpb=Tp // PROMPT_TM, sample=False, cast=(w_up[l],))
        *mlstm_out, w_down_b, w_out_b = _mlstm_prompt(q, k, v, o, gates, bg, wmh, L=MLSTM_CHUNK,
                                                      cast=(w_down[l], w_out[l]))
        wts = (
            w_in_b,
            w_gate,
            bg,
            w_conv[l],
            wmh,
            w_out_b,
            ln1_g[l].reshape(1, D), ln1_b[l].reshape(1, D),
            w_up_b,
            w_ffn_conv[l],
            w_down_b,
            ln2_g[l].reshape(1, D), ln2_b[l].reshape(1, D),
        )
        xs, *st_s = _layer_sample(xs, mod_s, state_conv[l], state_mlstm_C[l], state_mlstm_n[l],
                                  state_mlstm_m[l], state_ffn_conv[l], wts, alpha=alpha)
        xp, *st_p = _layer_prompt(xp, mod_p, wts, yc, ztail, mlstm_out, alpha=alpha)
        for acc, val in zip(outs_p, st_p):
            acc.append(val)
        for acc, val in zip(outs_s, st_s):
            acc.append(val)
    return (xp.astype(x_prompt.dtype), xs.astype(x_sample.dtype),
            *[jnp.stack(a) for a in outs_p], *[jnp.stack(a) for a in outs_s])
```

```python
import functools

import jax
import jax.numpy as jnp
from jax import lax
from jax.experimental import pallas as pl
from jax.experimental.pallas import tpu as pltpu

F32 = jnp.float32
BF16 = jnp.bfloat16

N_HEADS = 4
HEAD_DIM = 256
CONV_K = 3
LN_EPS = 1e-5
NEG = -1e30
LANES = 128
GATE_LANES = LANES
STATE_ROWS = 8
SLAB = 16
VMEM_LIMIT = 56 * 1024 * 1024

PROMPT_TM = 1024
OUTPROJ_TM = 512
FFN_TF = 512
MLSTM_CHUNK = LANES
SAMPLE_SEQS = 8


def _cparams(sem):
    return pltpu.CompilerParams(dimension_semantics=sem, vmem_limit_bytes=VMEM_LIMIT)


def _ln(x):
    mu = jnp.mean(x, axis=-1, keepdims=True)
    xc = x - mu
    var = jnp.mean(xc * xc, axis=-1, keepdims=True)
    return xc * lax.rsqrt(var + LN_EPS)


def _log_sigmoid(x):
    return jnp.minimum(x, 0.0) - jnp.log1p(jnp.exp(-jnp.abs(x)))


def _sigmoid(x):
    return 1.0 / (1.0 + jnp.exp(-x))


def _dot(a, b):
    return jnp.dot(a, b, preferred_element_type=F32)


def _dot_nt(a, b):
    return lax.dot_general(a, b, (((1,), (1,)), ((), ())), preferred_element_type=F32)


def _conv3_rows(z, w, prev):
    p0 = prev[STATE_ROWS - 2:STATE_ROWS - 1]
    p1 = prev[STATE_ROWS - 1:STATE_ROWS]
    t = lax.broadcasted_iota(jnp.int32, z.shape, 0)
    z1 = jnp.where(t >= 1, pltpu.roll(z, 1, 0), p1)
    z2 = jnp.where(t >= 2, pltpu.roll(z, 2, 0), jnp.where(t == 0, p0, p1))
    return w[0:1] * z2 + w[1:2] * z1 + w[2:3] * z, z[z.shape[0] - STATE_ROWS:]


def _carried_rows(carry_ref, idx, first):
    @pl.when(first)
    def _():
        carry_ref[idx] = jnp.zeros(carry_ref.shape[1:], F32)

    return carry_ref[idx]


def _conv3_sequences(z, w, s_ref, z_scr, y_scr, t_ref, T):
    nseq = z.shape[0] // T
    y = w[0:1] * pltpu.roll(z, 2, 0) + w[1:2] * pltpu.roll(z, 1, 0) + w[2:3] * z

    def rows(t):
        return pl.ds(t, nseq, stride=T)

    cols = []
    for c in range(z.shape[1] // LANES):
        cs = slice(c * LANES, (c + 1) * LANES)
        w0, w1, w2 = w[0:1, cs], w[1:2, cs], w[2:3, cs]
        z_scr[c] = z[:, cs]
        y_scr[c] = y[:, cs]
        s0 = s_ref[:, 0, cs]
        s1 = s_ref[:, 1, cs]
        z0 = z_scr[c, rows(0), :]
        z1 = z_scr[c, rows(1), :]
        y_scr[c, rows(0), :] = w0 * s0 + w1 * s1 + w2 * z0
        y_scr[c, rows(1), :] = w0 * s1 + w1 * z0 + w2 * z1
        t_ref[:, 0, cs] = z_scr[c, rows(T - 2), :]
        t_ref[:, 1, cs] = z_scr[c, rows(T - 1), :]
        cols.append(y_scr[c])
    return jnp.concatenate(cols, axis=1)


def _ada_kernel(c_ref, w_ref, b_ref, wt_ref, o_ref, wb_ref):
    c = c_ref[...]
    s = (c * _sigmoid(c)).astype(BF16)
    o_ref[...] = _dot(s, w_ref[...].astype(BF16)) + b_ref[...]
    wb_ref[...] = wt_ref[...].T.astype(BF16)


def _ada(c, w, b, wt, n_t, *, tn=768, tt=512):
    r, d = c.shape
    n = w.shape[1]
    k = wt.shape[1]
    steps, t_blocks = n // tn, n_t // tt
    assert steps * tn == n and t_blocks * tt == n_t and t_blocks <= steps

    def t_block(j):
        return jnp.minimum(j, t_blocks - 1)

    return pl.pallas_call(
        _ada_kernel,
        grid=(steps,),
        in_specs=[
            pl.BlockSpec((r, d), lambda j: (0, 0)),
            pl.BlockSpec((d, tn), lambda j: (0, j)),
            pl.BlockSpec((1, tn), lambda j: (0, j)),
            pl.BlockSpec((tt, k), lambda j: (t_block(j), 0)),
        ],
        out_specs=[pl.BlockSpec((r, tn), lambda j: (0, j)),
                   pl.BlockSpec((k, tt), lambda j: (0, t_block(j)))],
        out_shape=[jax.ShapeDtypeStruct((r, n), F32), jax.ShapeDtypeStruct((k, n_t), BF16)],
        compiler_params=_cparams(("arbitrary",)),
        name="ada",
    )(c, w, b.reshape(1, n), wt)


def _cast_side_job(mats, n_steps, index_map):
    specs, shapes = [], []
    for w in mats:
        slab = w.shape[0] // n_steps
        assert slab * n_steps == w.shape[0] and slab % SLAB == 0
        specs.append(pl.BlockSpec((slab, w.shape[1]), index_map))
        shapes.append(jax.ShapeDtypeStruct(w.shape, BF16))
    return specs, shapes


def _cast_slabs(srcs, dsts):
    for src, dst in zip(srcs, dsts):
        dst[...] = src[...].astype(BF16)


def _inproj_kernel(*refs, tpb, period, sample, n_cast=0):
    if sample:
        (x_ref, sh_ref, sc_ref, wb_ref, wc_ref, wh_ref, wq_ref, wk_ref, wv_ref, wo_ref, wg_ref,
         wconv_ref, s_ref,
         yc_ref, q_ref, k_ref, v_ref, o_ref, g_ref, t_ref, u_scr, z_scr, y_scr) = refs
    else:
        (x_ref, sh_ref, sc_ref, wb_ref, wc_ref, wh_ref, wq_ref, wk_ref, wv_ref, wo_ref, wg_ref,
         wconv_ref) = refs[:12]
        cast_in = refs[12:12 + n_cast]
        (yc_ref, q_ref, k_ref, v_ref, o_ref, g_ref, zt_ref) = refs[12 + n_cast:19 + n_cast]
        cast_out = refs[19 + n_cast:19 + 2 * n_cast]
        u_scr, carry_scr = refs[19 + 2 * n_cast:]
    m = pl.program_id(0)
    j = pl.program_id(1)

    if not sample:
        _cast_slabs(cast_in, cast_out)

        carried = _carried_rows(carry_scr, j, lax.rem(m, tpb) == 0)
        rs = u_scr.shape[0] // 2

        def body(first):
            prev = carried
            for i in range(2):
                r = slice(i * rs, (i + 1) * rs)
                if first:
                    u = (_ln(x_ref[r, :]) * (1.0 + sc_ref[0]) + sh_ref[0]).astype(BF16)
                    u_scr[r, :] = u
                    g_ref[0, r, :] = _dot_nt(u, wg_ref[...])
                else:
                    u = u_scr[r, :]
                z = _dot(u, wc_ref[...]) * _dot(u, wh_ref[...])
                yc, prev = _conv3_rows(z, wconv_ref[...], prev)
                yc_ref[0, r, :] = (_dot(u, wb_ref[...]) * yc).astype(BF16)
                q_ref[0, r, :] = _dot(u, wq_ref[...]).astype(BF16)
                k_ref[0, r, :] = (_dot(u, wk_ref[...]) * (HEAD_DIM ** -0.5)).astype(BF16)
                v_ref[0, r, :] = _dot(u, wv_ref[...]).astype(BF16)
                o_ref[0, r, :] = _dot(u, wo_ref[...]).astype(BF16)
            carry_scr[j] = prev
            zt_ref[0] = prev

        pl.when(j == 0)(lambda: body(True))
        pl.when(j > 0)(lambda: body(False))
        return

    @pl.when(j == 0)
    def _():
        x = x_ref[...].reshape(u_scr.shape)
        u = _ln(x) * (1.0 + sc_ref[0]) + sh_ref[0]
        ub = u.astype(BF16)
        u_scr[...] = ub
        g_ref[0] = _dot_nt(ub, wg_ref[...])

    u = u_scr[...]
    z = _dot(u, wc_ref[...]) * _dot(u, wh_ref[...])
    yc = _conv3_sequences(z, wconv_ref[...], s_ref, z_scr, y_scr, t_ref, period)
    half = u.shape[0] // 2
    bg = jnp.concatenate([_dot(u[:half], wb_ref[...]), _dot(u[half:], wb_ref[...])], axis=0)
    yc_ref[0] = (bg * yc).astype(BF16)
    q_ref[0] = _dot(u, wq_ref[...]).astype(BF16)
    k_ref[0] = (_dot(u, wk_ref[...]) * (HEAD_DIM ** -0.5)).astype(BF16)
    v_ref[0] = _dot(u, wv_ref[...]).astype(BF16)
    o_ref[0] = _dot(u, wo_ref[...]).astype(BF16)


def _inproj(x, mod, w_in, w_gate, w_conv, *, tm, tpb, sample, period=0, state=None, cast=()):
    d = x.shape[-1]
    rows = x.size // d
    dc = w_conv.shape[1]
    tn = HEAD_DIM
    nj = dc // tn
    nm = rows // tm
    nseq = nm // tpb
    r = tm if sample else 1

    def wspec(off):
        return pl.BlockSpec((d, tn), lambda m, j, off=off: (0, off * nj + j))

    in_specs = [
        (pl.BlockSpec((tm // period, period, d), lambda m, j: (0, 0, 0)) if sample
         else pl.BlockSpec((tm, d), lambda m, j: (m, 0))),
        pl.BlockSpec((1, r, d), lambda m, j: (m // tpb, 0, 0)),
        pl.BlockSpec((1, r, d), lambda m, j: (m // tpb, 0, 1)),
        wspec(0), wspec(1), wspec(2), wspec(3), wspec(4), wspec(5), wspec(6),
        pl.BlockSpec((GATE_LANES, d), lambda m, j: (0, 0)),
        pl.BlockSpec((CONV_K, tn), lambda m, j: (0, j)),
    ]
    args = [x, mod, mod, w_in, w_in, w_in, w_in, w_in, w_in, w_in, w_gate, w_conv]
    scratch = [pltpu.VMEM((tm, d), BF16)]
    act = pl.BlockSpec((1, tm, tn), lambda m, j: (m // tpb, m % tpb, j))
    out_specs = [act, act, act, act, act,
                 pl.BlockSpec((1, tm, GATE_LANES), lambda m, j: (m // tpb, m % tpb, 0))]
    out_shape = [jax.ShapeDtypeStruct((nseq, tpb * tm, dc), BF16)] * 5 + [
        jax.ShapeDtypeStruct((nseq, tpb * tm, GATE_LANES), F32)]
    if sample:
        assert nm == 1
        st = pl.BlockSpec((tm // period, CONV_K - 1, tn), lambda m, j: (0, 0, j))
        in_specs.append(st)
        args.append(state)
        out_specs.append(st)
        out_shape.append(jax.ShapeDtypeStruct((tm // period, CONV_K - 1, dc), F32))
        scratch += [pltpu.VMEM((tn // LANES, tm, LANES), F32)] * 2
    else:
        out_specs.append(pl.BlockSpec((1, STATE_ROWS, tn), lambda m, j: (m, 0, j)))
        out_shape.append(jax.ShapeDtypeStruct((nm, STATE_ROWS, dc), F32))
        scratch.append(pltpu.VMEM((nj, STATE_ROWS, tn), F32))
        cast_specs, cast_shapes = _cast_side_job(cast, nm * nj, lambda m, j: (m * nj + j, 0))
        in_specs += cast_specs
        args += list(cast)
        out_specs += cast_specs
        out_shape += cast_shapes
    return pl.pallas_call(
        functools.partial(_inproj_kernel, tpb=tpb, period=period, sample=sample, n_cast=len(cast)),
        grid=(nm, nj),
        in_specs=in_specs,
        out_specs=out_specs,
        out_shape=out_shape,
        scratch_shapes=scratch,
        compiler_params=_cparams(("arbitrary", "arbitrary")),
        name="inproj_sample" if sample else "inproj_prompt",
    )(*args)


def _split3(x):
    hi = x.astype(BF16)
    r1 = x - hi.astype(F32)
    mid = r1.astype(BF16)
    lo = (r1 - mid.astype(F32)).astype(BF16)
    return hi, mid, lo


def _head_out(hh, o, wmh):
    return (_sigmoid(o.astype(F32)) * (_ln(hh) * wmh)).astype(BF16)


def _split2(x):
    hi = x.astype(BF16)
    return hi, (x - hi.astype(F32)).astype(BF16)


def _rowsum(x, ones):
    hi, lo = _split2(x)
    return _dot(hi, ones) + _dot(lo, ones)


def _rep2(x):
    return jnp.concatenate([x, x], axis=1)


def _mlstm_chunk_kernel(*refs, B, L, n_cast):
    (q_ref, k_ref, v_ref, o_ref, g_ref, bg_ref, wmh_ref) = refs[:7]
    (y_ref, c_ref, n_ref, m_ref) = refs[7 + n_cast:11 + n_cast]
    nrep_scr = refs[-1]
    _cast_slabs(refs[7:7 + n_cast], refs[11 + n_cast:11 + 2 * n_cast])
    _mlstm_chunk(q_ref, k_ref, v_ref, o_ref, g_ref, bg_ref, wmh_ref, y_ref, c_ref, n_ref, m_ref,
                 nrep_scr, B=B, L=L)


def _mlstm_chunk(q_ref, k_ref, v_ref, o_ref, g_ref, bg_ref, wmh_ref,
                 y_ref, c_ref, n_ref, m_ref, nrep_scr, *, B, L):
    step = pl.program_id(0)

    @pl.when(step == 0)
    def _():
        c_ref[...] = jnp.zeros_like(c_ref)
        m_ref[...] = jnp.zeros_like(m_ref)
        nrep_scr[...] = jnp.zeros_like(nrep_scr)

    row = lax.broadcasted_iota(jnp.int32, (L, L), 0)
    col = lax.broadcasted_iota(jnp.int32, (L, L), 1)
    causal = col <= row
    tril = jnp.where(causal, 1.0, 0.0).astype(BF16)
    ones_l = jnp.ones((L, LANES), BF16)
    ones_d = jnp.ones((HEAD_DIM, LANES), BF16)
    inv_d = 1.0 / HEAD_DIM
    tn = (((0,), (0,)), ((), ()))

    heads =[(b, h) for b in range(B) for h in range(N_HEADS)]

    def hsl(h):
        return slice(h * HEAD_DIM, (h + 1) * HEAD_DIM)

    gate = []
    for b in range(B):
        g = g_ref[b] + bg_ref[...]
        hi, mid, lo = _split3(_log_sigmoid(g))
        bcum = _dot(tril, hi) + _dot(tril, mid) + _dot(tril, lo)
        gate.append((g, g.T, bcum, bcum.T, m_ref[b]))

    st = []
    for b, h in heads:
        g, g_t, bcum, bcum_t, m_all = gate[b]
        bc = jnp.broadcast_to(bcum[:, N_HEADS + h:N_HEADS + h + 1], (L, LANES))
        li = jnp.broadcast_to(g[:, h:h + 1], (L, LANES))
        br = bcum_t[N_HEADS + h:N_HEADS + h + 1, :]
        m_prev = m_all[h:h + 1, :]
        a = bc + m_prev
        dlog = jnp.where(causal, bc - br + g_t[h:h + 1, :], NEG)
        mt = jnp.maximum(a, jnp.max(dlog, axis=1, keepdims=True))
        st.append(dict(bc=bc, li=li, m_prev=m_prev, mt=mt, dw=jnp.exp(dlog - mt),
                       inter=jnp.exp(a - mt)))

    for (b, h), e in zip(heads, st):
        e["s"] = lax.dot_general(q_ref[b, :, hsl(h)], k_ref[b, :, hsl(h)], (((1,), (1,)), ((), ())),
                                 preferred_element_type=F32) * e["dw"]

    for i, ((b, h), e) in enumerate(zip(heads, st)):
        s_hi, s_lo = _split2(e["s"])
        cn = jnp.concatenate([c_ref[b, h], nrep_scr[i]], axis=1).astype(BF16)
        qc = _dot(q_ref[b, :, hsl(h)], cn)
        num = _dot(s_hi, v_ref[b, :, hsl(h)]) + _rep2(e["inter"]) * qc[:, :HEAD_DIM]
        den = _dot(s_hi, ones_l) + _dot(s_lo, ones_l) + e["inter"] * qc[:, HEAD_DIM:]
        rden = 1.0 / jnp.maximum(jnp.abs(den), jnp.exp(-e["mt"]))
        e["hh"] = num * _rep2(rden)

    for (b, h), e in zip(heads, st):
        hh = e["hh"]
        xc = hh - _rep2(_rowsum(hh, ones_d) * inv_d)
        rstd = lax.rsqrt(_rowsum(xc * xc, ones_d) * inv_d + LN_EPS)
        y = _sigmoid(o_ref[b, :, hsl(h)].astype(F32)) * (xc * _rep2(rstd) * wmh_ref[:, hsl(h)])
        y_ref[b, :, hsl(h)] = y.astype(BF16)

    m_rows = []
    for i, ((b, h), e) in enumerate(zip(heads, st)):
        kh = k_ref[b, :, hsl(h)]
        m_new = e["mt"][L - 1:L, :]
        b_last = e["bc"][L - 1:L, :]
        wc = jnp.exp(b_last - e["bc"] + e["li"] - m_new)
        dc = jnp.exp(b_last + e["m_prev"] - m_new)
        vw = (v_ref[b, :, hsl(h)].astype(F32) * _rep2(wc)).astype(BF16)
        c_ref[b, h] = _rep2(dc) * c_ref[b, h] + lax.dot_general(kh, vw, tn, preferred_element_type=F32)
        wc_hi, wc_lo = _split2(wc)
        nrep_scr[i] = (dc * nrep_scr[i] + lax.dot_general(kh, wc_hi, tn, preferred_element_type=F32)
                       + lax.dot_general(kh, wc_lo, tn, preferred_element_type=F32))
        m_rows.append(m_new)

    for b in range(B):
        m_ref[b] = jnp.concatenate(m_rows[b * N_HEADS:(b + 1) * N_HEADS], axis=0)

    @pl.when(step == pl.num_programs(0) - 1)
    def _():
        for b in range(B):
            n_ref[b] = jnp.concatenate(
                [nrep_scr[b * N_HEADS + h].T[0:1, :] for h in range(N_HEADS)], axis=0)


def _mlstm_prompt(q, k, v, o, gates, bg, wmh, *, L, cast=()):
    b, t, dm = q.shape
    assert L == LANES
    act = pl.BlockSpec((b, L, dm), lambda c: (0, c, 0))
    whole = lambda *shape: pl.BlockSpec(shape, lambda c: (0,) * len(shape))
    cast_specs, cast_shapes = _cast_side_job(cast, t // L, lambda c: (c, 0))
    return pl.pallas_call(
        functools.partial(_mlstm_chunk_kernel, B=b, L=L, n_cast=len(cast)),
        grid=(t // L,),
        in_specs=[act, act, act, act,
                  pl.BlockSpec((b, L, GATE_LANES), lambda c: (0, c, 0)),
                  whole(1, GATE_LANES), whole(1, dm)] + cast_specs,
        out_specs=[act,
                   whole(b, N_HEADS, HEAD_DIM, HEAD_DIM),
                   whole(b, N_HEADS, HEAD_DIM),
                   whole(b, N_HEADS, GATE_LANES)] + cast_specs,
        out_shape=[jax.ShapeDtypeStruct((b, t, dm), BF16),
                   jax.ShapeDtypeStruct((b, N_HEADS, HEAD_DIM, HEAD_DIM), F32),
                   jax.ShapeDtypeStruct((b, N_HEADS, HEAD_DIM), F32),
                   jax.ShapeDtypeStruct((b, N_HEADS, GATE_LANES), F32)] + cast_shapes,
        scratch_shapes=[pltpu.VMEM((b * N_HEADS, HEAD_DIM, LANES), F32)],
        compiler_params=_cparams(("arbitrary",)),
        name="mlstm_prompt",
    )(q, k, v, o, gates, bg, wmh, *cast)


def _mlstm_step_kernel(q_ref, k_ref, v_ref, o_ref, g_ref, bg_ref, mrow_ref, nrow_ref,
                       c0_ref, n0_ref, wmh_ref,
                       y_ref, c_ref, n_ref, m_ref, *, bb, T):
    R = bb * T
    per_slab = SLAB // T
    t = lax.rem(lax.broadcasted_iota(jnp.int32, (R, GATE_LANES), 0), T)

    def down(x, d):
        return pltpu.roll(x, d, 0)

    def up(x, d):
        return pltpu.roll(x, x.shape[0] - d, 0)

    def seg_last(x):
        out = x
        for d in range(1, T):
            out = jnp.where(t == T - 1 - d, up(x, d), out)
        return out

    g = g_ref[...] + bg_ref[...]
    li = pltpu.roll(g, N_HEADS, 1)
    lf = _log_sigmoid(g)
    b = lf
    for d in range(1, T):
        b = b + jnp.where(t >= d, down(lf, d), 0.0)
    m_prev = mrow_ref[...]
    a = b + m_prev
    dl = [li] + [jnp.where(t >= d, b - down(b, d) + down(li, d), NEG) for d in range(1, T)]
    mt = a
    for d in range(T):
        mt = jnp.maximum(mt, dl[d])
    dw = [jnp.exp(dl[d] - mt) for d in range(T)]
    inter = jnp.exp(a - mt)
    emt = jnp.exp(-mt)
    m_new = seg_last(mt)
    b_last = seg_last(b)
    wc = jnp.exp(b_last - b + li - m_new)
    dc = jnp.exp(b_last + m_prev - m_new)

    row_s = lax.broadcasted_iota(jnp.int32, (SLAB, HEAD_DIM), 0)
    row_r = lax.broadcasted_iota(jnp.int32, (R, HEAD_DIM), 0)

    for h in range(N_HEADS):
        hs = slice(h * HEAD_DIM, (h + 1) * HEAD_DIM)
        ln = N_HEADS + h

        def col(x):
            return x[:, ln:ln + 1]

        qb = q_ref[:, hs]
        kb = k_ref[:, hs]
        vb = v_ref[:, hs]
        qf = qb.astype(F32)
        kf = kb.astype(F32)
        vf = vb.astype(F32)
        num = jnp.zeros((R, HEAD_DIM), F32)
        den = jnp.zeros((R, 1), F32)
        for d in range(T):
            kd = kf if d == 0 else down(kf, d)
            vd = vf if d == 0 else down(vf, d)
            sw = jnp.sum(qf * kd, axis=1, keepdims=True) * col(dw[d])
            num = num + sw * vd
            den = den + sw

        qc_slabs = []
        for si in range(R // SLAB):
            q16 = qb[si * SLAB:(si + 1) * SLAB]
            acc = jnp.zeros((SLAB, HEAD_DIM), F32)
            for bl in range(per_slab):
                bi = si * per_slab + bl
                r = _dot(q16, c0_ref[bi, h].astype(BF16))
                acc = jnp.where(row_s // T == bl, r, acc)
            qc_slabs.append(acc)
        qc = jnp.concatenate(qc_slabs, axis=0)
        qn = jnp.sum(qf * nrow_ref[:, hs], axis=1, keepdims=True)
        num = num + col(inter) * qc
        den = den + col(inter) * qn
        hh = num / jnp.maximum(jnp.abs(den), col(emt))
        y_ref[:, hs] = _head_out(hh, o_ref[:, hs], wmh_ref[:, hs])

        wk = kf * col(wc)
        vw = vf * col(wc)
        for si in range(R // SLAB):
            k16 = kb[si * SLAB:(si + 1) * SLAB]
            vw16 = vw[si * SLAB:(si + 1) * SLAB]
            for bl in range(per_slab):
                bi = si * per_slab + bl
                last = bi * T + T - 1
                vwb = jnp.where(row_s // T == bl, vw16, 0.0).astype(BF16)
                dcb = dc[last:last + 1, ln:ln + 1]
                c_ref[bi, h] = dcb * c0_ref[bi, h] + lax.dot_general(
                    k16, vwb, (((0,), (0,)), ((), ())), preferred_element_type=F32)
                n_ref[bi, h:h + 1, :] = dcb * n0_ref[bi, h:h + 1, :] + jnp.sum(
                    jnp.where(row_r // T == bi, wk, 0.0), axis=0, keepdims=True)
                m_ref[bi, h:h + 1, :] = jnp.broadcast_to(
                    m_new[last:last + 1, ln:ln + 1], (1, GATE_LANES))


def _mlstm_sample(q, k, v, o, gates, bg, mrow, nrow, c0, n0, wmh, *, bb, T):
    rows, dm = q.shape
    nb = c0.shape[0]
    R = bb * T
    act = pl.BlockSpec((R, dm), lambda i: (i, 0))
    gat = pl.BlockSpec((R, GATE_LANES), lambda i: (i, 0))
    cspec = pl.BlockSpec((bb, N_HEADS, HEAD_DIM, HEAD_DIM), lambda i: (i, 0, 0, 0))
    nspec = pl.BlockSpec((bb, N_HEADS, HEAD_DIM), lambda i: (i, 0, 0))
    return pl.pallas_call(
        functools.partial(_mlstm_step_kernel, bb=bb, T=T),
        grid=(nb // bb,),
        in_specs=[act, act, act, act, gat,
                  pl.BlockSpec((1, GATE_LANES), lambda i: (0, 0)),
                  gat,
                  pl.BlockSpec((R, dm), lambda i: (i, 0)),
                  cspec, nspec,
                  pl.BlockSpec((1, dm), lambda i: (0, 0))],
        out_specs=[act, cspec, nspec,
                   pl.BlockSpec((bb, N_HEADS, GATE_LANES), lambda i: (i, 0, 0))],
        out_shape=[jax.ShapeDtypeStruct((rows, dm), BF16),
                   jax.ShapeDtypeStruct(c0.shape, F32),
                   jax.ShapeDtypeStruct(n0.shape, F32),
                   jax.ShapeDtypeStruct((nb, N_HEADS, GATE_LANES), F32)],
        compiler_params=_cparams(("arbitrary",)),
        name="mlstm_sample",
    )(q, k, v, o, gates, bg, mrow, nrow, c0, n0, wmh)


def _outproj_kernel(yc_ref, ym_ref, w_ref, x_ref, g1_ref, lg_ref, lb_ref, o_ref, *, alpha, splits):
    dc = yc_ref.shape[-1]
    rs = o_ref.shape[0] // splits
    x_rows = x_ref[...].reshape(o_ref.shape) if len(x_ref.shape) == 3 else None
    for i in range(splits):
        r = slice(i * rs, (i + 1) * rs)
        g1 = g1_ref[0] if g1_ref.shape[1] == 1 else g1_ref[0, r, :]
        x = x_ref[r, :] if x_rows is None else x_rows[r]
        mix = _dot(yc_ref[0, r, :], w_ref[0:dc, :]) + _dot(ym_ref[0, r, :], w_ref[dc:, :])
        o_ref[r, :] = _ln(alpha * x + (1.0 + g1) * mix) * lg_ref[...] + lb_ref[...]


def _outproj(yc, ym, w_out, x, mod, ln_g, ln_b, *, tm, tpb, alpha, splits=2):
    d = x.shape[-1]
    rows = x.size // d
    dc = yc.shape[-1]
    dm = ym.shape[-1]
    r = 1 if mod.shape[1] == 1 else tm
    vec = pl.BlockSpec((1, d), lambda m: (0, 0))
    x_spec = (pl.BlockSpec((tm, d), lambda m: (m, 0)) if x.ndim == 2
              else pl.BlockSpec(x.shape, lambda m: (0, 0, 0)))
    return pl.pallas_call(
        functools.partial(_outproj_kernel, alpha=alpha, splits=splits),
        grid=(rows // tm,),
        in_specs=[pl.BlockSpec((1, tm, dc), lambda m: (m // tpb, m % tpb, 0)),
                  pl.BlockSpec((1, tm, dm), lambda m: (m // tpb, m % tpb, 0)),
                  pl.BlockSpec((dc + dm, d), lambda m: (0, 0)),
                  x_spec,
                  pl.BlockSpec((1, r, d), lambda m: (m // tpb, 0, 2)),
                  vec, vec],
        out_specs=pl.BlockSpec((tm, d), lambda m: (m, 0)),
        out_shape=jax.ShapeDtypeStruct((rows, d), F32),
        compiler_params=_cparams(("arbitrary",)),
        name="outproj",
    )(yc, ym, w_out, x, mod, ln_g, ln_b)


def _ffn_kernel(*refs, nf, tpb, period, sample, alpha, splits):
    if sample:
        (x_ref, sh_ref, sc_ref, g2_ref, wa_ref, wg_ref, wconv_ref, wd_ref, lg_ref, lb_ref,
         s_ref, y_ref, t_ref, u_scr, z_scr, y_scr, acc_scr) = refs
    else:
        (x_ref, sh_ref, sc_ref, g2_ref, wa_ref, wg_ref, wconv_ref, wd_ref, lg_ref, lb_ref,
         y_ref, at_ref, u_scr, carry_scr) = refs
    m = pl.program_id(0)
    f = pl.program_id(1)

    if not sample:
        carried = _carried_rows(carry_scr, f, lax.rem(m, tpb) == 0)
        rs = u_scr.shape[0] // splits

        def body(first, last):
            prev = carried
            for i in range(splits):
                r = slice(i * rs, (i + 1) * rs)
                if first:
                    u = (_ln(x_ref[r, :]) * (1.0 + sc_ref[0]) + sh_ref[0]).astype(BF16)
                    u_scr[r, :] = u
                else:
                    u = u_scr[r, :]
                ac, prev = _conv3_rows(_dot(u, wa_ref[...]), wconv_ref[...], prev)
                hcur = (ac * _sigmoid(ac) * _dot(u, wg_ref[...])).astype(BF16)
                acc = _dot(hcur, wd_ref[...])
                if not first:
                    acc = y_ref[r, :] + acc
                if last:
                    acc = (_ln(alpha * x_ref[r, :] + (1.0 + g2_ref[0]) * acc)
                           * lg_ref[...] + lb_ref[...])
                y_ref[r, :] = acc
            carry_scr[f] = prev
            at_ref[0] = prev

        if nf == 1:
            body(True, True)
        else:
            pl.when(f == 0)(lambda: body(True, False))
            if nf > 2:
                pl.when(jnp.logical_and(f > 0, f < nf - 1))(lambda: body(False, False))
            pl.when(f == nf - 1)(lambda: body(False, True))
        return

    @pl.when(f == 0)
    def _():
        u = _ln(x_ref[...]) * (1.0 + sc_ref[0]) + sh_ref[0]
        u_scr[...] = u.astype(BF16)
        acc_scr[...] = jnp.zeros_like(acc_scr)

    u = u_scr[...]
    a = _dot(u, wa_ref[...])
    ac = _conv3_sequences(a, wconv_ref[...], s_ref, z_scr, y_scr, t_ref, period)
    hcur = (ac * _sigmoid(ac) * _dot(u, wg_ref[...])).astype(BF16)
    acc_scr[...] += _dot(hcur, wd_ref[...])

    @pl.when(f == nf - 1)
    def _():
        y = _ln(alpha * x_ref[...] + (1.0 + g2_ref[0]) * acc_scr[...]) * lg_ref[...] + lb_ref[...]
        y_ref[...] = y.reshape(y_ref.shape)


def _ffn(x, mod, w_up, w_conv, w_down, ln_g, ln_b, *, tm, tpb, tf, sample, alpha, period=0,
         splits=1, state=None):
    rows, d = x.shape
    ff = w_down.shape[0]
    nf = ff // tf
    nm = rows // tm
    r = tm if sample else 1
    vec = pl.BlockSpec((1, d), lambda m, f: (0, 0))
    in_specs = [
        pl.BlockSpec((tm, d), lambda m, f: (m, 0), pipeline_mode=pl.Buffered(1)),
        pl.BlockSpec((1, r, d), lambda m, f: (m // tpb, 0, 3)),
        pl.BlockSpec((1, r, d), lambda m, f: (m // tpb, 0, 4)),
        pl.BlockSpec((1, r, d), lambda m, f: (m // tpb, 0, 5)),
        pl.BlockSpec((d, tf), lambda m, f: (0, f)),
        pl.BlockSpec((d, tf), lambda m, f: (0, nf + f)),
        pl.BlockSpec((CONV_K, tf), lambda m, f: (0, f)),
        pl.BlockSpec((tf, d), lambda m, f: (f, 0)),
        vec, vec,
    ]
    args = [x, mod, mod, mod, w_up, w_up, w_conv, w_down, ln_g, ln_b]
    scratch = [pltpu.VMEM((tm, d), BF16)]
    out_specs = [pl.BlockSpec((tm, d), lambda m, f: (m, 0))]
    out_shape = [jax.ShapeDtypeStruct((rows, d), F32)]
    if sample:
        assert nm == 1
        st = pl.BlockSpec((tm // period, CONV_K - 1, tf), lambda m, f: (0, 0, f))
        in_specs.append(st)
        args.append(state)
        out_specs.append(st)
        out_shape.append(jax.ShapeDtypeStruct((tm // period, CONV_K - 1, ff), F32))
        scratch += [pltpu.VMEM((tf // LANES, tm, LANES), F32)] * 2 + [pltpu.VMEM((tm, d), F32)]
        out_specs[0] = pl.BlockSpec((tm // period, period, d), lambda m, f: (0, 0, 0))
        out_shape[0] = jax.ShapeDtypeStruct((tm // period, period, d), F32)
    else:
        out_specs.append(pl.BlockSpec((1, STATE_ROWS, tf), lambda m, f: (m, 0, f)))
        out_shape.append(jax.ShapeDtypeStruct((nm, STATE_ROWS, ff), F32))
        scratch.append(pltpu.VMEM((nf, STATE_ROWS, tf), F32))
    return pl.pallas_call(
        functools.partial(_ffn_kernel, nf=nf, tpb=tpb, period=period, sample=sample, alpha=alpha,
                          splits=splits),
        grid=(nm, nf),
        in_specs=in_specs,
        out_specs=out_specs,
        out_shape=out_shape,
        scratch_shapes=scratch,
        compiler_params=_cparams(("arbitrary", "arbitrary")),
        name="ffn_sample" if sample else "ffn_prompt",
    )(*args)


def _layer_prompt(x, mod, wts, yc, ztail, mlstm_out, *, alpha):
    B, T, D = x.shape
    (_, _, _, _, _, w_out, ln1_g, ln1_b, w_up, w_fconv, w_down, ln2_g, ln2_b) = wts
    tm = PROMPT_TM
    tpb = T // tm
    x2 = x.reshape(B * T, D)
    ym, C, n, m = mlstm_out
    x1 = _outproj(yc, ym, w_out, x2, mod, ln1_g, ln1_b, tm=OUTPROJ_TM, tpb=T // OUTPROJ_TM,
                  alpha=alpha)
    y, atail = _ffn(x1, mod, w_up, w_fconv, w_down, ln2_g, ln2_b,
                    tm=tm, tpb=tpb, tf=FFN_TF, sample=False, alpha=alpha, splits=2)
    return (y.reshape(B, T, D), ztail[tpb - 1::tpb, STATE_ROWS - 2:], C, n, m[..., 0],
            atail[tpb - 1::tpb, STATE_ROWS - 2:])


def _layer_sample(x, mod, conv_buf, C0, n0, m0, ffn_buf, wts, *, alpha):
    B, T, D = x.shape
    (w_in, w_gate, bg, w_conv, wmh, w_out, ln1_g, ln1_b, w_up, w_fconv, w_down, ln2_g, ln2_b) = wts
    rows = B * T
    yc, q, k, v, o, gates, conv_new = _inproj(x, mod, w_in, w_gate, w_conv, tm=rows, tpb=1,
                                              sample=True, period=T, state=conv_buf)
    mrow = jnp.pad(jnp.repeat(m0, T, axis=0), ((0, 0), (N_HEADS, GATE_LANES - 2 * N_HEADS)))
    nrow = jnp.repeat(n0.reshape(B, N_HEADS * HEAD_DIM), T, axis=0)
    ym, C, n, m = _mlstm_sample(q[0], k[0], v[0], o[0], gates[0], bg, mrow, nrow, C0, n0, wmh,
                                bb=SAMPLE_SEQS, T=T)
    x1 = _outproj(yc, ym[None], w_out, x, mod, ln1_g, ln1_b, tm=rows, tpb=1, alpha=alpha)
    y, ffn_new = _ffn(x1, mod, w_up, w_fconv, w_down, ln2_g, ln2_b, tm=rows, tpb=1, tf=FFN_TF,
                      sample=True, alpha=alpha, period=T, state=ffn_buf)
    return y, conv_new, C, n, m[..., 0], ffn_new


def kernel(x_prompt, x_sample, c_prompt, c_sample, state_conv, state_mlstm_C, state_mlstm_n,
           state_mlstm_m, state_ffn_conv, w_ada, b_ada, w_in, b_gate, w_conv, w_mh_norm, w_out,
           ln1_g, ln1_b, w_up, w_ffn_conv, w_down, ln2_g, ln2_b):
    depth = w_in.shape[0]
    alpha = (2 * depth) ** 0.25
    Bp = x_prompt.shape[0]
    Bs, Ts, D = x_sample.shape
    dc = w_conv.shape[-1]
    dm = w_mh_norm.shape[-1]
    n_main = 3 * dc + 4 * dm
    assert dc == dm == N_HEADS * HEAD_DIM and Ts >= CONV_K - 1 and SLAB % Ts == 0

    xp, xs = x_prompt, x_sample
    outs_p = [[] for _ in range(5)]
    outs_s = [[] for _ in range(5)]
    for l in range(depth):
        c_all = jnp.concatenate([jnp.repeat(c_sample, Ts, axis=0), c_prompt], axis=0)
        w_in_t = jnp.swapaxes(w_in[l], 0, 1)
        mod, w_in_b = _ada(c_all, w_ada[l], b_ada[l], w_in_t, n_main)
        mod_s = mod.reshape(1, Bs * Ts + Bp, 6 * D)
        mod_p = mod[Bs * Ts:].reshape(Bp, 1, 6 * D)
        w_gate = jnp.pad(w_in_t[n_main:], ((0, GATE_LANES - 2 * N_HEADS), (0, 0))).astype(BF16)
        bg = jnp.pad(b_gate[l], (0, GATE_LANES - 2 * N_HEADS)).reshape(1, GATE_LANES)
        wmh = w_mh_norm[l].reshape(1, dm)
        Tp = xp.shape[1]
        yc, q, k, v, o, gates, ztail, w_up_b = _inproj(
            xp.reshape(Bp * Tp, D), mod_p, w_in_b, w_gate, w_conv[l], tm=PROMPT_TM,
            tpb=Tp // PROMPT_TM, sample=False, cast=(w_up[l],))
        *mlstm_out, w_down_b, w_out_b = _mlstm_prompt(q, k, v, o, gates, bg, wmh, L=MLSTM_CHUNK,
                                                      cast=(w_down[l], w_out[l]))
        wts = (
            w_in_b,
            w_gate,
            bg,
            w_conv[l],
            wmh,
            w_out_b,
            ln1_g[l].reshape(1, D), ln1_b[l].reshape(1, D),
            w_up_b,
            w_ffn_conv[l],
            w_down_b,
            ln2_g[l].reshape(1, D), ln2_b[l].reshape(1, D),
        )
        xs, *st_s = _layer_sample(xs, mod_s, state_conv[l], state_mlstm_C[l], state_mlstm_n[l],
                                  state_mlstm_m[l], state_ffn_conv[l], wts, alpha=alpha)
        xp, *st_p = _layer_prompt(xp, mod_p, wts, yc, ztail, mlstm_out, alpha=alpha)
        for acc, val in zip(outs_p, st_p):
            acc.append(val)
        for acc, val in zip(outs_s, st_s):
            acc.append(val)
    return (xp.astype(x_prompt.dtype), xs.astype(x_sample.dtype),
            *[jnp.stack(a) for a in outs_p], *[jnp.stack(a) for a in outs_s])
```

```python
import functools

import jax
import jax.numpy as jnp
from jax import lax
from jax.experimental import pallas as pl
from jax.experimental.pallas import tpu as pltpu

F32 = jnp.float32
BF16 = jnp.bfloat16

N_HEADS = 4
HEAD_DIM = 256
CONV_K = 3
LN_EPS = 1e-5
NEG = -1e30
LANES = 128
GATE_LANES = LANES
STATE_ROWS = 8
SLAB = 16
VMEM_LIMIT = 56 * 1024 * 1024

PROMPT_TM = 1024
OUTPROJ_TM = 512
FFN_TF = 512
MLSTM_CHUNK = LANES
SAMPLE_SEQS = 8


def _cparams(sem):
    return pltpu.CompilerParams(dimension_semantics=sem, vmem_limit_bytes=VMEM_LIMIT)


def _ln(x):
    mu = jnp.mean(x, axis=-1, keepdims=True)
    xc = x - mu
    var = jnp.mean(xc * xc, axis=-1, keepdims=True)
    return xc * lax.rsqrt(var + LN_EPS)


def _log_sigmoid(x):
    return jnp.minimum(x, 0.0) - jnp.log1p(jnp.exp(-jnp.abs(x)))


def _sigmoid(x):
    return 1.0 / (1.0 + jnp.exp(-x))


def _dot(a, b):
    return jnp.dot(a, b, preferred_element_type=F32)


def _dot_nt(a, b):
    return lax.dot_general(a, b, (((1,), (1,)), ((), ())), preferred_element_type=F32)


def _conv3_rows(z, w, prev):
    p0 = prev[STATE_ROWS - 2:STATE_ROWS - 1]
    p1 = prev[STATE_ROWS - 1:STATE_ROWS]
    t = lax.broadcasted_iota(jnp.int32, z.shape, 0)
    z1 = jnp.where(t >= 1, pltpu.roll(z, 1, 0), p1)
    z2 = jnp.where(t >= 2, pltpu.roll(z, 2, 0), jnp.where(t == 0, p0, p1))
    return w[0:1] * z2 + w[1:2] * z1 + w[2:3] * z, z[z.shape[0] - STATE_ROWS:]


def _carried_rows(carry_ref, idx, first):
    @pl.when(first)
    def _():
        carry_ref[idx] = jnp.zeros(carry_ref.shape[1:], F32)

    return carry_ref[idx]


def _conv3_sequences(z, w, s_ref, z_scr, y_scr, t_ref, T):
    nseq = z.shape[0] // T
    y = w[0:1] * pltpu.roll(z, 2, 0) + w[1:2] * pltpu.roll(z, 1, 0) + w[2:3] * z

    def rows(t):
        return pl.ds(t, nseq, stride=T)

    cols = []
    for c in range(z.shape[1] // LANES):
        cs = slice(c * LANES, (c + 1) * LANES)
        w0, w1, w2 = w[0:1, cs], w[1:2, cs], w[2:3, cs]
        z_scr[c] = z[:, cs]
        y_scr[c] = y[:, cs]
        s0 = s_ref[:, 0, cs]
        s1 = s_ref[:, 1, cs]
        z0 = z_scr[c, rows(0), :]
        z1 = z_scr[c, rows(1), :]
        y_scr[c, rows(0), :] = w0 * s0 + w1 * s1 + w2 * z0
        y_scr[c, rows(1), :] = w0 * s1 + w1 * z0 + w2 * z1
        t_ref[:, 0, cs] = z_scr[c, rows(T - 2), :]
        t_ref[:, 1, cs] = z_scr[c, rows(T - 1), :]
        cols.append(y_scr[c])
    return jnp.concatenate(cols, axis=1)


def _ada_kernel(c_ref, w_ref, b_ref, wt_ref, o_ref, wb_ref, s_scr, *, n_seq, reps):
    @pl.when(pl.program_id(0) == 0)
    def _():
        c = c_ref[...]
        s = (c * _sigmoid(c)).astype(BF16)
        shape = (s_scr.shape[0], c.shape[0])
        r = lax.broadcasted_iota(jnp.int32, shape, 0)
        src = jnp.where(r < n_seq * reps, r // reps, r - n_seq * (reps - 1))
        sel = jnp.where(lax.broadcasted_iota(jnp.int32, shape, 1) == src, 1.0, 0.0).astype(BF16)
        s_scr[...] = _dot(sel, s).astype(BF16)

    o_ref[...] = _dot(s_scr[...], w_ref[...].astype(BF16)) + b_ref[...]
    wb_ref[...] = wt_ref[...].T.astype(BF16)


def _ada(c, w, b, wt, n_t, *, n_seq, reps, rows, tn=768, tt=512):
    u, d = c.shape
    r = rows
    n = w.shape[1]
    k = wt.shape[1]
    steps, t_blocks = n // tn, n_t // tt
    assert steps * tn == n and t_blocks * tt == n_t and t_blocks <= steps

    def t_block(j):
        return jnp.minimum(j, t_blocks - 1)

    return pl.pallas_call(
        functools.partial(_ada_kernel, n_seq=n_seq, reps=reps),
        grid=(steps,),
        in_specs=[
            pl.BlockSpec((u, d), lambda j: (0, 0)),
            pl.BlockSpec((d, tn), lambda j: (0, j)),
            pl.BlockSpec((1, tn), lambda j: (0, j)),
            pl.BlockSpec((tt, k), lambda j: (t_block(j), 0)),
        ],
        out_specs=[pl.BlockSpec((r, tn), lambda j: (0, j)),
                   pl.BlockSpec((k, tt), lambda j: (0, t_block(j)))],
        out_shape=[jax.ShapeDtypeStruct((r, n), F32), jax.ShapeDtypeStruct((k, n_t), BF16)],
        scratch_shapes=[pltpu.VMEM((r, d), BF16)],
        compiler_params=_cparams(("arbitrary",)),
        name="ada",
    )(c, w, b.reshape(1, n), wt)


def _cast_side_job(mats, n_steps, index_map):
    specs, shapes = [], []
    for w in mats:
        slab = w.shape[0] // n_steps
        assert slab * n_steps == w.shape[0] and slab % SLAB == 0
        specs.append(pl.BlockSpec((slab, w.shape[1]), index_map))
        shapes.append(jax.ShapeDtypeStruct(w.shape, BF16))
    return specs, shapes


def _cast_slabs(srcs, dsts):
    for src, dst in zip(srcs, dsts):
        dst[...] = src[...].astype(BF16)


def _inproj_kernel(*refs, tpb, period, sample, n_cast=0):
    if sample:
        (x_ref, sh_ref, sc_ref, wb_ref, wc_ref, wh_ref, wq_ref, wk_ref, wv_ref, wo_ref, wg_ref,
         wconv_ref, s_ref,
         yc_ref, q_ref, k_ref, v_ref, o_ref, g_ref, t_ref, u_scr, z_scr, y_scr) = refs
    else:
        (x_ref, sh_ref, sc_ref, wb_ref, wc_ref, wh_ref, wq_ref, wk_ref, wv_ref, wo_ref, wg_ref,
         wconv_ref) = refs[:12]
        cast_in = refs[12:12 + n_cast]
        (yc_ref, q_ref, k_ref, v_ref, o_ref, g_ref, zt_ref) = refs[12 + n_cast:19 + n_cast]
        cast_out = refs[19 + n_cast:19 + 2 * n_cast]
        u_scr, carry_scr = refs[19 + 2 * n_cast:]
    m = pl.program_id(0)
    j = pl.program_id(1)

    if not sample:
        _cast_slabs(cast_in, cast_out)

        carried = _carried_rows(carry_scr, j, lax.rem(m, tpb) == 0)
        rs = u_scr.shape[0] // 2

        def body(first):
            prev = carried
            for i in range(2):
                r = slice(i * rs, (i + 1) * rs)
                if first:
                    u = (_ln(x_ref[r, :]) * (1.0 + sc_ref[0]) + sh_ref[0]).astype(BF16)
                    u_scr[r, :] = u
                    g_ref[0, r, :] = _dot_nt(u, wg_ref[...])
                else:
                    u = u_scr[r, :]
                z = _dot(u, wc_ref[...]) * _dot(u, wh_ref[...])
                yc, prev = _conv3_rows(z, wconv_ref[...], prev)
                yc_ref[0, r, :] = (_dot(u, wb_ref[...]) * yc).astype(BF16)
                q_ref[0, r, :] = _dot(u, wq_ref[...]).astype(BF16)
                k_ref[0, r, :] = (_dot(u, wk_ref[...]) * (HEAD_DIM ** -0.5)).astype(BF16)
                v_ref[0, r, :] = _dot(u, wv_ref[...]).astype(BF16)
                o_ref[0, r, :] = _dot(u, wo_ref[...]).astype(BF16)
            carry_scr[j] = prev
            zt_ref[0] = prev

        pl.when(j == 0)(lambda: body(True))
        pl.when(j > 0)(lambda: body(False))
        return

    @pl.when(j == 0)
    def _():
        x = x_ref[...].reshape(u_scr.shape)
        u = _ln(x) * (1.0 + sc_ref[0]) + sh_ref[0]
        ub = u.astype(BF16)
        u_scr[...] = ub
        g_ref[0] = _dot_nt(ub, wg_ref[...])

    u = u_scr[...]
    z = _dot(u, wc_ref[...]) * _dot(u, wh_ref[...])
    yc = _conv3_sequences(z, wconv_ref[...], s_ref, z_scr, y_scr, t_ref, period)
    half = u.shape[0] // 2
    bg = jnp.concatenate([_dot(u[:half], wb_ref[...]), _dot(u[half:], wb_ref[...])], axis=0)
    yc_ref[0] = (bg * yc).astype(BF16)
    q_ref[0] = _dot(u, wq_ref[...]).astype(BF16)
    k_ref[0] = (_dot(u, wk_ref[...]) * (HEAD_DIM ** -0.5)).astype(BF16)
    v_ref[0] = _dot(u, wv_ref[...]).astype(BF16)
    o_ref[0] = _dot(u, wo_ref[...]).astype(BF16)


def _inproj(x, mod, w_in, w_gate, w_conv, *, tm, tpb, sample, period=0, state=None, cast=()):
    d = x.shape[-1]
    rows = x.size // d
    dc = w_conv.shape[1]
    tn = HEAD_DIM
    nj = dc // tn
    nm = rows // tm
    nseq = nm // tpb
    r = tm if sample else 1

    def wspec(off):
        return pl.BlockSpec((d, tn), lambda m, j, off=off: (0, off * nj + j))

    in_specs = [
        (pl.BlockSpec((tm // period, period, d), lambda m, j: (0, 0, 0)) if sample
         else pl.BlockSpec((tm, d), lambda m, j: (m, 0))),
        pl.BlockSpec((1, r, d), lambda m, j: (m // tpb, 0, 0)),
        pl.BlockSpec((1, r, d), lambda m, j: (m // tpb, 0, 1)),
        wspec(0), wspec(1), wspec(2), wspec(3), wspec(4), wspec(5), wspec(6),
        pl.BlockSpec((GATE_LANES, d), lambda m, j: (0, 0)),
        pl.BlockSpec((CONV_K, tn), lambda m, j: (0, j)),
    ]
    args = [x, mod, mod, w_in, w_in, w_in, w_in, w_in, w_in, w_in, w_gate, w_conv]
    scratch = [pltpu.VMEM((tm, d), BF16)]
    act = pl.BlockSpec((1, tm, tn), lambda m, j: (m // tpb, m % tpb, j))
    out_specs = [act, act, act, act, act,
                 pl.BlockSpec((1, tm, GATE_LANES), lambda m, j: (m // tpb, m % tpb, 0))]
    out_shape = [jax.ShapeDtypeStruct((nseq, tpb * tm, dc), BF16)] * 5 + [
        jax.ShapeDtypeStruct((nseq, tpb * tm, GATE_LANES), F32)]
    if sample:
        assert nm == 1
        st = pl.BlockSpec((tm // period, CONV_K - 1, tn), lambda m, j: (0, 0, j))
        in_specs.append(st)
        args.append(state)
        out_specs.append(st)
        out_shape.append(jax.ShapeDtypeStruct((tm // period, CONV_K - 1, dc), F32))
        scratch += [pltpu.VMEM((tn // LANES, tm, LANES), F32)] * 2
    else:
        out_specs.append(pl.BlockSpec((1, STATE_ROWS, tn), lambda m, j: (m, 0, j)))
        out_shape.append(jax.ShapeDtypeStruct((nm, STATE_ROWS, dc), F32))
        scratch.append(pltpu.VMEM((nj, STATE_ROWS, tn), F32))
        cast_specs, cast_shapes = _cast_side_job(cast, nm * nj, lambda m, j: (m * nj + j, 0))
        in_specs += cast_specs
        args += list(cast)
        out_specs += cast_specs
        out_shape += cast_shapes
    return pl.pallas_call(
        functools.partial(_inproj_kernel, tpb=tpb, period=period, sample=sample, n_cast=len(cast)),
        grid=(nm, nj),
        in_specs=in_specs,
        out_specs=out_specs,
        out_shape=out_shape,
        scratch_shapes=scratch,
        compiler_params=_cparams(("arbitrary", "arbitrary")),
        name="inproj_sample" if sample else "inproj_prompt",
    )(*args)


def _split3(x):
    hi = x.astype(BF16)
    r1 = x - hi.astype(F32)
    mid = r1.astype(BF16)
    lo = (r1 - mid.astype(F32)).astype(BF16)
    return hi, mid, lo


def _head_out(hh, o, wmh):
    return (_sigmoid(o.astype(F32)) * (_ln(hh) * wmh)).astype(BF16)


def _split2(x):
    hi = x.astype(BF16)
    return hi, (x - hi.astype(F32)).astype(BF16)


def _rowsum(x, ones):
    hi, lo = _split2(x)
    return _dot(hi, ones) + _dot(lo, ones)


def _rep2(x):
    return jnp.concatenate([x, x], axis=1)


def _mlstm_chunk_kernel(*refs, B, L, n_cast):
    (q_ref, k_ref, v_ref, o_ref, g_ref, bg_ref, wmh_ref) = refs[:7]
    (y_ref, c_ref, n_ref, m_ref) = refs[7 + n_cast:11 + n_cast]
    nrep_scr = refs[-1]
    _cast_slabs(refs[7:7 + n_cast], refs[11 + n_cast:11 + 2 * n_cast])
    _mlstm_chunk(q_ref, k_ref, v_ref, o_ref, g_ref, bg_ref, wmh_ref, y_ref, c_ref, n_ref, m_ref,
                 nrep_scr, B=B, L=L)


def _mlstm_chunk(q_ref, k_ref, v_ref, o_ref, g_ref, bg_ref, wmh_ref,
                 y_ref, c_ref, n_ref, m_ref, nrep_scr, *, B, L):
    step = pl.program_id(0)

    @pl.when(step == 0)
    def _():
        c_ref[...] = jnp.zeros_like(c_ref)
        m_ref[...] = jnp.zeros_like(m_ref)
        nrep_scr[...] = jnp.zeros_like(nrep_scr)

    row = lax.broadcasted_iota(jnp.int32, (L, L), 0)
    col = lax.broadcasted_iota(jnp.int32, (L, L), 1)
    causal = col <= row
    tril = jnp.where(causal, 1.0, 0.0).astype(BF16)
    ones_l = jnp.ones((L, LANES), BF16)
    ones_d = jnp.ones((HEAD_DIM, LANES), BF16)
    inv_d = 1.0 / HEAD_DIM
    tn = (((0,), (0,)), ((), ()))

    heads =[(b, h) for b in range(B) for h in range(N_HEADS)]

    def hsl(h):
        return slice(h * HEAD_DIM, (h + 1) * HEAD_DIM)

    gate = []
    for b in range(B):
        g = g_ref[b] + bg_ref[...]
        hi, mid, lo = _split3(_log_sigmoid(g))
        bcum = _dot(tril, hi) + _dot(tril, mid) + _dot(tril, lo)
        gate.append((g, g.T, bcum, bcum.T, m_ref[b]))

    st = []
    for b, h in heads:
        g, g_t, bcum, bcum_t, m_all = gate[b]
        bc = jnp.broadcast_to(bcum[:, N_HEADS + h:N_HEADS + h + 1], (L, LANES))
        li = jnp.broadcast_to(g[:, h:h + 1], (L, LANES))
        br = bcum_t[N_HEADS + h:N_HEADS + h + 1, :]
        m_prev = m_all[h:h + 1, :]
        a = bc + m_prev
        dlog = jnp.where(causal, bc - br + g_t[h:h + 1, :], NEG)
        mt = jnp.maximum(a, jnp.max(dlog, axis=1, keepdims=True))
        st.append(dict(bc=bc, li=li, m_prev=m_prev, mt=mt, dw=jnp.exp(dlog - mt),
                       inter=jnp.exp(a - mt)))

    for (b, h), e in zip(heads, st):
        e["s"] = lax.dot_general(q_ref[b, :, hsl(h)], k_ref[b, :, hsl(h)], (((1,), (1,)), ((), ())),
                                 preferred_element_type=F32) * e["dw"]

    for i, ((b, h), e) in enumerate(zip(heads, st)):
        s_hi, s_lo = _split2(e["s"])
        cn = jnp.concatenate([c_ref[b, h], nrep_scr[i]], axis=1).astype(BF16)
        qc = _dot(q_ref[b, :, hsl(h)], cn)
        num = _dot(s_hi, v_ref[b, :, hsl(h)]) + _rep2(e["inter"]) * qc[:, :HEAD_DIM]
        den = _dot(s_hi, ones_l) + _dot(s_lo, ones_l) + e["inter"] * qc[:, HEAD_DIM:]
        rden = 1.0 / jnp.maximum(jnp.abs(den), jnp.exp(-e["mt"]))
        e["hh"] = num * _rep2(rden)

    for (b, h), e in zip(heads, st):
        hh = e["hh"]
        xc = hh - _rep2(_rowsum(hh, ones_d) * inv_d)
        rstd = lax.rsqrt(_rowsum(xc * xc, ones_d) * inv_d + LN_EPS)
        y = _sigmoid(o_ref[b, :, hsl(h)].astype(F32)) * (xc * _rep2(rstd) * wmh_ref[:, hsl(h)])
        y_ref[b, :, hsl(h)] = y.astype(BF16)

    m_rows = []
    for i, ((b, h), e) in enumerate(zip(heads, st)):
        kh = k_ref[b, :, hsl(h)]
        m_new = e["mt"][L - 1:L, :]
        b_last = e["bc"][L - 1:L, :]
        wc = jnp.exp(b_last - e["bc"] + e["li"] - m_new)
        dc = jnp.exp(b_last + e["m_prev"] - m_new)
        vw = (v_ref[b, :, hsl(h)].astype(F32) * _rep2(wc)).astype(BF16)
        c_ref[b, h] = _rep2(dc) * c_ref[b, h] + lax.dot_general(kh, vw, tn, preferred_element_type=F32)
        wc_hi, wc_lo = _split2(wc)
        nrep_scr[i] = (dc * nrep_scr[i] + lax.dot_general(kh, wc_hi, tn, preferred_element_type=F32)
                       + lax.dot_general(kh, wc_lo, tn, preferred_element_type=F32))
        m_rows.append(m_new)

    for b in range(B):
        m_ref[b] = jnp.concatenate(m_rows[b * N_HEADS:(b + 1) * N_HEADS], axis=0)

    @pl.when(step == pl.num_programs(0) - 1)
    def _():
        for b in range(B):
            n_ref[b] = jnp.concatenate(
                [nrep_scr[b * N_HEADS + h].T[0:1, :] for h in range(N_HEADS)], axis=0)


def _mlstm_prompt(q, k, v, o, gates, bg, wmh, *, L, cast=()):
    b, t, dm = q.shape
    assert L == LANES
    act = pl.BlockSpec((b, L, dm), lambda c: (0, c, 0))
    whole = lambda *shape: pl.BlockSpec(shape, lambda c: (0,) * len(shape))
    cast_specs, cast_shapes = _cast_side_job(cast, t // L, lambda c: (c, 0))
    return pl.pallas_call(
        functools.partial(_mlstm_chunk_kernel, B=b, L=L, n_cast=len(cast)),
        grid=(t // L,),
        in_specs=[act, act, act, act,
                  pl.BlockSpec((b, L, GATE_LANES), lambda c: (0, c, 0)),
                  whole(1, GATE_LANES), whole(1, dm)] + cast_specs,
        out_specs=[act,
                   whole(b, N_HEADS, HEAD_DIM, HEAD_DIM),
                   whole(b, N_HEADS, HEAD_DIM),
                   whole(b, N_HEADS, GATE_LANES)] + cast_specs,
        out_shape=[jax.ShapeDtypeStruct((b, t, dm), BF16),
                   jax.ShapeDtypeStruct((b, N_HEADS, HEAD_DIM, HEAD_DIM), F32),
                   jax.ShapeDtypeStruct((b, N_HEADS, HEAD_DIM), F32),
                   jax.ShapeDtypeStruct((b, N_HEADS, GATE_LANES), F32)] + cast_shapes,
        scratch_shapes=[pltpu.VMEM((b * N_HEADS, HEAD_DIM, LANES), F32)],
        compiler_params=_cparams(("arbitrary",)),
        name="mlstm_prompt",
    )(q, k, v, o, gates, bg, wmh, *cast)


def _mlstm_step_kernel(q_ref, k_ref, v_ref, o_ref, g_ref, bg_ref, mrow_ref, nrow_ref,
                       c0_ref, n0_ref, wmh_ref,
                       y_ref, c_ref, n_ref, m_ref, *, bb, T):
    R = bb * T
    per_slab = SLAB // T
    t = lax.rem(lax.broadcasted_iota(jnp.int32, (R, GATE_LANES), 0), T)

    def down(x, d):
        return pltpu.roll(x, d, 0)

    def up(x, d):
        return pltpu.roll(x, x.shape[0] - d, 0)

    def seg_last(x):
        out = x
        for d in range(1, T):
            out = jnp.where(t == T - 1 - d, up(x, d), out)
        return out

    g = g_ref[...] + bg_ref[...]
    li = pltpu.roll(g, N_HEADS, 1)
    lf = _log_sigmoid(g)
    b = lf
    for d in range(1, T):
        b = b + jnp.where(t >= d, down(lf, d), 0.0)
    m_prev = mrow_ref[...]
    a = b + m_prev
    dl = [li] + [jnp.where(t >= d, b - down(b, d) + down(li, d), NEG) for d in range(1, T)]
    mt = a
    for d in range(T):
        mt = jnp.maximum(mt, dl[d])
    dw = [jnp.exp(dl[d] - mt) for d in range(T)]
    inter = jnp.exp(a - mt)
    emt = jnp.exp(-mt)
    m_new = seg_last(mt)
    b_last = seg_last(b)
    wc = jnp.exp(b_last - b + li - m_new)
    dc = jnp.exp(b_last + m_prev - m_new)

    row_s = lax.broadcasted_iota(jnp.int32, (SLAB, HEAD_DIM), 0)
    row_r = lax.broadcasted_iota(jnp.int32, (R, HEAD_DIM), 0)

    for h in range(N_HEADS):
        hs = slice(h * HEAD_DIM, (h + 1) * HEAD_DIM)
        ln = N_HEADS + h

        def col(x):
            return x[:, ln:ln + 1]

        qb = q_ref[:, hs]
        kb = k_ref[:, hs]
        vb = v_ref[:, hs]
        qf = qb.astype(F32)
        kf = kb.astype(F32)
        vf = vb.astype(F32)
        num = jnp.zeros((R, HEAD_DIM), F32)
        den = jnp.zeros((R, 1), F32)
        for d in range(T):
            kd = kf if d == 0 else down(kf, d)
            vd = vf if d == 0 else down(vf, d)
            sw = jnp.sum(qf * kd, axis=1, keepdims=True) * col(dw[d])
            num = num + sw * vd
            den = den + sw

        qc_slabs = []
        for si in range(R // SLAB):
            q16 = qb[si * SLAB:(si + 1) * SLAB]
            acc = jnp.zeros((SLAB, HEAD_DIM), F32)
            for bl in range(per_slab):
                bi = si * per_slab + bl
                r = _dot(q16, c0_ref[bi, h].astype(BF16))
                acc = jnp.where(row_s // T == bl, r, acc)
            qc_slabs.append(acc)
        qc = jnp.concatenate(qc_slabs, axis=0)
        qn = jnp.sum(qf * nrow_ref[:, hs], axis=1, keepdims=True)
        num = num + col(inter) * qc
        den = den + col(inter) * qn
        hh = num / jnp.maximum(jnp.abs(den), col(emt))
        y_ref[:, hs] = _head_out(hh, o_ref[:, hs], wmh_ref[:, hs])

        wk = kf * col(wc)
        vw = vf * col(wc)
        for si in range(R // SLAB):
            k16 = kb[si * SLAB:(si + 1) * SLAB]
            vw16 = vw[si * SLAB:(si + 1) * SLAB]
            for bl in range(per_slab):
                bi = si * per_slab + bl
                last = bi * T + T - 1
                vwb = jnp.where(row_s // T == bl, vw16, 0.0).astype(BF16)
                dcb = dc[last:last + 1, ln:ln + 1]
                c_ref[bi, h] = dcb * c0_ref[bi, h] + lax.dot_general(
                    k16, vwb, (((0,), (0,)), ((), ())), preferred_element_type=F32)
                n_ref[bi, h:h + 1, :] = dcb * n0_ref[bi, h:h + 1, :] + jnp.sum(
                    jnp.where(row_r // T == bi, wk, 0.0), axis=0, keepdims=True)
                m_ref[bi, h:h + 1, :] = jnp.broadcast_to(
                    m_new[last:last + 1, ln:ln + 1], (1, GATE_LANES))


def _mlstm_sample(q, k, v, o, gates, bg, mrow, nrow, c0, n0, wmh, *, bb, T):
    rows, dm = q.shape
    nb = c0.shape[0]
    R = bb * T
    act = pl.BlockSpec((R, dm), lambda i: (i, 0))
    gat = pl.BlockSpec((R, GATE_LANES), lambda i: (i, 0))
    cspec = pl.BlockSpec((bb, N_HEADS, HEAD_DIM, HEAD_DIM), lambda i: (i, 0, 0, 0))
    nspec = pl.BlockSpec((bb, N_HEADS, HEAD_DIM), lambda i: (i, 0, 0))
    return pl.pallas_call(
        functools.partial(_mlstm_step_kernel, bb=bb, T=T),
        grid=(nb // bb,),
        in_specs=[act, act, act, act, gat,
                  pl.BlockSpec((1, GATE_LANES), lambda i: (0, 0)),
                  gat,
                  pl.BlockSpec((R, dm), lambda i: (i, 0)),
                  cspec, nspec,
                  pl.BlockSpec((1, dm), lambda i: (0, 0))],
        out_specs=[act, cspec, nspec,
                   pl.BlockSpec((bb, N_HEADS, GATE_LANES), lambda i: (i, 0, 0))],
        out_shape=[jax.ShapeDtypeStruct((rows, dm), BF16),
                   jax.ShapeDtypeStruct(c0.shape, F32),
                   jax.ShapeDtypeStruct(n0.shape, F32),
                   jax.ShapeDtypeStruct((nb, N_HEADS, GATE_LANES), F32)],
        compiler_params=_cparams(("arbitrary",)),
        name="mlstm_sample",
    )(q, k, v, o, gates, bg, mrow, nrow, c0, n0, wmh)


def _outproj_kernel(yc_ref, ym_ref, w_ref, x_ref, g1_ref, lg_ref, lb_ref, o_ref, *, alpha, splits):
    dc = yc_ref.shape[-1]
    rs = o_ref.shape[0] // splits
    x_rows = x_ref[...].reshape(o_ref.shape) if len(x_ref.shape) == 3 else None
    for i in range(splits):
        r = slice(i * rs, (i + 1) * rs)
        g1 = g1_ref[0] if g1_ref.shape[1] == 1 else g1_ref[0, r, :]
        x = x_ref[r, :] if x_rows is None else x_rows[r]
        mix = _dot(yc_ref[0, r, :], w_ref[0:dc, :]) + _dot(ym_ref[0, r, :], w_ref[dc:, :])
        o_ref[r, :] = _ln(alpha * x + (1.0 + g1) * mix) * lg_ref[...] + lb_ref[...]


def _outproj(yc, ym, w_out, x, mod, ln_g, ln_b, *, tm, tpb, alpha, splits=2):
    d = x.shape[-1]
    rows = x.size // d
    dc = yc.shape[-1]
    dm = ym.shape[-1]
    r = 1 if mod.shape[1] == 1 else tm
    vec = pl.BlockSpec((1, d), lambda m: (0, 0))
    x_spec = (pl.BlockSpec((tm, d), lambda m: (m, 0)) if x.ndim == 2
              else pl.BlockSpec(x.shape, lambda m: (0, 0, 0)))
    return pl.pallas_call(
        functools.partial(_outproj_kernel, alpha=alpha, splits=splits),
        grid=(rows // tm,),
        in_specs=[pl.BlockSpec((1, tm, dc), lambda m: (m // tpb, m % tpb, 0)),
                  pl.BlockSpec((1, tm, dm), lambda m: (m // tpb, m % tpb, 0)),
                  pl.BlockSpec((dc + dm, d), lambda m: (0, 0)),
                  x_spec,
                  pl.BlockSpec((1, r, d), lambda m: (m // tpb, 0, 2)),
                  vec, vec],
        out_specs=pl.BlockSpec((tm, d), lambda m: (m, 0)),
        out_shape=jax.ShapeDtypeStruct((rows, d), F32),
        compiler_params=_cparams(("arbitrary",)),
        name="outproj",
    )(yc, ym, w_out, x, mod, ln_g, ln_b)


def _ffn_kernel(*refs, nf, tpb, period, sample, alpha, splits):
    if sample:
        (x_ref, sh_ref, sc_ref, g2_ref, wa_ref, wg_ref, wconv_ref, wd_ref, lg_ref, lb_ref,
         s_ref, y_ref, t_ref, u_scr, z_scr, y_scr, acc_scr) = refs
    else:
        (x_ref, sh_ref, sc_ref, g2_ref, wa_ref, wg_ref, wconv_ref, wd_ref, lg_ref, lb_ref,
         y_ref, at_ref, u_scr, carry_scr) = refs
    m = pl.program_id(0)
    f = pl.program_id(1)

    if not sample:
        carried = _carried_rows(carry_scr, f, lax.rem(m, tpb) == 0)
        rs = u_scr.shape[0] // splits

        def body(first, last):
            prev = carried
            for i in range(splits):
                r = slice(i * rs, (i + 1) * rs)
                if first:
                    u = (_ln(x_ref[r, :]) * (1.0 + sc_ref[0]) + sh_ref[0]).astype(BF16)
                    u_scr[r, :] = u
                else:
                    u = u_scr[r, :]
                ac, prev = _conv3_rows(_dot(u, wa_ref[...]), wconv_ref[...], prev)
                hcur = (ac * _sigmoid(ac) * _dot(u, wg_ref[...])).astype(BF16)
                acc = _dot(hcur, wd_ref[...])
                if not first:
                    acc = y_ref[r, :] + acc
                if last:
                    acc = (_ln(alpha * x_ref[r, :] + (1.0 + g2_ref[0]) * acc)
                           * lg_ref[...] + lb_ref[...])
                y_ref[r, :] = acc
            carry_scr[f] = prev
            at_ref[0] = prev

        if nf == 1:
            body(True, True)
        else:
            pl.when(f == 0)(lambda: body(True, False))
            if nf > 2:
                pl.when(jnp.logical_and(f > 0, f < nf - 1))(lambda: body(False, False))
            pl.when(f == nf - 1)(lambda: body(False, True))
        return

    @pl.when(f == 0)
    def _():
        u = _ln(x_ref[...]) * (1.0 + sc_ref[0]) + sh_ref[0]
        u_scr[...] = u.astype(BF16)
        acc_scr[...] = jnp.zeros_like(acc_scr)

    u = u_scr[...]
    a = _dot(u, wa_ref[...])
    ac = _conv3_sequences(a, wconv_ref[...], s_ref, z_scr, y_scr, t_ref, period)
    hcur = (ac * _sigmoid(ac) * _dot(u, wg_ref[...])).astype(BF16)
    acc_scr[...] += _dot(hcur, wd_ref[...])

    @pl.when(f == nf - 1)
    def _():
        y = _ln(alpha * x_ref[...] + (1.0 + g2_ref[0]) * acc_scr[...]) * lg_ref[...] + lb_ref[...]
        y_ref[...] = y.reshape(y_ref.shape)


def _ffn(x, mod, w_up, w_conv, w_down, ln_g, ln_b, *, tm, tpb, tf, sample, alpha, period=0,
         splits=1, state=None):
    rows, d = x.shape
    ff = w_down.shape[0]
    nf = ff // tf
    nm = rows // tm
    r = tm if sample else 1
    vec = pl.BlockSpec((1, d), lambda m, f: (0, 0))
    in_specs = [
        pl.BlockSpec((tm, d), lambda m, f: (m, 0), pipeline_mode=pl.Buffered(1)),
        pl.BlockSpec((1, r, d), lambda m, f: (m // tpb, 0, 3)),
        pl.BlockSpec((1, r, d), lambda m, f: (m // tpb, 0, 4)),
        pl.BlockSpec((1, r, d), lambda m, f: (m // tpb, 0, 5)),
        pl.BlockSpec((d, tf), lambda m, f: (0, f)),
        pl.BlockSpec((d, tf), lambda m, f: (0, nf + f)),
        pl.BlockSpec((CONV_K, tf), lambda m, f: (0, f)),
        pl.BlockSpec((tf, d), lambda m, f: (f, 0)),
        vec, vec,
    ]
    args = [x, mod, mod, mod, w_up, w_up, w_conv, w_down, ln_g, ln_b]
    scratch = [pltpu.VMEM((tm, d), BF16)]
    out_specs = [pl.BlockSpec((tm, d), lambda m, f: (m, 0))]
    out_shape = [jax.ShapeDtypeStruct((rows, d), F32)]
    if sample:
        assert nm == 1
        st = pl.BlockSpec((tm // period, CONV_K - 1, tf), lambda m, f: (0, 0, f))
        in_specs.append(st)
        args.append(state)
        out_specs.append(st)
        out_shape.append(jax.ShapeDtypeStruct((tm // period, CONV_K - 1, ff), F32))
        scratch += [pltpu.VMEM((tf // LANES, tm, LANES), F32)] * 2 + [pltpu.VMEM((tm, d), F32)]
        out_specs[0] = pl.BlockSpec((tm // period, period, d), lambda m, f: (0, 0, 0))
        out_shape[0] = jax.ShapeDtypeStruct((tm // period, period, d), F32)
    else:
        out_specs.append(pl.BlockSpec((1, STATE_ROWS, tf), lambda m, f: (m, 0, f)))
        out_shape.append(jax.ShapeDtypeStruct((nm, STATE_ROWS, ff), F32))
        scratch.append(pltpu.VMEM((nf, STATE_ROWS, tf), F32))
    return pl.pallas_call(
        functools.partial(_ffn_kernel, nf=nf, tpb=tpb, period=period, sample=sample, alpha=alpha,
                          splits=splits),
        grid=(nm, nf),
        in_specs=in_specs,
        out_specs=out_specs,
        out_shape=out_shape,
        scratch_shapes=scratch,
        compiler_params=_cparams(("arbitrary", "arbitrary")),
        name="ffn_sample" if sample else "ffn_prompt",
    )(*args)


def _layer_prompt(x, mod, wts, yc, ztail, mlstm_out, *, alpha):
    B, T, D = x.shape
    (_, _, _, _, _, w_out, ln1_g, ln1_b, w_up, w_fconv, w_down, ln2_g, ln2_b) = wts
    tm = PROMPT_TM
    tpb = T // tm
    x2 = x.reshape(B * T, D)
    ym, C, n, m = mlstm_out
    x1 = _outproj(yc, ym, w_out, x2, mod, ln1_g, ln1_b, tm=OUTPROJ_TM, tpb=T // OUTPROJ_TM,
                  alpha=alpha)
    y, atail = _ffn(x1, mod, w_up, w_fconv, w_down, ln2_g, ln2_b,
                    tm=tm, tpb=tpb, tf=FFN_TF, sample=False, alpha=alpha, splits=2)
    return (y.reshape(B, T, D), ztail[tpb - 1::tpb, STATE_ROWS - 2:], C, n, m[..., 0],
            atail[tpb - 1::tpb, STATE_ROWS - 2:])


def _layer_sample(x, mod, conv_buf, C0, n0, m0, ffn_buf, wts, *, alpha):
    B, T, D = x.shape
    (w_in, w_gate, bg, w_conv, wmh, w_out, ln1_g, ln1_b, w_up, w_fconv, w_down, ln2_g, ln2_b) = wts
    rows = B * T
    yc, q, k, v, o, gates, conv_new = _inproj(x, mod, w_in, w_gate, w_conv, tm=rows, tpb=1,
                                              sample=True, period=T, state=conv_buf)
    mrow = jnp.pad(jnp.repeat(m0, T, axis=0), ((0, 0), (N_HEADS, GATE_LANES - 2 * N_HEADS)))
    nrow = jnp.repeat(n0.reshape(B, N_HEADS * HEAD_DIM), T, axis=0)
    ym, C, n, m = _mlstm_sample(q[0], k[0], v[0], o[0], gates[0], bg, mrow, nrow, C0, n0, wmh,
                                bb=SAMPLE_SEQS, T=T)
    x1 = _outproj(yc, ym[None], w_out, x, mod, ln1_g, ln1_b, tm=rows, tpb=1, alpha=alpha)
    y, ffn_new = _ffn(x1, mod, w_up, w_fconv, w_down, ln2_g, ln2_b, tm=rows, tpb=1, tf=FFN_TF,
                      sample=True, alpha=alpha, period=T, state=ffn_buf)
    return y, conv_new, C, n, m[..., 0], ffn_new


def kernel(x_prompt, x_sample, c_prompt, c_sample, state_conv, state_mlstm_C, state_mlstm_n,
           state_mlstm_m, state_ffn_conv, w_ada, b_ada, w_in, b_gate, w_conv, w_mh_norm, w_out,
           ln1_g, ln1_b, w_up, w_ffn_conv, w_down, ln2_g, ln2_b):
    depth = w_in.shape[0]
    alpha = (2 * depth) ** 0.25
    Bp = x_prompt.shape[0]
    Bs, Ts, D = x_sample.shape
    dc = w_conv.shape[-1]
    dm = w_mh_norm.shape[-1]
    n_main = 3 * dc + 4 * dm
    assert dc == dm == N_HEADS * HEAD_DIM and Ts >= CONV_K - 1 and SLAB % Ts == 0

    xp, xs = x_prompt, x_sample
    outs_p = [[] for _ in range(5)]
    outs_s = [[] for _ in range(5)]
    for l in range(depth):
        c_all = jnp.pad(jnp.concatenate([c_sample, c_prompt], axis=0), ((0, -(Bs + Bp) % LANES), (0, 0)))
        w_in_t = jnp.swapaxes(w_in[l], 0, 1)
        mod, w_in_b = _ada(c_all, w_ada[l], b_ada[l], w_in_t, n_main, n_seq=Bs, reps=Ts,
                           rows=Bs * Ts + Bp)
        mod_s = mod.reshape(1, Bs * Ts + Bp, 6 * D)
        mod_p = mod[Bs * Ts:].reshape(Bp, 1, 6 * D)
        w_gate = jnp.pad(w_in_t[n_main:], ((0, GATE_LANES - 2 * N_HEADS), (0, 0))).astype(BF16)
        bg = jnp.pad(b_gate[l], (0, GATE_LANES - 2 * N_HEADS)).reshape(1, GATE_LANES)
        wmh = w_mh_norm[l].reshape(1, dm)
        Tp = xp.shape[1]
        yc, q, k, v, o, gates, ztail, w_up_b = _inproj(
            xp.reshape(Bp * Tp, D), mod_p, w_in_b, w_gate, w_conv[l], tm=PROMPT_TM,
            tpb=Tp // PROMPT_TM, sample=False, cast=(w_up[l],))
        *mlstm_out, w_down_b, w_out_b = _mlstm_prompt(q, k, v, o, gates, bg, wmh, L=MLSTM_CHUNK,
                                                      cast=(w_down[l], w_out[l]))
        wts = (
            w_in_b,
            w_gate,
            bg,
            w_conv[l],
            wmh,
            w_out_b,
            ln1_g[l].reshape(1, D), ln1_b[l].reshape(1, D),
            w_up_b,
            w_ffn_conv[l],
            w_down_b,
            ln2_g[l].reshape(1, D), ln2_b[l].reshape(1, D),
        )
        xs, *st_s = _layer_sample(xs, mod_s, state_conv[l], state_mlstm_C[l], state_mlstm_n[l],
                                  state_mlstm_m[l], state_ffn_conv[l], wts, alpha=alpha)
        xp, *st_p = _layer_prompt(xp, mod_p, wts, yc, ztail, mlstm_out, alpha=alpha)
        for acc, val in zip(outs_p, st_p):
            acc.append(val)
        for acc, val in zip(outs_s, st_s):
            acc.append(val)
    return (xp.astype(x_prompt.dtype), xs.astype(x_sample.dtype),
            *[jnp.stack(a) for a in outs_p], *[jnp.stack(a) for a in outs_s])
```

```python
import functools

import jax
import jax.numpy as jnp
from jax import lax
from jax.experimental import pallas as pl
from jax.experimental.pallas import tpu as pltpu

F32 = jnp.float32
BF16 = jnp.bfloat16

N_HEADS = 4
HEAD_DIM = 256
CONV_K = 3
LN_EPS = 1e-5
NEG = -1e30
LANES = 128
GATE_LANES = LANES
STATE_ROWS = 8
SLAB = 16
VMEM_LIMIT = 56 * 1024 * 1024

PROMPT_TM = 1024
OUTPROJ_TM = 512
FFN_TF = 512
MLSTM_CHUNK = LANES
SAMPLE_SEQS = 8


def _cparams(sem):
    return pltpu.CompilerParams(dimension_semantics=sem, vmem_limit_bytes=VMEM_LIMIT)


def _ln(x):
    mu = jnp.mean(x, axis=-1, keepdims=True)
    xc = x - mu
    var = jnp.mean(xc * xc, axis=-1, keepdims=True)
    return xc * lax.rsqrt(var + LN_EPS)


def _log_sigmoid(x):
    return jnp.minimum(x, 0.0) - jnp.log1p(jnp.exp(-jnp.abs(x)))


def _sigmoid(x):
    return 1.0 / (1.0 + jnp.exp(-x))


def _dot(a, b):
    return jnp.dot(a, b, preferred_element_type=F32)


def _dot_nt(a, b):
    return lax.dot_general(a, b, (((1,), (1,)), ((), ())), preferred_element_type=F32)


def _conv3_rows(z, w, prev):
    p0 = prev[STATE_ROWS - 2:STATE_ROWS - 1]
    p1 = prev[STATE_ROWS - 1:STATE_ROWS]
    t = lax.broadcasted_iota(jnp.int32, z.shape, 0)
    z1 = jnp.where(t >= 1, pltpu.roll(z, 1, 0), p1)
    z2 = jnp.where(t >= 2, pltpu.roll(z, 2, 0), jnp.where(t == 0, p0, p1))
    return w[0:1] * z2 + w[1:2] * z1 + w[2:3] * z, z[z.shape[0] - STATE_ROWS:]


def _carried_rows(carry_ref, idx, first):
    @pl.when(first)
    def _():
        carry_ref[idx] = jnp.zeros(carry_ref.shape[1:], F32)

    return carry_ref[idx]


def _conv3_sequences(z, w, s_ref, z_scr, y_scr, t_ref, T):
    nseq = z.shape[0] // T
    y = w[0:1] * pltpu.roll(z, 2, 0) + w[1:2] * pltpu.roll(z, 1, 0) + w[2:3] * z

    def rows(t):
        return pl.ds(t, nseq, stride=T)

    cols = []
    for c in range(z.shape[1] // LANES):
        cs = slice(c * LANES, (c + 1) * LANES)
        w0, w1, w2 = w[0:1, cs], w[1:2, cs], w[2:3, cs]
        z_scr[c] = z[:, cs]
        y_scr[c] = y[:, cs]
        s0 = s_ref[:, 0, cs]
        s1 = s_ref[:, 1, cs]
        z0 = z_scr[c, rows(0), :]
        z1 = z_scr[c, rows(1), :]
        y_scr[c, rows(0), :] = w0 * s0 + w1 * s1 + w2 * z0
        y_scr[c, rows(1), :] = w0 * s1 + w1 * z0 + w2 * z1
        t_ref[:, 0, cs] = z_scr[c, rows(T - 2), :]
        t_ref[:, 1, cs] = z_scr[c, rows(T - 1), :]
        cols.append(y_scr[c])
    return jnp.concatenate(cols, axis=1)


def _ada_kernel(c_ref, w_ref, b_ref, wt_ref, o_ref, wb_ref, s_scr, *, n_seq, reps):
    @pl.when(pl.program_id(0) == 0)
    def _():
        c = c_ref[...]
        s = (c * _sigmoid(c)).astype(BF16)
        shape = (s_scr.shape[0], c.shape[0])
        r = lax.broadcasted_iota(jnp.int32, shape, 0)
        src = jnp.where(r < n_seq * reps, r // reps, r - n_seq * (reps - 1))
        sel = jnp.where(lax.broadcasted_iota(jnp.int32, shape, 1) == src, 1.0, 0.0).astype(BF16)
        s_scr[...] = _dot(sel, s).astype(BF16)

    o_ref[...] = _dot(s_scr[...], w_ref[...].astype(BF16)) + b_ref[...]
    wb_ref[...] = wt_ref[...].T.astype(BF16)


def _ada(c, w, b, wt, n_t, *, n_seq, reps, rows, tn=768, tt=512):
    u, d = c.shape
    r = rows
    n = w.shape[1]
    k = wt.shape[1]
    steps, t_blocks = n // tn, n_t // tt
    assert steps * tn == n and t_blocks * tt == n_t and t_blocks <= steps

    def t_block(j):
        return jnp.minimum(j, t_blocks - 1)

    return pl.pallas_call(
        functools.partial(_ada_kernel, n_seq=n_seq, reps=reps),
        grid=(steps,),
        in_specs=[
            pl.BlockSpec((u, d), lambda j: (0, 0)),
            pl.BlockSpec((d, tn), lambda j: (0, j)),
            pl.BlockSpec((1, tn), lambda j: (0, j)),
            pl.BlockSpec((tt, k), lambda j: (t_block(j), 0)),
        ],
        out_specs=[pl.BlockSpec((r, tn), lambda j: (0, j)),
                   pl.BlockSpec((k, tt), lambda j: (0, t_block(j)))],
        out_shape=[jax.ShapeDtypeStruct((r, n), F32), jax.ShapeDtypeStruct((k, n_t), BF16)],
        scratch_shapes=[pltpu.VMEM((r, d), BF16)],
        compiler_params=_cparams(("arbitrary",)),
        name="ada",
    )(c, w, b.reshape(1, n), wt)


def _cast_side_job(mats, n_steps, index_map):
    specs, shapes = [], []
    for w in mats:
        slab = w.shape[0] // n_steps
        assert slab * n_steps == w.shape[0] and slab % SLAB == 0
        specs.append(pl.BlockSpec((slab, w.shape[1]), index_map))
        shapes.append(jax.ShapeDtypeStruct(w.shape, BF16))
    return specs, shapes


def _cast_slabs(srcs, dsts):
    for src, dst in zip(srcs, dsts):
        dst[...] = src[...].astype(BF16)


def _inproj_kernel(*refs, tpb, period, sample, n_cast=0):
    if sample:
        (x_ref, sh_ref, sc_ref, wb_ref, wc_ref, wh_ref, wq_ref, wk_ref, wv_ref, wo_ref, wg_ref,
         wconv_ref, s_ref,
         yc_ref, q_ref, k_ref, v_ref, o_ref, g_ref, t_ref, u_scr, z_scr, y_scr) = refs
    else:
        (x_ref, sh_ref, sc_ref, wb_ref, wc_ref, wh_ref, wq_ref, wk_ref, wv_ref, wo_ref, wg_ref,
         wconv_ref) = refs[:12]
        cast_in = refs[12:12 + n_cast]
        (yc_ref, q_ref, k_ref, v_ref, o_ref, g_ref, zt_ref) = refs[12 + n_cast:19 + n_cast]
        cast_out = refs[19 + n_cast:19 + 2 * n_cast]
        u_scr, carry_scr = refs[19 + 2 * n_cast:]
    m = pl.program_id(0)
    j = pl.program_id(1)

    if not sample:
        _cast_slabs(cast_in, cast_out)

        carried = _carried_rows(carry_scr, j, lax.rem(m, tpb) == 0)
        rs = u_scr.shape[0] // 2

        def body(first):
            prev = carried
            for i in range(2):
                r = slice(i * rs, (i + 1) * rs)
                if first:
                    u = (_ln(x_ref[r, :]) * (1.0 + sc_ref[0]) + sh_ref[0]).astype(BF16)
                    u_scr[r, :] = u
                    g_ref[0, r, :] = _dot_nt(u, wg_ref[...])
                else:
                    u = u_scr[r, :]
                z = _dot(u, wc_ref[...]) * _dot(u, wh_ref[...])
                yc, prev = _conv3_rows(z, wconv_ref[...], prev)
                yc_ref[0, r, :] = (_dot(u, wb_ref[...]) * yc).astype(BF16)
                q_ref[0, r, :] = _dot(u, wq_ref[...]).astype(BF16)
                k_ref[0, r, :] = (_dot(u, wk_ref[...]) * (HEAD_DIM ** -0.5)).astype(BF16)
                v_ref[0, r, :] = _dot(u, wv_ref[...]).astype(BF16)
                o_ref[0, r, :] = _dot(u, wo_ref[...]).astype(BF16)
            carry_scr[j] = prev
            zt_ref[0] = prev

        pl.when(j == 0)(lambda: body(True))
        pl.when(j > 0)(lambda: body(False))
        return

    @pl.when(j == 0)
    def _():
        x = x_ref[...].reshape(u_scr.shape)
        u = _ln(x) * (1.0 + sc_ref[0]) + sh_ref[0]
        ub = u.astype(BF16)
        u_scr[...] = ub
        g_ref[0] = _dot_nt(ub, wg_ref[...])

    u = u_scr[...]
    z = _dot(u, wc_ref[...]) * _dot(u, wh_ref[...])
    yc = _conv3_sequences(z, wconv_ref[...], s_ref, z_scr, y_scr, t_ref, period)
    half = u.shape[0] // 2
    bg = jnp.concatenate([_dot(u[:half], wb_ref[...]), _dot(u[half:], wb_ref[...])], axis=0)
    yc_ref[0] = (bg * yc).astype(BF16)
    q_ref[0] = _dot(u, wq_ref[...]).astype(BF16)
    k_ref[0] = (_dot(u, wk_ref[...]) * (HEAD_DIM ** -0.5)).astype(BF16)
    v_ref[0] = _dot(u, wv_ref[...]).astype(BF16)
    o_ref[0] = _dot(u, wo_ref[...]).astype(BF16)


def _inproj(x, mod, w_in, w_gate, w_conv, *, tm, tpb, sample, period=0, state=None, cast=()):
    d = x.shape[-1]
    rows = x.size // d
    dc = w_conv.shape[1]
    tn = HEAD_DIM
    nj = dc // tn
    nm = rows // tm
    nseq = nm // tpb
    r = tm if sample else 1

    def wspec(off):
        return pl.BlockSpec((d, tn), lambda m, j, off=off: (0, off * nj + j))

    in_specs = [
        (pl.BlockSpec((tm // period, period, d), lambda m, j: (0, 0, 0)) if sample
         else pl.BlockSpec((tm, d), lambda m, j: (m, 0))),
        pl.BlockSpec((1, r, d), lambda m, j: (m // tpb, 0, 0)),
        pl.BlockSpec((1, r, d), lambda m, j: (m // tpb, 0, 1)),
        wspec(0), wspec(1), wspec(2), wspec(3), wspec(4), wspec(5), wspec(6),
        pl.BlockSpec((GATE_LANES, d), lambda m, j: (0, 0)),
        pl.BlockSpec((CONV_K, tn), lambda m, j: (0, j)),
    ]
    args = [x, mod, mod, w_in, w_in, w_in, w_in, w_in, w_in, w_in, w_gate, w_conv]
    scratch = [pltpu.VMEM((tm, d), BF16)]
    act = pl.BlockSpec((1, tm, tn), lambda m, j: (m // tpb, m % tpb, j))
    out_specs = [act, act, act, act, act,
                 pl.BlockSpec((1, tm, GATE_LANES), lambda m, j: (m // tpb, m % tpb, 0))]
    out_shape = [jax.ShapeDtypeStruct((nseq, tpb * tm, dc), BF16)] * 5 + [
        jax.ShapeDtypeStruct((nseq, tpb * tm, GATE_LANES), F32)]
    if sample:
        assert nm == 1
        st = pl.BlockSpec((tm // period, CONV_K - 1, tn), lambda m, j: (0, 0, j))
        in_specs.append(st)
        args.append(state)
        out_specs.append(st)
        out_shape.append(jax.ShapeDtypeStruct((tm // period, CONV_K - 1, dc), F32))
        scratch += [pltpu.VMEM((tn // LANES, tm, LANES), F32)] * 2
    else:
        out_specs.append(pl.BlockSpec((1, STATE_ROWS, tn), lambda m, j: (m, 0, j)))
        out_shape.append(jax.ShapeDtypeStruct((nm, STATE_ROWS, dc), F32))
        scratch.append(pltpu.VMEM((nj, STATE_ROWS, tn), F32))
        cast_specs, cast_shapes = _cast_side_job(cast, nm * nj, lambda m, j: (m * nj + j, 0))
        in_specs += cast_specs
        args += list(cast)
        out_specs += cast_specs
        out_shape += cast_shapes
    return pl.pallas_call(
        functools.partial(_inproj_kernel, tpb=tpb, period=period, sample=sample, n_cast=len(cast)),
        grid=(nm, nj),
        in_specs=in_specs,
        out_specs=out_specs,
        out_shape=out_shape,
        scratch_shapes=scratch,
        compiler_params=_cparams(("arbitrary", "arbitrary")),
        name="inproj_sample" if sample else "inproj_prompt",
    )(*args)


def _split3(x):
    hi = x.astype(BF16)
    r1 = x - hi.astype(F32)
    mid = r1.astype(BF16)
    lo = (r1 - mid.astype(F32)).astype(BF16)
    return hi, mid, lo


def _head_out(hh, o, wmh):
    return (_sigmoid(o.astype(F32)) * (_ln(hh) * wmh)).astype(BF16)


def _split2(x):
    hi = x.astype(BF16)
    return hi, (x - hi.astype(F32)).astype(BF16)


def _rowsum(x, ones):
    hi, lo = _split2(x)
    return _dot(hi, ones) + _dot(lo, ones)


def _rep2(x):
    return jnp.concatenate([x, x], axis=1)


def _mlstm_chunk_kernel(*refs, B, L, n_cast):
    (q_ref, k_ref, v_ref, o_ref, g_ref, bg_ref, wmh_ref) = refs[:7]
    (y_ref, c_ref, n_ref, m_ref) = refs[7 + n_cast:11 + n_cast]
    nrep_scr = refs[-1]
    _cast_slabs(refs[7:7 + n_cast], refs[11 + n_cast:11 + 2 * n_cast])
    _mlstm_chunk(q_ref, k_ref, v_ref, o_ref, g_ref, bg_ref, wmh_ref, y_ref, c_ref, n_ref, m_ref,
                 nrep_scr, B=B, L=L)


def _mlstm_chunk(q_ref, k_ref, v_ref, o_ref, g_ref, bg_ref, wmh_ref,
                 y_ref, c_ref, n_ref, m_ref, nrep_scr, *, B, L):
    step = pl.program_id(0)

    @pl.when(step == 0)
    def _():
        c_ref[...] = jnp.zeros_like(c_ref)
        m_ref[...] = jnp.zeros_like(m_ref)
        nrep_scr[...] = jnp.zeros_like(nrep_scr)

    row = lax.broadcasted_iota(jnp.int32, (L, L), 0)
    col = lax.broadcasted_iota(jnp.int32, (L, L), 1)
    causal = col <= row
    tril = jnp.where(causal, 1.0, 0.0).astype(BF16)
    ones_l = jnp.ones((L, LANES), BF16)
    ones_d = jnp.ones((HEAD_DIM, LANES), BF16)
    inv_d = 1.0 / HEAD_DIM
    tn = (((0,), (0,)), ((), ()))

    heads =[(b, h) for b in range(B) for h in range(N_HEADS)]

    def hsl(h):
        return slice(h * HEAD_DIM, (h + 1) * HEAD_DIM)

    gate = []
    for b in range(B):
        g = g_ref[b] + bg_ref[...]
        hi, mid, lo = _split3(_log_sigmoid(g))
        bcum = _dot(tril, hi) + _dot(tril, mid) + _dot(tril, lo)
        gate.append((g, g.T, bcum, bcum.T, m_ref[b]))

    st = []
    for b, h in heads:
        g, g_t, bcum, bcum_t, m_all = gate[b]
        bc = jnp.broadcast_to(bcum[:, N_HEADS + h:N_HEADS + h + 1], (L, LANES))
        li = jnp.broadcast_to(g[:, h:h + 1], (L, LANES))
        br = bcum_t[N_HEADS + h:N_HEADS + h + 1, :]
        m_prev = m_all[h:h + 1, :]
        a = bc + m_prev
        dlog = jnp.where(causal, bc - br + g_t[h:h + 1, :], NEG)
        mt = jnp.maximum(a, jnp.max(dlog, axis=1, keepdims=True))
        st.append(dict(bc=bc, li=li, m_prev=m_prev, mt=mt, dw=jnp.exp(dlog - mt),
                       inter=jnp.exp(a - mt)))

    for (b, h), e in zip(heads, st):
        e["s"] = lax.dot_general(q_ref[b, :, hsl(h)], k_ref[b, :, hsl(h)], (((1,), (1,)), ((), ())),
                                 preferred_element_type=F32) * e["dw"]

    for i, ((b, h), e) in enumerate(zip(heads, st)):
        s_hi, s_lo = _split2(e["s"])
        cn = jnp.concatenate([c_ref[b, h], nrep_scr[i]], axis=1).astype(BF16)
        qc = _dot(q_ref[b, :, hsl(h)], cn)
        num = _dot(s_hi, v_ref[b, :, hsl(h)]) + _rep2(e["inter"]) * qc[:, :HEAD_DIM]
        den = _dot(s_hi, ones_l) + _dot(s_lo, ones_l) + e["inter"] * qc[:, HEAD_DIM:]
        rden = 1.0 / jnp.maximum(jnp.abs(den), jnp.exp(-e["mt"]))
        e["hh"] = num * _rep2(rden)

    for (b, h), e in zip(heads, st):
        hh = e["hh"]
        xc = hh - _rep2(_rowsum(hh, ones_d) * inv_d)
        rstd = lax.rsqrt(_rowsum(xc * xc, ones_d) * inv_d + LN_EPS)
        y = _sigmoid(o_ref[b, :, hsl(h)].astype(F32)) * (xc * _rep2(rstd) * wmh_ref[:, hsl(h)])
        y_ref[b, :, hsl(h)] = y.astype(BF16)

    m_rows = []
    for i, ((b, h), e) in enumerate(zip(heads, st)):
        kh = k_ref[b, :, hsl(h)]
        m_new = e["mt"][L - 1:L, :]
        b_last = e["bc"][L - 1:L, :]
        wc = jnp.exp(b_last - e["bc"] + e["li"] - m_new)
        dc = jnp.exp(b_last + e["m_prev"] - m_new)
        vw = (v_ref[b, :, hsl(h)].astype(F32) * _rep2(wc)).astype(BF16)
        c_ref[b, h] = _rep2(dc) * c_ref[b, h] + lax.dot_general(kh, vw, tn, preferred_element_type=F32)
        wc_hi, wc_lo = _split2(wc)
        nrep_scr[i] = (dc * nrep_scr[i] + lax.dot_general(kh, wc_hi, tn, preferred_element_type=F32)
                       + lax.dot_general(kh, wc_lo, tn, preferred_element_type=F32))
        m_rows.append(m_new)

    for b in range(B):
        m_ref[b] = jnp.concatenate(m_rows[b * N_HEADS:(b + 1) * N_HEADS], axis=0)

    @pl.when(step == pl.num_programs(0) - 1)
    def _():
        for b in range(B):
            n_ref[b] = jnp.concatenate(
                [nrep_scr[b * N_HEADS + h].T[0:1, :] for h in range(N_HEADS)], axis=0)


def _mlstm_prompt(q, k, v, o, gates, bg, wmh, *, L, cast=()):
    b, t, dm = q.shape
    assert L == LANES
    act = pl.BlockSpec((b, L, dm), lambda c: (0, c, 0))
    whole = lambda *shape: pl.BlockSpec(shape, lambda c: (0,) * len(shape))
    cast_specs, cast_shapes = _cast_side_job(cast, t // L, lambda c: (c, 0))
    return pl.pallas_call(
        functools.partial(_mlstm_chunk_kernel, B=b, L=L, n_cast=len(cast)),
        grid=(t // L,),
        in_specs=[act, act, act, act,
                  pl.BlockSpec((b, L, GATE_LANES), lambda c: (0, c, 0)),
                  whole(1, GATE_LANES), whole(1, dm)] + cast_specs,
        out_specs=[act,
                   whole(b, N_HEADS, HEAD_DIM, HEAD_DIM),
                   whole(b, N_HEADS, HEAD_DIM),
                   whole(b, N_HEADS, GATE_LANES)] + cast_specs,
        out_shape=[jax.ShapeDtypeStruct((b, t, dm), BF16),
                   jax.ShapeDtypeStruct((b, N_HEADS, HEAD_DIM, HEAD_DIM), F32),
                   jax.ShapeDtypeStruct((b, N_HEADS, HEAD_DIM), F32),
                   jax.ShapeDtypeStruct((b, N_HEADS, GATE_LANES), F32)] + cast_shapes,
        scratch_shapes=[pltpu.VMEM((b * N_HEADS, HEAD_DIM, LANES), F32)],
        compiler_params=_cparams(("arbitrary",)),
        name="mlstm_prompt",
    )(q, k, v, o, gates, bg, wmh, *cast)


def _mlstm_step_kernel(q_ref, k_ref, v_ref, o_ref, g_ref, bg_ref, mrow_ref, nrow_ref,
                       c0_ref, n0_ref, wmh_ref,
                       y_ref, c_ref, n_ref, m_ref, *, bb, T):
    R = bb * T
    per_slab = SLAB // T
    t = lax.rem(lax.broadcasted_iota(jnp.int32, (R, GATE_LANES), 0), T)

    def down(x, d):
        return pltpu.roll(x, d, 0)

    def up(x, d):
        return pltpu.roll(x, x.shape[0] - d, 0)

    def seg_last(x):
        out = x
        for d in range(1, T):
            out = jnp.where(t == T - 1 - d, up(x, d), out)
        return out

    g = g_ref[...] + bg_ref[...]
    li = pltpu.roll(g, N_HEADS, 1)
    lf = _log_sigmoid(g)
    b = lf
    for d in range(1, T):
        b = b + jnp.where(t >= d, down(lf, d), 0.0)
    m_prev = mrow_ref[...]
    a = b + m_prev
    dl = [li] + [jnp.where(t >= d, b - down(b, d) + down(li, d), NEG) for d in range(1, T)]
    mt = a
    for d in range(T):
        mt = jnp.maximum(mt, dl[d])
    dw = [jnp.exp(dl[d] - mt) for d in range(T)]
    inter = jnp.exp(a - mt)
    emt = jnp.exp(-mt)
    m_new = seg_last(mt)
    b_last = seg_last(b)
    wc = jnp.exp(b_last - b + li - m_new)
    dc = jnp.exp(b_last + m_prev - m_new)

    row_s = lax.broadcasted_iota(jnp.int32, (SLAB, HEAD_DIM), 0)
    row_r = lax.broadcasted_iota(jnp.int32, (R, HEAD_DIM), 0)

    for h in range(N_HEADS):
        hs = slice(h * HEAD_DIM, (h + 1) * HEAD_DIM)
        ln = N_HEADS + h

        def col(x):
            return x[:, ln:ln + 1]

        qb = q_ref[:, hs]
        kb = k_ref[:, hs]
        vb = v_ref[:, hs]
        qf = qb.astype(F32)
        kf = kb.astype(F32)
        vf = vb.astype(F32)
        num = jnp.zeros((R, HEAD_DIM), F32)
        den = jnp.zeros((R, 1), F32)
        for d in range(T):
            kd = kf if d == 0 else down(kf, d)
            vd = vf if d == 0 else down(vf, d)
            sw = jnp.sum(qf * kd, axis=1, keepdims=True) * col(dw[d])
            num = num + sw * vd
            den = den + sw

        qc_slabs = []
        for si in range(R // SLAB):
            q16 = qb[si * SLAB:(si + 1) * SLAB]
            acc = jnp.zeros((SLAB, HEAD_DIM), F32)
            for bl in range(per_slab):
                bi = si * per_slab + bl
                r = _dot(q16, c0_ref[bi, h].astype(BF16))
                acc = jnp.where(row_s // T == bl, r, acc)
            qc_slabs.append(acc)
        qc = jnp.concatenate(qc_slabs, axis=0)
        qn = jnp.sum(qf * nrow_ref[:, hs], axis=1, keepdims=True)
        num = num + col(inter) * qc
        den = den + col(inter) * qn
        hh = num / jnp.maximum(jnp.abs(den), col(emt))
        y_ref[:, hs] = _head_out(hh, o_ref[:, hs], wmh_ref[:, hs])

        wk = kf * col(wc)
        vw = vf * col(wc)
        for si in range(R // SLAB):
            k16 = kb[si * SLAB:(si + 1) * SLAB]
            vw16 = vw[si * SLAB:(si + 1) * SLAB]
            for bl in range(per_slab):
                bi = si * per_slab + bl
                last = bi * T + T - 1
                vwb = jnp.where(row_s // T == bl, vw16, 0.0).astype(BF16)
                dcb = dc[last:last + 1, ln:ln + 1]
                c_ref[bi, h] = dcb * c0_ref[bi, h] + lax.dot_general(
                    k16, vwb, (((0,), (0,)), ((), ())), preferred_element_type=F32)
                n_ref[bi, h:h + 1, :] = dcb * n0_ref[bi, h:h + 1, :] + jnp.sum(
                    jnp.where(row_r // T == bi, wk, 0.0), axis=0, keepdims=True)
                m_ref[bi, h:h + 1, :] = jnp.broadcast_to(
                    m_new[last:last + 1, ln:ln + 1], (1, GATE_LANES))


def _mlstm_sample(q, k, v, o, gates, bg, mrow, nrow, c0, n0, wmh, *, bb, T):
    rows, dm = q.shape
    nb = c0.shape[0]
    R = bb * T
    act = pl.BlockSpec((R, dm), lambda i: (i, 0))
    gat = pl.BlockSpec((R, GATE_LANES), lambda i: (i, 0))
    cspec = pl.BlockSpec((bb, N_HEADS, HEAD_DIM, HEAD_DIM), lambda i: (i, 0, 0, 0))
    nspec = pl.BlockSpec((bb, N_HEADS, HEAD_DIM), lambda i: (i, 0, 0))
    return pl.pallas_call(
        functools.partial(_mlstm_step_kernel, bb=bb, T=T),
        grid=(nb // bb,),
        in_specs=[act, act, act, act, gat,
                  pl.BlockSpec((1, GATE_LANES), lambda i: (0, 0)),
                  gat,
                  pl.BlockSpec((R, dm), lambda i: (i, 0)),
                  cspec, nspec,
                  pl.BlockSpec((1, dm), lambda i: (0, 0))],
        out_specs=[act, cspec, nspec,
                   pl.BlockSpec((bb, N_HEADS, GATE_LANES), lambda i: (i, 0, 0))],
        out_shape=[jax.ShapeDtypeStruct((rows, dm), BF16),
                   jax.ShapeDtypeStruct(c0.shape, F32),
                   jax.ShapeDtypeStruct(n0.shape, F32),
                   jax.ShapeDtypeStruct((nb, N_HEADS, GATE_LANES), F32)],
        compiler_params=_cparams(("arbitrary",)),
        name="mlstm_sample",
    )(q, k, v, o, gates, bg, mrow, nrow, c0, n0, wmh)


def _outproj_kernel(yc_ref, ym_ref, w_ref, x_ref, g1_ref, lg_ref, lb_ref, o_ref, *, alpha, splits):
    dc = yc_ref.shape[-1]
    rs = o_ref.shape[0] // splits
    x_rows = x_ref[...].reshape(o_ref.shape) if len(x_ref.shape) == 3 else None
    for i in range(splits):
        r = slice(i * rs, (i + 1) * rs)
        g1 = g1_ref[0] if g1_ref.shape[1] == 1 else g1_ref[0, r, :]
        x = x_ref[r, :] if x_rows is None else x_rows[r]
        mix = _dot(yc_ref[0, r, :], w_ref[0:dc, :]) + _dot(ym_ref[0, r, :], w_ref[dc:, :])
        o_ref[r, :] = _ln(alpha * x + (1.0 + g1) * mix) * lg_ref[...] + lb_ref[...]


def _outproj(yc, ym, w_out, x, mod, ln_g, ln_b, *, tm, tpb, alpha, splits=2):
    d = x.shape[-1]
    rows = x.size // d
    dc = yc.shape[-1]
    dm = ym.shape[-1]
    r = 1 if mod.shape[1] == 1 else tm
    vec = pl.BlockSpec((1, d), lambda m: (0, 0))
    x_spec = (pl.BlockSpec((tm, d), lambda m: (m, 0)) if x.ndim == 2
              else pl.BlockSpec(x.shape, lambda m: (0, 0, 0)))
    return pl.pallas_call(
        functools.partial(_outproj_kernel, alpha=alpha, splits=splits),
        grid=(rows // tm,),
        in_specs=[pl.BlockSpec((1, tm, dc), lambda m: (m // tpb, m % tpb, 0)),
                  pl.BlockSpec((1, tm, dm), lambda m: (m // tpb, m % tpb, 0)),
                  pl.BlockSpec((dc + dm, d), lambda m: (0, 0)),
                  x_spec,
                  pl.BlockSpec((1, r, d), lambda m: (m // tpb, 0, 2)),
                  vec, vec],
        out_specs=pl.BlockSpec((tm, d), lambda m: (m, 0)),
        out_shape=jax.ShapeDtypeStruct((rows, d), F32),
        compiler_params=_cparams(("arbitrary",)),
        name="outproj",
    )(yc, ym, w_out, x, mod, ln_g, ln_b)


def _ffn_kernel(*refs, nf, tpb, period, sample, alpha, splits):
    if sample:
        (x_ref, sh_ref, sc_ref, g2_ref, wa_ref, wg_ref, wconv_ref, wd_ref, lg_ref, lb_ref,
         s_ref, y_ref, t_ref, u_scr, z_scr, y_scr, acc_scr) = refs
    else:
        (x_ref, sh_ref, sc_ref, g2_ref, wa_ref, wg_ref, wconv_ref, wd_ref, lg_ref, lb_ref,
         y_ref, at_ref, u_scr, carry_scr) = refs
    m = pl.program_id(0)
    f = pl.program_id(1)

    if not sample:
        carried = _carried_rows(carry_scr, f, lax.rem(m, tpb) == 0)
        rs = u_scr.shape[0] // splits

        def body(first, last):
            prev = carried
            for i in range(splits):
                r = slice(i * rs, (i + 1) * rs)
                if first:
                    u = (_ln(x_ref[r, :]) * (1.0 + sc_ref[0]) + sh_ref[0]).astype(BF16)
                    u_scr[r, :] = u
                else:
                    u = u_scr[r, :]
                ac, prev = _conv3_rows(_dot(u, wa_ref[...]), wconv_ref[...], prev)
                hcur = (ac * _sigmoid(ac) * _dot(u, wg_ref[...])).astype(BF16)
                acc = _dot(hcur, wd_ref[...])
                if not first:
                    acc = y_ref[r, :] + acc
                if last:
                    acc = (_ln(alpha * x_ref[r, :] + (1.0 + g2_ref[0]) * acc)
                           * lg_ref[...] + lb_ref[...])
                y_ref[r, :] = acc
            carry_scr[f] = prev
            at_ref[0] = prev

        if nf == 1:
            body(True, True)
        else:
            pl.when(f == 0)(lambda: body(True, False))
            if nf > 2:
                pl.when(jnp.logical_and(f > 0, f < nf - 1))(lambda: body(False, False))
            pl.when(f == nf - 1)(lambda: body(False, True))
        return

    @pl.when(f == 0)
    def _():
        u = _ln(x_ref[...]) * (1.0 + sc_ref[0]) + sh_ref[0]
        u_scr[...] = u.astype(BF16)
        acc_scr[...] = jnp.zeros_like(acc_scr)

    u = u_scr[...]
    a = _dot(u, wa_ref[...])
    ac = _conv3_sequences(a, wconv_ref[...], s_ref, z_scr, y_scr, t_ref, period)
    hcur = (ac * _sigmoid(ac) * _dot(u, wg_ref[...])).astype(BF16)
    acc_scr[...] += _dot(hcur, wd_ref[...])

    @pl.when(f == nf - 1)
    def _():
        y = _ln(alpha * x_ref[...] + (1.0 + g2_ref[0]) * acc_scr[...]) * lg_ref[...] + lb_ref[...]
        y_ref[...] = y.reshape(y_ref.shape)


def _ffn(x, mod, w_up, w_conv, w_down, ln_g, ln_b, *, tm, tpb, tf, sample, alpha, period=0,
         splits=1, state=None):
    rows, d = x.shape
    ff = w_down.shape[0]
    nf = ff // tf
    nm = rows // tm
    r = tm if sample else 1
    vec = pl.BlockSpec((1, d), lambda m, f: (0, 0))
    in_specs = [
        pl.BlockSpec((tm, d), lambda m, f: (m, 0)),
        pl.BlockSpec((1, r, d), lambda m, f: (m // tpb, 0, 3)),
        pl.BlockSpec((1, r, d), lambda m, f: (m // tpb, 0, 4)),
        pl.BlockSpec((1, r, d), lambda m, f: (m // tpb, 0, 5)),
        pl.BlockSpec((d, tf), lambda m, f: (0, f)),
        pl.BlockSpec((d, tf), lambda m, f: (0, nf + f)),
        pl.BlockSpec((CONV_K, tf), lambda m, f: (0, f)),
        pl.BlockSpec((tf, d), lambda m, f: (f, 0)),
        vec, vec,
    ]
    args = [x, mod, mod, mod, w_up, w_up, w_conv, w_down, ln_g, ln_b]
    scratch = [pltpu.VMEM((tm, d), BF16)]
    out_specs = [pl.BlockSpec((tm, d), lambda m, f: (m, 0))]
    out_shape = [jax.ShapeDtypeStruct((rows, d), F32)]
    if sample:
        assert nm == 1
        st = pl.BlockSpec((tm // period, CONV_K - 1, tf), lambda m, f: (0, 0, f))
        in_specs.append(st)
        args.append(state)
        out_specs.append(st)
        out_shape.append(jax.ShapeDtypeStruct((tm // period, CONV_K - 1, ff), F32))
        scratch += [pltpu.VMEM((tf // LANES, tm, LANES), F32)] * 2 + [pltpu.VMEM((tm, d), F32)]
        out_specs[0] = pl.BlockSpec((tm // period, period, d), lambda m, f: (0, 0, 0))
        out_shape[0] = jax.ShapeDtypeStruct((tm // period, period, d), F32)
    else:
        out_specs.append(pl.BlockSpec((1, STATE_ROWS, tf), lambda m, f: (m, 0, f)))
        out_shape.append(jax.ShapeDtypeStruct((nm, STATE_ROWS, ff), F32))
        scratch.append(pltpu.VMEM((nf, STATE_ROWS, tf), F32))
    return pl.pallas_call(
        functools.partial(_ffn_kernel, nf=nf, tpb=tpb, period=period, sample=sample, alpha=alpha,
                          splits=splits),
        grid=(nm, nf),
        in_specs=in_specs,
        out_specs=out_specs,
        out_shape=out_shape,
        scratch_shapes=scratch,
        compiler_params=_cparams(("arbitrary", "arbitrary")),
        name="ffn_sample" if sample else "ffn_prompt",
    )(*args)


def _layer_prompt(x, mod, wts, yc, ztail, mlstm_out, *, alpha):
    B, T, D = x.shape
    (_, _, _, _, _, w_out, ln1_g, ln1_b, w_up, w_fconv, w_down, ln2_g, ln2_b) = wts
    tm = PROMPT_TM
    tpb = T // tm
    x2 = x.reshape(B * T, D)
    ym, C, n, m = mlstm_out
    x1 = _outproj(yc, ym, w_out, x2, mod, ln1_g, ln1_b, tm=OUTPROJ_TM, tpb=T // OUTPROJ_TM,
                  alpha=alpha)
    y, atail = _ffn(x1, mod, w_up, w_fconv, w_down, ln2_g, ln2_b,
                    tm=tm, tpb=tpb, tf=FFN_TF, sample=False, alpha=alpha, splits=2)
    return (y.reshape(B, T, D), ztail[tpb - 1::tpb, STATE_ROWS - 2:], C, n, m[..., 0],
            atail[tpb - 1::tpb, STATE_ROWS - 2:])


def _layer_sample(x, mod, conv_buf, C0, n0, m0, ffn_buf, wts, *, alpha):
    B, T, D = x.shape
    (w_in, w_gate, bg, w_conv, wmh, w_out, ln1_g, ln1_b, w_up, w_fconv, w_down, ln2_g, ln2_b) = wts
    rows = B * T
    yc, q, k, v, o, gates, conv_new = _inproj(x, mod, w_in, w_gate, w_conv, tm=rows, tpb=1,
                                              sample=True, period=T, state=conv_buf)
    mrow = jnp.pad(jnp.repeat(m0, T, axis=0), ((0, 0), (N_HEADS, GATE_LANES - 2 * N_HEADS)))
    nrow = jnp.repeat(n0.reshape(B, N_HEADS * HEAD_DIM), T, axis=0)
    ym, C, n, m = _mlstm_sample(q[0], k[0], v[0], o[0], gates[0], bg, mrow, nrow, C0, n0, wmh,
                                bb=SAMPLE_SEQS, T=T)
    x1 = _outproj(yc, ym[None], w_out, x, mod, ln1_g, ln1_b, tm=rows, tpb=1, alpha=alpha)
    y, ffn_new = _ffn(x1, mod, w_up, w_fconv, w_down, ln2_g, ln2_b, tm=rows, tpb=1, tf=FFN_TF,
                      sample=True, alpha=alpha, period=T, state=ffn_buf)
    return y, conv_new, C, n, m[..., 0], ffn_new


def kernel(x_prompt, x_sample, c_prompt, c_sample, state_conv, state_mlstm_C, state_mlstm_n,
           state_mlstm_m, state_ffn_conv, w_ada, b_ada, w_in, b_gate, w_conv, w_mh_norm, w_out,
           ln1_g, ln1_b, w_up, w_ffn_conv, w_down, ln2_g, ln2_b):
    depth = w_in.shape[0]
    alpha = (2 * depth) ** 0.25
    Bp = x_prompt.shape[0]
    Bs, Ts, D = x_sample.shape
    dc = w_conv.shape[-1]
    dm = w_mh_norm.shape[-1]
    n_main = 3 * dc + 4 * dm
    assert dc == dm == N_HEADS * HEAD_DIM and Ts >= CONV_K - 1 and SLAB % Ts == 0

    xp, xs = x_prompt, x_sample
    outs_p = [[] for _ in range(5)]
    outs_s = [[] for _ in range(5)]
    for l in range(depth):
        c_all = jnp.pad(jnp.concatenate([c_sample, c_prompt], axis=0), ((0, -(Bs + Bp) % LANES), (0, 0)))
        w_in_t = jnp.swapaxes(w_in[l], 0, 1)
        mod, w_in_b = _ada(c_all, w_ada[l], b_ada[l], w_in_t, n_main, n_seq=Bs, reps=Ts,
                           rows=Bs * Ts + Bp)
        mod_s = mod.reshape(1, Bs * Ts + Bp, 6 * D)
        mod_p = mod[Bs * Ts:].reshape(Bp, 1, 6 * D)
        w_gate = jnp.pad(w_in_t[n_main:], ((0, GATE_LANES - 2 * N_HEADS), (0, 0))).astype(BF16)
        bg = jnp.pad(b_gate[l], (0, GATE_LANES - 2 * N_HEADS)).reshape(1, GATE_LANES)
        wmh = w_mh_norm[l].reshape(1, dm)
        Tp = xp.shape[1]
        yc, q, k, v, o, gates, ztail, w_up_b = _inproj(
            xp.reshape(Bp * Tp, D), mod_p, w_in_b, w_gate, w_conv[l], tm=PROMPT_TM,
            tpb=Tp // PROMPT_TM, sample=False, cast=(w_up[l],))
        *mlstm_out, w_down_b, w_out_b = _mlstm_prompt(q, k, v, o, gates, bg, wmh, L=MLSTM_CHUNK,
                                                      cast=(w_down[l], w_out[l]))
        wts = (
            w_in_b,
            w_gate,
            bg,
            w_conv[l],
            wmh,
            w_out_b,
            ln1_g[l].reshape(1, D), ln1_b[l].reshape(1, D),
            w_up_b,
            w_ffn_conv[l],
            w_down_b,
            ln2_g[l].reshape(1, D), ln2_b[l].reshape(1, D),
        )
        xs, *st_s = _layer_sample(xs, mod_s, state_conv[l], state_mlstm_C[l], state_mlstm_n[l],
                                  state_mlstm_m[l], state_ffn_conv[l], wts, alpha=alpha)
        xp, *st_p = _layer_prompt(xp, mod_p, wts, yc, ztail, mlstm_out, alpha=alpha)
        for acc, val in zip(outs_p, st_p):
            acc.append(val)
        for acc, val in zip(outs_s, st_s):
            acc.append(val)
    return (xp.astype(x_prompt.dtype), xs.astype(x_sample.dtype),
            *[jnp.stack(a) for a in outs_p], *[jnp.stack(a) for a in outs_s])
```

```python
import functools

import jax
import jax.numpy as jnp
from jax import lax
from jax.experimental import pallas as pl
from jax.experimental.pallas import tpu as pltpu

F32 = jnp.float32
BF16 = jnp.bfloat16

N_HEADS = 4
HEAD_DIM = 256
CONV_K = 3
LN_EPS = 1e-5
NEG = -1e30
LANES = 128
GATE_LANES = LANES
STATE_ROWS = 8
SLAB = 16
VMEM_LIMIT = 56 * 1024 * 1024

PROMPT_TM = 1024
OUTPROJ_TM = 512
FFN_TF = 512
MLSTM_CHUNK = LANES
SAMPLE_SEQS = 8


def _cparams(sem):
    return pltpu.CompilerParams(dimension_semantics=sem, vmem_limit_bytes=VMEM_LIMIT)


def _row_param(v):
    return jnp.pad(v.reshape(1, -1), ((0, 2 * STATE_ROWS - 1), (0, 0)))


def _row_spec(n):
    return pl.BlockSpec((STATE_ROWS, n), lambda *_: (0, 0))


def _ln(x):
    mu = jnp.mean(x, axis=-1, keepdims=True)
    xc = x - mu
    var = jnp.mean(xc * xc, axis=-1, keepdims=True)
    return xc * lax.rsqrt(var + LN_EPS)


def _log_sigmoid(x):
    return jnp.minimum(x, 0.0) - jnp.log1p(jnp.exp(-jnp.abs(x)))


def _sigmoid(x):
    return 1.0 / (1.0 + jnp.exp(-x))


def _dot(a, b):
    return jnp.dot(a, b, preferred_element_type=F32)


def _dot_nt(a, b):
    return lax.dot_general(a, b, (((1,), (1,)), ((), ())), preferred_element_type=F32)


def _conv3_rows(z, w, prev):
    p0 = prev[STATE_ROWS - 2:STATE_ROWS - 1]
    p1 = prev[STATE_ROWS - 1:STATE_ROWS]
    t = lax.broadcasted_iota(jnp.int32, z.shape, 0)
    z1 = jnp.where(t >= 1, pltpu.roll(z, 1, 0), p1)
    z2 = jnp.where(t >= 2, pltpu.roll(z, 2, 0), jnp.where(t == 0, p0, p1))
    return w[0:1] * z2 + w[1:2] * z1 + w[2:3] * z, z[z.shape[0] - STATE_ROWS:]


def _carried_rows(carry_ref, idx, first):
    @pl.when(first)
    def _():
        carry_ref[idx] = jnp.zeros(carry_ref.shape[1:], F32)

    return carry_ref[idx]


def _conv3_sequences(z, w, s_ref, z_scr, y_scr, t_ref, T):
    nseq = z.shape[0] // T
    y = w[0:1] * pltpu.roll(z, 2, 0) + w[1:2] * pltpu.roll(z, 1, 0) + w[2:3] * z

    def rows(t):
        return pl.ds(t, nseq, stride=T)

    cols = []
    for c in range(z.shape[1] // LANES):
        cs = slice(c * LANES, (c + 1) * LANES)
        w0, w1, w2 = w[0:1, cs], w[1:2, cs], w[2:3, cs]
        z_scr[c] = z[:, cs]
        y_scr[c] = y[:, cs]
        s0 = s_ref[:, 0, cs]
        s1 = s_ref[:, 1, cs]
        z0 = z_scr[c, rows(0), :]
        z1 = z_scr[c, rows(1), :]
        y_scr[c, rows(0), :] = w0 * s0 + w1 * s1 + w2 * z0
        y_scr[c, rows(1), :] = w0 * s1 + w1 * z0 + w2 * z1
        t_ref[:, 0, cs] = z_scr[c, rows(T - 2), :]
        t_ref[:, 1, cs] = z_scr[c, rows(T - 1), :]
        cols.append(y_scr[c])
    return jnp.concatenate(cols, axis=1)


def _ada_kernel(c_ref, w_ref, b_ref, wt_ref, o_ref, wb_ref, s_scr, *, n_seq, reps):
    @pl.when(pl.program_id(0) == 0)
    def _():
        c = c_ref[...]
        s = (c * _sigmoid(c)).astype(BF16)
        shape = (s_scr.shape[0], c.shape[0])
        r = lax.broadcasted_iota(jnp.int32, shape, 0)
        src = jnp.where(r < n_seq * reps, r // reps, r - n_seq * (reps - 1))
        sel = jnp.where(lax.broadcasted_iota(jnp.int32, shape, 1) == src, 1.0, 0.0).astype(BF16)
        s_scr[...] = _dot(sel, s).astype(BF16)

    o_ref[...] = _dot(s_scr[...], w_ref[...].astype(BF16)) + b_ref[...]
    wb_ref[...] = wt_ref[...].T.astype(BF16)


def _ada(c, w, b, wt, n_t, *, n_seq, reps, rows, tn=768, tt=512):
    u, d = c.shape[0] - LANES, c.shape[1]
    r = rows
    n = w.shape[1]
    k = wt.shape[1]
    steps, t_blocks = n // tn, n_t // tt
    assert steps * tn == n and t_blocks * tt == n_t and t_blocks <= steps

    def t_block(j):
        return jnp.minimum(j, t_blocks - 1)

    return pl.pallas_call(
        functools.partial(_ada_kernel, n_seq=n_seq, reps=reps),
        grid=(steps,),
        in_specs=[
            pl.BlockSpec((u, d), lambda j: (0, 0)),
            pl.BlockSpec((d, tn), lambda j: (0, j)),
            pl.BlockSpec((1, tn), lambda j: (0, j)),
            pl.BlockSpec((tt, k), lambda j: (t_block(j), 0)),
        ],
        out_specs=[pl.BlockSpec((r, tn), lambda j: (0, j)),
                   pl.BlockSpec((k, tt), lambda j: (0, t_block(j)))],
        out_shape=[jax.ShapeDtypeStruct((r, n), F32), jax.ShapeDtypeStruct((k, n_t), BF16)],
        scratch_shapes=[pltpu.VMEM((r, d), BF16)],
        compiler_params=_cparams(("arbitrary",)),
        name="ada",
    )(c, w, b.reshape(1, n), wt)


def _cast_side_job(mats, n_steps, index_map):
    specs, shapes = [], []
    for w in mats:
        slab = w.shape[0] // n_steps
        assert slab * n_steps == w.shape[0] and slab % SLAB == 0
        specs.append(pl.BlockSpec((slab, w.shape[1]), index_map))
        shapes.append(jax.ShapeDtypeStruct(w.shape, BF16))
    return specs, shapes


def _cast_slabs(srcs, dsts):
    for src, dst in zip(srcs, dsts):
        dst[...] = src[...].astype(BF16)


def _inproj_kernel(*refs, tpb, period, sample, n_cast=0):
    if sample:
        (x_ref, sh_ref, sc_ref, wb_ref, wc_ref, wh_ref, wq_ref, wk_ref, wv_ref, wo_ref, wg_ref,
         wconv_ref, s_ref,
         yc_ref, q_ref, k_ref, v_ref, o_ref, g_ref, t_ref, u_scr, z_scr, y_scr) = refs
    else:
        (x_ref, sh_ref, sc_ref, wb_ref, wc_ref, wh_ref, wq_ref, wk_ref, wv_ref, wo_ref, wg_ref,
         wconv_ref) = refs[:12]
        cast_in = refs[12:12 + n_cast]
        (yc_ref, q_ref, k_ref, v_ref, o_ref, g_ref, zt_ref) = refs[12 + n_cast:19 + n_cast]
        cast_out = refs[19 + n_cast:19 + 2 * n_cast]
        u_scr, carry_scr = refs[19 + 2 * n_cast:]
    m = pl.program_id(0)
    j = pl.program_id(1)

    if not sample:
        _cast_slabs(cast_in, cast_out)

        carried = _carried_rows(carry_scr, j, lax.rem(m, tpb) == 0)
        rs = u_scr.shape[0] // 2

        def body(first):
            prev = carried
            for i in range(2):
                r = slice(i * rs, (i + 1) * rs)
                if first:
                    u = (_ln(x_ref[r, :]) * (1.0 + sc_ref[0]) + sh_ref[0]).astype(BF16)
                    u_scr[r, :] = u
                    g_ref[0, r, :] = _dot_nt(u, wg_ref[...])
                else:
                    u = u_scr[r, :]
                z = _dot(u, wc_ref[...]) * _dot(u, wh_ref[...])
                yc, prev = _conv3_rows(z, wconv_ref[...], prev)
                yc_ref[0, r, :] = (_dot(u, wb_ref[...]) * yc).astype(BF16)
                q_ref[0, r, :] = _dot(u, wq_ref[...]).astype(BF16)
                k_ref[0, r, :] = (_dot(u, wk_ref[...]) * (HEAD_DIM ** -0.5)).astype(BF16)
                v_ref[0, r, :] = _dot(u, wv_ref[...]).astype(BF16)
                o_ref[0, r, :] = _dot(u, wo_ref[...]).astype(BF16)
            carry_scr[j] = prev
            zt_ref[0] = prev

        pl.when(j == 0)(lambda: body(True))
        pl.when(j > 0)(lambda: body(False))
        return

    @pl.when(j == 0)
    def _():
        x = x_ref[...].reshape(u_scr.shape)
        u = _ln(x) * (1.0 + sc_ref[0]) + sh_ref[0]
        ub = u.astype(BF16)
        u_scr[...] = ub
        g_ref[0] = _dot_nt(ub, wg_ref[...])

    u = u_scr[...]
    z = _dot(u, wc_ref[...]) * _dot(u, wh_ref[...])
    yc = _conv3_sequences(z, wconv_ref[...], s_ref, z_scr, y_scr, t_ref, period)
    half = u.shape[0] // 2
    bg = jnp.concatenate([_dot(u[:half], wb_ref[...]), _dot(u[half:], wb_ref[...])], axis=0)
    yc_ref[0] = (bg * yc).astype(BF16)
    q_ref[0] = _dot(u, wq_ref[...]).astype(BF16)
    k_ref[0] = (_dot(u, wk_ref[...]) * (HEAD_DIM ** -0.5)).astype(BF16)
    v_ref[0] = _dot(u, wv_ref[...]).astype(BF16)
    o_ref[0] = _dot(u, wo_ref[...]).astype(BF16)


def _inproj(x, mod, w_in, w_gate, w_conv, *, tm, tpb, sample, period=0, state=None, cast=()):
    d = x.shape[-1]
    rows = x.size // d
    dc = w_conv.shape[1]
    tn = HEAD_DIM
    nj = dc // tn
    nm = rows // tm
    nseq = nm // tpb
    r = tm if sample else 1

    def wspec(off):
        return pl.BlockSpec((d, tn), lambda m, j, off=off: (0, off * nj + j))

    in_specs = [
        (pl.BlockSpec((tm // period, period, d), lambda m, j: (0, 0, 0)) if sample
         else pl.BlockSpec((tm, d), lambda m, j: (m, 0))),
        pl.BlockSpec((1, r, d), lambda m, j: (m // tpb, 0, 0)),
        pl.BlockSpec((1, r, d), lambda m, j: (m // tpb, 0, 1)),
        wspec(0), wspec(1), wspec(2), wspec(3), wspec(4), wspec(5), wspec(6),
        pl.BlockSpec((GATE_LANES, d), lambda m, j: (0, 0)),
        pl.BlockSpec((CONV_K, tn), lambda m, j: (0, j)),
    ]
    args = [x, mod, mod, w_in, w_in, w_in, w_in, w_in, w_in, w_in, w_gate, w_conv]
    scratch = [pltpu.VMEM((tm, d), BF16)]
    act = pl.BlockSpec((1, tm, tn), lambda m, j: (m // tpb, m % tpb, j))
    out_specs = [act, act, act, act, act,
                 pl.BlockSpec((1, tm, GATE_LANES), lambda m, j: (m // tpb, m % tpb, 0))]
    out_shape = [jax.ShapeDtypeStruct((nseq, tpb * tm, dc), BF16)] * 5 + [
        jax.ShapeDtypeStruct((nseq, tpb * tm, GATE_LANES), F32)]
    if sample:
        assert nm == 1
        st = pl.BlockSpec((tm // period, CONV_K - 1, tn), lambda m, j: (0, 0, j))
        in_specs.append(st)
        args.append(state)
        out_specs.append(st)
        out_shape.append(jax.ShapeDtypeStruct((tm // period, CONV_K - 1, dc), F32))
        scratch += [pltpu.VMEM((tn // LANES, tm, LANES), F32)] * 2
    else:
        out_specs.append(pl.BlockSpec((1, STATE_ROWS, tn), lambda m, j: (m, 0, j)))
        out_shape.append(jax.ShapeDtypeStruct((nm, STATE_ROWS, dc), F32))
        scratch.append(pltpu.VMEM((nj, STATE_ROWS, tn), F32))
        cast_specs, cast_shapes = _cast_side_job(cast, nm * nj, lambda m, j: (m * nj + j, 0))
        in_specs += cast_specs
        args += list(cast)
        out_specs += cast_specs
        out_shape += cast_shapes
    return pl.pallas_call(
        functools.partial(_inproj_kernel, tpb=tpb, period=period, sample=sample, n_cast=len(cast)),
        grid=(nm, nj),
        in_specs=in_specs,
        out_specs=out_specs,
        out_shape=out_shape,
        scratch_shapes=scratch,
        compiler_params=_cparams(("arbitrary", "arbitrary")),
        name="inproj_sample" if sample else "inproj_prompt",
    )(*args)


def _split3(x):
    hi = x.astype(BF16)
    r1 = x - hi.astype(F32)
    mid = r1.astype(BF16)
    lo = (r1 - mid.astype(F32)).astype(BF16)
    return hi, mid, lo


def _head_out(hh, o, wmh):
    return (_sigmoid(o.astype(F32)) * (_ln(hh) * wmh)).astype(BF16)


def _split2(x):
    hi = x.astype(BF16)
    return hi, (x - hi.astype(F32)).astype(BF16)


def _rowsum(x, ones):
    hi, lo = _split2(x)
    return _dot(hi, ones) + _dot(lo, ones)


def _rep2(x):
    return jnp.concatenate([x, x], axis=1)


def _mlstm_chunk_kernel(*refs, B, L, n_cast):
    (q_ref, k_ref, v_ref, o_ref, g_ref, bg_ref, wmh_ref) = refs[:7]
    (y_ref, c_ref, n_ref, m_ref) = refs[7 + n_cast:11 + n_cast]
    nrep_scr = refs[-1]
    _cast_slabs(refs[7:7 + n_cast], refs[11 + n_cast:11 + 2 * n_cast])
    _mlstm_chunk(q_ref, k_ref, v_ref, o_ref, g_ref, bg_ref, wmh_ref, y_ref, c_ref, n_ref, m_ref,
                 nrep_scr, B=B, L=L)


def _mlstm_chunk(q_ref, k_ref, v_ref, o_ref, g_ref, bg_ref, wmh_ref,
                 y_ref, c_ref, n_ref, m_ref, nrep_scr, *, B, L):
    step = pl.program_id(0)

    @pl.when(step == 0)
    def _():
        c_ref[...] = jnp.zeros_like(c_ref)
        m_ref[...] = jnp.zeros_like(m_ref)
        nrep_scr[...] = jnp.zeros_like(nrep_scr)

    row = lax.broadcasted_iota(jnp.int32, (L, L), 0)
    col = lax.broadcasted_iota(jnp.int32, (L, L), 1)
    causal = col <= row
    tril = jnp.where(causal, 1.0, 0.0).astype(BF16)
    ones_l = jnp.ones((L, LANES), BF16)
    ones_d = jnp.ones((HEAD_DIM, LANES), BF16)
    inv_d = 1.0 / HEAD_DIM
    tn = (((0,), (0,)), ((), ()))

    heads =[(b, h) for b in range(B) for h in range(N_HEADS)]

    def hsl(h):
        return slice(h * HEAD_DIM, (h + 1) * HEAD_DIM)

    gate = []
    for b in range(B):
        g = g_ref[b] + bg_ref[0:1, :]
        hi, mid, lo = _split3(_log_sigmoid(g))
        bcum = _dot(tril, hi) + _dot(tril, mid) + _dot(tril, lo)
        gate.append((g, g.T, bcum, bcum.T, m_ref[b]))

    st = []
    for b, h in heads:
        g, g_t, bcum, bcum_t, m_all = gate[b]
        bc = jnp.broadcast_to(bcum[:, N_HEADS + h:N_HEADS + h + 1], (L, LANES))
        li = jnp.broadcast_to(g[:, h:h + 1], (L, LANES))
        br = bcum_t[N_HEADS + h:N_HEADS + h + 1, :]
        m_prev = m_all[h:h + 1, :]
        a = bc + m_prev
        dlog = jnp.where(causal, bc - br + g_t[h:h + 1, :], NEG)
        mt = jnp.maximum(a, jnp.max(dlog, axis=1, keepdims=True))
        st.append(dict(bc=bc, li=li, m_prev=m_prev, mt=mt, dw=jnp.exp(dlog - mt),
                       inter=jnp.exp(a - mt)))

    for (b, h), e in zip(heads, st):
        e["s"] = lax.dot_general(q_ref[b, :, hsl(h)], k_ref[b, :, hsl(h)], (((1,), (1,)), ((), ())),
                                 preferred_element_type=F32) * e["dw"]

    for i, ((b, h), e) in enumerate(zip(heads, st)):
        s_hi, s_lo = _split2(e["s"])
        cn = jnp.concatenate([c_ref[b, h], nrep_scr[i]], axis=1).astype(BF16)
        qc = _dot(q_ref[b, :, hsl(h)], cn)
        num = _dot(s_hi, v_ref[b, :, hsl(h)]) + _rep2(e["inter"]) * qc[:, :HEAD_DIM]
        den = _dot(s_hi, ones_l) + _dot(s_lo, ones_l) + e["inter"] * qc[:, HEAD_DIM:]
        rden = 1.0 / jnp.maximum(jnp.abs(den), jnp.exp(-e["mt"]))
        e["hh"] = num * _rep2(rden)

    for (b, h), e in zip(heads, st):
        hh = e["hh"]
        xc = hh - _rep2(_rowsum(hh, ones_d) * inv_d)
        rstd = lax.rsqrt(_rowsum(xc * xc, ones_d) * inv_d + LN_EPS)
        y = _sigmoid(o_ref[b, :, hsl(h)].astype(F32)) * (xc * _rep2(rstd) * wmh_ref[0:1, hsl(h)])
        y_ref[b, :, hsl(h)] = y.astype(BF16)

    m_rows = []
    for i, ((b, h), e) in enumerate(zip(heads, st)):
        kh = k_ref[b, :, hsl(h)]
        m_new = e["mt"][L - 1:L, :]
        b_last = e["bc"][L - 1:L, :]
        wc = jnp.exp(b_last - e["bc"] + e["li"] - m_new)
        dc = jnp.exp(b_last + e["m_prev"] - m_new)
        vw = (v_ref[b, :, hsl(h)].astype(F32) * _rep2(wc)).astype(BF16)
        c_ref[b, h] = _rep2(dc) * c_ref[b, h] + lax.dot_general(kh, vw, tn, preferred_element_type=F32)
        wc_hi, wc_lo = _split2(wc)
        nrep_scr[i] = (dc * nrep_scr[i] + lax.dot_general(kh, wc_hi, tn, preferred_element_type=F32)
                       + lax.dot_general(kh, wc_lo, tn, preferred_element_type=F32))
        m_rows.append(m_new)

    for b in range(B):
        m_ref[b] = jnp.concatenate(m_rows[b * N_HEADS:(b + 1) * N_HEADS], axis=0)

    @pl.when(step == pl.num_programs(0) - 1)
    def _():
        for b in range(B):
            n_ref[b] = jnp.concatenate(
                [nrep_scr[b * N_HEADS + h].T[0:1, :] for h in range(N_HEADS)], axis=0)


def _mlstm_prompt(q, k, v, o, gates, bg, wmh, *, L, cast=()):
    b, t, dm = q.shape
    assert L == LANES
    act = pl.BlockSpec((b, L, dm), lambda c: (0, c, 0))
    whole = lambda *shape: pl.BlockSpec(shape, lambda c: (0,) * len(shape))
    cast_specs, cast_shapes = _cast_side_job(cast, t // L, lambda c: (c, 0))
    return pl.pallas_call(
        functools.partial(_mlstm_chunk_kernel, B=b, L=L, n_cast=len(cast)),
        grid=(t // L,),
        in_specs=[act, act, act, act,
                  pl.BlockSpec((b, L, GATE_LANES), lambda c: (0, c, 0)),
                  _row_spec(GATE_LANES), _row_spec(dm)] + cast_specs,
        out_specs=[act,
                   whole(b, N_HEADS, HEAD_DIM, HEAD_DIM),
                   whole(b, N_HEADS, HEAD_DIM),
                   whole(b, N_HEADS, GATE_LANES)] + cast_specs,
        out_shape=[jax.ShapeDtypeStruct((b, t, dm), BF16),
                   jax.ShapeDtypeStruct((b, N_HEADS, HEAD_DIM, HEAD_DIM), F32),
                   jax.ShapeDtypeStruct((b, N_HEADS, HEAD_DIM), F32),
                   jax.ShapeDtypeStruct((b, N_HEADS, GATE_LANES), F32)] + cast_shapes,
        scratch_shapes=[pltpu.VMEM((b * N_HEADS, HEAD_DIM, LANES), F32)],
        compiler_params=_cparams(("arbitrary",)),
        name="mlstm_prompt",
    )(q, k, v, o, gates, bg, wmh, *cast)


def _mlstm_step_kernel(q_ref, k_ref, v_ref, o_ref, g_ref, bg_ref, mrow_ref, nrow_ref,
                       c0_ref, n0_ref, wmh_ref,
                       y_ref, c_ref, n_ref, m_ref, *, bb, T):
    R = bb * T
    per_slab = SLAB // T
    t = lax.rem(lax.broadcasted_iota(jnp.int32, (R, GATE_LANES), 0), T)

    def down(x, d):
        return pltpu.roll(x, d, 0)

    def up(x, d):
        return pltpu.roll(x, x.shape[0] - d, 0)

    def seg_last(x):
        out = x
        for d in range(1, T):
            out = jnp.where(t == T - 1 - d, up(x, d), out)
        return out

    g = g_ref[...] + bg_ref[0:1, :]
    li = pltpu.roll(g, N_HEADS, 1)
    lf = _log_sigmoid(g)
    b = lf
    for d in range(1, T):
        b = b + jnp.where(t >= d, down(lf, d), 0.0)
    m_prev = mrow_ref[...]
    a = b + m_prev
    dl = [li] + [jnp.where(t >= d, b - down(b, d) + down(li, d), NEG) for d in range(1, T)]
    mt = a
    for d in range(T):
        mt = jnp.maximum(mt, dl[d])
    dw = [jnp.exp(dl[d] - mt) for d in range(T)]
    inter = jnp.exp(a - mt)
    emt = jnp.exp(-mt)
    m_new = seg_last(mt)
    b_last = seg_last(b)
    wc = jnp.exp(b_last - b + li - m_new)
    dc = jnp.exp(b_last + m_prev - m_new)

    row_s = lax.broadcasted_iota(jnp.int32, (SLAB, HEAD_DIM), 0)
    row_r = lax.broadcasted_iota(jnp.int32, (R, HEAD_DIM), 0)

    for h in range(N_HEADS):
        hs = slice(h * HEAD_DIM, (h + 1) * HEAD_DIM)
        ln = N_HEADS + h

        def col(x):
            return x[:, ln:ln + 1]

        qb = q_ref[:, hs]
        kb = k_ref[:, hs]
        vb = v_ref[:, hs]
        qf = qb.astype(F32)
        kf = kb.astype(F32)
        vf = vb.astype(F32)
        num = jnp.zeros((R, HEAD_DIM), F32)
        den = jnp.zeros((R, 1), F32)
        for d in range(T):
            kd = kf if d == 0 else down(kf, d)
            vd = vf if d == 0 else down(vf, d)
            sw = jnp.sum(qf * kd, axis=1, keepdims=True) * col(dw[d])
            num = num + sw * vd
            den = den + sw

        qc_slabs = []
        for si in range(R // SLAB):
            q16 = qb[si * SLAB:(si + 1) * SLAB]
            acc = jnp.zeros((SLAB, HEAD_DIM), F32)
            for bl in range(per_slab):
                bi = si * per_slab + bl
                r = _dot(q16, c0_ref[bi, h].astype(BF16))
                acc = jnp.where(row_s // T == bl, r, acc)
            qc_slabs.append(acc)
        qc = jnp.concatenate(qc_slabs, axis=0)
        qn = jnp.sum(qf * nrow_ref[:, hs], axis=1, keepdims=True)
        num = num + col(inter) * qc
        den = den + col(inter) * qn
        hh = num / jnp.maximum(jnp.abs(den), col(emt))
        y_ref[:, hs] = _head_out(hh, o_ref[:, hs], wmh_ref[0:1, hs])

        wk = kf * col(wc)
        vw = vf * col(wc)
        for si in range(R // SLAB):
            k16 = kb[si * SLAB:(si + 1) * SLAB]
            vw16 = vw[si * SLAB:(si + 1) * SLAB]
            for bl in range(per_slab):
                bi = si * per_slab + bl
                last = bi * T + T - 1
                vwb = jnp.where(row_s // T == bl, vw16, 0.0).astype(BF16)
                dcb = dc[last:last + 1, ln:ln + 1]
                c_ref[bi, h] = dcb * c0_ref[bi, h] + lax.dot_general(
                    k16, vwb, (((0,), (0,)), ((), ())), preferred_element_type=F32)
                n_ref[bi, h:h + 1, :] = dcb * n0_ref[bi, h:h + 1, :] + jnp.sum(
                    jnp.where(row_r // T == bi, wk, 0.0), axis=0, keepdims=True)
                m_ref[bi, h:h + 1, :] = jnp.broadcast_to(
                    m_new[last:last + 1, ln:ln + 1], (1, GATE_LANES))


def _mlstm_sample(q, k, v, o, gates, bg, mrow, nrow, c0, n0, wmh, *, bb, T):
    rows, dm = q.shape
    nb = c0.shape[0]
    R = bb * T
    act = pl.BlockSpec((R, dm), lambda i: (i, 0))
    gat = pl.BlockSpec((R, GATE_LANES), lambda i: (i, 0))
    cspec = pl.BlockSpec((bb, N_HEADS, HEAD_DIM, HEAD_DIM), lambda i: (i, 0, 0, 0))
    nspec = pl.BlockSpec((bb, N_HEADS, HEAD_DIM), lambda i: (i, 0, 0))
    return pl.pallas_call(
        functools.partial(_mlstm_step_kernel, bb=bb, T=T),
        grid=(nb // bb,),
        in_specs=[act, act, act, act, gat,
                  _row_spec(GATE_LANES),
                  gat,
                  pl.BlockSpec((R, dm), lambda i: (i, 0)),
                  cspec, nspec,
                  _row_spec(dm)],
        out_specs=[act, cspec, nspec,
                   pl.BlockSpec((bb, N_HEADS, GATE_LANES), lambda i: (i, 0, 0))],
        out_shape=[jax.ShapeDtypeStruct((rows, dm), BF16),
                   jax.ShapeDtypeStruct(c0.shape, F32),
                   jax.ShapeDtypeStruct(n0.shape, F32),
                   jax.ShapeDtypeStruct((nb, N_HEADS, GATE_LANES), F32)],
        compiler_params=_cparams(("arbitrary",)),
        name="mlstm_sample",
    )(q, k, v, o, gates, bg, mrow, nrow, c0, n0, wmh)


def _outproj_kernel(yc_ref, ym_ref, wc_ref, wm_ref, x_ref, g1_ref, lg_ref, lb_ref, o_ref,
                    *, alpha, splits):
    rs = o_ref.shape[0] // splits
    x_rows = x_ref[...].reshape(o_ref.shape) if len(x_ref.shape) == 3 else None
    for i in range(splits):
        r = slice(i * rs, (i + 1) * rs)
        g1 = g1_ref[0] if g1_ref.shape[1] == 1 else g1_ref[0, r, :]
        x = x_ref[r, :] if x_rows is None else x_rows[r]
        mix = _dot(yc_ref[0, r, :], wc_ref[...]) + _dot(ym_ref[0, r, :], wm_ref[...])
        o_ref[r, :] = _ln(alpha * x + (1.0 + g1) * mix) * lg_ref[0:1, :] + lb_ref[0:1, :]


def _outproj(yc, ym, w_out, x, mod, ln_g, ln_b, *, tm, tpb, alpha, splits=2):
    d = x.shape[-1]
    rows = x.size // d
    dc = yc.shape[-1]
    dm = ym.shape[-1]
    r = 1 if mod.shape[1] == 1 else tm
    vec = _row_spec(d)
    x_spec = (pl.BlockSpec((tm, d), lambda m: (m, 0)) if x.ndim == 2
              else pl.BlockSpec(x.shape, lambda m: (0, 0, 0)))
    return pl.pallas_call(
        functools.partial(_outproj_kernel, alpha=alpha, splits=splits),
        grid=(rows // tm,),
        in_specs=[pl.BlockSpec((1, tm, dc), lambda m: (m // tpb, m % tpb, 0)),
                  pl.BlockSpec((1, tm, dm), lambda m: (m // tpb, m % tpb, 0)),
                  pl.BlockSpec((dc, d), lambda m: (0, 0)),
                  pl.BlockSpec((dm, d), lambda m: (dc // dm, 0)),
                  x_spec,
                  pl.BlockSpec((1, r, d), lambda m: (m // tpb, 0, 2)),
                  vec, vec],
        out_specs=pl.BlockSpec((tm, d), lambda m: (m, 0)),
        out_shape=jax.ShapeDtypeStruct((rows, d), F32),
        compiler_params=_cparams(("arbitrary",)),
        name="outproj",
    )(yc, ym, w_out, w_out, x, mod, ln_g, ln_b)


def _ffn_kernel(*refs, nf, tpb, period, sample, alpha, splits):
    if sample:
        (x_ref, sh_ref, sc_ref, g2_ref, wa_ref, wg_ref, wconv_ref, wd_ref, lg_ref, lb_ref,
         s_ref, y_ref, t_ref, u_scr, z_scr, y_scr, acc_scr) = refs
    else:
        (x_ref, sh_ref, sc_ref, g2_ref, wa_ref, wg_ref, wconv_ref, wd_ref, lg_ref, lb_ref,
         y_ref, at_ref, u_scr, carry_scr) = refs
    m = pl.program_id(0)
    f = pl.program_id(1)

    if not sample:
        carried = _carried_rows(carry_scr, f, lax.rem(m, tpb) == 0)
        rs = u_scr.shape[0] // splits

        def body(first, last):
            prev = carried
            for i in range(splits):
                r = slice(i * rs, (i + 1) * rs)
                if first:
                    u = (_ln(x_ref[r, :]) * (1.0 + sc_ref[0]) + sh_ref[0]).astype(BF16)
                    u_scr[r, :] = u
                else:
                    u = u_scr[r, :]
                ac, prev = _conv3_rows(_dot(u, wa_ref[...]), wconv_ref[...], prev)
                hcur = (ac * _sigmoid(ac) * _dot(u, wg_ref[...])).astype(BF16)
                acc = _dot(hcur, wd_ref[...])
                if not first:
                    acc = y_ref[r, :] + acc
                if last:
                    acc = (_ln(alpha * x_ref[r, :] + (1.0 + g2_ref[0]) * acc)
                           * lg_ref[0:1, :] + lb_ref[0:1, :])
                y_ref[r, :] = acc
            carry_scr[f] = prev
            at_ref[0] = prev

        if nf == 1:
            body(True, True)
        else:
            pl.when(f == 0)(lambda: body(True, False))
            if nf > 2:
                pl.when(jnp.logical_and(f > 0, f < nf - 1))(lambda: body(False, False))
            pl.when(f == nf - 1)(lambda: body(False, True))
        return

    @pl.when(f == 0)
    def _():
        u = _ln(x_ref[...]) * (1.0 + sc_ref[0]) + sh_ref[0]
        u_scr[...] = u.astype(BF16)
        acc_scr[...] = jnp.zeros_like(acc_scr)

    u = u_scr[...]
    a = _dot(u, wa_ref[...])
    ac = _conv3_sequences(a, wconv_ref[...], s_ref, z_scr, y_scr, t_ref, period)
    hcur = (ac * _sigmoid(ac) * _dot(u, wg_ref[...])).astype(BF16)
    acc_scr[...] += _dot(hcur, wd_ref[...])

    @pl.when(f == nf - 1)
    def _():
        y = _ln(alpha * x_ref[...] + (1.0 + g2_ref[0]) * acc_scr[...]) * lg_ref[0:1, :] + lb_ref[0:1, :]
        y_ref[...] = y.reshape(y_ref.shape)


def _ffn(x, mod, w_up, w_conv, w_down, ln_g, ln_b, *, tm, tpb, tf, sample, alpha, period=0,
         splits=1, state=None):
    rows, d = x.shape
    ff = w_down.shape[0]
    nf = ff // tf
    nm = rows // tm
    r = tm if sample else 1
    vec = _row_spec(d)
    in_specs = [
        pl.BlockSpec((tm, d), lambda m, f: (m, 0)),
        pl.BlockSpec((1, r, d), lambda m, f: (m // tpb, 0, 3)),
        pl.BlockSpec((1, r, d), lambda m, f: (m // tpb, 0, 4)),
        pl.BlockSpec((1, r, d), lambda m, f: (m // tpb, 0, 5)),
        pl.BlockSpec((d, tf), lambda m, f: (0, f)),
        pl.BlockSpec((d, tf), lambda m, f: (0, nf + f)),
        pl.BlockSpec((CONV_K, tf), lambda m, f: (0, f)),
        pl.BlockSpec((tf, d), lambda m, f: (f, 0)),
        vec, vec,
    ]
    args = [x, mod, mod, mod, w_up, w_up, w_conv, w_down, ln_g, ln_b]
    scratch = [pltpu.VMEM((tm, d), BF16)]
    out_specs = [pl.BlockSpec((tm, d), lambda m, f: (m, 0))]
    out_shape = [jax.ShapeDtypeStruct((rows, d), F32)]
    if sample:
        assert nm == 1
        st = pl.BlockSpec((tm // period, CONV_K - 1, tf), lambda m, f: (0, 0, f))
        in_specs.append(st)
        args.append(state)
        out_specs.append(st)
        out_shape.append(jax.ShapeDtypeStruct((tm // period, CONV_K - 1, ff), F32))
        scratch += [pltpu.VMEM((tf // LANES, tm, LANES), F32)] * 2 + [pltpu.VMEM((tm, d), F32)]
        out_specs[0] = pl.BlockSpec((tm // period, period, d), lambda m, f: (0, 0, 0))
        out_shape[0] = jax.ShapeDtypeStruct((tm // period, period, d), F32)
    else:
        out_specs.append(pl.BlockSpec((1, STATE_ROWS, tf), lambda m, f: (m, 0, f)))
        out_shape.append(jax.ShapeDtypeStruct((nm, STATE_ROWS, ff), F32))
        scratch.append(pltpu.VMEM((nf, STATE_ROWS, tf), F32))
    return pl.pallas_call(
        functools.partial(_ffn_kernel, nf=nf, tpb=tpb, period=period, sample=sample, alpha=alpha,
                          splits=splits),
        grid=(nm, nf),
        in_specs=in_specs,
        out_specs=out_specs,
        out_shape=out_shape,
        scratch_shapes=scratch,
        compiler_params=_cparams(("arbitrary", "arbitrary")),
        name="ffn_sample" if sample else "ffn_prompt",
    )(*args)


def _layer_prompt(x, mod, wts, yc, ztail, mlstm_out, *, alpha):
    B, T, D = x.shape
    (_, _, _, _, _, w_out, ln1_g, ln1_b, w_up, w_fconv, w_down, ln2_g, ln2_b) = wts
    tm = PROMPT_TM
    tpb = T // tm
    x2 = x.reshape(B * T, D)
    ym, C, n, m = mlstm_out
    x1 = _outproj(yc, ym, w_out, x2, mod, ln1_g, ln1_b, tm=OUTPROJ_TM, tpb=T // OUTPROJ_TM,
                  alpha=alpha)
    y, atail = _ffn(x1, mod, w_up, w_fconv, w_down, ln2_g, ln2_b,
                    tm=tm, tpb=tpb, tf=FFN_TF, sample=False, alpha=alpha, splits=2)
    return (y.reshape(B, T, D), ztail[tpb - 1::tpb, STATE_ROWS - 2:], C, n, m[..., 0],
            atail[tpb - 1::tpb, STATE_ROWS - 2:])


def _layer_sample(x, mod, conv_buf, C0, n0, m0, ffn_buf, wts, *, alpha):
    B, T, D = x.shape
    (w_in, w_gate, bg, w_conv, wmh, w_out, ln1_g, ln1_b, w_up, w_fconv, w_down, ln2_g, ln2_b) = wts
    rows = B * T
    yc, q, k, v, o, gates, conv_new = _inproj(x, mod, w_in, w_gate, w_conv, tm=rows, tpb=1,
                                              sample=True, period=T, state=conv_buf)
    mrow = jnp.pad(jnp.repeat(m0, T, axis=0), ((0, 0), (N_HEADS, GATE_LANES - 2 * N_HEADS)))
    nrow = jnp.repeat(n0.reshape(B, N_HEADS * HEAD_DIM), T, axis=0)
    ym, C, n, m = _mlstm_sample(q[0], k[0], v[0], o[0], gates[0], bg, mrow, nrow, C0, n0, wmh,
                                bb=SAMPLE_SEQS, T=T)
    x1 = _outproj(yc, ym[None], w_out, x, mod, ln1_g, ln1_b, tm=rows, tpb=1, alpha=alpha)
    y, ffn_new = _ffn(x1, mod, w_up, w_fconv, w_down, ln2_g, ln2_b, tm=rows, tpb=1, tf=FFN_TF,
                      sample=True, alpha=alpha, period=T, state=ffn_buf)
    return y, conv_new, C, n, m[..., 0], ffn_new


def kernel(x_prompt, x_sample, c_prompt, c_sample, state_conv, state_mlstm_C, state_mlstm_n,
           state_mlstm_m, state_ffn_conv, w_ada, b_ada, w_in, b_gate, w_conv, w_mh_norm, w_out,
           ln1_g, ln1_b, w_up, w_ffn_conv, w_down, ln2_g, ln2_b):
    depth = w_in.shape[0]
    alpha = (2 * depth) ** 0.25
    Bp = x_prompt.shape[0]
    Bs, Ts, D = x_sample.shape
    dc = w_conv.shape[-1]
    dm = w_mh_norm.shape[-1]
    n_main = 3 * dc + 4 * dm
    assert dc == dm == N_HEADS * HEAD_DIM and Ts >= CONV_K - 1 and SLAB % Ts == 0

    xp, xs = x_prompt, x_sample
    outs_p = [[] for _ in range(5)]
    outs_s = [[] for _ in range(5)]
    for l in range(depth):
        c_all = jnp.pad(jnp.concatenate([c_sample, c_prompt], axis=0),
                        ((0, -(Bs + Bp) % LANES + LANES), (0, 0)))
        w_in_t = jnp.swapaxes(w_in[l], 0, 1)
        mod, w_in_b = _ada(c_all, w_ada[l], b_ada[l], w_in_t, n_main, n_seq=Bs, reps=Ts,
                           rows=Bs * Ts + Bp)
        mod_s = mod.reshape(1, Bs * Ts + Bp, 6 * D)
        mod_p = mod[Bs * Ts:].reshape(Bp, 1, 6 * D)
        w_gate = jnp.pad(w_in_t[n_main:], ((0, 2 * GATE_LANES - 2 * N_HEADS), (0, 0))).astype(BF16)
        bg = _row_param(jnp.pad(b_gate[l], (0, GATE_LANES - 2 * N_HEADS)))
        wmh = _row_param(w_mh_norm[l])
        Tp = xp.shape[1]
        yc, q, k, v, o, gates, ztail, w_up_b = _inproj(
            xp.reshape(Bp * Tp, D), mod_p, w_in_b, w_gate, w_conv[l], tm=PROMPT_TM,
            tpb=Tp // PROMPT_TM, sample=False, cast=(w_up[l],))
        *mlstm_out, w_down_b, w_out_b = _mlstm_prompt(q, k, v, o, gates, bg, wmh, L=MLSTM_CHUNK,
                                                      cast=(w_down[l], w_out[l]))
        wts = (
            w_in_b,
            w_gate,
            bg,
            w_conv[l],
            wmh,
            w_out_b,
            _row_param(ln1_g[l]), _row_param(ln1_b[l]),
            w_up_b,
            w_ffn_conv[l],
            w_down_b,
            _row_param(ln2_g[l]), _row_param(ln2_b[l]),
        )
        xs, *st_s = _layer_sample(xs, mod_s, state_conv[l], state_mlstm_C[l], state_mlstm_n[l],
                                  state_mlstm_m[l], state_ffn_conv[l], wts, alpha=alpha)
        xp, *st_p = _layer_prompt(xp, mod_p, wts, yc, ztail, mlstm_out, alpha=alpha)
        for acc, val in zip(outs_p, st_p):
            acc.append(val)
        for acc, val in zip(outs_s, st_s):
            acc.append(val)
    return (xp.astype(x_prompt.dtype), xs.astype(x_sample.dtype),
            *[jnp.stack(a) for a in outs_p], *[jnp.stack(a) for a in outs_s])
```

```python
import functools

import jax
import jax.numpy as jnp
from jax import lax
from jax.experimental import pallas as pl
from jax.experimental.pallas import tpu as pltpu

F32 = jnp.float32
BF16 = jnp.bfloat16

N_HEADS = 4
HEAD_DIM = 256
CONV_K = 3
LN_EPS = 1e-5
NEG = -1e30
LANES = 128
GATE_LANES = LANES
STATE_ROWS = 8
SLAB = 16
VMEM_LIMIT = 56 * 1024 * 1024

PROMPT_TM = 1024
OUTPROJ_TM = 512
FFN_TF = 512
MLSTM_CHUNK = LANES
SAMPLE_SEQS = 8


def _cparams(sem):
    return pltpu.CompilerParams(dimension_semantics=sem, vmem_limit_bytes=VMEM_LIMIT)


def _ln(x):
    mu = jnp.mean(x, axis=-1, keepdims=True)
    xc = x - mu
    var = jnp.mean(xc * xc, axis=-1, keepdims=True)
    return xc * lax.rsqrt(var + LN_EPS)


def _log_sigmoid(x):
    return jnp.minimum(x, 0.0) - jnp.log1p(jnp.exp(-jnp.abs(x)))


def _sigmoid(x):
    return 1.0 / (1.0 + jnp.exp(-x))


def _dot(a, b):
    return jnp.dot(a, b, preferred_element_type=F32)


def _dot_nt(a, b):
    return lax.dot_general(a, b, (((1,), (1,)), ((), ())), preferred_element_type=F32)


def _conv3_rows(z, w, prev):
    p0 = prev[STATE_ROWS - 2:STATE_ROWS - 1]
    p1 = prev[STATE_ROWS - 1:STATE_ROWS]
    t = lax.broadcasted_iota(jnp.int32, z.shape, 0)
    z1 = jnp.where(t >= 1, pltpu.roll(z, 1, 0), p1)
    z2 = jnp.where(t >= 2, pltpu.roll(z, 2, 0), jnp.where(t == 0, p0, p1))
    return w[0:1] * z2 + w[1:2] * z1 + w[2:3] * z, z[z.shape[0] - STATE_ROWS:]


def _carried_rows(carry_ref, idx, first):
    @pl.when(first)
    def _():
        carry_ref[idx] = jnp.zeros(carry_ref.shape[1:], F32)

    return carry_ref[idx]


def _conv3_sequences(z, w, s_ref, z_scr, y_scr, t_ref, T):
    nseq = z.shape[0] // T
    y = w[0:1] * pltpu.roll(z, 2, 0) + w[1:2] * pltpu.roll(z, 1, 0) + w[2:3] * z

    def rows(t):
        return pl.ds(t, nseq, stride=T)

    cols = []
    for c in range(z.shape[1] // LANES):
        cs = slice(c * LANES, (c + 1) * LANES)
        w0, w1, w2 = w[0:1, cs], w[1:2, cs], w[2:3, cs]
        z_scr[c] = z[:, cs]
        y_scr[c] = y[:, cs]
        s0 = s_ref[:, 0, cs]
        s1 = s_ref[:, 1, cs]
        z0 = z_scr[c, rows(0), :]
        z1 = z_scr[c, rows(1), :]
        y_scr[c, rows(0), :] = w0 * s0 + w1 * s1 + w2 * z0
        y_scr[c, rows(1), :] = w0 * s1 + w1 * z0 + w2 * z1
        t_ref[:, 0, cs] = z_scr[c, rows(T - 2), :]
        t_ref[:, 1, cs] = z_scr[c, rows(T - 1), :]
        cols.append(y_scr[c])
    return jnp.concatenate(cols, axis=1)


def _ada_kernel(c_ref, w_ref, b_ref, wt_ref, o_ref, wb_ref, s_scr, *, n_seq, reps):
    @pl.when(pl.program_id(0) == 0)
    def _():
        c = c_ref[...]
        s = (c * _sigmoid(c)).astype(BF16)
        shape = (s_scr.shape[0], c.shape[0])
        r = lax.broadcasted_iota(jnp.int32, shape, 0)
        src = jnp.where(r < n_seq * reps, r // reps, r - n_seq * (reps - 1))
        sel = jnp.where(lax.broadcasted_iota(jnp.int32, shape, 1) == src, 1.0, 0.0).astype(BF16)
        s_scr[...] = _dot(sel, s).astype(BF16)

    o_ref[...] = _dot(s_scr[...], w_ref[...].astype(BF16)) + b_ref[...]
    wb_ref[...] = wt_ref[...].T.astype(BF16)


def _ada(c, w, b, wt, n_t, *, n_seq, reps, rows, tn=768, tt=512):
    u, d = c.shape
    r = rows
    n = w.shape[1]
    k = wt.shape[1]
    steps, t_blocks = n // tn, n_t // tt
    assert steps * tn == n and t_blocks * tt == n_t and t_blocks <= steps

    def t_block(j):
        return jnp.minimum(j, t_blocks - 1)

    return pl.pallas_call(
        functools.partial(_ada_kernel, n_seq=n_seq, reps=reps),
        grid=(steps,),
        in_specs=[
            pl.BlockSpec((u, d), lambda j: (0, 0)),
            pl.BlockSpec((d, tn), lambda j: (0, j)),
            pl.BlockSpec((1, tn), lambda j: (0, j)),
            pl.BlockSpec((tt, k), lambda j: (t_block(j), 0)),
        ],
        out_specs=[pl.BlockSpec((r, tn), lambda j: (0, j)),
                   pl.BlockSpec((k, tt), lambda j: (0, t_block(j)))],
        out_shape=[jax.ShapeDtypeStruct((r, n), F32), jax.ShapeDtypeStruct((k, n_t), BF16)],
        scratch_shapes=[pltpu.VMEM((r, d), BF16)],
        compiler_params=_cparams(("arbitrary",)),
        name="ada",
    )(c, w, b.reshape(1, n), wt)


def _cast_side_job(mats, n_steps, index_map):
    specs, shapes = [], []
    for w in mats:
        slab = w.shape[0] // n_steps
        assert slab * n_steps == w.shape[0] and slab % SLAB == 0
        specs.append(pl.BlockSpec((slab, w.shape[1]), index_map))
        shapes.append(jax.ShapeDtypeStruct(w.shape, BF16))
    return specs, shapes


def _cast_slabs(srcs, dsts):
    for src, dst in zip(srcs, dsts):
        dst[...] = src[...].astype(BF16)


def _inproj_kernel(*refs, tpb, period, sample, n_cast=0):
    if sample:
        (x_ref, sh_ref, sc_ref, wb_ref, wc_ref, wh_ref, wq_ref, wk_ref, wv_ref, wo_ref, wg_ref,
         wconv_ref, s_ref,
         yc_ref, q_ref, k_ref, v_ref, o_ref, g_ref, t_ref, u_scr, z_scr, y_scr) = refs
    else:
        (x_ref, sh_ref, sc_ref, wb_ref, wc_ref, wh_ref, wq_ref, wk_ref, wv_ref, wo_ref, wg_ref,
         wconv_ref) = refs[:12]
        cast_in = refs[12:12 + n_cast]
        (yc_ref, q_ref, k_ref, v_ref, o_ref, g_ref, zt_ref) = refs[12 + n_cast:19 + n_cast]
        cast_out = refs[19 + n_cast:19 + 2 * n_cast]
        u_scr, carry_scr = refs[19 + 2 * n_cast:]
    m = pl.program_id(0)
    j = pl.program_id(1)

    if not sample:
        _cast_slabs(cast_in, cast_out)

        carried = _carried_rows(carry_scr, j, lax.rem(m, tpb) == 0)
        rs = u_scr.shape[0] // 2

        def body(first):
            prev = carried
            for i in range(2):
                r = slice(i * rs, (i + 1) * rs)
                if first:
                    u = (_ln(x_ref[r, :]) * (1.0 + sc_ref[0]) + sh_ref[0]).astype(BF16)
                    u_scr[r, :] = u
                    g_ref[0, r, :] = _dot_nt(u, wg_ref[...])
                else:
                    u = u_scr[r, :]
                z = _dot(u, wc_ref[...]) * _dot(u, wh_ref[...])
                yc, prev = _conv3_rows(z, wconv_ref[...], prev)
                yc_ref[0, r, :] = (_dot(u, wb_ref[...]) * yc).astype(BF16)
                q_ref[0, r, :] = _dot(u, wq_ref[...]).astype(BF16)
                k_ref[0, r, :] = (_dot(u, wk_ref[...]) * (HEAD_DIM ** -0.5)).astype(BF16)
                v_ref[0, r, :] = _dot(u, wv_ref[...]).astype(BF16)
                o_ref[0, r, :] = _dot(u, wo_ref[...]).astype(BF16)
            carry_scr[j] = prev
            zt_ref[0] = prev

        pl.when(j == 0)(lambda: body(True))
        pl.when(j > 0)(lambda: body(False))
        return

    @pl.when(j == 0)
    def _():
        x = x_ref[...].reshape(u_scr.shape)
        u = _ln(x) * (1.0 + sc_ref[0]) + sh_ref[0]
        ub = u.astype(BF16)
        u_scr[...] = ub
        g_ref[0] = _dot_nt(ub, wg_ref[...])

    u = u_scr[...]
    z = _dot(u, wc_ref[...]) * _dot(u, wh_ref[...])
    yc = _conv3_sequences(z, wconv_ref[...], s_ref, z_scr, y_scr, t_ref, period)
    half = u.shape[0] // 2
    bg = jnp.concatenate([_dot(u[:half], wb_ref[...]), _dot(u[half:], wb_ref[...])], axis=0)
    yc_ref[0] = (bg * yc).astype(BF16)
    q_ref[0] = _dot(u, wq_ref[...]).astype(BF16)
    k_ref[0] = (_dot(u, wk_ref[...]) * (HEAD_DIM ** -0.5)).astype(BF16)
    v_ref[0] = _dot(u, wv_ref[...]).astype(BF16)
    o_ref[0] = _dot(u, wo_ref[...]).astype(BF16)


def _inproj(x, mod, w_in, w_gate, w_conv, *, tm, tpb, sample, period=0, state=None, cast=()):
    d = x.shape[-1]
    rows = x.size // d
    dc = w_conv.shape[1]
    tn = HEAD_DIM
    nj = dc // tn
    nm = rows // tm
    nseq = nm // tpb
    r = tm if sample else 1

    def wspec(off):
        return pl.BlockSpec((d, tn), lambda m, j, off=off: (0, off * nj + j))

    in_specs = [
        (pl.BlockSpec((tm // period, period, d), lambda m, j: (0, 0, 0)) if sample
         else pl.BlockSpec((tm, d), lambda m, j: (m, 0))),
        pl.BlockSpec((1, r, d), lambda m, j: (m // tpb, 0, 0)),
        pl.BlockSpec((1, r, d), lambda m, j: (m // tpb, 0, 1)),
        wspec(0), wspec(1), wspec(2), wspec(3), wspec(4), wspec(5), wspec(6),
        pl.BlockSpec((GATE_LANES, d), lambda m, j: (0, 0)),
        pl.BlockSpec((CONV_K, tn), lambda m, j: (0, j)),
    ]
    args = [x, mod, mod, w_in, w_in, w_in, w_in, w_in, w_in, w_in, w_gate, w_conv]
    scratch = [pltpu.VMEM((tm, d), BF16)]
    act = pl.BlockSpec((1, tm, tn), lambda m, j: (m // tpb, m % tpb, j))
    out_specs = [act, act, act, act, act,
                 pl.BlockSpec((1, tm, GATE_LANES), lambda m, j: (m // tpb, m % tpb, 0))]
    out_shape = [jax.ShapeDtypeStruct((nseq, tpb * tm, dc), BF16)] * 5 + [
        jax.ShapeDtypeStruct((nseq, tpb * tm, GATE_LANES), F32)]
    if sample:
        assert nm == 1
        st = pl.BlockSpec((tm // period, CONV_K - 1, tn), lambda m, j: (0, 0, j))
        in_specs.append(st)
        args.append(state)
        out_specs.append(st)
        out_shape.append(jax.ShapeDtypeStruct((tm // period, CONV_K - 1, dc), F32))
        scratch += [pltpu.VMEM((tn // LANES, tm, LANES), F32)] * 2
    else:
        out_specs.append(pl.BlockSpec((1, STATE_ROWS, tn), lambda m, j: (m, 0, j)))
        out_shape.append(jax.ShapeDtypeStruct((nm, STATE_ROWS, dc), F32))
        scratch.append(pltpu.VMEM((nj, STATE_ROWS, tn), F32))
        cast_specs, cast_shapes = _cast_side_job(cast, nm * nj, lambda m, j: (m * nj + j, 0))
        in_specs += cast_specs
        args += list(cast)
        out_specs += cast_specs
        out_shape += cast_shapes
    return pl.pallas_call(
        functools.partial(_inproj_kernel, tpb=tpb, period=period, sample=sample, n_cast=len(cast)),
        grid=(nm, nj),
        in_specs=in_specs,
        out_specs=out_specs,
        out_shape=out_shape,
        scratch_shapes=scratch,
        compiler_params=_cparams(("arbitrary", "arbitrary")),
        name="inproj_sample" if sample else "inproj_prompt",
    )(*args)


def _split3(x):
    hi = x.astype(BF16)
    r1 = x - hi.astype(F32)
    mid = r1.astype(BF16)
    lo = (r1 - mid.astype(F32)).astype(BF16)
    return hi, mid, lo


def _head_out(hh, o, wmh):
    return (_sigmoid(o.astype(F32)) * (_ln(hh) * wmh)).astype(BF16)


def _split2(x):
    hi = x.astype(BF16)
    return hi, (x - hi.astype(F32)).astype(BF16)


def _rowsum(x, ones):
    hi, lo = _split2(x)
    return _dot(hi, ones) + _dot(lo, ones)


def _rep2(x):
    return jnp.concatenate([x, x], axis=1)


def _mlstm_chunk_kernel(*refs, B, L, n_cast):
    (q_ref, k_ref, v_ref, o_ref, g_ref, bg_ref, wmh_ref) = refs[:7]
    (y_ref, c_ref, n_ref, m_ref) = refs[7 + n_cast:11 + n_cast]
    nrep_scr = refs[-1]
    _cast_slabs(refs[7:7 + n_cast], refs[11 + n_cast:11 + 2 * n_cast])
    _mlstm_chunk(q_ref, k_ref, v_ref, o_ref, g_ref, bg_ref, wmh_ref, y_ref, c_ref, n_ref, m_ref,
                 nrep_scr, B=B, L=L)


def _mlstm_chunk(q_ref, k_ref, v_ref, o_ref, g_ref, bg_ref, wmh_ref,
                 y_ref, c_ref, n_ref, m_ref, nrep_scr, *, B, L):
    step = pl.program_id(0)

    @pl.when(step == 0)
    def _():
        c_ref[...] = jnp.zeros_like(c_ref)
        m_ref[...] = jnp.zeros_like(m_ref)
        nrep_scr[...] = jnp.zeros_like(nrep_scr)

    row = lax.broadcasted_iota(jnp.int32, (L, L), 0)
    col = lax.broadcasted_iota(jnp.int32, (L, L), 1)
    causal = col <= row
    tril = jnp.where(causal, 1.0, 0.0).astype(BF16)
    ones_l = jnp.ones((L, LANES), BF16)
    ones_d = jnp.ones((HEAD_DIM, LANES), BF16)
    inv_d = 1.0 / HEAD_DIM
    tn = (((0,), (0,)), ((), ()))

    heads = [(b, h) for b in range(B) for h in range(N_HEADS)]

    def hsl(h):
        return slice(h * HEAD_DIM, (h + 1) * HEAD_DIM)

    gate = []
    for b in range(B):
        g = g_ref[b] + bg_ref[...]
        hi, mid, lo = _split3(_log_sigmoid(g))
        bcum = _dot(tril, hi) + _dot(tril, mid) + _dot(tril, lo)
        gate.append((g, g.T, bcum, bcum.T, m_ref[b]))

    st = []
    for b, h in heads:
        g, g_t, bcum, bcum_t, m_all = gate[b]
        bc = jnp.broadcast_to(bcum[:, N_HEADS + h:N_HEADS + h + 1], (L, LANES))
        li = jnp.broadcast_to(g[:, h:h + 1], (L, LANES))
        br = bcum_t[N_HEADS + h:N_HEADS + h + 1, :]
        m_prev = m_all[h:h + 1, :]
        a = bc + m_prev
        dlog = jnp.where(causal, bc - br + g_t[h:h + 1, :], NEG)
        mt = jnp.maximum(a, jnp.max(dlog, axis=1, keepdims=True))
        st.append(dict(bc=bc, li=li, m_prev=m_prev, mt=mt, dw=jnp.exp(dlog - mt),
                       inter=jnp.exp(a - mt)))

    for (b, h), e in zip(heads, st):
        e["s"] = lax.dot_general(q_ref[b, :, hsl(h)], k_ref[b, :, hsl(h)], (((1,), (1,)), ((), ())),
                                 preferred_element_type=F32) * e["dw"]

    for i, ((b, h), e) in enumerate(zip(heads, st)):
        s_hi, s_lo = _split2(e["s"])
        cn = jnp.concatenate([c_ref[b, h], nrep_scr[i]], axis=1).astype(BF16)
        qc = _dot(q_ref[b, :, hsl(h)], cn)
        num = _dot(s_hi, v_ref[b, :, hsl(h)]) + _rep2(e["inter"]) * qc[:, :HEAD_DIM]
        den = _dot(s_hi, ones_l) + _dot(s_lo, ones_l) + e["inter"] * qc[:, HEAD_DIM:]
        rden = 1.0 / jnp.maximum(jnp.abs(den), jnp.exp(-e["mt"]))
        e["hh"] = num * _rep2(rden)

    for (b, h), e in zip(heads, st):
        hh = e["hh"]
        xc = hh - _rep2(_rowsum(hh, ones_d) * inv_d)
        rstd = lax.rsqrt(_rowsum(xc * xc, ones_d) * inv_d + LN_EPS)
        y = _sigmoid(o_ref[b, :, hsl(h)].astype(F32)) * (xc * _rep2(rstd) * wmh_ref[:, hsl(h)])
        y_ref[b, :, hsl(h)] = y.astype(BF16)

    m_rows = []
    for i, ((b, h), e) in enumerate(zip(heads, st)):
        kh = k_ref[b, :, hsl(h)]
        m_new = e["mt"][L - 1:L, :]
        b_last = e["bc"][L - 1:L, :]
        wc = jnp.exp(b_last - e["bc"] + e["li"] - m_new)
        dc = jnp.exp(b_last + e["m_prev"] - m_new)
        vw = (v_ref[b, :, hsl(h)].astype(F32) * _rep2(wc)).astype(BF16)
        c_ref[b, h] = _rep2(dc) * c_ref[b, h] + lax.dot_general(kh, vw, tn, preferred_element_type=F32)
        wc_hi, wc_lo = _split2(wc)
        nrep_scr[i] = (dc * nrep_scr[i] + lax.dot_general(kh, wc_hi, tn, preferred_element_type=F32)
                       + lax.dot_general(kh, wc_lo, tn, preferred_element_type=F32))
        m_rows.append(m_new)

    for b in range(B):
        m_ref[b] = jnp.concatenate(m_rows[b * N_HEADS:(b + 1) * N_HEADS], axis=0)

    @pl.when(step == pl.num_programs(0) - 1)
    def _():
        for b in range(B):
            n_ref[b] = jnp.concatenate(
                [nrep_scr[b * N_HEADS + h].T[0:1, :] for h in range(N_HEADS)], axis=0)


def _mlstm_prompt(q, k, v, o, gates, bg, wmh, *, L, cast=()):
    b, t, dm = q.shape
    assert L == LANES
    act = pl.BlockSpec((b, L, dm), lambda c: (0, c, 0))
    whole = lambda *shape: pl.BlockSpec(shape, lambda c: (0,) * len(shape))
    cast_specs, cast_shapes = _cast_side_job(cast, t // L, lambda c: (c, 0))
    return pl.pallas_call(
        functools.partial(_mlstm_chunk_kernel, B=b, L=L, n_cast=len(cast)),
        grid=(t // L,),
        in_specs=[act, act, act, act,
                  pl.BlockSpec((b, L, GATE_LANES), lambda c: (0, c, 0)),
                  whole(1, GATE_LANES), whole(1, dm)] + cast_specs,
        out_specs=[act,
                   whole(b, N_HEADS, HEAD_DIM, HEAD_DIM),
                   whole(b, N_HEADS, HEAD_DIM),
                   whole(b, N_HEADS, GATE_LANES)] + cast_specs,
        out_shape=[jax.ShapeDtypeStruct((b, t, dm), BF16),
                   jax.ShapeDtypeStruct((b, N_HEADS, HEAD_DIM, HEAD_DIM), F32),
                   jax.ShapeDtypeStruct((b, N_HEADS, HEAD_DIM), F32),
                   jax.ShapeDtypeStruct((b, N_HEADS, GATE_LANES), F32)] + cast_shapes,
        scratch_shapes=[pltpu.VMEM((b * N_HEADS, HEAD_DIM, LANES), F32)],
        compiler_params=_cparams(("arbitrary",)),
        name="mlstm_prompt",
    )(q, k, v, o, gates, bg, wmh, *cast)


def _mlstm_step_kernel(q_ref, k_ref, v_ref, o_ref, g_ref, bg_ref, mrow_ref, nrow_ref,
                       c0_ref, n0_ref, wmh_ref,
                       y_ref, c_ref, n_ref, m_ref, *, bb, T):
    R = bb * T
    per_slab = SLAB // T
    t = lax.rem(lax.broadcasted_iota(jnp.int32, (R, GATE_LANES), 0), T)

    def down(x, d):
        return pltpu.roll(x, d, 0)

    def up(x, d):
        return pltpu.roll(x, x.shape[0] - d, 0)

    def seg_last(x):
        out = x
        for d in range(1, T):
            out = jnp.where(t == T - 1 - d, up(x, d), out)
        return out

    g = g_ref[...] + bg_ref[...]
    li = pltpu.roll(g, N_HEADS, 1)
    lf = _log_sigmoid(g)
    b = lf
    for d in range(1, T):
        b = b + jnp.where(t >= d, down(lf, d), 0.0)
    m_prev = mrow_ref[...]
    a = b + m_prev
    dl = [li] + [jnp.where(t >= d, b - down(b, d) + down(li, d), NEG) for d in range(1, T)]
    mt = a
    for d in range(T):
        mt = jnp.maximum(mt, dl[d])
    dw = [jnp.exp(dl[d] - mt) for d in range(T)]
    inter = jnp.exp(a - mt)
    emt = jnp.exp(-mt)
    m_new = seg_last(mt)
    b_last = seg_last(b)
    wc = jnp.exp(b_last - b + li - m_new)
    dc = jnp.exp(b_last + m_prev - m_new)

    row_s = lax.broadcasted_iota(jnp.int32, (SLAB, HEAD_DIM), 0)
    row_r = lax.broadcasted_iota(jnp.int32, (R, HEAD_DIM), 0)

    for h in range(N_HEADS):
        hs = slice(h * HEAD_DIM, (h + 1) * HEAD_DIM)
        ln = N_HEADS + h

        def col(x):
            return x[:, ln:ln + 1]

        qb = q_ref[:, hs]
        kb = k_ref[:, hs]
        vb = v_ref[:, hs]
        qf = qb.astype(F32)
        kf = kb.astype(F32)
        vf = vb.astype(F32)
        num = jnp.zeros((R, HEAD_DIM), F32)
        den = jnp.zeros((R, 1), F32)
        for d in range(T):
            kd = kf if d == 0 else down(kf, d)
            vd = vf if d == 0 else down(vf, d)
            sw = jnp.sum(qf * kd, axis=1, keepdims=True) * col(dw[d])
            num = num + sw * vd
            den = den + sw

        qc_slabs = []
        for si in range(R // SLAB):
            q16 = qb[si * SLAB:(si + 1) * SLAB]
            acc = jnp.zeros((SLAB, HEAD_DIM), F32)
            for bl in range(per_slab):
                bi = si * per_slab + bl
                r = _dot(q16, c0_ref[bi, h].astype(BF16))
                acc = jnp.where(row_s // T == bl, r, acc)
            qc_slabs.append(acc)
        qc = jnp.concatenate(qc_slabs, axis=0)
        qn = jnp.sum(qf * nrow_ref[:, hs], axis=1, keepdims=True)
        num = num + col(inter) * qc
        den = den + col(inter) * qn
        hh = num / jnp.maximum(jnp.abs(den), col(emt))
        y_ref[:, hs] = _head_out(hh, o_ref[:, hs], wmh_ref[:, hs])

        wk = kf * col(wc)
        vw = vf * col(wc)
        for si in range(R // SLAB):
            k16 = kb[si * SLAB:(si + 1) * SLAB]
            vw16 = vw[si * SLAB:(si + 1) * SLAB]
            for bl in range(per_slab):
                bi = si * per_slab + bl
                last = bi * T + T - 1
                vwb = jnp.where(row_s // T == bl, vw16, 0.0).astype(BF16)
                dcb = dc[last:last + 1, ln:ln + 1]
                c_ref[bi, h] = dcb * c0_ref[bi, h] + lax.dot_general(
                    k16, vwb, (((0,), (0,)), ((), ())), preferred_element_type=F32)
                n_ref[bi, h:h + 1, :] = dcb * n0_ref[bi, h:h + 1, :] + jnp.sum(
                    jnp.where(row_r // T == bi, wk, 0.0), axis=0, keepdims=True)
                m_ref[bi, h:h + 1, :] = jnp.broadcast_to(
                    m_new[last:last + 1, ln:ln + 1], (1, GATE_LANES))


def _mlstm_sample(q, k, v, o, gates, bg, mrow, nrow, c0, n0, wmh, *, bb, T):
    rows, dm = q.shape
    nb = c0.shape[0]
    R = bb * T
    act = pl.BlockSpec((R, dm), lambda i: (i, 0))
    gat = pl.BlockSpec((R, GATE_LANES), lambda i: (i, 0))
    cspec = pl.BlockSpec((bb, N_HEADS, HEAD_DIM, HEAD_DIM), lambda i: (i, 0, 0, 0))
    nspec = pl.BlockSpec((bb, N_HEADS, HEAD_DIM), lambda i: (i, 0, 0))
    return pl.pallas_call(
        functools.partial(_mlstm_step_kernel, bb=bb, T=T),
        grid=(nb // bb,),
        in_specs=[act, act, act, act, gat,
                  pl.BlockSpec((1, GATE_LANES), lambda i: (0, 0)),
                  gat,
                  pl.BlockSpec((R, dm), lambda i: (i, 0)),
                  cspec, nspec,
                  pl.BlockSpec((1, dm), lambda i: (0, 0))],
        out_specs=[act, cspec, nspec,
                   pl.BlockSpec((bb, N_HEADS, GATE_LANES), lambda i: (i, 0, 0))],
        out_shape=[jax.ShapeDtypeStruct((rows, dm), BF16),
                   jax.ShapeDtypeStruct(c0.shape, F32),
                   jax.ShapeDtypeStruct(n0.shape, F32),
                   jax.ShapeDtypeStruct((nb, N_HEADS, GATE_LANES), F32)],
        compiler_params=_cparams(("arbitrary",)),
        name="mlstm_sample",
    )(q, k, v, o, gates, bg, mrow, nrow, c0, n0, wmh)


def _outproj_kernel(yc_ref, ym_ref, w_ref, x_ref, g1_ref, lg_ref, lb_ref, o_ref, *, alpha, splits):
    dc = yc_ref.shape[-1]
    rs = o_ref.shape[0] // splits
    x_rows = x_ref[...].reshape(o_ref.shape) if len(x_ref.shape) == 3 else None
    for i in range(splits):
        r = slice(i * rs, (i + 1) * rs)
        g1 = g1_ref[0] if g1_ref.shape[1] == 1 else g1_ref[0, r, :]
        x = x_ref[r, :] if x_rows is None else x_rows[r]
        mix = _dot(yc_ref[0, r, :], w_ref[0:dc, :]) + _dot(ym_ref[0, r, :], w_ref[dc:, :])
        o_ref[r, :] = _ln(alpha * x + (1.0 + g1) * mix) * lg_ref[...] + lb_ref[...]


def _outproj(yc, ym, w_out, x, mod, ln_g, ln_b, *, tm, tpb, alpha, splits=2):
    d = x.shape[-1]
    rows = x.size // d
    dc = yc.shape[-1]
    dm = ym.shape[-1]
    r = 1 if mod.shape[1] == 1 else tm
    vec = pl.BlockSpec((1, d), lambda m: (0, 0))
    x_spec = (pl.BlockSpec((tm, d), lambda m: (m, 0)) if x.ndim == 2
              else pl.BlockSpec(x.shape, lambda m: (0, 0, 0)))
    return pl.pallas_call(
        functools.partial(_outproj_kernel, alpha=alpha, splits=splits),
        grid=(rows // tm,),
        in_specs=[pl.BlockSpec((1, tm, dc), lambda m: (m // tpb, m % tpb, 0)),
                  pl.BlockSpec((1, tm, dm), lambda m: (m // tpb, m % tpb, 0)),
                  pl.BlockSpec((dc + dm, d), lambda m: (0, 0)),
                  x_spec,
                  pl.BlockSpec((1, r, d), lambda m: (m // tpb, 0, 2)),
                  vec, vec],
        out_specs=pl.BlockSpec((tm, d), lambda m: (m, 0)),
        out_shape=jax.ShapeDtypeStruct((rows, d), F32),
        compiler_params=_cparams(("arbitrary",)),
        name="outproj",
    )(yc, ym, w_out, x, mod, ln_g, ln_b)


def _ffn_kernel(*refs, nf, tpb, period, sample, alpha, splits):
    if sample:
        (x_ref, sh_ref, sc_ref, g2_ref, wa_ref, wg_ref, wconv_ref, wd_ref, lg_ref, lb_ref,
         s_ref, y_ref, t_ref, u_scr, z_scr, y_scr, acc_scr) = refs
    else:
        (x_ref, sh_ref, sc_ref, g2_ref, wa_ref, wg_ref, wconv_ref, wd_ref, lg_ref, lb_ref,
         y_ref, at_ref, u_scr, carry_scr) = refs
    m = pl.program_id(0)
    f = pl.program_id(1)

    if not sample:
        carried = _carried_rows(carry_scr, f, lax.rem(m, tpb) == 0)
        rs = u_scr.shape[0] // splits

        def body(first, last):
            prev = carried
            for i in range(splits):
                r = slice(i * rs, (i + 1) * rs)
                if first:
                    u = (_ln(x_ref[r, :]) * (1.0 + sc_ref[0]) + sh_ref[0]).astype(BF16)
                    u_scr[r, :] = u
                else:
                    u = u_scr[r, :]
                ac, prev = _conv3_rows(_dot(u, wa_ref[...]), wconv_ref[...], prev)
                hcur = (ac * _sigmoid(ac) * _dot(u, wg_ref[...])).astype(BF16)
                acc = _dot(hcur, wd_ref[...])
                if not first:
                    acc = y_ref[r, :] + acc
                if last:
                    acc = (_ln(alpha * x_ref[r, :] + (1.0 + g2_ref[0]) * acc)
                           * lg_ref[...] + lb_ref[...])
                y_ref[r, :] = acc
            carry_scr[f] = prev
            at_ref[0] = prev

        if nf == 1:
            body(True, True)
        else:
            pl.when(f == 0)(lambda: body(True, False))
            if nf > 2:
                pl.when(jnp.logical_and(f > 0, f < nf - 1))(lambda: body(False, False))
            pl.when(f == nf - 1)(lambda: body(False, True))
        return

    @pl.when(f == 0)
    def _():
        u = _ln(x_ref[...]) * (1.0 + sc_ref[0]) + sh_ref[0]
        u_scr[...] = u.astype(BF16)
        acc_scr[...] = jnp.zeros_like(acc_scr)

    u = u_scr[...]
    a = _dot(u, wa_ref[...])
    ac = _conv3_sequences(a, wconv_ref[...], s_ref, z_scr, y_scr, t_ref, period)
    hcur = (ac * _sigmoid(ac) * _dot(u, wg_ref[...])).astype(BF16)
    acc_scr[...] += _dot(hcur, wd_ref[...])

    @pl.when(f == nf - 1)
    def _():
        y = _ln(alpha * x_ref[...] + (1.0 + g2_ref[0]) * acc_scr[...]) * lg_ref[...] + lb_ref[...]
        y_ref[...] = y.reshape(y_ref.shape)


def _ffn(x, mod, w_up, w_conv, w_down, ln_g, ln_b, *, tm, tpb, tf, sample, alpha, period=0,
         splits=1, state=None):
    rows, d = x.shape
    ff = w_down.shape[0]
    nf = ff // tf
    nm = rows // tm
    r = tm if sample else 1
    vec = pl.BlockSpec((1, d), lambda m, f: (0, 0))
    in_specs = [
        pl.BlockSpec((tm, d), lambda m, f: (m, 0)),
        pl.BlockSpec((1, r, d), lambda m, f: (m // tpb, 0, 3)),
        pl.BlockSpec((1, r, d), lambda m, f: (m // tpb, 0, 4)),
        pl.BlockSpec((1, r, d), lambda m, f: (m // tpb, 0, 5)),
        pl.BlockSpec((d, tf), lambda m, f: (0, f)),
        pl.BlockSpec((d, tf), lambda m, f: (0, nf + f)),
        pl.BlockSpec((CONV_K, tf), lambda m, f: (0, f)),
        pl.BlockSpec((tf, d), lambda m, f: (f, 0)),
        vec, vec,
    ]
    args = [x, mod, mod, mod, w_up, w_up, w_conv, w_down, ln_g, ln_b]
    scratch = [pltpu.VMEM((tm, d), BF16)]
    out_specs = [pl.BlockSpec((tm, d), lambda m, f: (m, 0))]
    out_shape = [jax.ShapeDtypeStruct((rows, d), F32)]
    if sample:
        assert nm == 1
        st = pl.BlockSpec((tm // period, CONV_K - 1, tf), lambda m, f: (0, 0, f))
        in_specs.append(st)
        args.append(state)
        out_specs.append(st)
        out_shape.append(jax.ShapeDtypeStruct((tm // period, CONV_K - 1, ff), F32))
        scratch += [pltpu.VMEM((tf // LANES, tm, LANES), F32)] * 2 + [pltpu.VMEM((tm, d), F32)]
        out_specs[0] = pl.BlockSpec((tm // period, period, d), lambda m, f: (0, 0, 0))
        out_shape[0] = jax.ShapeDtypeStruct((tm // period, period, d), F32)
    else:
        out_specs.append(pl.BlockSpec((1, STATE_ROWS, tf), lambda m, f: (m, 0, f)))
        out_shape.append(jax.ShapeDtypeStruct((nm, STATE_ROWS, ff), F32))
        scratch.append(pltpu.VMEM((nf, STATE_ROWS, tf), F32))
    return pl.pallas_call(
        functools.partial(_ffn_kernel, nf=nf, tpb=tpb, period=period, sample=sample, alpha=alpha,
                          splits=splits),
        grid=(nm, nf),
        in_specs=in_specs,
        out_specs=out_specs,
        out_shape=out_shape,
        scratch_shapes=scratch,
        compiler_params=_cparams(("arbitrary", "arbitrary")),
        name="ffn_sample" if sample else "ffn_prompt",
    )(*args)


def _layer_prompt(x, mod, wts, yc, ztail, mlstm_out, *, alpha):
    B, T, D = x.shape
    (_, _, _, _, _, w_out, ln1_g, ln1_b, w_up, w_fconv, w_down, ln2_g, ln2_b) = wts
    tm = PROMPT_TM
    tpb = T // tm
    x2 = x.reshape(B * T, D)
    ym, C, n, m = mlstm_out
    x1 = _outproj(yc, ym, w_out, x2, mod, ln1_g, ln1_b, tm=OUTPROJ_TM, tpb=T // OUTPROJ_TM,
                  alpha=alpha)
    y, atail = _ffn(x1, mod, w_up, w_fconv, w_down, ln2_g, ln2_b,
                    tm=tm, tpb=tpb, tf=FFN_TF, sample=False, alpha=alpha, splits=2)
    return (y.reshape(B, T, D), ztail[tpb - 1::tpb, STATE_ROWS - 2:], C, n, m[..., 0],
            atail[tpb - 1::tpb, STATE_ROWS - 2:])


def _layer_sample(x, mod, conv_buf, C0, n0, m0, ffn_buf, wts, *, alpha):
    B, T, D = x.shape
    (w_in, w_gate, bg, w_conv, wmh, w_out, ln1_g, ln1_b, w_up, w_fconv, w_down, ln2_g, ln2_b) = wts
    rows = B * T
    yc, q, k, v, o, gates, conv_new = _inproj(x, mod, w_in, w_gate, w_conv, tm=rows, tpb=1,
                                              sample=True, period=T, state=conv_buf)
    mrow = jnp.pad(jnp.repeat(m0, T, axis=0), ((0, 0), (N_HEADS, GATE_LANES - 2 * N_HEADS)))
    nrow = jnp.repeat(n0.reshape(B, N_HEADS * HEAD_DIM), T, axis=0)
    ym, C, n, m = _mlstm_sample(q[0], k[0], v[0], o[0], gates[0], bg, mrow, nrow, C0, n0, wmh,
                                bb=SAMPLE_SEQS, T=T)
    x1 = _outproj(yc, ym[None], w_out, x, mod, ln1_g, ln1_b, tm=rows, tpb=1, alpha=alpha)
    y, ffn_new = _ffn(x1, mod, w_up, w_fconv, w_down, ln2_g, ln2_b, tm=rows, tpb=1, tf=FFN_TF,
                      sample=True, alpha=alpha, period=T, state=ffn_buf)
    return y, conv_new, C, n, m[..., 0], ffn_new


def kernel(x_prompt, x_sample, c_prompt, c_sample, state_conv, state_mlstm_C, state_mlstm_n,
           state_mlstm_m, state_ffn_conv, w_ada, b_ada, w_in, b_gate, w_conv, w_mh_norm, w_out,
           ln1_g, ln1_b, w_up, w_ffn_conv, w_down, ln2_g, ln2_b):
    depth = w_in.shape[0]
    alpha = (2 * depth) ** 0.25
    Bp = x_prompt.shape[0]
    Bs, Ts, D = x_sample.shape
    dc = w_conv.shape[-1]
    dm = w_mh_norm.shape[-1]
    n_main = 3 * dc + 4 * dm
    assert dc == dm == N_HEADS * HEAD_DIM and Ts >= CONV_K - 1 and SLAB % Ts == 0

    xp, xs = x_prompt, x_sample
    outs_p = [[] for _ in range(5)]
    outs_s = [[] for _ in range(5)]
    for l in range(depth):
        c_all = jnp.pad(jnp.concatenate([c_sample, c_prompt], axis=0), ((0, -(Bs + Bp) % LANES), (0, 0)))
        w_in_t = jnp.swapaxes(w_in[l], 0, 1)
        mod, w_in_b = _ada(c_all, w_ada[l], b_ada[l], w_in_t, n_main, n_seq=Bs, reps=Ts,
                           rows=Bs * Ts + Bp)
        mod_s = mod.reshape(1, Bs * Ts + Bp, 6 * D)
        mod_p = mod[Bs * Ts:].reshape(Bp, 1, 6 * D)
        w_gate = jnp.pad(w_in_t[n_main:], ((0, GATE_LANES - 2 * N_HEADS), (0, 0))).astype(BF16)
        bg = jnp.pad(b_gate[l], (0, GATE_LANES - 2 * N_HEADS)).reshape(1, GATE_LANES)
        wmh = w_mh_norm[l].reshape(1, dm)
        Tp = xp.shape[1]
        yc, q, k, v, o, gates, ztail, w_up_b = _inproj(
            xp.reshape(Bp * Tp, D), mod_p, w_in_b, w_gate, w_conv[l], tm=PROMPT_TM,
            tpb=Tp // PROMPT_TM, sample=False, cast=(w_up[l],))
        *mlstm_out, w_down_b, w_out_b = _mlstm_prompt(q, k, v, o, gates, bg, wmh, L=MLSTM_CHUNK,
                                                      cast=(w_down[l], w_out[l]))
        wts = (
            w_in_b,
            w_gate,
            bg,
            w_conv[l],
            wmh,
            w_out_b,
            ln1_g[l].reshape(1, D), ln1_b[l].reshape(1, D),
            w_up_b,
            w_ffn_conv[l],
            w_down_b,
            ln2_g[l].reshape(1, D), ln2_b[l].reshape(1, D),
        )
        xs, *st_s = _layer_sample(xs, mod_s, state_conv[l], state_mlstm_C[l], state_mlstm_n[l],
                                  state_mlstm_m[l], state_ffn_conv[l], wts, alpha=alpha)
        xp, *st_p = _layer_prompt(xp, mod_p, wts, yc, ztail, mlstm_out, alpha=alpha)
        for acc, val in zip(outs_p, st_p):
            acc.append(val)
        for acc, val in zip(outs_s, st_s):
            acc.append(val)
    return (xp.astype(x_prompt.dtype), xs.astype(x_sample.dtype),
            *[jnp.stack(a) for a in outs_p], *[jnp.stack(a) for a in outs_s])
```
